```python
import math
import jax, jax.numpy as jnp
from jax import lax
import numpy as np

D_MODEL = 1024
BATCH = 4
SEQ = 8192
DEPTH = 1
DEC_BATCH = 16
DEC_SEQ = 16
PAST_LEN = 2048

CHUNK = 64
GDN_HEADS = 4
GDN_DK = 128
GDN_DV = 128
GDN_CONV = 4
GDN_QK_W = GDN_HEADS * GDN_DK
GDN_V_W = GDN_HEADS * GDN_DV
GDN_CONV_CH = 2 * GDN_QK_W + GDN_V_W
MLA_HEADS = 4
MLA_D_NOPE = 128
MLA_D_ROPE = 64
MLA_D_V = 128
MLA_KV_RANK = 256
MLA_Q_W = MLA_HEADS * (MLA_D_NOPE + MLA_D_ROPE)
MLA_SCALE = (MLA_D_NOPE + MLA_D_ROPE) ** -0.5
ROPE_THETA = 10000.0
Q_BLOCK = 128
IN_W = GDN_CONV_CH + GDN_V_W + 2 * GDN_HEADS + MLA_Q_W + MLA_KV_RANK + MLA_D_ROPE
IN_SPLITS = (GDN_CONV_CH,
             GDN_CONV_CH + GDN_V_W,
             GDN_CONV_CH + GDN_V_W + GDN_HEADS,
             GDN_CONV_CH + GDN_V_W + 2 * GDN_HEADS,
             GDN_CONV_CH + GDN_V_W + 2 * GDN_HEADS + MLA_Q_W,
             GDN_CONV_CH + GDN_V_W + 2 * GDN_HEADS + MLA_Q_W + MLA_KV_RANK)
MIX_W = GDN_V_W + MLA_HEADS * MLA_D_V
N_EXPERTS = 256
N_GROUPS = 8
TOPK_GROUPS = 4
TOP_K = 8
EXP_HIDDEN = 256
SHARED_HIDDEN = 256
ROUTED_SCALE = 2.5
MOE_BLOCK = 128
DEEPNORM_ALPHA = (2.0 * DEPTH) ** 0.25
DEEPNORM_BETA = (8.0 * DEPTH) ** -0.25
LN_EPS = 1e-5
RMS_EPS = 1e-6
L2_EPS = 1e-6

kernel_name = 'hymba_gdn_mla_moe_deepnorm_stream_step'

F32 = jnp.float32


def _layernorm(x, g, b):
    xf = x.astype(F32)
    mu = jnp.mean(xf, axis=-1, keepdims=True)
    var = jnp.mean(jnp.square(xf - mu), axis=-1, keepdims=True)
    y = (xf - mu) * lax.rsqrt(var + LN_EPS) * g.astype(F32) + b.astype(F32)
    return y.astype(x.dtype)


def _rmsnorm(x, g):
    xf = x.astype(F32)
    y = xf * lax.rsqrt(jnp.mean(jnp.square(xf), axis=-1, keepdims=True) + RMS_EPS) * g.astype(F32)
    return y.astype(x.dtype)


def _l2norm(x):
    xf = x.astype(F32)
    return xf * lax.rsqrt(jnp.sum(jnp.square(xf), axis=-1, keepdims=True) + L2_EPS)


def _rope_tables(pos):
    inv_freq = ROPE_THETA ** (-jnp.arange(0, MLA_D_ROPE, 2, dtype=F32) / MLA_D_ROPE)
    ang = pos.astype(F32)[:, None] * inv_freq[None, :]
    return jnp.cos(ang), jnp.sin(ang)


def _apply_rope(x, cos, sin):
    xf = x.astype(F32)
    x1, x2 = xf[..., :MLA_D_ROPE // 2], xf[..., MLA_D_ROPE // 2:]
    return jnp.concatenate([x1 * cos - x2 * sin, x2 * cos + x1 * sin], axis=-1).astype(x.dtype)


def _causal_conv_silu(xpad, w):
    t_out = xpad.shape[1] - (GDN_CONV - 1)
    y = xpad[:, 0:t_out] * w[0]
    for i in range(1, GDN_CONV):
        y = y + xpad[:, i:i + t_out] * w[i]
    return jax.nn.silu(y)


def _gated_delta_chunked(q, k, v, g, beta, s0):
    bsz, t, _, _ = q.shape
    pad = (-t) % CHUNK
    if pad:
        pw = ((0, 0), (0, pad), (0, 0), (0, 0))
        q, k, v = jnp.pad(q, pw), jnp.pad(k, pw), jnp.pad(v, pw)
        g, beta = jnp.pad(g, pw[:3]), jnp.pad(beta, pw[:3])
    n = (t + pad) // CHUNK

    def blocks(a):
        a = a.reshape((bsz, n, CHUNK, GDN_HEADS) + a.shape[3:])
        return jnp.moveaxis(a, (1, 3), (0, 2))

    qc, kc, vc, bc = blocks(q), blocks(k), blocks(v), blocks(beta)
    gc = jnp.cumsum(blocks(g), axis=-1)
    idx = jnp.arange(CHUNK)
    incl = idx[:, None] >= idx[None, :]
    strict = idx[:, None] > idx[None, :]
    decay = jnp.exp(jnp.where(incl, gc[..., :, None] - gc[..., None, :], -jnp.inf))
    kb = kc * bc[..., None]
    a_mat = jnp.where(strict, jnp.einsum('nbhid,nbhjd->nbhij', kb, kc) * decay, 0.0)
    rhs = jnp.concatenate([vc * bc[..., None], kb * jnp.exp(gc)[..., None]], axis=-1)
    sol = lax.linalg.triangular_solve(a_mat + jnp.eye(CHUNK, dtype=F32), rhs,
                                      left_side=True, lower=True, unit_diagonal=True)
    u, w = sol[..., :GDN_DV], sol[..., GDN_DV:]
    intra = jnp.einsum('nbhid,nbhjd->nbhij', qc, kc) * decay
    q_dec = qc * jnp.exp(gc)[..., None]
    k_dec = kc * jnp.exp(gc[..., -1:] - gc)[..., None]
    g_last = jnp.exp(gc[..., -1])

    def step(s, xs):
        u_i, w_i, intra_i, qd_i, kd_i, gl_i = xs
        v_new = u_i - jnp.einsum('bhck,bhkv->bhcv', w_i, s)
        o_i = jnp.einsum('bhck,bhkv->bhcv', qd_i, s) + jnp.einsum('bhij,bhjv->bhiv', intra_i, v_new)
        s = s * gl_i[..., None, None] + jnp.einsum('bhck,bhcv->bhkv', kd_i, v_new)
        return s, o_i

    s_fin, o = lax.scan(step, s0, (u, w, intra, q_dec, k_dec, g_last))
    o = jnp.moveaxis(o, (0, 2), (1, 3)).reshape(bsz, n * CHUNK, GDN_HEADS, GDN_DV)[:, :t]
    return o, s_fin


def _mla_attend(q_nope, q_rope, k_nope, k_rope, v, mask):
    s = jnp.einsum('bqhd,bkhd->bhqk', q_nope, k_nope) + jnp.einsum('bqhd,bkd->bhqk', q_rope, k_rope)
    s = s.astype(F32) * MLA_SCALE
    if mask is not None:
        s = jnp.where(mask, s, -jnp.inf)
    p = jax.nn.softmax(s, axis=-1).astype(v.dtype)
    return jnp.einsum('bhqk,bkhd->bqhd', p, v)


def _mla_prompt(q_nope, q_rope, k_nope, k_rope, v):
    t = q_nope.shape[1]
    outs = []
    for q0 in range(0, t, Q_BLOCK):
        q1 = min(q0 + Q_BLOCK, t)
        q_chunk = (q0 + jnp.arange(q1 - q0)) // CHUNK
        k_chunk = jnp.arange(q1) // CHUNK
        mask = k_chunk[None, :] <= q_chunk[:, None]
        outs.append(_mla_attend(q_nope[:, q0:q1], q_rope[:, q0:q1], k_nope[:, :q1],
                                k_rope[:, :q1], v[:, :q1], mask))
    return jnp.concatenate(outs, axis=1)


def _token_mixers(h, pos, conv_hist, s0, past_latent, past_krope, p):
    bsz, t, _ = h.shape
    proj = h @ p['w_in']
    qkv_raw, z, a_raw, b_raw, q_mla, c_raw, kr_raw = jnp.split(proj, IN_SPLITS, axis=-1)

    xpad = jnp.concatenate([conv_hist.astype(h.dtype), qkv_raw], axis=1)
    conv_new = xpad[:, xpad.shape[1] - (GDN_CONV - 1):]
    qkv = _causal_conv_silu(xpad, p['gdn_conv_w'])
    q_g, k_g, v_g = jnp.split(qkv, (GDN_QK_W, 2 * GDN_QK_W), axis=-1)
    q_g = _l2norm(q_g.reshape(bsz, t, GDN_HEADS, GDN_DK)) * GDN_DK ** -0.5
    k_g = _l2norm(k_g.reshape(bsz, t, GDN_HEADS, GDN_DK))
    v_g = v_g.reshape(bsz, t, GDN_HEADS, GDN_DV).astype(F32)
    g = -jnp.exp(p['gdn_a_log'].astype(F32)) * jax.nn.softplus(a_raw.astype(F32) + p['gdn_dt_bias'].astype(F32))
    beta = jax.nn.sigmoid(b_raw.astype(F32))
    o_g, s_new = _gated_delta_chunked(q_g, k_g, v_g, g, beta, s0.astype(F32))
    o_g = _rmsnorm(o_g, p['gdn_norm_w']) * jax.nn.silu(z.reshape(bsz, t, GDN_HEADS, GDN_DV).astype(F32))
    o_g = o_g.reshape(bsz, t, GDN_V_W).astype(h.dtype)

    q_m = q_mla.reshape(bsz, t, MLA_HEADS, MLA_D_NOPE + MLA_D_ROPE)
    q_nope, q_rope = q_m[..., :MLA_D_NOPE], q_m[..., MLA_D_NOPE:]
    cos, sin = _rope_tables(pos)
    q_rope = _apply_rope(q_rope, cos[:, None, :], sin[:, None, :])
    latent = _rmsnorm(c_raw, p['mla_kv_norm_w'])
    k_rope = _apply_rope(kr_raw, cos, sin)
    if past_latent is None:
        lat_all, kr_all = latent, k_rope
    else:
        lat_all = jnp.concatenate([past_latent.astype(h.dtype), latent], axis=1)
        kr_all = jnp.concatenate([past_krope.astype(h.dtype), k_rope], axis=1)
    k_nope = jnp.einsum('btr,rhd->bthd', lat_all, p['mla_w_uk'])
    v_m = jnp.einsum('btr,rhd->bthd', lat_all, p['mla_w_uv'])
    if past_latent is None:
        o_m = _mla_prompt(q_nope, q_rope, k_nope, kr_all, v_m)
    else:
        o_m = _mla_attend(q_nope, q_rope, k_nope, kr_all, v_m, None)

    mix = jnp.concatenate([o_g, o_m.reshape(bsz, t, MLA_HEADS * MLA_D_V).astype(h.dtype)], axis=-1) @ p['w_out']
    return mix, latent, k_rope, s_new, conv_new


def _routed_experts(xt, idx, gate, w_gate, w_up, w_down):
    n_tok, d = xt.shape
    n_slots = n_tok * TOP_K
    flat_e = idx.reshape(-1).astype(jnp.int32)
    order = jnp.argsort(flat_e)
    e_sorted = flat_e[order]
    counts = jnp.bincount(flat_e, length=N_EXPERTS).astype(jnp.int32)
    padded = (counts + MOE_BLOCK - 1) // MOE_BLOCK * MOE_BLOCK
    start = jnp.cumsum(counts) - counts
    pend = jnp.cumsum(padded)
    pstart = pend - padded
    dest = pstart[e_sorted] + (jnp.arange(n_slots, dtype=jnp.int32) - start[e_sorted])
    n_blocks = -(-n_slots // MOE_BLOCK) + N_EXPERTS
    n_rows = n_blocks * MOE_BLOCK
    rows = jnp.full((n_rows,), n_tok, jnp.int32).at[dest].set((order // TOP_K).astype(jnp.int32))
    row_gate = jnp.zeros((n_rows,), gate.dtype).at[dest].set(gate.reshape(-1)[order])
    block_start = jnp.arange(n_blocks, dtype=jnp.int32) * MOE_BLOCK
    block_e = jnp.minimum(jnp.searchsorted(pend, block_start, side='right'), N_EXPERTS - 1)
    x_ext = jnp.concatenate([xt, jnp.zeros((1, d), xt.dtype)], axis=0)

    def run_block(args):
        r, e, gb = args
        xb = x_ext[r]
        hb = jax.nn.silu(xb @ w_gate[e]) * (xb @ w_up[e])
        return (hb @ w_down[e]) * gb[:, None]

    yb = lax.map(run_block, (rows.reshape(n_blocks, MOE_BLOCK), block_e, row_gate.reshape(n_blocks, MOE_BLOCK)))
    y = jnp.zeros((n_tok + 1, d), xt.dtype).at[rows].add(yb.reshape(n_rows, d))
    return y[:n_tok]


def _moe_ffn(h, p):
    bsz, t, d = h.shape
    xt = h.reshape(bsz * t, d)
    scores = jax.nn.sigmoid((xt @ p['router_w']).astype(F32))
    biased = scores + p['router_bias'].astype(F32)
    grp = biased.reshape(-1, N_GROUPS, N_EXPERTS // N_GROUPS)
    grp_score = jnp.sum(lax.top_k(grp, 2)[0], axis=-1)
    _, top_g = lax.top_k(grp_score, TOPK_GROUPS)
    gmask = jnp.any(top_g[..., None] == jnp.arange(N_GROUPS), axis=-2)
    allowed = jnp.repeat(gmask, N_EXPERTS // N_GROUPS, axis=-1)
    _, idx = lax.top_k(jnp.where(allowed, biased, -jnp.inf), TOP_K)
    w = jnp.take_along_axis(scores, idx, axis=-1)
    w = w / jnp.sum(w, axis=-1, keepdims=True) * ROUTED_SCALE
    routed = _routed_experts(xt, idx, w.astype(xt.dtype), p['exp_w_gate'], p['exp_w_up'], p['exp_w_down'])
    shared = (jax.nn.silu(xt @ p['shared_w_gate']) * (xt @ p['shared_w_up'])) @ p['shared_w_down']
    return (routed + shared).reshape(bsz, t, d)


def _layer(x, pos, conv_hist, s0, past_latent, past_krope, p):
    mix, latent, k_rope, s_new, conv_new = _token_mixers(x, pos, conv_hist, s0, past_latent, past_krope, p)
    x1 = _layernorm(DEEPNORM_ALPHA * x + mix, p['ln1_g'], p['ln1_b'])
    x2 = _layernorm(DEEPNORM_ALPHA * x1 + _moe_ffn(x1, p), p['ln2_g'], p['ln2_b'])
    return x2, latent, k_rope, s_new, conv_new


def setup_inputs(seed: int = 0) -> dict:
    key = jax.random.key(seed)
    ks = jax.random.split(key, 32)
    L = DEPTH

    def nrm(k, shape, scale):
        return jax.random.normal(k, shape, F32) * scale

    x_prompt = nrm(ks[0], (BATCH, SEQ, D_MODEL), 1.0)
    x_sample = nrm(ks[1], (DEC_BATCH, DEC_SEQ, D_MODEL), 1.0)
    cache_kv_latent = nrm(ks[2], (L, DEC_BATCH, PAST_LEN, MLA_KV_RANK), 1.0)
    cache_k_rope = nrm(ks[3], (L, DEC_BATCH, PAST_LEN, MLA_D_ROPE), 1.0)
    state_gdn = nrm(ks[4], (L, DEC_BATCH, GDN_HEADS, GDN_DK, GDN_DV), 0.05)
    state_conv = nrm(ks[5], (L, DEC_BATCH, GDN_CONV - 1, GDN_CONV_CH), 1.0)
    col_scale = jnp.ones((IN_W,), F32).at[2 * GDN_QK_W:GDN_CONV_CH].set(DEEPNORM_BETA)
    w_in = nrm(ks[6], (L, D_MODEL, IN_W), D_MODEL ** -0.5) * col_scale
    gdn_conv_w = nrm(ks[7], (L, GDN_CONV, GDN_CONV_CH), GDN_CONV ** -0.5)
    gdn_a_log = jnp.log(jax.random.uniform(ks[8], (L, GDN_HEADS), F32, 1.0, 16.0))
    dt = jnp.exp(jax.random.uniform(ks[9], (L, GDN_HEADS), F32, math.log(1e-3), math.log(1e-1)))
    gdn_dt_bias = dt + jnp.log(-jnp.expm1(-dt))
    gdn_norm_w = 1.0 + nrm(ks[10], (L, GDN_DV), 0.02)
    mla_kv_norm_w = 1.0 + nrm(ks[11], (L, MLA_KV_RANK), 0.02)
    mla_w_uk = nrm(ks[12], (L, MLA_KV_RANK, MLA_HEADS, MLA_D_NOPE), MLA_KV_RANK ** -0.5)
    mla_w_uv = nrm(ks[13], (L, MLA_KV_RANK, MLA_HEADS, MLA_D_V), MLA_KV_RANK ** -0.5 * DEEPNORM_BETA)
    w_out = nrm(ks[14], (L, MIX_W, D_MODEL), MIX_W ** -0.5 * DEEPNORM_BETA)
    ln1_g = 1.0 + nrm(ks[15], (L, D_MODEL), 0.02)
    ln1_b = nrm(ks[16], (L, D_MODEL), 0.02)
    router_w = nrm(ks[17], (L, D_MODEL, N_EXPERTS), D_MODEL ** -0.5)
    router_bias = nrm(ks[18], (L, N_EXPERTS), 0.01)
    exp_w_gate = nrm(ks[19], (L, N_EXPERTS, D_MODEL, EXP_HIDDEN), D_MODEL ** -0.5)
    exp_w_up = nrm(ks[20], (L, N_EXPERTS, D_MODEL, EXP_HIDDEN), D_MODEL ** -0.5)
    exp_w_down = nrm(ks[21], (L, N_EXPERTS, EXP_HIDDEN, D_MODEL), EXP_HIDDEN ** -0.5 * DEEPNORM_BETA)
    shared_w_gate = nrm(ks[22], (L, D_MODEL, SHARED_HIDDEN), D_MODEL ** -0.5)
    shared_w_up = nrm(ks[23], (L, D_MODEL, SHARED_HIDDEN), D_MODEL ** -0.5)
    shared_w_down = nrm(ks[24], (L, SHARED_HIDDEN, D_MODEL), SHARED_HIDDEN ** -0.5 * DEEPNORM_BETA)
    ln2_g = 1.0 + nrm(ks[25], (L, D_MODEL), 0.02)
    ln2_b = nrm(ks[26], (L, D_MODEL), 0.02)
    return {'x_prompt': x_prompt, 'x_sample': x_sample,
            'cache_kv_latent': cache_kv_latent, 'cache_k_rope': cache_k_rope,
            'state_gdn': state_gdn, 'state_conv': state_conv,
            'w_in': w_in, 'gdn_conv_w': gdn_conv_w, 'gdn_a_log': gdn_a_log, 'gdn_dt_bias': gdn_dt_bias,
            'gdn_norm_w': gdn_norm_w, 'mla_kv_norm_w': mla_kv_norm_w, 'mla_w_uk': mla_w_uk,
            'mla_w_uv': mla_w_uv, 'w_out': w_out, 'ln1_g': ln1_g, 'ln1_b': ln1_b,
            'router_w': router_w, 'router_bias': router_bias, 'exp_w_gate': exp_w_gate,
            'exp_w_up': exp_w_up, 'exp_w_down': exp_w_down, 'shared_w_gate': shared_w_gate,
            'shared_w_up': shared_w_up, 'shared_w_down': shared_w_down, 'ln2_g': ln2_g, 'ln2_b': ln2_b}


def reference(x_prompt, x_sample, cache_kv_latent, cache_k_rope, state_gdn, state_conv,
              w_in, gdn_conv_w, gdn_a_log, gdn_dt_bias, gdn_norm_w, mla_kv_norm_w, mla_w_uk,
              mla_w_uv, w_out, ln1_g, ln1_b, router_w, router_bias, exp_w_gate, exp_w_up,
              exp_w_down, shared_w_gate, shared_w_up, shared_w_down, ln2_g, ln2_b):
    b_p, t_p, _ = x_prompt.shape
    t_s = x_sample.shape[1]
    past = cache_kv_latent.shape[2]
    pos_p = jnp.arange(t_p)
    pos_s = past + jnp.arange(t_s)
    conv0 = jnp.zeros((b_p, GDN_CONV - 1, GDN_CONV_CH), x_prompt.dtype)
    s0 = jnp.zeros((b_p, GDN_HEADS, GDN_DK, GDN_DV), F32)
    h_p, h_s = x_prompt, x_sample
    lat_p, kr_p, sg_p, cv_p = [], [], [], []
    lat_s, kr_s, sg_s, cv_s = [], [], [], []
    for l in range(DEPTH):
        p = {'w_in': w_in[l], 'gdn_conv_w': gdn_conv_w[l], 'gdn_a_log': gdn_a_log[l],
             'gdn_dt_bias': gdn_dt_bias[l], 'gdn_norm_w': gdn_norm_w[l], 'mla_kv_norm_w': mla_kv_norm_w[l],
             'mla_w_uk': mla_w_uk[l], 'mla_w_uv': mla_w_uv[l], 'w_out': w_out[l],
             'ln1_g': ln1_g[l], 'ln1_b': ln1_b[l], 'router_w': router_w[l], 'router_bias': router_bias[l],
             'exp_w_gate': exp_w_gate[l], 'exp_w_up': exp_w_up[l], 'exp_w_down': exp_w_down[l],
             'shared_w_gate': shared_w_gate[l], 'shared_w_up': shared_w_up[l],
             'shared_w_down': shared_w_down[l], 'ln2_g': ln2_g[l], 'ln2_b': ln2_b[l]}
        h_p, a1, a2, a3, a4 = _layer(h_p, pos_p, conv0, s0, None, None, p)
        lat_p.append(a1)
        kr_p.append(a2)
        sg_p.append(a3)
        cv_p.append(a4)
        h_s, c1, c2, c3, c4 = _layer(h_s, pos_s, state_conv[l], state_gdn[l],
                                     cache_kv_latent[l], cache_k_rope[l], p)
        lat_s.append(c1)
        kr_s.append(c2)
        sg_s.append(c3)
        cv_s.append(c4)
    return (h_p, h_s,
            jnp.stack(lat_p), jnp.stack(kr_p), jnp.stack(sg_p).astype(state_gdn.dtype), jnp.stack(cv_p),
            jnp.stack(lat_s), jnp.stack(kr_s), jnp.stack(sg_s).astype(state_gdn.dtype), jnp.stack(cv_s))
```

```python
import functools
import math

import jax
import jax.numpy as jnp
from jax import lax
from jax.experimental import pallas as pl
from jax.experimental.pallas import tpu as pltpu

F32 = jnp.float32
BF16 = jnp.bfloat16
I32 = jnp.int32

D_MODEL = 1024
CHUNK = 64
GDN_HEADS = 4
GDN_DK = 128
GDN_DV = 128
GDN_CONV = 4
GDN_QK_W = GDN_HEADS * GDN_DK
GDN_V_W = GDN_HEADS * GDN_DV
GDN_CONV_CH = 2 * GDN_QK_W + GDN_V_W
MLA_HEADS = 4
MLA_D_NOPE = 128
MLA_D_ROPE = 64
MLA_D_V = 128
MLA_KV_RANK = 256
MLA_SCALE = (MLA_D_NOPE + MLA_D_ROPE) ** -0.5
ROPE_THETA = 10000.0
N_EXPERTS = 256
N_GROUPS = 8
GROUP_SIZE = N_EXPERTS // N_GROUPS
TOPK_GROUPS = 4
TOP_K = 8
EXP_HIDDEN = 256
ROUTED_SCALE = 2.5
DEPTH = 1
DEEPNORM_ALPHA = (2.0 * DEPTH) ** 0.25
LN_EPS = 1e-5
RMS_EPS = 1e-6
L2_EPS = 1e-6

LANES = 128
PK_QKV = 0
PK_Z = PK_QKV + GDN_CONV_CH
PK_QNOPE = PK_Z + GDN_V_W
PK_QROPE = PK_QNOPE + MLA_HEADS * MLA_D_NOPE
PK_CKV = PK_QROPE + MLA_HEADS * LANES
PK_KROPE = PK_CKV + MLA_KV_RANK
PK_AB = PK_KROPE + LANES
PK_W = PK_AB + LANES
MLA_QK_W = 2 * LANES

MOE_BM = 256
VMEM_LIMIT = 56 * 1024 * 1024


def _mm(a, b):
    return jnp.dot(a.astype(BF16), b.astype(BF16), preferred_element_type=F32)


def _mm_nt(a, b):
    return lax.dot_general(a.astype(BF16), b.astype(BF16), (((1,), (1,)), ((), ())),
                           preferred_element_type=F32)


def _mm_tn(a, b):
    return lax.dot_general(a.astype(BF16), b.astype(BF16), (((0,), (0,)), ((), ())),
                           preferred_element_type=F32)


def _split3(x):
    hi = x.astype(BF16)
    r = x - hi.astype(F32)
    mid = r.astype(BF16)
    lo = (r - mid.astype(F32)).astype(BF16)
    return hi, mid, lo


def _sigmoid(x):
    return 1.0 / (1.0 + jnp.exp(-x))


def _silu(x):
    return x * _sigmoid(x)


def _softplus(x):
    return jnp.maximum(x, 0.0) + jnp.log1p(jnp.exp(-jnp.abs(x)))


def _rope(x, cs, sn):
    w = x.shape[-1]
    n = w // LANES
    if n > 1:
        cs = jnp.concatenate([cs] * n, axis=1)
        sn = jnp.concatenate([sn] * n, axis=1)
    lane = lax.broadcasted_iota(I32, x.shape, 1) & (LANES - 1)
    half = MLA_D_ROPE // 2
    swapped = jnp.where(lane < half, pltpu.roll(x, w - half, 1), pltpu.roll(x, half, 1))
    return x * cs + swapped * sn


def _params(*sem):
    return pltpu.CompilerParams(dimension_semantics=sem, vmem_limit_bytes=VMEM_LIMIT)


def _front_kernel(x_ref, w_ref, convw_ref, hist_ref, gpar_ref, kvnw_ref, wukv_ref, cs_ref, sn_ref,
                  qkv_ref, z_ref, gb_ref, q_ref, k_ref, v_ref, lat_ref, kr_ref, convnew_ref,
                  xp_scr, *, tt):
    t = pl.program_id(1)
    hrow = 8 - (GDN_CONV - 1)

    @pl.when(t == 0)
    def _():
        xp_scr[hrow:8, :] = hist_ref[0]

    proj = _mm(x_ref[0], w_ref[...])

    raw = proj[:, PK_QKV:PK_Z]
    xp_scr[8:8 + tt, :] = raw
    cw = convw_ref[...]
    y = raw * cw[GDN_CONV - 1:GDN_CONV]
    for i in range(GDN_CONV - 1):
        y = y + xp_scr[hrow + i:hrow + i + tt, :] * cw[i:i + 1]
    tail = xp_scr[tt + hrow:tt + 8, :]
    convnew_ref[0] = tail
    xp_scr[hrow:8, :] = tail
    qkv = _silu(y)
    for h in range(2 * GDN_HEADS):
        xh = qkv[:, h * GDN_DK:(h + 1) * GDN_DK]
        xh = xh * lax.rsqrt(jnp.sum(xh * xh, axis=-1, keepdims=True) + L2_EPS)
        if h < GDN_HEADS:
            xh = xh * GDN_DK ** -0.5
        qkv_ref[0, :, h * GDN_DK:(h + 1) * GDN_DK] = xh
    qkv_ref[0, :, 2 * GDN_QK_W:] = qkv[:, 2 * GDN_QK_W:]
    z_ref[0] = proj[:, PK_Z:PK_QNOPE]

    ab = proj[:, PK_AB:PK_W]
    gpar = gpar_ref[...]
    g = -jnp.exp(gpar[0:1]) * _softplus(ab + gpar[1:2])
    beta = _sigmoid(ab)
    lane = lax.broadcasted_iota(I32, ab.shape, 1)
    gb_ref[0] = jnp.where(lane < GDN_HEADS, g, jnp.where(lane < 2 * GDN_HEADS, beta, 0.0))

    cs = cs_ref[...]
    sn = sn_ref[...]
    q_nope = proj[:, PK_QNOPE:PK_QROPE]
    q_rope = _rope(proj[:, PK_QROPE:PK_CKV], cs, sn)
    c_raw = proj[:, PK_CKV:PK_KROPE]
    latent = c_raw * lax.rsqrt(jnp.mean(c_raw * c_raw, axis=-1, keepdims=True) + RMS_EPS) * kvnw_ref[...]
    lat_ref[0] = latent
    k_rope = _rope(proj[:, PK_KROPE:PK_AB], cs, sn)
    kr_ref[0] = k_rope[:, :MLA_D_ROPE]
    kv = _mm(latent, wukv_ref[...])
    k_rope_b = k_rope.astype(BF16)
    for h in range(MLA_HEADS):
        q_ref[0, h, :, :LANES] = q_nope[:, h * LANES:(h + 1) * LANES].astype(BF16)
        q_ref[0, h, :, LANES:] = q_rope[:, h * LANES:(h + 1) * LANES].astype(BF16)
        k_ref[0, h, :, :LANES] = kv[:, h * LANES:(h + 1) * LANES].astype(BF16)
        k_ref[0, h, :, LANES:] = k_rope_b
        v_ref[0, h] = kv[:, (MLA_HEADS + h) * LANES:(MLA_HEADS + h + 1) * LANES].astype(BF16)


def _front(x, w_pack, conv_w, hist, gpar, kvnw, wukv, cs_tab, sn_tab, tt):
    b, t, _ = x.shape
    nt = t // tt
    const2 = lambda bi, ti: (0, 0)
    out_shape = (
        jax.ShapeDtypeStruct((b, t, GDN_CONV_CH), F32),
        jax.ShapeDtypeStruct((b, t, GDN_V_W), F32),
        jax.ShapeDtypeStruct((b, t, LANES), F32),
        jax.ShapeDtypeStruct((b, MLA_HEADS, t, MLA_QK_W), BF16),
        jax.ShapeDtypeStruct((b, MLA_HEADS, t, MLA_QK_W), BF16),
        jax.ShapeDtypeStruct((b, MLA_HEADS, t, MLA_D_V), BF16),
        jax.ShapeDtypeStruct((b, t, MLA_KV_RANK), F32),
        jax.ShapeDtypeStruct((b, t, MLA_D_ROPE), F32),
        jax.ShapeDtypeStruct((b, GDN_CONV - 1, GDN_CONV_CH), F32),
    )
    row3 = lambda w: pl.BlockSpec((1, tt, w), lambda bi, ti: (bi, ti, 0))
    head4 = lambda w: pl.BlockSpec((1, MLA_HEADS, tt, w), lambda bi, ti: (bi, 0, ti, 0))
    return pl.pallas_call(
        functools.partial(_front_kernel, tt=tt),
        grid=(b, nt),
        in_specs=[
            row3(D_MODEL),
            pl.BlockSpec((D_MODEL, PK_W), const2),
            pl.BlockSpec((GDN_CONV, GDN_CONV_CH), const2),
            pl.BlockSpec((1, GDN_CONV - 1, GDN_CONV_CH), lambda bi, ti: (bi, 0, 0)),
            pl.BlockSpec((2, LANES), const2),
            pl.BlockSpec((1, MLA_KV_RANK), const2),
            pl.BlockSpec((MLA_KV_RANK, 2 * MLA_HEADS * LANES), const2),
            pl.BlockSpec((tt, LANES), lambda bi, ti: (ti, 0)),
            pl.BlockSpec((tt, LANES), lambda bi, ti: (ti, 0)),
        ],
        out_specs=(
            row3(GDN_CONV_CH), row3(GDN_V_W), row3(LANES),
            head4(MLA_QK_W), head4(MLA_QK_W), head4(MLA_D_V),
            row3(MLA_KV_RANK), row3(MLA_D_ROPE),
            pl.BlockSpec((1, GDN_CONV - 1, GDN_CONV_CH), lambda bi, ti: (bi, 0, 0)),
        ),
        out_shape=out_shape,
        scratch_shapes=[pltpu.VMEM((tt + 8, GDN_CONV_CH), F32)],
        compiler_params=_params("arbitrary", "arbitrary"),
        name="front",
    )(x, w_pack, conv_w, hist, gpar, kvnw, wukv, cs_tab, sn_tab)


def _gdn_kernel(qkv_ref, z_ref, gb_ref, s0_ref, nw_ref, og_ref, sout_ref, s_scr, *, tg, c):
    t = pl.program_id(1)
    nh = GDN_HEADS
    r = nh * c
    sh = c.bit_length() - 1

    @pl.when(t == 0)
    def _():
        s_scr[...] = s0_ref[0]

    row = lax.broadcasted_iota(I32, (r, r), 0)
    col = lax.broadcasted_iota(I32, (r, r), 1)
    same = (row >> sh) == (col >> sh)
    incl = same & (row >= col)
    strict = same & (row > col)
    ltri = jnp.where(incl, 1.0, 0.0).astype(BF16)
    eye = jnp.where(row == col, 1.0, 0.0)
    lane0 = jnp.where(lax.broadcasted_iota(I32, (r, LANES), 1) == 0, 1.0, 0.0).astype(BF16)
    nw = nw_ref[...]

    def stacked(ref, r0, base):
        return jnp.concatenate(
            [ref[0, r0:r0 + c, base + h * LANES:base + (h + 1) * LANES] for h in range(nh)], axis=0)

    for ci in range(tg // c):
        r0 = ci * c
        qs = stacked(qkv_ref, r0, 0)
        ks = stacked(qkv_ref, r0, GDN_QK_W)
        vs = stacked(qkv_ref, r0, 2 * GDN_QK_W)
        zs = stacked(z_ref, r0, 0)
        gbc = gb_ref[0, r0:r0 + c, :]
        g_b = jnp.concatenate([jnp.broadcast_to(gbc[:, h:h + 1], (c, LANES)) for h in range(nh)], axis=0)
        beta_b = jnp.concatenate(
            [jnp.broadcast_to(gbc[:, nh + h:nh + h + 1], (c, LANES)) for h in range(nh)], axis=0)

        gc = sum(jnp.dot(ltri, p, preferred_element_type=F32) for p in _split3(g_b))
        gc_row = sum(lax.dot_general(lane0, p, (((1,), (1,)), ((), ())), preferred_element_type=F32)
                     for p in _split3(gc))
        gc_col = jnp.broadcast_to(gc[:, :1], (r, r))
        decay = jnp.exp(jnp.where(incl, gc_col - gc_row, -jnp.inf))

        qk_kk = _mm_nt(jnp.concatenate([qs, ks], axis=0), ks)
        intra = qk_kk[:r] * decay
        a_mat = jnp.where(strict, jnp.broadcast_to(beta_b[:, :1], (r, r)) * qk_kk[r:] * decay, 0.0)

        n_pow = -a_mat
        t_inv = eye + n_pow
        for _ in range(sh - 1):
            n_pow = _mm(n_pow, n_pow)
            t_inv = t_inv + _mm(t_inv, n_pow)

        egc = jnp.exp(gc)
        kb = ks * beta_b
        uw = _mm(t_inv, jnp.concatenate([vs * beta_b, kb * egc], axis=1))
        u = uw[:, :GDN_DV]
        w = uw[:, GDN_DV:]
        qd = qs * egc

        vn, qs_s = [], []
        for h in range(nh):
            hs = slice(h * c, (h + 1) * c)
            s_h = s_scr[h]
            wq = _mm(jnp.concatenate([w[hs], qd[hs]], axis=0), s_h)
            vn_h = u[hs] - wq[:c]
            g_last = gc[h * c + c - 1:h * c + c, :]
            kd = ks[hs] * jnp.exp(g_last - gc[hs])
            s_scr[h] = s_h * jnp.exp(g_last) + _mm_tn(kd, vn_h)
            vn.append(vn_h)
            qs_s.append(wq[c:])
        o = jnp.concatenate(qs_s, axis=0) + _mm(intra, jnp.concatenate(vn, axis=0))

        o = o * lax.rsqrt(jnp.mean(o * o, axis=-1, keepdims=True) + RMS_EPS) * nw
        o = o * _silu(zs)
        for h in range(nh):
            og_ref[0, r0:r0 + c, h * LANES:(h + 1) * LANES] = o[h * c:(h + 1) * c].astype(BF16)

    @pl.when(t == pl.num_programs(1) - 1)
    def _():
        sout_ref[0] = s_scr[...]


def _gdn(qkv, z, gb, s0, nw, tg, c):
    b, t, _ = qkv.shape
    row3 = lambda w: pl.BlockSpec((1, tg, w), lambda bi, ti: (bi, ti, 0))
    st = pl.BlockSpec((1, GDN_HEADS, GDN_DK, GDN_DV), lambda bi, ti: (bi, 0, 0, 0))
    return pl.pallas_call(
        functools.partial(_gdn_kernel, tg=tg, c=c),
        grid=(b, t // tg),
        in_specs=[row3(GDN_CONV_CH), row3(GDN_V_W), row3(LANES), st,
                  pl.BlockSpec((1, GDN_DV), lambda bi, ti: (0, 0))],
        out_specs=(row3(GDN_V_W), st),
        out_shape=(jax.ShapeDtypeStruct((b, t, GDN_V_W), BF16),
                   jax.ShapeDtypeStruct((b, GDN_HEADS, GDN_DK, GDN_DV), F32)),
        scratch_shapes=[pltpu.VMEM((GDN_HEADS, GDN_DK, GDN_DV), F32)],
        compiler_params=_params("arbitrary", "arbitrary"),
        name="gdn",
    )(qkv, z, gb, s0, nw)


def _attn_kernel(q_ref, k_ref, v_ref, o_ref, m_scr, l_scr, acc_scr, *, tq, tk):
    qi = pl.program_id(2)
    ki = pl.program_id(3)
    last_k = ((qi + 1) * tq - 1) // tk
    csh = CHUNK.bit_length() - 1

    @pl.when(ki == 0)
    def _():
        m_scr[...] = jnp.full(m_scr.shape, -jnp.inf, F32)
        l_scr[...] = jnp.zeros(l_scr.shape, F32)
        acc_scr[...] = jnp.zeros(acc_scr.shape, F32)

    @pl.when(ki <= last_k)
    def _():
        s = lax.dot_general(q_ref[0, 0], k_ref[0, 0], (((1,), (1,)), ((), ())),
                            preferred_element_type=F32) * MLA_SCALE
        qc = (qi * tq + lax.broadcasted_iota(I32, (tq, tk), 0)) >> csh
        kc = (ki * tk + lax.broadcasted_iota(I32, (tq, tk), 1)) >> csh
        s = jnp.where(kc <= qc, s, -jnp.inf)
        m_prev = m_scr[...]
        m_new = jnp.maximum(m_prev, jnp.max(s, axis=-1, keepdims=True))
        alpha = jnp.exp(m_prev - m_new)
        p = jnp.exp(s - m_new[:, :1])
        l_scr[...] = alpha * l_scr[...] + jnp.sum(p, axis=-1, keepdims=True)
        acc_scr[...] = alpha * acc_scr[...] + jnp.dot(p.astype(BF16), v_ref[0, 0],
                                                       preferred_element_type=F32)
        m_scr[...] = m_new

    @pl.when(ki == last_k)
    def _():
        o_ref[0] = (acc_scr[...] / l_scr[...]).astype(BF16)


def _attn_prompt(q, k, v, tq, tk):
    b, nh, t, _ = q.shape
    kv_idx = lambda bi, hi, qi, ki: (bi, hi, jnp.minimum(ki, ((qi + 1) * tq - 1) // tk), 0)
    return pl.pallas_call(
        functools.partial(_attn_kernel, tq=tq, tk=tk),
        grid=(b, nh, t // tq, t // tk),
        in_specs=[pl.BlockSpec((1, 1, tq, MLA_QK_W), lambda bi, hi, qi, ki: (bi, hi, qi, 0)),
                  pl.BlockSpec((1, 1, tk, MLA_QK_W), kv_idx),
                  pl.BlockSpec((1, 1, tk, MLA_D_V), kv_idx)],
        out_specs=pl.BlockSpec((1, tq, MLA_D_V), lambda bi, hi, qi, ki: (bi, qi, hi)),
        out_shape=jax.ShapeDtypeStruct((b, t, MLA_HEADS * MLA_D_V), BF16),
        scratch_shapes=[pltpu.VMEM((tq, LANES), F32), pltpu.VMEM((tq, LANES), F32),
                        pltpu.VMEM((tq, MLA_D_V), F32)],
        compiler_params=_params("arbitrary", "arbitrary", "arbitrary", "arbitrary"),
        name="attn_prompt",
    )(q, k, v)


def _attn_sample_kernel(q_ref, kn_ref, vn_ref, plat_ref, pkr_ref, wukv_ref, o_ref):
    kvp = _mm(plat_ref[0], wukv_ref[...])
    pkr = pkr_ref[0].astype(BF16)
    for h in range(MLA_HEADS):
        q = q_ref[0, h]
        s_past = (_mm_nt(q[:, :MLA_D_NOPE], kvp[:, h * LANES:(h + 1) * LANES])
                  + _mm_nt(q[:, MLA_D_NOPE:MLA_D_NOPE + MLA_D_ROPE], pkr)) * MLA_SCALE
        s_new = _mm_nt(q, kn_ref[0, h]) * MLA_SCALE
        m = jnp.maximum(jnp.max(s_past, axis=-1, keepdims=True), jnp.max(s_new, axis=-1, keepdims=True))
        p_past = jnp.exp(s_past - m)
        p_new = jnp.exp(s_new - m)
        l = jnp.sum(p_past, axis=-1, keepdims=True) + jnp.sum(p_new, axis=-1, keepdims=True)
        o = _mm(p_past, kvp[:, (MLA_HEADS + h) * LANES:(MLA_HEADS + h + 1) * LANES]) + _mm(p_new, vn_ref[0, h])
        o_ref[0, :, h * MLA_D_V:(h + 1) * MLA_D_V] = (o / l).astype(BF16)


def _attn_sample(q, k_new, v_new, past_lat, past_kr, wukv):
    b, nh, ts, _ = q.shape
    past = past_lat.shape[1]
    b4 = lambda w: pl.BlockSpec((1, nh, ts, w), lambda bi: (bi, 0, 0, 0))
    return pl.pallas_call(
        _attn_sample_kernel,
        grid=(b,),
        in_specs=[b4(MLA_QK_W), b4(MLA_QK_W), b4(MLA_D_V),
                  pl.BlockSpec((1, past, MLA_KV_RANK), lambda bi: (bi, 0, 0)),
                  pl.BlockSpec((1, past, MLA_D_ROPE), lambda bi: (bi, 0, 0)),
                  pl.BlockSpec((MLA_KV_RANK, 2 * MLA_HEADS * LANES), lambda bi: (0, 0))],
        out_specs=pl.BlockSpec((1, ts, MLA_HEADS * MLA_D_V), lambda bi: (bi, 0, 0)),
        out_shape=jax.ShapeDtypeStruct((b, ts, MLA_HEADS * MLA_D_V), BF16),
        compiler_params=_params("arbitrary"),
        name="attn_sample",
    )(q, k_new, v_new, past_lat, past_kr, wukv)


def _layernorm(y, g, b):
    mu = jnp.mean(y, axis=-1, keepdims=True)
    d = y - mu
    var = jnp.mean(d * d, axis=-1, keepdims=True)
    return d * lax.rsqrt(var + LN_EPS) * g + b


def _mixln_kernel(og_ref, om_ref, x_ref, w_ref, g_ref, b_ref, prev_ref, o_ref):
    del prev_ref
    mix = (jnp.dot(og_ref[...], w_ref[:GDN_V_W, :], preferred_element_type=F32)
           + jnp.dot(om_ref[...], w_ref[GDN_V_W:, :], preferred_element_type=F32))
    o_ref[...] = _layernorm(DEEPNORM_ALPHA * x_ref[...] + mix, g_ref[...], b_ref[...])


def _mixln(og, om, x, w_out, g, b, x1_all, row0, tm):
    n = x.shape[0]
    blk0 = row0 // tm
    rows = lambda w: pl.BlockSpec((tm, w), lambda i: (i, 0))
    const = lambda s: pl.BlockSpec(s, lambda i: (0, 0))
    return pl.pallas_call(
        _mixln_kernel,
        grid=(n // tm,),
        in_specs=[rows(GDN_V_W), rows(MLA_HEADS * MLA_D_V), rows(D_MODEL),
                  const((D_MODEL, D_MODEL)), const((1, D_MODEL)), const((1, D_MODEL)),
                  pl.BlockSpec(memory_space=pl.ANY)],
        out_specs=pl.BlockSpec((tm, D_MODEL), lambda i: (blk0 + i, 0)),
        out_shape=jax.ShapeDtypeStruct(x1_all.shape, F32),
        input_output_aliases={6: 0},
        compiler_params=_params("arbitrary"),
        name="mixln",
    )(og, om, x, w_out, g, b, x1_all)


def _router_kernel(x_ref, rw_ref, rb_ref, idx_ref, gate_ref, rank_ref, cnt_ref, carry_scr, *, tt):
    s = pl.program_id(0)

    @pl.when(s == 0)
    def _():
        carry_scr[...] = jnp.zeros(carry_scr.shape, F32)

    ninf = -jnp.inf
    big = float(2 * N_EXPERTS)
    scores = _sigmoid(_mm(x_ref[...], rw_ref[...]).T)
    biased = scores + rb_ref[...]
    eio = lax.broadcasted_iota(I32, (N_EXPERTS, tt), 0).astype(F32)

    def first_argmax(vals, io):
        m = jnp.max(vals, axis=0, keepdims=True)
        i = jnp.min(jnp.where(vals == m, io, big), axis=0, keepdims=True)
        return m, i

    gs = []
    for g in range(N_GROUPS):
        blk = biased[g * GROUP_SIZE:(g + 1) * GROUP_SIZE]
        io = (lax.broadcasted_iota(I32, (GROUP_SIZE, tt), 0) + g * GROUP_SIZE).astype(F32)
        m1, i1 = first_argmax(blk, io)
        m2 = jnp.max(jnp.where(io == i1, ninf, blk), axis=0, keepdims=True)
        gs.append(m1 + m2)
    gio = lax.broadcasted_iota(I32, (N_GROUPS, tt), 0).astype(F32)
    gsc = jnp.zeros((N_GROUPS, tt), F32)
    for g in range(N_GROUPS):
        gsc = jnp.where(gio == float(g), gs[g], gsc)
    gsel = jnp.zeros((N_GROUPS, tt), F32)
    for _ in range(TOPK_GROUPS):
        _, gi = first_argmax(gsc, gio)
        hit = gio == gi
        gsel = jnp.where(hit, 1.0, gsel)
        gsc = jnp.where(hit, ninf, gsc)
    masked = jnp.concatenate(
        [jnp.where(jnp.max(jnp.where(gio == float(g), gsel, 0.0), axis=0, keepdims=True) > 0.0,
                   biased[g * GROUP_SIZE:(g + 1) * GROUP_SIZE], ninf) for g in range(N_GROUPS)], axis=0)

    idx, wts = [], []
    sel = jnp.zeros((N_EXPERTS, tt), F32)
    for _ in range(TOP_K):
        _, ei = first_argmax(masked, eio)
        hit = eio == ei
        wts.append(jnp.sum(jnp.where(hit, scores, 0.0), axis=0, keepdims=True))
        masked = jnp.where(hit, ninf, masked)
        sel = jnp.where(hit, 1.0, sel)
        idx.append(ei)
    wsum = wts[0]
    for w in wts[1:]:
        wsum = wsum + w

    t0 = lax.broadcasted_iota(I32, (tt, tt), 0)
    t1 = lax.broadcasted_iota(I32, (tt, tt), 1)
    before = jnp.where(t0 < t1, 1.0, 0.0).astype(BF16)
    sel_b = sel.astype(BF16)
    base = carry_scr[:, :1] + jnp.dot(sel_b, before, preferred_element_type=F32)
    ranks = [jnp.sum(jnp.where(eio == ei, base, 0.0), axis=0, keepdims=True) for ei in idx]
    carry_scr[...] = carry_scr[...] + jnp.dot(sel_b, jnp.ones((tt, LANES), BF16), preferred_element_type=F32)

    for k in range(TOP_K):
        idx_ref[k:k + 1, :] = idx[k].astype(I32)
        gate_ref[k:k + 1, :] = wts[k] / wsum * ROUTED_SCALE
        rank_ref[k:k + 1, :] = ranks[k].astype(I32)
    cnt_ref[...] = carry_scr[...]


def _router(x1, rw, rb, tt):
    n = x1.shape[0]
    kt = lambda dt: jax.ShapeDtypeStruct((TOP_K, n), dt)
    kspec = pl.BlockSpec((TOP_K, tt), lambda i: (0, i))
    return pl.pallas_call(
        functools.partial(_router_kernel, tt=tt),
        grid=(n // tt,),
        in_specs=[pl.BlockSpec((tt, D_MODEL), lambda i: (i, 0)),
                  pl.BlockSpec((D_MODEL, N_EXPERTS), lambda i: (0, 0)),
                  pl.BlockSpec((N_EXPERTS, 1), lambda i: (0, 0))],
        out_specs=(kspec, kspec, kspec, pl.BlockSpec((N_EXPERTS, LANES), lambda i: (0, 0))),
        out_shape=(kt(I32), kt(F32), kt(I32), jax.ShapeDtypeStruct((N_EXPERTS, LANES), F32)),
        scratch_shapes=[pltpu.VMEM((N_EXPERTS, LANES), F32)],
        compiler_params=_params("arbitrary"),
        name="router",
    )(x1, rw, rb)


def _dispatch_kernel(dest_ref, x_hbm, xs_in, xs_out, sem, *, td):
    del xs_in
    s = pl.program_id(0)

    def row_copy(tok, d):
        return pltpu.make_async_copy(x_hbm.at[pl.ds(tok, 1)], xs_out.at[pl.ds(d, 1)], sem)

    def issue(i, carry):
        for k in range(TOP_K):
            row_copy(s * td + i, dest_ref[0, 0, i * TOP_K + k]).start()
        return carry

    lax.fori_loop(0, td, issue, 0)

    def drain(i, carry):
        for k in range(TOP_K):
            row_copy(0, 0).wait()
        return carry

    lax.fori_loop(0, td, drain, 0)


def _dispatch(dest3, x1, xs_zero, td):
    n = x1.shape[0]
    return pl.pallas_call(
        functools.partial(_dispatch_kernel, td=td),
        grid=(n // td,),
        in_specs=[pl.BlockSpec((1, 1, td * TOP_K), lambda i: (i, 0, 0), memory_space=pltpu.SMEM),
                  pl.BlockSpec(memory_space=pl.ANY), pl.BlockSpec(memory_space=pl.ANY)],
        out_specs=pl.BlockSpec(memory_space=pl.ANY),
        out_shape=jax.ShapeDtypeStruct(xs_zero.shape, F32),
        scratch_shapes=[pltpu.SemaphoreType.DMA(())],
        input_output_aliases={2: 0},
        compiler_params=_params("arbitrary"),
        name="dispatch",
    )(dest3, x1, xs_zero)


def _expert_kernel(be_ref, nact_ref, xs_ref, wg_ref, wu_ref, wd_ref, y_ref):
    del be_ref

    @pl.when(pl.program_id(0) < nact_ref[0])
    def _():
        xb = xs_ref[...].astype(BF16)
        hg = _mm(xb, wg_ref[0])
        hu = _mm(xb, wu_ref[0])
        y_ref[...] = _mm(_silu(hg) * hu, wd_ref[0])


def _experts(block_e, nact, xs, wg, wu, wd):
    n_rows = xs.shape[0]
    nb = n_rows // MOE_BM
    act = lambda b, na: jnp.minimum(b, na[0] - 1)
    rows = pl.BlockSpec((MOE_BM, D_MODEL), lambda b, be, na: (act(b, na), 0))
    wspec = lambda s: pl.BlockSpec((1,) + s, lambda b, be, na: (be[act(b, na)], 0, 0))
    return pl.pallas_call(
        _expert_kernel,
        grid_spec=pltpu.PrefetchScalarGridSpec(
            num_scalar_prefetch=2,
            grid=(nb,),
            in_specs=[rows, wspec((D_MODEL, EXP_HIDDEN)), wspec((D_MODEL, EXP_HIDDEN)),
                      wspec((EXP_HIDDEN, D_MODEL))],
            out_specs=rows),
        out_shape=jax.ShapeDtypeStruct((n_rows, D_MODEL), F32),
        compiler_params=_params("arbitrary"),
        name="experts",
    )(block_e, nact, xs, wg, wu, wd)


def _combine_kernel(dcur_ref, dnxt_ref, y_hbm, gate_ref, x_ref, wsg_ref, wsu_ref, wsd_ref, g_ref, b_ref,
                    outp_ref, outs_ref, buf, sem, *, tc, np_tiles):
    s = pl.program_id(0)
    ns = pl.num_programs(0)
    slot = s % 2

    def row_copy(d, slot_, k, i):
        return pltpu.make_async_copy(y_hbm.at[pl.ds(d, 1)], buf.at[slot_, k, pl.ds(i, 1)], sem.at[slot_])

    def issue(dref, slot_):
        def body(i, carry):
            for k in range(TOP_K):
                row_copy(dref[0, 0, i * TOP_K + k], slot_, k, i).start()
            return carry
        lax.fori_loop(0, tc, body, 0)

    @pl.when(s == 0)
    def _():
        issue(dcur_ref, 0)

    @pl.when(s + 1 < ns)
    def _():
        issue(dnxt_ref, 1 - slot)

    def drain(i, carry):
        for k in range(TOP_K):
            row_copy(0, slot, k, i).wait()
        return carry

    lax.fori_loop(0, tc, drain, 0)

    x = x_ref[...]
    gate = gate_ref[...]
    routed = buf[slot, 0] * gate[:, 0:1]
    for k in range(1, TOP_K):
        routed = routed + buf[slot, k] * gate[:, k:k + 1]
    xb = x.astype(BF16)
    shared = _mm(_silu(_mm(xb, wsg_ref[...])) * _mm(xb, wsu_ref[...]), wsd_ref[...])
    out = _layernorm(DEEPNORM_ALPHA * x + (routed + shared), g_ref[...], b_ref[...])

    @pl.when(s < np_tiles)
    def _():
        outp_ref[...] = out

    @pl.when(s >= np_tiles)
    def _():
        outs_ref[...] = out


def _combine(dest3, y_sorted, gate, x1, wsg, wsu, wsd, g, b, n_prompt, tc):
    n = x1.shape[0]
    ns = n // tc
    np_tiles = n_prompt // tc
    const = lambda s: pl.BlockSpec(s, lambda i: (0, 0))
    dspec = lambda f: pl.BlockSpec((1, 1, tc * TOP_K), f, memory_space=pltpu.SMEM)
    return pl.pallas_call(
        functools.partial(_combine_kernel, tc=tc, np_tiles=np_tiles),
        grid=(ns,),
        in_specs=[dspec(lambda i: (i, 0, 0)), dspec(lambda i: (jnp.minimum(i + 1, ns - 1), 0, 0)),
                  pl.BlockSpec(memory_space=pl.ANY),
                  pl.BlockSpec((tc, TOP_K), lambda i: (i, 0)),
                  pl.BlockSpec((tc, D_MODEL), lambda i: (i, 0)),
                  const((D_MODEL, EXP_HIDDEN)), const((D_MODEL, EXP_HIDDEN)), const((EXP_HIDDEN, D_MODEL)),
                  const((1, D_MODEL)), const((1, D_MODEL))],
        out_specs=(pl.BlockSpec((tc, D_MODEL), lambda i: (jnp.minimum(i, np_tiles - 1), 0)),
                   pl.BlockSpec((tc, D_MODEL), lambda i: (jnp.maximum(i - np_tiles, 0), 0))),
        out_shape=(jax.ShapeDtypeStruct((n_prompt, D_MODEL), F32),
                   jax.ShapeDtypeStruct((n - n_prompt, D_MODEL), F32)),
        scratch_shapes=[pltpu.VMEM((2, TOP_K, tc, D_MODEL), F32), pltpu.SemaphoreType.DMA((2,))],
        compiler_params=_params("arbitrary"),
        name="combine",
    )(dest3, dest3, y_sorted, gate, x1, wsg, wsu, wsd, g, b)


def _pack_w_in(w_in):
    d = w_in.shape[0]
    o_z = GDN_CONV_CH
    o_a = o_z + GDN_V_W
    o_b = o_a + GDN_HEADS
    o_q = o_b + GDN_HEADS
    o_c = o_q + MLA_HEADS * (MLA_D_NOPE + MLA_D_ROPE)
    o_kr = o_c + MLA_KV_RANK
    zeros = lambda w: jnp.zeros((d, w), w_in.dtype)
    wq = w_in[:, o_q:o_c].reshape(d, MLA_HEADS, MLA_D_NOPE + MLA_D_ROPE)
    q_nope = wq[:, :, :MLA_D_NOPE].reshape(d, MLA_HEADS * MLA_D_NOPE)
    q_rope = jnp.pad(wq[:, :, MLA_D_NOPE:], ((0, 0), (0, 0), (0, LANES - MLA_D_ROPE))).reshape(d, MLA_HEADS * LANES)
    cols = [w_in[:, :o_a], q_nope, q_rope, w_in[:, o_c:o_kr], w_in[:, o_kr:], zeros(LANES - MLA_D_ROPE),
            w_in[:, o_a:o_q], zeros(LANES - 2 * GDN_HEADS)]
    return jnp.concatenate(cols, axis=1).astype(BF16)


def _rope_tables(pos):
    inv_freq = ROPE_THETA ** (-jnp.arange(0, MLA_D_ROPE, 2, dtype=F32) / MLA_D_ROPE)
    ang = pos.astype(F32)[:, None] * inv_freq[None, :]
    cos, sin = jnp.cos(ang), jnp.sin(ang)
    pad = jnp.zeros((pos.shape[0], LANES - MLA_D_ROPE), F32)
    return jnp.concatenate([cos, cos, pad], axis=1), jnp.concatenate([-sin, sin, pad], axis=1)


def _pick(t, pref):
    return pref if t % pref == 0 else t


def _token_mixers(x, pos, conv_hist, s0, past, wts, x1_all, row0):
    b, t, _ = x.shape
    cs_tab, sn_tab = _rope_tables(pos)
    tt = _pick(t, 512)
    qkv, z, gb, q, k, v, latent, k_rope, conv_new = _front(
        x, wts["w_pack"], wts["conv_w"], conv_hist, wts["gpar"], wts["kvnw"], wts["wukv"], cs_tab, sn_tab, tt)
    c = min(CHUNK, t)
    og, s_new = _gdn(qkv, z, gb, s0, wts["gdn_nw"], _pick(t, 4 * CHUNK), c)
    if past is None:
        om = _attn_prompt(q, k, v, _pick(t, 512), _pick(t, 512))
    else:
        om = _attn_sample(q, k, v, past[0], past[1], wts["wukv"])
    n = b * t
    x1_all = _mixln(og.reshape(n, -1), om.reshape(n, -1), x.reshape(n, D_MODEL), wts["w_out"],
                    wts["ln1_g"], wts["ln1_b"], x1_all, row0, _pick(n, 256))
    return x1_all, latent, k_rope, s_new, conv_new


def _moe(x1_all, n_prompt, wts):
    n = x1_all.shape[0]
    idx, gate, rank, cnt = _router(x1_all, wts["router_w"], wts["router_b"], _pick(n, 256))
    counts = cnt[:, 0].astype(I32)
    padded = (counts + MOE_BM - 1) // MOE_BM * MOE_BM
    pend = jnp.cumsum(padded)
    pstart = pend - padded
    dest = (pstart[idx] + rank).T
    n_blocks = n * TOP_K // MOE_BM + N_EXPERTS
    block_e = jnp.minimum(jnp.searchsorted(pend, jnp.arange(n_blocks, dtype=I32) * MOE_BM, side="right"),
                          N_EXPERTS - 1).astype(I32)
    nact = (pend[-1:] // MOE_BM).astype(I32)

    td = _pick(n, 256)
    xs = _dispatch(dest.reshape(n // td, 1, td * TOP_K), x1_all,
                   jnp.zeros((n_blocks * MOE_BM, D_MODEL), F32), td)
    y_sorted = _experts(block_e, nact, xs, wts["exp_wg"], wts["exp_wu"], wts["exp_wd"])
    tc = _pick(math.gcd(n_prompt, n - n_prompt), 128)
    return _combine(dest.reshape(n // tc, 1, tc * TOP_K), y_sorted, gate.T, x1_all,
                    wts["sh_wg"], wts["sh_wu"], wts["sh_wd"], wts["ln2_g"], wts["ln2_b"], n_prompt, tc)


def kernel(x_prompt, x_sample, cache_kv_latent, cache_k_rope, state_gdn, state_conv, w_in, gdn_conv_w, gdn_a_log, gdn_dt_bias, gdn_norm_w, mla_kv_norm_w, mla_w_uk, mla_w_uv, w_out, ln1_g, ln1_b, router_w, router_bias, exp_w_gate, exp_w_up, exp_w_down, shared_w_gate, shared_w_up, shared_w_down, ln2_g, ln2_b):
    assert w_in.shape[0] == 1, "single-layer stack"
    b_p, t_p, _ = x_prompt.shape
    b_s, t_s, _ = x_sample.shape
    past = cache_kv_latent.shape[2]
    l = 0
    pad4 = lambda a: jnp.pad(a.astype(F32), (0, LANES - GDN_HEADS))
    wts = {
        "w_pack": _pack_w_in(w_in[l]),
        "conv_w": gdn_conv_w[l],
        "gpar": jnp.stack([pad4(gdn_a_log[l]), pad4(gdn_dt_bias[l])]),
        "kvnw": mla_kv_norm_w[l].reshape(1, -1),
        "wukv": jnp.concatenate([mla_w_uk[l].reshape(MLA_KV_RANK, -1), mla_w_uv[l].reshape(MLA_KV_RANK, -1)],
                                axis=1).astype(BF16),
        "gdn_nw": gdn_norm_w[l].reshape(1, -1),
        "w_out": w_out[l].astype(BF16),
        "ln1_g": ln1_g[l].reshape(1, -1), "ln1_b": ln1_b[l].reshape(1, -1),
        "router_w": router_w[l].astype(BF16), "router_b": router_bias[l].reshape(-1, 1),
        "exp_wg": exp_w_gate[l], "exp_wu": exp_w_up[l], "exp_wd": exp_w_down[l],
        "sh_wg": shared_w_gate[l].astype(BF16), "sh_wu": shared_w_up[l].astype(BF16),
        "sh_wd": shared_w_down[l].astype(BF16),
        "ln2_g": ln2_g[l].reshape(1, -1), "ln2_b": ln2_b[l].reshape(1, -1),
    }
    n_p, n_s = b_p * t_p, b_s * t_s
    x1_all = jnp.zeros((n_p + n_s, D_MODEL), F32)
    conv0 = jnp.zeros((b_p, GDN_CONV - 1, GDN_CONV_CH), F32)
    s0 = jnp.zeros((b_p, GDN_HEADS, GDN_DK, GDN_DV), F32)
    x1_all, lat_p, kr_p, sg_p, cv_p = _token_mixers(x_prompt, jnp.arange(t_p), conv0, s0, None, wts, x1_all, 0)
    x1_all, lat_s, kr_s, sg_s, cv_s = _token_mixers(
        x_sample, past + jnp.arange(t_s), state_conv[l], state_gdn[l],
        (cache_kv_latent[l], cache_k_rope[l]), wts, x1_all, n_p)
    y_p, y_s = _moe(x1_all, n_p, wts)
    return (y_p.reshape(b_p, t_p, D_MODEL), y_s.reshape(b_s, t_s, D_MODEL),
            lat_p[None], kr_p[None], sg_p[None], cv_p[None],
            lat_s[None], kr_s[None], sg_s[None], cv_s[None])
```

```python
import functools
import math

import jax
import jax.numpy as jnp
from jax import lax
from jax.experimental import pallas as pl
from jax.experimental.pallas import tpu as pltpu

F32 = jnp.float32
BF16 = jnp.bfloat16
I32 = jnp.int32

D_MODEL = 1024
CHUNK = 64
GDN_HEADS = 4
GDN_DK = 128
GDN_DV = 128
GDN_CONV = 4
GDN_QK_W = GDN_HEADS * GDN_DK
GDN_V_W = GDN_HEADS * GDN_DV
GDN_CONV_CH = 2 * GDN_QK_W + GDN_V_W
MLA_HEADS = 4
MLA_D_NOPE = 128
MLA_D_ROPE = 64
MLA_D_V = 128
MLA_KV_RANK = 256
MLA_SCALE = (MLA_D_NOPE + MLA_D_ROPE) ** -0.5
ROPE_THETA = 10000.0
N_EXPERTS = 256
N_GROUPS = 8
GROUP_SIZE = N_EXPERTS // N_GROUPS
TOPK_GROUPS = 4
TOP_K = 8
EXP_HIDDEN = 256
ROUTED_SCALE = 2.5
DEPTH = 1
DEEPNORM_ALPHA = (2.0 * DEPTH) ** 0.25
LN_EPS = 1e-5
RMS_EPS = 1e-6
L2_EPS = 1e-6

LANES = 128
PK_QKV = 0
PK_Z = PK_QKV + GDN_CONV_CH
PK_QNOPE = PK_Z + GDN_V_W
PK_QROPE = PK_QNOPE + MLA_HEADS * MLA_D_NOPE
PK_CKV = PK_QROPE + MLA_HEADS * LANES
PK_KROPE = PK_CKV + MLA_KV_RANK
PK_AB = PK_KROPE + LANES
PK_W = PK_AB + LANES
MLA_QK_W = 2 * LANES

MOE_BM = 256
VMEM_LIMIT = 56 * 1024 * 1024


def _mm(a, b):
    return jnp.dot(a.astype(BF16), b.astype(BF16), preferred_element_type=F32)


def _mm_nt(a, b):
    return lax.dot_general(a.astype(BF16), b.astype(BF16), (((1,), (1,)), ((), ())),
                           preferred_element_type=F32)


def _mm_tn(a, b):
    return lax.dot_general(a.astype(BF16), b.astype(BF16), (((0,), (0,)), ((), ())),
                           preferred_element_type=F32)


def _split3(x):
    hi = x.astype(BF16)
    r = x - hi.astype(F32)
    mid = r.astype(BF16)
    lo = (r - mid.astype(F32)).astype(BF16)
    return hi, mid, lo


def _sigmoid(x):
    return 1.0 / (1.0 + jnp.exp(-x))


def _silu(x):
    return x * _sigmoid(x)


def _softplus(x):
    return jnp.maximum(x, 0.0) + jnp.log1p(jnp.exp(-jnp.abs(x)))


def _rope(x, cs, sn):
    w = x.shape[-1]
    n = w // LANES
    if n > 1:
        cs = jnp.concatenate([cs] * n, axis=1)
        sn = jnp.concatenate([sn] * n, axis=1)
    lane = lax.broadcasted_iota(I32, x.shape, 1) & (LANES - 1)
    half = MLA_D_ROPE // 2
    swapped = jnp.where(lane < half, pltpu.roll(x, w - half, 1), pltpu.roll(x, half, 1))
    return x * cs + swapped * sn


def _params(*sem):
    return pltpu.CompilerParams(dimension_semantics=sem, vmem_limit_bytes=VMEM_LIMIT)


def _front_kernel(x_ref, w_ref, convw_ref, hist_ref, gpar_ref, kvnw_ref, wukv_ref, cs_ref, sn_ref,
                  qkv_ref, z_ref, gb_ref, q_ref, k_ref, v_ref, lat_ref, kr_ref, convnew_ref,
                  xp_scr, *, tt):
    t = pl.program_id(1)
    hrow = 8 - (GDN_CONV - 1)

    @pl.when(t == 0)
    def _():
        xp_scr[hrow:8, :] = hist_ref[0]

    proj = _mm(x_ref[0], w_ref[...])

    raw = proj[:, PK_QKV:PK_Z]
    xp_scr[8:8 + tt, :] = raw
    cw = convw_ref[...]
    y = raw * cw[GDN_CONV - 1:GDN_CONV]
    for i in range(GDN_CONV - 1):
        y = y + xp_scr[hrow + i:hrow + i + tt, :] * cw[i:i + 1]
    tail = xp_scr[tt + hrow:tt + 8, :]
    convnew_ref[0] = tail
    xp_scr[hrow:8, :] = tail
    qkv = _silu(y)
    for h in range(2 * GDN_HEADS):
        xh = qkv[:, h * GDN_DK:(h + 1) * GDN_DK]
        xh = xh * lax.rsqrt(jnp.sum(xh * xh, axis=-1, keepdims=True) + L2_EPS)
        if h < GDN_HEADS:
            xh = xh * GDN_DK ** -0.5
        qkv_ref[0, :, h * GDN_DK:(h + 1) * GDN_DK] = xh
    qkv_ref[0, :, 2 * GDN_QK_W:] = qkv[:, 2 * GDN_QK_W:]
    z_ref[0] = proj[:, PK_Z:PK_QNOPE]

    ab = proj[:, PK_AB:PK_W]
    gpar = gpar_ref[...]
    g = -jnp.exp(gpar[0:1]) * _softplus(ab + gpar[1:2])
    beta = _sigmoid(ab)
    lane = lax.broadcasted_iota(I32, ab.shape, 1)
    gb_ref[0] = jnp.where(lane < GDN_HEADS, g, jnp.where(lane < 2 * GDN_HEADS, beta, 0.0))

    cs = cs_ref[...]
    sn = sn_ref[...]
    q_nope = proj[:, PK_QNOPE:PK_QROPE]
    q_rope = _rope(proj[:, PK_QROPE:PK_CKV], cs, sn)
    c_raw = proj[:, PK_CKV:PK_KROPE]
    latent = c_raw * lax.rsqrt(jnp.mean(c_raw * c_raw, axis=-1, keepdims=True) + RMS_EPS) * kvnw_ref[...]
    lat_ref[0] = latent
    k_rope = _rope(proj[:, PK_KROPE:PK_AB], cs, sn)
    kr_ref[0] = k_rope[:, :MLA_D_ROPE]
    kv = _mm(latent, wukv_ref[...])
    k_rope_b = k_rope.astype(BF16)
    for h in range(MLA_HEADS):
        q_ref[0, h, :, :LANES] = q_nope[:, h * LANES:(h + 1) * LANES].astype(BF16)
        q_ref[0, h, :, LANES:] = q_rope[:, h * LANES:(h + 1) * LANES].astype(BF16)
        k_ref[0, h, :, :LANES] = kv[:, h * LANES:(h + 1) * LANES].astype(BF16)
        k_ref[0, h, :, LANES:] = k_rope_b
        v_ref[0, h] = kv[:, (MLA_HEADS + h) * LANES:(MLA_HEADS + h + 1) * LANES].astype(BF16)


def _front(x, w_pack, conv_w, hist, gpar, kvnw, wukv, cs_tab, sn_tab, tt):
    b, t, _ = x.shape
    nt = t // tt
    const2 = lambda bi, ti: (0, 0)
    out_shape = (
        jax.ShapeDtypeStruct((b, t, GDN_CONV_CH), F32),
        jax.ShapeDtypeStruct((b, t, GDN_V_W), F32),
        jax.ShapeDtypeStruct((b, t, LANES), F32),
        jax.ShapeDtypeStruct((b, MLA_HEADS, t, MLA_QK_W), BF16),
        jax.ShapeDtypeStruct((b, MLA_HEADS, t, MLA_QK_W), BF16),
        jax.ShapeDtypeStruct((b, MLA_HEADS, t, MLA_D_V), BF16),
        jax.ShapeDtypeStruct((b, t, MLA_KV_RANK), F32),
        jax.ShapeDtypeStruct((b, t, MLA_D_ROPE), F32),
        jax.ShapeDtypeStruct((b, GDN_CONV - 1, GDN_CONV_CH), F32),
    )
    row3 = lambda w: pl.BlockSpec((1, tt, w), lambda bi, ti: (bi, ti, 0))
    head4 = lambda w: pl.BlockSpec((1, MLA_HEADS, tt, w), lambda bi, ti: (bi, 0, ti, 0))
    return pl.pallas_call(
        functools.partial(_front_kernel, tt=tt),
        grid=(b, nt),
        in_specs=[
            row3(D_MODEL),
            pl.BlockSpec((D_MODEL, PK_W), const2),
            pl.BlockSpec((GDN_CONV, GDN_CONV_CH), const2),
            pl.BlockSpec((1, GDN_CONV - 1, GDN_CONV_CH), lambda bi, ti: (bi, 0, 0)),
            pl.BlockSpec((2, LANES), const2),
            pl.BlockSpec((1, MLA_KV_RANK), const2),
            pl.BlockSpec((MLA_KV_RANK, 2 * MLA_HEADS * LANES), const2),
            pl.BlockSpec((tt, LANES), lambda bi, ti: (ti, 0)),
            pl.BlockSpec((tt, LANES), lambda bi, ti: (ti, 0)),
        ],
        out_specs=(
            row3(GDN_CONV_CH), row3(GDN_V_W), row3(LANES),
            head4(MLA_QK_W), head4(MLA_QK_W), head4(MLA_D_V),
            row3(MLA_KV_RANK), row3(MLA_D_ROPE),
            pl.BlockSpec((1, GDN_CONV - 1, GDN_CONV_CH), lambda bi, ti: (bi, 0, 0)),
        ),
        out_shape=out_shape,
        scratch_shapes=[pltpu.VMEM((tt + 8, GDN_CONV_CH), F32)],
        compiler_params=_params("arbitrary", "arbitrary"),
        name="front",
    )(x, w_pack, conv_w, hist, gpar, kvnw, wukv, cs_tab, sn_tab)


def _gdn_kernel(qkv_ref, z_ref, gb_ref, s0_ref, nw_ref, og_ref, sout_ref, s_scr, *, tg, c):
    t = pl.program_id(1)
    nh = GDN_HEADS
    r = nh * c
    sh = c.bit_length() - 1

    @pl.when(t == 0)
    def _():
        s_scr[...] = s0_ref[0]

    row = lax.broadcasted_iota(I32, (r, r), 0)
    col = lax.broadcasted_iota(I32, (r, r), 1)
    same = (row >> sh) == (col >> sh)
    incl = same & (row >= col)
    strict = same & (row > col)
    ltri = jnp.where(incl, 1.0, 0.0).astype(BF16)
    eye = jnp.where(row == col, 1.0, 0.0)
    lane0 = jnp.where(lax.broadcasted_iota(I32, (r, LANES), 1) == 0, 1.0, 0.0).astype(BF16)
    nw = nw_ref[...]

    def stacked(ref, r0, base):
        return jnp.concatenate(
            [ref[0, r0:r0 + c, base + h * LANES:base + (h + 1) * LANES] for h in range(nh)], axis=0)

    for ci in range(tg // c):
        r0 = ci * c
        qs = stacked(qkv_ref, r0, 0)
        ks = stacked(qkv_ref, r0, GDN_QK_W)
        vs = stacked(qkv_ref, r0, 2 * GDN_QK_W)
        zs = stacked(z_ref, r0, 0)
        gbc = gb_ref[0, r0:r0 + c, :]
        g_b = jnp.concatenate([jnp.broadcast_to(gbc[:, h:h + 1], (c, LANES)) for h in range(nh)], axis=0)
        beta_b = jnp.concatenate(
            [jnp.broadcast_to(gbc[:, nh + h:nh + h + 1], (c, LANES)) for h in range(nh)], axis=0)

        gc = sum(jnp.dot(ltri, p, preferred_element_type=F32) for p in _split3(g_b))
        gc_row = sum(lax.dot_general(lane0, p, (((1,), (1,)), ((), ())), preferred_element_type=F32)
                     for p in _split3(gc))
        gc_col = jnp.broadcast_to(gc[:, :1], (r, r))
        decay = jnp.exp(jnp.where(incl, gc_col - gc_row, -jnp.inf))

        qk_kk = _mm_nt(jnp.concatenate([qs, ks], axis=0), ks)
        intra = qk_kk[:r] * decay
        a_mat = jnp.where(strict, jnp.broadcast_to(beta_b[:, :1], (r, r)) * qk_kk[r:] * decay, 0.0)

        n_pow = -a_mat
        t_inv = eye + n_pow
        for _ in range(sh - 1):
            n_pow = _mm(n_pow, n_pow)
            t_inv = t_inv + _mm(t_inv, n_pow)

        egc = jnp.exp(gc)
        kb = ks * beta_b
        uw = _mm(t_inv, jnp.concatenate([vs * beta_b, kb * egc], axis=1))
        u = uw[:, :GDN_DV]
        w = uw[:, GDN_DV:]
        qd = qs * egc

        vn, qs_s = [], []
        for h in range(nh):
            hs = slice(h * c, (h + 1) * c)
            s_h = s_scr[h]
            wq = _mm(jnp.concatenate([w[hs], qd[hs]], axis=0), s_h)
            vn_h = u[hs] - wq[:c]
            g_last = gc[h * c + c - 1:h * c + c, :]
            kd = ks[hs] * jnp.exp(g_last - gc[hs])
            s_scr[h] = s_h * jnp.exp(g_last) + _mm_tn(kd, vn_h)
            vn.append(vn_h)
            qs_s.append(wq[c:])
        o = jnp.concatenate(qs_s, axis=0) + _mm(intra, jnp.concatenate(vn, axis=0))

        o = o * lax.rsqrt(jnp.mean(o * o, axis=-1, keepdims=True) + RMS_EPS) * nw
        o = o * _silu(zs)
        for h in range(nh):
            og_ref[0, r0:r0 + c, h * LANES:(h + 1) * LANES] = o[h * c:(h + 1) * c].astype(BF16)

    @pl.when(t == pl.num_programs(1) - 1)
    def _():
        sout_ref[0] = s_scr[...]


def _gdn(qkv, z, gb, s0, nw, tg, c):
    b, t, _ = qkv.shape
    row3 = lambda w: pl.BlockSpec((1, tg, w), lambda bi, ti: (bi, ti, 0))
    st = pl.BlockSpec((1, GDN_HEADS, GDN_DK, GDN_DV), lambda bi, ti: (bi, 0, 0, 0))
    return pl.pallas_call(
        functools.partial(_gdn_kernel, tg=tg, c=c),
        grid=(b, t // tg),
        in_specs=[row3(GDN_CONV_CH), row3(GDN_V_W), row3(LANES), st,
                  pl.BlockSpec((1, GDN_DV), lambda bi, ti: (0, 0))],
        out_specs=(row3(GDN_V_W), st),
        out_shape=(jax.ShapeDtypeStruct((b, t, GDN_V_W), BF16),
                   jax.ShapeDtypeStruct((b, GDN_HEADS, GDN_DK, GDN_DV), F32)),
        scratch_shapes=[pltpu.VMEM((GDN_HEADS, GDN_DK, GDN_DV), F32)],
        compiler_params=_params("arbitrary", "arbitrary"),
        name="gdn",
    )(qkv, z, gb, s0, nw)


def _attn_kernel(qi_ref, ki_ref, q_ref, k_ref, v_ref, o_ref, m_scr, l_scr, acc_scr, *, tb, sub):
    step = pl.program_id(2)
    qi = qi_ref[step]
    ki = ki_ref[step]
    csh = CHUNK.bit_length() - 1
    c2 = MLA_SCALE * math.log2(math.e)

    @pl.when(ki == 0)
    def _():
        m_scr[...] = jnp.full(m_scr.shape, -jnp.inf, F32)
        l_scr[...] = jnp.zeros(l_scr.shape, F32)
        acc_scr[...] = jnp.zeros(acc_scr.shape, F32)

    def update(r0, j, masked):
        rows = slice(r0, tb)
        keys = slice(j * sub, (j + 1) * sub)
        s = lax.dot_general(q_ref[0, 0, rows, :], k_ref[0, 0, keys, :], (((1,), (1,)), ((), ())),
                            preferred_element_type=F32)
        if masked:
            qc = (r0 + lax.broadcasted_iota(I32, s.shape, 0)) >> csh
            kc = (j * sub + lax.broadcasted_iota(I32, s.shape, 1)) >> csh
            s = jnp.where(kc <= qc, s, -jnp.inf)
        m_prev = m_scr[rows, :]
        m_new = jnp.maximum(m_prev, jnp.max(s, axis=-1, keepdims=True))
        alpha = jnp.exp2((m_prev - m_new) * c2)
        p = jnp.exp2((s - m_new[:, :1]) * c2)
        l_scr[rows, :] = alpha * l_scr[rows, :] + jnp.sum(p, axis=-1, keepdims=True)
        acc_scr[rows, :] = alpha * acc_scr[rows, :] + jnp.dot(p.astype(BF16), v_ref[0, 0, keys, :],
                                                             preferred_element_type=F32)
        m_scr[rows, :] = m_new

    @pl.when(ki < qi)
    def _():
        for j in range(tb // sub):
            update(0, j, False)

    @pl.when(ki == qi)
    def _():
        for j in range(tb // sub):
            update(j * sub, j, True)
        o_ref[0] = (acc_scr[...] / l_scr[...]).astype(BF16)


def _attn_prompt(q, k, v, tb, sub):
    b, nh, t, _ = q.shape
    nt = t // tb
    pairs = [(qi, ki) for qi in range(nt) for ki in range(qi + 1)]
    qi_of = jnp.asarray([p[0] for p in pairs], I32)
    ki_of = jnp.asarray([p[1] for p in pairs], I32)
    return pl.pallas_call(
        functools.partial(_attn_kernel, tb=tb, sub=sub),
        grid_spec=pltpu.PrefetchScalarGridSpec(
            num_scalar_prefetch=2,
            grid=(b, nh, len(pairs)),
            in_specs=[pl.BlockSpec((1, 1, tb, MLA_QK_W), lambda bi, hi, s, qo, ko: (bi, hi, qo[s], 0)),
                      pl.BlockSpec((1, 1, tb, MLA_QK_W), lambda bi, hi, s, qo, ko: (bi, hi, ko[s], 0)),
                      pl.BlockSpec((1, 1, tb, MLA_D_V), lambda bi, hi, s, qo, ko: (bi, hi, ko[s], 0))],
            out_specs=pl.BlockSpec((1, tb, MLA_D_V), lambda bi, hi, s, qo, ko: (bi, qo[s], hi)),
            scratch_shapes=[pltpu.VMEM((tb, LANES), F32), pltpu.VMEM((tb, LANES), F32),
                            pltpu.VMEM((tb, MLA_D_V), F32)]),
        out_shape=jax.ShapeDtypeStruct((b, t, MLA_HEADS * MLA_D_V), BF16),
        compiler_params=_params("arbitrary", "arbitrary", "arbitrary"),
        name="attn_prompt",
    )(qi_of, ki_of, q, k, v)


def _attn_sample_kernel(q_ref, kn_ref, vn_ref, plat_ref, pkr_ref, wukv_ref, o_ref):
    kvp = _mm(plat_ref[0], wukv_ref[...])
    pkr = pkr_ref[0].astype(BF16)
    for h in range(MLA_HEADS):
        q = q_ref[0, h]
        s_past = (_mm_nt(q[:, :MLA_D_NOPE], kvp[:, h * LANES:(h + 1) * LANES])
                  + _mm_nt(q[:, MLA_D_NOPE:MLA_D_NOPE + MLA_D_ROPE], pkr)) * MLA_SCALE
        s_new = _mm_nt(q, kn_ref[0, h]) * MLA_SCALE
        m = jnp.maximum(jnp.max(s_past, axis=-1, keepdims=True), jnp.max(s_new, axis=-1, keepdims=True))
        p_past = jnp.exp(s_past - m)
        p_new = jnp.exp(s_new - m)
        l = jnp.sum(p_past, axis=-1, keepdims=True) + jnp.sum(p_new, axis=-1, keepdims=True)
        o = _mm(p_past, kvp[:, (MLA_HEADS + h) * LANES:(MLA_HEADS + h + 1) * LANES]) + _mm(p_new, vn_ref[0, h])
        o_ref[0, :, h * MLA_D_V:(h + 1) * MLA_D_V] = (o / l).astype(BF16)


def _attn_sample(q, k_new, v_new, past_lat, past_kr, wukv):
    b, nh, ts, _ = q.shape
    past = past_lat.shape[1]
    b4 = lambda w: pl.BlockSpec((1, nh, ts, w), lambda bi: (bi, 0, 0, 0))
    return pl.pallas_call(
        _attn_sample_kernel,
        grid=(b,),
        in_specs=[b4(MLA_QK_W), b4(MLA_QK_W), b4(MLA_D_V),
                  pl.BlockSpec((1, past, MLA_KV_RANK), lambda bi: (bi, 0, 0)),
                  pl.BlockSpec((1, past, MLA_D_ROPE), lambda bi: (bi, 0, 0)),
                  pl.BlockSpec((MLA_KV_RANK, 2 * MLA_HEADS * LANES), lambda bi: (0, 0))],
        out_specs=pl.BlockSpec((1, ts, MLA_HEADS * MLA_D_V), lambda bi: (bi, 0, 0)),
        out_shape=jax.ShapeDtypeStruct((b, ts, MLA_HEADS * MLA_D_V), BF16),
        compiler_params=_params("arbitrary"),
        name="attn_sample",
    )(q, k_new, v_new, past_lat, past_kr, wukv)


def _layernorm(y, g, b):
    mu = jnp.mean(y, axis=-1, keepdims=True)
    d = y - mu
    var = jnp.mean(d * d, axis=-1, keepdims=True)
    return d * lax.rsqrt(var + LN_EPS) * g + b


def _mixln_kernel(og_ref, om_ref, x_ref, w_ref, g_ref, b_ref, prev_ref, o_ref):
    del prev_ref
    mix = (jnp.dot(og_ref[...], w_ref[:GDN_V_W, :], preferred_element_type=F32)
           + jnp.dot(om_ref[...], w_ref[GDN_V_W:, :], preferred_element_type=F32))
    o_ref[...] = _layernorm(DEEPNORM_ALPHA * x_ref[...] + mix, g_ref[...], b_ref[...])


def _mixln(og, om, x, w_out, g, b, x1_all, row0, tm):
    n = x.shape[0]
    blk0 = row0 // tm
    rows = lambda w: pl.BlockSpec((tm, w), lambda i: (i, 0))
    const = lambda s: pl.BlockSpec(s, lambda i: (0, 0))
    return pl.pallas_call(
        _mixln_kernel,
        grid=(n // tm,),
        in_specs=[rows(GDN_V_W), rows(MLA_HEADS * MLA_D_V), rows(D_MODEL),
                  const((D_MODEL, D_MODEL)), const((1, D_MODEL)), const((1, D_MODEL)),
                  pl.BlockSpec(memory_space=pl.ANY)],
        out_specs=pl.BlockSpec((tm, D_MODEL), lambda i: (blk0 + i, 0)),
        out_shape=jax.ShapeDtypeStruct(x1_all.shape, F32),
        input_output_aliases={6: 0},
        compiler_params=_params("arbitrary"),
        name="mixln",
    )(og, om, x, w_out, g, b, x1_all)


def _router_kernel(x_ref, rw_ref, rb_ref, idx_ref, gate_ref, rank_ref, cnt_ref, carry_scr, *, tt):
    s = pl.program_id(0)

    @pl.when(s == 0)
    def _():
        carry_scr[...] = jnp.zeros(carry_scr.shape, F32)

    ninf = -jnp.inf
    big = float(2 * N_EXPERTS)
    scores = _sigmoid(_mm(x_ref[...], rw_ref[...]).T)
    biased = scores + rb_ref[...]
    eio = lax.broadcasted_iota(I32, (N_EXPERTS, tt), 0).astype(F32)

    def first_argmax(vals, io):
        m = jnp.max(vals, axis=0, keepdims=True)
        i = jnp.min(jnp.where(vals == m, io, big), axis=0, keepdims=True)
        return m, i

    gs = []
    for g in range(N_GROUPS):
        blk = biased[g * GROUP_SIZE:(g + 1) * GROUP_SIZE]
        io = (lax.broadcasted_iota(I32, (GROUP_SIZE, tt), 0) + g * GROUP_SIZE).astype(F32)
        m1, i1 = first_argmax(blk, io)
        m2 = jnp.max(jnp.where(io == i1, ninf, blk), axis=0, keepdims=True)
        gs.append(m1 + m2)
    gio = lax.broadcasted_iota(I32, (N_GROUPS, tt), 0).astype(F32)
    gsc = jnp.zeros((N_GROUPS, tt), F32)
    for g in range(N_GROUPS):
        gsc = jnp.where(gio == float(g), gs[g], gsc)
    gsel = jnp.zeros((N_GROUPS, tt), F32)
    for _ in range(TOPK_GROUPS):
        _, gi = first_argmax(gsc, gio)
        hit = gio == gi
        gsel = jnp.where(hit, 1.0, gsel)
        gsc = jnp.where(hit, ninf, gsc)
    masked = jnp.concatenate(
        [jnp.where(jnp.max(jnp.where(gio == float(g), gsel, 0.0), axis=0, keepdims=True) > 0.0,
                   biased[g * GROUP_SIZE:(g + 1) * GROUP_SIZE], ninf) for g in range(N_GROUPS)], axis=0)

    idx, wts = [], []
    sel = jnp.zeros((N_EXPERTS, tt), F32)
    for _ in range(TOP_K):
        _, ei = first_argmax(masked, eio)
        hit = eio == ei
        wts.append(jnp.sum(jnp.where(hit, scores, 0.0), axis=0, keepdims=True))
        masked = jnp.where(hit, ninf, masked)
        sel = jnp.where(hit, 1.0, sel)
        idx.append(ei)
    wsum = wts[0]
    for w in wts[1:]:
        wsum = wsum + w

    t0 = lax.broadcasted_iota(I32, (tt, tt), 0)
    t1 = lax.broadcasted_iota(I32, (tt, tt), 1)
    before = jnp.where(t0 < t1, 1.0, 0.0).astype(BF16)
    sel_b = sel.astype(BF16)
    base = carry_scr[:, :1] + jnp.dot(sel_b, before, preferred_element_type=F32)
    ranks = [jnp.sum(jnp.where(eio == ei, base, 0.0), axis=0, keepdims=True) for ei in idx]
    carry_scr[...] = carry_scr[...] + jnp.dot(sel_b, jnp.ones((tt, LANES), BF16), preferred_element_type=F32)

    for k in range(TOP_K):
        idx_ref[k:k + 1, :] = idx[k].astype(I32)
        gate_ref[k:k + 1, :] = wts[k] / wsum * ROUTED_SCALE
        rank_ref[k:k + 1, :] = ranks[k].astype(I32)
    cnt_ref[...] = carry_scr[...]


def _router(x1, rw, rb, tt):
    n = x1.shape[0]
    kt = lambda dt: jax.ShapeDtypeStruct((TOP_K, n), dt)
    kspec = pl.BlockSpec((TOP_K, tt), lambda i: (0, i))
    return pl.pallas_call(
        functools.partial(_router_kernel, tt=tt),
        grid=(n // tt,),
        in_specs=[pl.BlockSpec((tt, D_MODEL), lambda i: (i, 0)),
                  pl.BlockSpec((D_MODEL, N_EXPERTS), lambda i: (0, 0)),
                  pl.BlockSpec((N_EXPERTS, 1), lambda i: (0, 0))],
        out_specs=(kspec, kspec, kspec, pl.BlockSpec((N_EXPERTS, LANES), lambda i: (0, 0))),
        out_shape=(kt(I32), kt(F32), kt(I32), jax.ShapeDtypeStruct((N_EXPERTS, LANES), F32)),
        scratch_shapes=[pltpu.VMEM((N_EXPERTS, LANES), F32)],
        compiler_params=_params("arbitrary"),
        name="router",
    )(x1, rw, rb)


def _dest_kernel(idx_ref, rank_ref, pstart_ref, dest_ref, *, tt):
    eio = lax.broadcasted_iota(I32, (N_EXPERTS, tt), 0)
    pstart = pstart_ref[...]
    for k in range(TOP_K):
        start = jnp.sum(jnp.where(eio == idx_ref[k:k + 1, :], pstart, 0.0), axis=0, keepdims=True)
        dest_ref[k:k + 1, :] = start.astype(I32) + rank_ref[k:k + 1, :]


def _dest(idx, rank, pstart, tt):
    n = idx.shape[1]
    kspec = pl.BlockSpec((TOP_K, tt), lambda i: (0, i))
    return pl.pallas_call(
        functools.partial(_dest_kernel, tt=tt),
        grid=(n // tt,),
        in_specs=[kspec, kspec, pl.BlockSpec((N_EXPERTS, 1), lambda i: (0, 0))],
        out_specs=kspec,
        out_shape=jax.ShapeDtypeStruct((TOP_K, n), I32),
        compiler_params=_params("arbitrary"),
        name="dest",
    )(idx, rank, pstart)


def _dispatch_kernel(tail_ref, dest_ref, x_ref, xs_out, zbuf, sem, zsem, *, td):
    s = pl.program_id(0)

    @pl.when(s == 0)
    def _():
        zbuf[...] = jnp.zeros(zbuf.shape, F32)

        def zero_copy(e):
            return pltpu.make_async_copy(zbuf, xs_out.at[pl.ds(pl.multiple_of(tail_ref[e], MOE_BM), MOE_BM)], zsem)

        def zstart(e, carry):
            zero_copy(e).start()
            return carry

        def zwait(e, carry):
            zero_copy(e).wait()
            return carry

        lax.fori_loop(0, N_EXPERTS, zstart, 0)
        lax.fori_loop(0, N_EXPERTS, zwait, 0)

    def row_copy(i, d):
        return pltpu.make_async_copy(x_ref.at[pl.ds(i, 1)], xs_out.at[pl.ds(d, 1)], sem)

    def issue(i, carry):
        for k in range(TOP_K):
            row_copy(i, dest_ref[k, i]).start()
        return carry

    lax.fori_loop(0, td, issue, 0)

    def drain(i, carry):
        for k in range(TOP_K):
            row_copy(0, 0).wait()
        return carry

    lax.fori_loop(0, td, drain, 0)


def _dispatch(tail, dest, x1, n_rows, td):
    n = x1.shape[0]
    return pl.pallas_call(
        functools.partial(_dispatch_kernel, td=td),
        grid_spec=pltpu.PrefetchScalarGridSpec(
            num_scalar_prefetch=1,
            grid=(n // td,),
            in_specs=[pl.BlockSpec((TOP_K, td), lambda i, tl: (0, i), memory_space=pltpu.SMEM),
                      pl.BlockSpec((td, D_MODEL), lambda i, tl: (i, 0))],
            out_specs=pl.BlockSpec(memory_space=pl.ANY),
            scratch_shapes=[pltpu.VMEM((MOE_BM, D_MODEL), F32), pltpu.SemaphoreType.DMA(()),
                            pltpu.SemaphoreType.DMA(())]),
        out_shape=jax.ShapeDtypeStruct((n_rows, D_MODEL), F32),
        compiler_params=_params("arbitrary"),
        name="dispatch",
    )(tail, dest, x1)


def _expert_kernel(be_ref, nact_ref, xs_ref, wg_ref, wu_ref, wd_ref, y_ref):
    del be_ref

    @pl.when(pl.program_id(0) < nact_ref[0])
    def _():
        xb = xs_ref[...].astype(BF16)
        hg = _mm(xb, wg_ref[0])
        hu = _mm(xb, wu_ref[0])
        y_ref[...] = _mm(_silu(hg) * hu, wd_ref[0])


def _experts(block_e, nact, xs, wg, wu, wd):
    n_rows = xs.shape[0]
    nb = n_rows // MOE_BM
    act = lambda b, na: jnp.minimum(b, na[0] - 1)
    rows = pl.BlockSpec((MOE_BM, D_MODEL), lambda b, be, na: (act(b, na), 0))
    wspec = lambda s: pl.BlockSpec((1,) + s, lambda b, be, na: (be[act(b, na)], 0, 0))
    return pl.pallas_call(
        _expert_kernel,
        grid_spec=pltpu.PrefetchScalarGridSpec(
            num_scalar_prefetch=2,
            grid=(nb,),
            in_specs=[rows, wspec((D_MODEL, EXP_HIDDEN)), wspec((D_MODEL, EXP_HIDDEN)),
                      wspec((EXP_HIDDEN, D_MODEL))],
            out_specs=rows),
        out_shape=jax.ShapeDtypeStruct((n_rows, D_MODEL), F32),
        compiler_params=_params("arbitrary"),
        name="experts",
    )(block_e, nact, xs, wg, wu, wd)


def _combine_kernel(dcur_ref, dnxt_ref, y_hbm, gate_ref, x_ref, wsg_ref, wsu_ref, wsd_ref, g_ref, b_ref,
                    outp_ref, outs_ref, buf, sem, *, tc, np_tiles):
    s = pl.program_id(0)
    ns = pl.num_programs(0)
    slot = s % 2

    def row_copy(d, slot_, k, i):
        return pltpu.make_async_copy(y_hbm.at[pl.ds(d, 1)], buf.at[slot_, k, pl.ds(i, 1)], sem.at[slot_])

    def issue(dref, slot_):
        def body(i, carry):
            for k in range(TOP_K):
                row_copy(dref[k, i], slot_, k, i).start()
            return carry
        lax.fori_loop(0, tc, body, 0)

    @pl.when(s == 0)
    def _():
        issue(dcur_ref, 0)

    @pl.when(s + 1 < ns)
    def _():
        issue(dnxt_ref, 1 - slot)

    def drain(i, carry):
        for k in range(TOP_K):
            row_copy(0, slot, k, i).wait()
        return carry

    lax.fori_loop(0, tc, drain, 0)

    x = x_ref[...]
    gate = gate_ref[...]
    routed = buf[slot, 0] * gate[:, 0:1]
    for k in range(1, TOP_K):
        routed = routed + buf[slot, k] * gate[:, k:k + 1]
    xb = x.astype(BF16)
    shared = _mm(_silu(_mm(xb, wsg_ref[...])) * _mm(xb, wsu_ref[...]), wsd_ref[...])
    out = _layernorm(DEEPNORM_ALPHA * x + (routed + shared), g_ref[...], b_ref[...])

    @pl.when(s < np_tiles)
    def _():
        outp_ref[...] = out

    @pl.when(s >= np_tiles)
    def _():
        outs_ref[...] = out


def _combine(dest, y_sorted, gate, x1, wsg, wsu, wsd, g, b, n_prompt, tc):
    n = x1.shape[0]
    ns = n // tc
    np_tiles = n_prompt // tc
    const = lambda s: pl.BlockSpec(s, lambda i: (0, 0))
    dspec = lambda f: pl.BlockSpec((TOP_K, tc), f, memory_space=pltpu.SMEM)
    return pl.pallas_call(
        functools.partial(_combine_kernel, tc=tc, np_tiles=np_tiles),
        grid=(ns,),
        in_specs=[dspec(lambda i: (0, i)), dspec(lambda i: (0, jnp.minimum(i + 1, ns - 1))),
                  pl.BlockSpec(memory_space=pl.ANY),
                  pl.BlockSpec((tc, TOP_K), lambda i: (i, 0)),
                  pl.BlockSpec((tc, D_MODEL), lambda i: (i, 0)),
                  const((D_MODEL, EXP_HIDDEN)), const((D_MODEL, EXP_HIDDEN)), const((EXP_HIDDEN, D_MODEL)),
                  const((1, D_MODEL)), const((1, D_MODEL))],
        out_specs=(pl.BlockSpec((tc, D_MODEL), lambda i: (jnp.minimum(i, np_tiles - 1), 0)),
                   pl.BlockSpec((tc, D_MODEL), lambda i: (jnp.maximum(i - np_tiles, 0), 0))),
        out_shape=(jax.ShapeDtypeStruct((n_prompt, D_MODEL), F32),
                   jax.ShapeDtypeStruct((n - n_prompt, D_MODEL), F32)),
        scratch_shapes=[pltpu.VMEM((2, TOP_K, tc, D_MODEL), F32), pltpu.SemaphoreType.DMA((2,))],
        compiler_params=_params("arbitrary"),
        name="combine",
    )(dest, dest, y_sorted, gate, x1, wsg, wsu, wsd, g, b)


def _pack_w_in(w_in):
    d = w_in.shape[0]
    o_z = GDN_CONV_CH
    o_a = o_z + GDN_V_W
    o_b = o_a + GDN_HEADS
    o_q = o_b + GDN_HEADS
    o_c = o_q + MLA_HEADS * (MLA_D_NOPE + MLA_D_ROPE)
    o_kr = o_c + MLA_KV_RANK
    zeros = lambda w: jnp.zeros((d, w), w_in.dtype)
    wq = w_in[:, o_q:o_c].reshape(d, MLA_HEADS, MLA_D_NOPE + MLA_D_ROPE)
    q_nope = wq[:, :, :MLA_D_NOPE].reshape(d, MLA_HEADS * MLA_D_NOPE)
    q_rope = jnp.pad(wq[:, :, MLA_D_NOPE:], ((0, 0), (0, 0), (0, LANES - MLA_D_ROPE))).reshape(d, MLA_HEADS * LANES)
    cols = [w_in[:, :o_a], q_nope, q_rope, w_in[:, o_c:o_kr], w_in[:, o_kr:], zeros(LANES - MLA_D_ROPE),
            w_in[:, o_a:o_q], zeros(LANES - 2 * GDN_HEADS)]
    return jnp.concatenate(cols, axis=1).astype(BF16)


def _rope_tables(pos):
    inv_freq = ROPE_THETA ** (-jnp.arange(0, MLA_D_ROPE, 2, dtype=F32) / MLA_D_ROPE)
    ang = pos.astype(F32)[:, None] * inv_freq[None, :]
    cos, sin = jnp.cos(ang), jnp.sin(ang)
    pad = jnp.zeros((pos.shape[0], LANES - MLA_D_ROPE), F32)
    return jnp.concatenate([cos, cos, pad], axis=1), jnp.concatenate([-sin, sin, pad], axis=1)


def _pick(t, pref):
    return pref if t % pref == 0 else t


def _token_mixers(x, pos, conv_hist, s0, past, wts, x1_all, row0):
    b, t, _ = x.shape
    cs_tab, sn_tab = _rope_tables(pos)
    tt = _pick(t, 512)
    qkv, z, gb, q, k, v, latent, k_rope, conv_new = _front(
        x, wts["w_pack"], wts["conv_w"], conv_hist, wts["gpar"], wts["kvnw"], wts["wukv"], cs_tab, sn_tab, tt)
    c = min(CHUNK, t)
    og, s_new = _gdn(qkv, z, gb, s0, wts["gdn_nw"], _pick(t, 4 * CHUNK), c)
    if past is None:
        tb = _pick(t, 1024)
        om = _attn_prompt(q, k, v, tb, _pick(tb, 512))
    else:
        om = _attn_sample(q, k, v, past[0], past[1], wts["wukv"])
    n = b * t
    x1_all = _mixln(og.reshape(n, -1), om.reshape(n, -1), x.reshape(n, D_MODEL), wts["w_out"],
                    wts["ln1_g"], wts["ln1_b"], x1_all, row0, _pick(n, 256))
    return x1_all, latent, k_rope, s_new, conv_new


def _moe(x1_all, n_prompt, wts):
    n = x1_all.shape[0]
    idx, gate, rank, cnt = _router(x1_all, wts["router_w"], wts["router_b"], _pick(n, 256))
    counts = cnt[:, 0].astype(I32)
    padded = (counts + MOE_BM - 1) // MOE_BM * MOE_BM
    pend = jnp.cumsum(padded)
    pstart = pend - padded
    td = _pick(n, 256)
    dest = _dest(idx, rank, pstart.astype(F32).reshape(-1, 1), td)
    n_blocks = n * TOP_K // MOE_BM + N_EXPERTS
    block_start = jnp.arange(n_blocks, dtype=I32) * MOE_BM
    block_e = jnp.minimum(jnp.sum((pend[None, :] <= block_start[:, None]).astype(I32), axis=1), N_EXPERTS - 1)
    nact = (pend[-1:] // MOE_BM).astype(I32)
    tail = jnp.maximum(pend - MOE_BM, 0).astype(I32)

    xs = _dispatch(tail, dest, x1_all, n_blocks * MOE_BM, td)
    y_sorted = _experts(block_e, nact, xs, wts["exp_wg"], wts["exp_wu"], wts["exp_wd"])
    tc = _pick(math.gcd(n_prompt, n - n_prompt), 128)
    return _combine(dest, y_sorted, gate.T, x1_all,
                    wts["sh_wg"], wts["sh_wu"], wts["sh_wd"], wts["ln2_g"], wts["ln2_b"], n_prompt, tc)


def kernel(x_prompt, x_sample, cache_kv_latent, cache_k_rope, state_gdn, state_conv, w_in, gdn_conv_w, gdn_a_log, gdn_dt_bias, gdn_norm_w, mla_kv_norm_w, mla_w_uk, mla_w_uv, w_out, ln1_g, ln1_b, router_w, router_bias, exp_w_gate, exp_w_up, exp_w_down, shared_w_gate, shared_w_up, shared_w_down, ln2_g, ln2_b):
    assert w_in.shape[0] == 1, "single-layer stack"
    b_p, t_p, _ = x_prompt.shape
    b_s, t_s, _ = x_sample.shape
    past = cache_kv_latent.shape[2]
    l = 0
    pad4 = lambda a: jnp.pad(a.astype(F32), (0, LANES - GDN_HEADS))
    wts = {
        "w_pack": _pack_w_in(w_in[l]),
        "conv_w": gdn_conv_w[l],
        "gpar": jnp.stack([pad4(gdn_a_log[l]), pad4(gdn_dt_bias[l])]),
        "kvnw": mla_kv_norm_w[l].reshape(1, -1),
        "wukv": jnp.concatenate([mla_w_uk[l].reshape(MLA_KV_RANK, -1), mla_w_uv[l].reshape(MLA_KV_RANK, -1)],
                                axis=1).astype(BF16),
        "gdn_nw": gdn_norm_w[l].reshape(1, -1),
        "w_out": w_out[l].astype(BF16),
        "ln1_g": ln1_g[l].reshape(1, -1), "ln1_b": ln1_b[l].reshape(1, -1),
        "router_w": router_w[l].astype(BF16), "router_b": router_bias[l].reshape(-1, 1),
        "exp_wg": exp_w_gate[l], "exp_wu": exp_w_up[l], "exp_wd": exp_w_down[l],
        "sh_wg": shared_w_gate[l].astype(BF16), "sh_wu": shared_w_up[l].astype(BF16),
        "sh_wd": shared_w_down[l].astype(BF16),
        "ln2_g": ln2_g[l].reshape(1, -1), "ln2_b": ln2_b[l].reshape(1, -1),
    }
    n_p, n_s = b_p * t_p, b_s * t_s
    x1_all = jnp.zeros((n_p + n_s, D_MODEL), F32)
    conv0 = jnp.zeros((b_p, GDN_CONV - 1, GDN_CONV_CH), F32)
    s0 = jnp.zeros((b_p, GDN_HEADS, GDN_DK, GDN_DV), F32)
    x1_all, lat_p, kr_p, sg_p, cv_p = _token_mixers(x_prompt, jnp.arange(t_p), conv0, s0, None, wts, x1_all, 0)
    x1_all, lat_s, kr_s, sg_s, cv_s = _token_mixers(
        x_sample, past + jnp.arange(t_s), state_conv[l], state_gdn[l],
        (cache_kv_latent[l], cache_k_rope[l]), wts, x1_all, n_p)
    y_p, y_s = _moe(x1_all, n_p, wts)
    return (y_p.reshape(b_p, t_p, D_MODEL), y_s.reshape(b_s, t_s, D_MODEL),
            lat_p[None], kr_p[None], sg_p[None], cv_p[None],
            lat_s[None], kr_s[None], sg_s[None], cv_s[None])
```

```python
import functools
import math

import jax
import jax.numpy as jnp
from jax import lax
from jax.experimental import pallas as pl
from jax.experimental.pallas import tpu as pltpu

F32 = jnp.float32
BF16 = jnp.bfloat16
I32 = jnp.int32

D_MODEL = 1024
CHUNK = 64
GDN_HEADS = 4
GDN_DK = 128
GDN_DV = 128
GDN_CONV = 4
GDN_QK_W = GDN_HEADS * GDN_DK
GDN_V_W = GDN_HEADS * GDN_DV
GDN_CONV_CH = 2 * GDN_QK_W + GDN_V_W
MLA_HEADS = 4
MLA_D_NOPE = 128
MLA_D_ROPE = 64
MLA_D_V = 128
MLA_KV_RANK = 256
MLA_SCALE = (MLA_D_NOPE + MLA_D_ROPE) ** -0.5
ROPE_THETA = 10000.0
N_EXPERTS = 256
N_GROUPS = 8
GROUP_SIZE = N_EXPERTS // N_GROUPS
TOPK_GROUPS = 4
TOP_K = 8
EXP_HIDDEN = 256
ROUTED_SCALE = 2.5
DEPTH = 1
DEEPNORM_ALPHA = (2.0 * DEPTH) ** 0.25
LN_EPS = 1e-5
RMS_EPS = 1e-6
L2_EPS = 1e-6

LANES = 128
PK_QKV = 0
PK_Z = PK_QKV + GDN_CONV_CH
PK_QNOPE = PK_Z + GDN_V_W
PK_QROPE = PK_QNOPE + MLA_HEADS * MLA_D_NOPE
PK_CKV = PK_QROPE + MLA_HEADS * LANES
PK_KROPE = PK_CKV + MLA_KV_RANK
PK_AB = PK_KROPE + LANES
PK_W = PK_AB + LANES
MLA_QK_W = 2 * LANES

MOE_BM = 256
VMEM_LIMIT = 56 * 1024 * 1024


def _mm(a, b):
    return jnp.dot(a.astype(BF16), b.astype(BF16), preferred_element_type=F32)


def _mm_nt(a, b):
    return lax.dot_general(a.astype(BF16), b.astype(BF16), (((1,), (1,)), ((), ())),
                           preferred_element_type=F32)


def _mm_tn(a, b):
    return lax.dot_general(a.astype(BF16), b.astype(BF16), (((0,), (0,)), ((), ())),
                           preferred_element_type=F32)


def _split3(x):
    hi = x.astype(BF16)
    r = x - hi.astype(F32)
    mid = r.astype(BF16)
    lo = (r - mid.astype(F32)).astype(BF16)
    return hi, mid, lo


def _sigmoid(x):
    return 1.0 / (1.0 + jnp.exp(-x))


def _silu(x):
    return x * _sigmoid(x)


def _softplus(x):
    return jnp.maximum(x, 0.0) + jnp.log1p(jnp.exp(-jnp.abs(x)))


def _rope(x, cs, sn):
    w = x.shape[-1]
    n = w // LANES
    if n > 1:
        cs = jnp.concatenate([cs] * n, axis=1)
        sn = jnp.concatenate([sn] * n, axis=1)
    lane = lax.broadcasted_iota(I32, x.shape, 1) & (LANES - 1)
    half = MLA_D_ROPE // 2
    swapped = jnp.where(lane < half, pltpu.roll(x, w - half, 1), pltpu.roll(x, half, 1))
    return x * cs + swapped * sn


def _params(*sem):
    return pltpu.CompilerParams(dimension_semantics=sem, vmem_limit_bytes=VMEM_LIMIT)


def _front_kernel(x_ref, w_ref, convw_ref, hist_ref, gpar_ref, kvnw_ref, wukv_ref, cs_ref, sn_ref,
                  qkv_ref, z_ref, gb_ref, q_ref, k_ref, v_ref, lat_ref, kr_ref, convnew_ref,
                  xp_scr, *, tt, c):
    t = pl.program_id(1)
    hrow = 8 - (GDN_CONV - 1)

    @pl.when(t == 0)
    def _():
        xp_scr[hrow:8, :] = hist_ref[0]

    proj = _mm(x_ref[0], w_ref[...])

    raw = proj[:, PK_QKV:PK_Z]
    xp_scr[8:8 + tt, :] = raw
    cw = convw_ref[...]
    y = raw * cw[GDN_CONV - 1:GDN_CONV]
    for i in range(GDN_CONV - 1):
        y = y + xp_scr[hrow + i:hrow + i + tt, :] * cw[i:i + 1]
    tail = xp_scr[tt + hrow:tt + 8, :]
    convnew_ref[0] = tail
    xp_scr[hrow:8, :] = tail
    qkv = _silu(y)
    for h in range(2 * GDN_HEADS):
        xh = qkv[:, h * GDN_DK:(h + 1) * GDN_DK]
        xh = xh * lax.rsqrt(jnp.sum(xh * xh, axis=-1, keepdims=True) + L2_EPS)
        if h < GDN_HEADS:
            xh = xh * GDN_DK ** -0.5
        qkv_ref[0, :, h * GDN_DK:(h + 1) * GDN_DK] = xh
    qkv_ref[0, :, 2 * GDN_QK_W:] = qkv[:, 2 * GDN_QK_W:]
    z_ref[0] = proj[:, PK_Z:PK_QNOPE]

    ab = proj[:, PK_AB:PK_W]
    gpar = gpar_ref[...]
    gc = -jnp.exp(gpar[0:1]) * _softplus(ab + gpar[1:2])
    pos = lax.broadcasted_iota(I32, ab.shape, 0) & (c - 1)
    step = 1
    while step < c:
        gc = gc + jnp.where(pos >= step, pltpu.roll(gc, step, 0), 0.0)
        step *= 2
    beta = _sigmoid(ab)
    lane = lax.broadcasted_iota(I32, ab.shape, 1)
    gb_ref[0] = jnp.where(lane < GDN_HEADS, gc, jnp.where(lane < 2 * GDN_HEADS, beta, 0.0))

    cs = cs_ref[...]
    sn = sn_ref[...]
    q_nope = proj[:, PK_QNOPE:PK_QROPE]
    q_rope = _rope(proj[:, PK_QROPE:PK_CKV], cs, sn)
    c_raw = proj[:, PK_CKV:PK_KROPE]
    latent = c_raw * lax.rsqrt(jnp.mean(c_raw * c_raw, axis=-1, keepdims=True) + RMS_EPS) * kvnw_ref[...]
    lat_ref[0] = latent
    k_rope = _rope(proj[:, PK_KROPE:PK_AB], cs, sn)
    kr_ref[0] = k_rope[:, :MLA_D_ROPE]
    kv = _mm(latent, wukv_ref[...])
    k_rope_b = k_rope.astype(BF16)
    for h in range(MLA_HEADS):
        q_ref[0, h, :, :LANES] = q_nope[:, h * LANES:(h + 1) * LANES].astype(BF16)
        q_ref[0, h, :, LANES:] = q_rope[:, h * LANES:(h + 1) * LANES].astype(BF16)
        k_ref[0, h, :, :LANES] = kv[:, h * LANES:(h + 1) * LANES].astype(BF16)
        k_ref[0, h, :, LANES:] = k_rope_b
        v_ref[0, h] = kv[:, (MLA_HEADS + h) * LANES:(MLA_HEADS + h + 1) * LANES].astype(BF16)


def _front(x, w_pack, conv_w, hist, gpar, kvnw, wukv, cs_tab, sn_tab, tt, c):
    b, t, _ = x.shape
    nt = t // tt
    const2 = lambda bi, ti: (0, 0)
    out_shape = (
        jax.ShapeDtypeStruct((b, t, GDN_CONV_CH), F32),
        jax.ShapeDtypeStruct((b, t, GDN_V_W), F32),
        jax.ShapeDtypeStruct((b, t, LANES), F32),
        jax.ShapeDtypeStruct((b, MLA_HEADS, t, MLA_QK_W), BF16),
        jax.ShapeDtypeStruct((b, MLA_HEADS, t, MLA_QK_W), BF16),
        jax.ShapeDtypeStruct((b, MLA_HEADS, t, MLA_D_V), BF16),
        jax.ShapeDtypeStruct((b, t, MLA_KV_RANK), F32),
        jax.ShapeDtypeStruct((b, t, MLA_D_ROPE), F32),
        jax.ShapeDtypeStruct((b, GDN_CONV - 1, GDN_CONV_CH), F32),
    )
    row3 = lambda w: pl.BlockSpec((1, tt, w), lambda bi, ti: (bi, ti, 0))
    head4 = lambda w: pl.BlockSpec((1, MLA_HEADS, tt, w), lambda bi, ti: (bi, 0, ti, 0))
    return pl.pallas_call(
        functools.partial(_front_kernel, tt=tt, c=c),
        grid=(b, nt),
        in_specs=[
            row3(D_MODEL),
            pl.BlockSpec((D_MODEL, PK_W), const2),
            pl.BlockSpec((GDN_CONV, GDN_CONV_CH), const2),
            pl.BlockSpec((1, GDN_CONV - 1, GDN_CONV_CH), lambda bi, ti: (bi, 0, 0)),
            pl.BlockSpec((2, LANES), const2),
            pl.BlockSpec((1, MLA_KV_RANK), const2),
            pl.BlockSpec((MLA_KV_RANK, 2 * MLA_HEADS * LANES), const2),
            pl.BlockSpec((tt, LANES), lambda bi, ti: (ti, 0)),
            pl.BlockSpec((tt, LANES), lambda bi, ti: (ti, 0)),
        ],
        out_specs=(
            row3(GDN_CONV_CH), row3(GDN_V_W), row3(LANES),
            head4(MLA_QK_W), head4(MLA_QK_W), head4(MLA_D_V),
            row3(MLA_KV_RANK), row3(MLA_D_ROPE),
            pl.BlockSpec((1, GDN_CONV - 1, GDN_CONV_CH), lambda bi, ti: (bi, 0, 0)),
        ),
        out_shape=out_shape,
        scratch_shapes=[pltpu.VMEM((tt + 8, GDN_CONV_CH), F32)],
        compiler_params=_params("arbitrary", "arbitrary"),
        name="front",
    )(x, w_pack, conv_w, hist, gpar, kvnw, wukv, cs_tab, sn_tab)


def _gdn_kernel(qkv_ref, z_ref, gb_ref, s0_ref, nw_ref, og_ref, sout_ref, s_scr, *, tg, c):
    t = pl.program_id(1)
    nh = GDN_HEADS
    r = nh * c
    sh = c.bit_length() - 1

    @pl.when(t == 0)
    def _():
        s_scr[...] = s0_ref[0]

    row = lax.broadcasted_iota(I32, (r, r), 0)
    col = lax.broadcasted_iota(I32, (r, r), 1)
    same = (row >> sh) == (col >> sh)
    incl = same & (row >= col)
    strict = same & (row > col)
    eye = jnp.where(row == col, 1.0, 0.0)
    lane0 = jnp.where(lax.broadcasted_iota(I32, (r, LANES), 1) == 0, 1.0, 0.0).astype(BF16)
    nw = nw_ref[...]
    chunks = range(tg // c)

    def stacked(ref, ci, base):
        return jnp.concatenate(
            [ref[0, ci * c:(ci + 1) * c, base + h * LANES:base + (h + 1) * LANES] for h in range(nh)], axis=0)

    def col_bcast(ci, lane):
        gbc = gb_ref[0, ci * c:(ci + 1) * c, :]
        return jnp.concatenate(
            [jnp.broadcast_to(gbc[:, lane + h:lane + h + 1], (c, LANES)) for h in range(nh)], axis=0)

    def as_col(gc_b):
        return _lane_tile(gc_b, r) if r % LANES == 0 else gc_b[:, :r]

    def as_row(gc_b):
        if r % LANES == 0:
            return as_col(gc_b).T
        return sum(lax.dot_general(lane0, p, (((1,), (1,)), ((), ())), preferred_element_type=F32)
                   for p in _split3(gc_b))

    ks = [stacked(qkv_ref, ci, GDN_QK_W) for ci in chunks]
    gc = [col_bcast(ci, 0) for ci in chunks]
    beta = [col_bcast(ci, nh) for ci in chunks]
    decay, qk_kk = [], []
    for ci in chunks:
        decay.append(jnp.exp(jnp.where(incl, as_col(gc[ci]) - as_row(gc[ci]), -jnp.inf)))
        qk_kk.append(_mm_nt(jnp.concatenate([stacked(qkv_ref, ci, 0), ks[ci]], axis=0), ks[ci]))
    intra = [qk_kk[ci][:r] * decay[ci] for ci in chunks]
    n_pow = [jnp.where(strict, -as_col(beta[ci]) * qk_kk[ci][r:] * decay[ci], 0.0)
             for ci in chunks]
    t_inv = [eye + n_pow[ci] for ci in chunks]
    for _ in range(sh - 1):
        n_pow = [_mm(n_pow[ci], n_pow[ci]) for ci in chunks]
        t_inv = [t_inv[ci] + _mm(t_inv[ci], n_pow[ci]) for ci in chunks]
    egc = [jnp.exp(gc[ci]) for ci in chunks]
    uw = [_mm(t_inv[ci], jnp.concatenate([stacked(qkv_ref, ci, 2 * GDN_QK_W) * beta[ci],
                                          ks[ci] * beta[ci] * egc[ci]], axis=1)) for ci in chunks]

    for ci in chunks:
        u = uw[ci][:, :GDN_DV]
        w = uw[ci][:, GDN_DV:]
        qd = stacked(qkv_ref, ci, 0) * egc[ci]
        vn, qs_s = [], []
        for h in range(nh):
            hs = slice(h * c, (h + 1) * c)
            s_h = s_scr[h]
            wq = _mm(jnp.concatenate([w[hs], qd[hs]], axis=0), s_h)
            vn_h = u[hs] - wq[:c]
            g_last = gc[ci][h * c + c - 1:h * c + c, :]
            kd = ks[ci][hs] * jnp.exp(g_last - gc[ci][hs])
            s_scr[h] = s_h * jnp.exp(g_last) + _mm_tn(kd, vn_h)
            vn.append(vn_h)
            qs_s.append(wq[c:])
        o = jnp.concatenate(qs_s, axis=0) + _mm(intra[ci], jnp.concatenate(vn, axis=0))

        o = o * lax.rsqrt(jnp.mean(o * o, axis=-1, keepdims=True) + RMS_EPS) * nw
        o = o * _silu(stacked(z_ref, ci, 0))
        for h in range(nh):
            og_ref[0, ci * c:(ci + 1) * c, h * LANES:(h + 1) * LANES] = o[h * c:(h + 1) * c].astype(BF16)

    @pl.when(t == pl.num_programs(1) - 1)
    def _():
        sout_ref[0] = s_scr[...]


def _gdn(qkv, z, gb, s0, nw, tg, c):
    b, t, _ = qkv.shape
    row3 = lambda w: pl.BlockSpec((1, tg, w), lambda bi, ti: (bi, ti, 0))
    st = pl.BlockSpec((1, GDN_HEADS, GDN_DK, GDN_DV), lambda bi, ti: (bi, 0, 0, 0))
    return pl.pallas_call(
        functools.partial(_gdn_kernel, tg=tg, c=c),
        grid=(b, t // tg),
        in_specs=[row3(GDN_CONV_CH), row3(GDN_V_W), row3(LANES), st,
                  pl.BlockSpec((1, GDN_DV), lambda bi, ti: (0, 0))],
        out_specs=(row3(GDN_V_W), st),
        out_shape=(jax.ShapeDtypeStruct((b, t, GDN_V_W), BF16),
                   jax.ShapeDtypeStruct((b, GDN_HEADS, GDN_DK, GDN_DV), F32)),
        scratch_shapes=[pltpu.VMEM((GDN_HEADS, GDN_DK, GDN_DV), F32)],
        compiler_params=_params("arbitrary", "arbitrary"),
        name="gdn",
    )(qkv, z, gb, s0, nw)


ATTN_ROW_BLOCK = 32


def _lane_tile(x, width):
    return x if width == LANES else jnp.concatenate([x] * (width // LANES), axis=1)


def _attn_kernel(qi_ref, ki_ref, q_ref, k_ref, v_ref, o_ref, m_scr, l_scr, acc_scr, a_scr, s_scr, p_scr,
                 *, tb, sub):
    step = pl.program_id(2)
    qi = qi_ref[step]
    ki = ki_ref[step]
    csh = CHUNK.bit_length() - 1
    c2 = MLA_SCALE * math.log2(math.e)

    @pl.when(ki == 0)
    def _():
        m_scr[...] = jnp.full(m_scr.shape, -jnp.inf, F32)
        l_scr[...] = jnp.zeros(l_scr.shape, F32)
        acc_scr[...] = jnp.zeros(acc_scr.shape, F32)

    def update(r0, j, masked):
        rows = slice(r0, tb)
        keys = slice(j * sub, (j + 1) * sub)
        s_scr[rows, :] = lax.dot_general(q_ref[0, 0, rows, :], k_ref[0, 0, keys, :], (((1,), (1,)), ((), ())),
                                         preferred_element_type=F32)
        rb = min(ATTN_ROW_BLOCK, tb - r0)

        def block(i):
            rr = pl.ds(pl.multiple_of(r0 + i * rb, rb), rb)
            s = s_scr[rr, :]
            if masked:
                qc = (r0 + i * rb + lax.broadcasted_iota(I32, s.shape, 0)) >> csh
                kc = (j * sub + lax.broadcasted_iota(I32, s.shape, 1)) >> csh
                s = jnp.where(kc <= qc, s, -jnp.inf)
            return rr, s

        def row_max(i, carry):
            rr, s = block(i)
            m_prev = m_scr[rr, :]
            m_new = jnp.maximum(m_prev, jnp.max(s, axis=-1, keepdims=True))
            a_scr[rr, :] = jnp.exp2((m_prev - m_new) * c2)
            m_scr[rr, :] = m_new
            return carry

        def row_exp(i, carry):
            rr, s = block(i)
            p = jnp.exp2((s - _lane_tile(m_scr[rr, :], sub)) * c2)
            l_scr[rr, :] = a_scr[rr, :] * l_scr[rr, :] + jnp.sum(p, axis=-1, keepdims=True)
            p_scr[rr, :] = p.astype(BF16)
            return carry

        for i in range((tb - r0) // rb):
            row_max(i, 0)
        for i in range((tb - r0) // rb):
            row_exp(i, 0)
        acc_scr[rows, :] = a_scr[rows, :] * acc_scr[rows, :] + jnp.dot(
            p_scr[rows, :], v_ref[0, 0, keys, :], preferred_element_type=F32)

    @pl.when(ki < qi)
    def _():
        for j in range(tb // sub):
            update(0, j, False)

    @pl.when(ki == qi)
    def _():
        for j in range(tb // sub):
            update(j * sub, j, True)
        o_ref[0] = (acc_scr[...] / l_scr[...]).astype(BF16)


def _attn_prompt(q, k, v, tb, sub):
    b, nh, t, _ = q.shape
    nt = t // tb
    pairs = [(qi, ki) for qi in range(nt) for ki in range(qi + 1)]
    qi_of = jnp.asarray([p[0] for p in pairs], I32)
    ki_of = jnp.asarray([p[1] for p in pairs], I32)
    return pl.pallas_call(
        functools.partial(_attn_kernel, tb=tb, sub=sub),
        grid_spec=pltpu.PrefetchScalarGridSpec(
            num_scalar_prefetch=2,
            grid=(b, nh, len(pairs)),
            in_specs=[pl.BlockSpec((1, 1, tb, MLA_QK_W), lambda bi, hi, s, qo, ko: (bi, hi, qo[s], 0)),
                      pl.BlockSpec((1, 1, tb, MLA_QK_W), lambda bi, hi, s, qo, ko: (bi, hi, ko[s], 0)),
                      pl.BlockSpec((1, 1, tb, MLA_D_V), lambda bi, hi, s, qo, ko: (bi, hi, ko[s], 0))],
            out_specs=pl.BlockSpec((1, tb, MLA_D_V), lambda bi, hi, s, qo, ko: (bi, qo[s], hi)),
            scratch_shapes=[pltpu.VMEM((tb, LANES), F32), pltpu.VMEM((tb, LANES), F32),
                            pltpu.VMEM((tb, MLA_D_V), F32), pltpu.VMEM((tb, LANES), F32),
                            pltpu.VMEM((tb, sub), F32), pltpu.VMEM((tb, sub), BF16)]),
        out_shape=jax.ShapeDtypeStruct((b, t, MLA_HEADS * MLA_D_V), BF16),
        compiler_params=_params("arbitrary", "arbitrary", "arbitrary"),
        name="attn_prompt",
    )(qi_of, ki_of, q, k, v)


def _attn_sample_kernel(q_ref, kn_ref, vn_ref, plat_ref, pkr_ref, wukv_ref, o_ref):
    kvp = _mm(plat_ref[0], wukv_ref[...])
    pkr = pkr_ref[0].astype(BF16)
    for h in range(MLA_HEADS):
        q = q_ref[0, h]
        s_past = (_mm_nt(q[:, :MLA_D_NOPE], kvp[:, h * LANES:(h + 1) * LANES])
                  + _mm_nt(q[:, MLA_D_NOPE:MLA_D_NOPE + MLA_D_ROPE], pkr)) * MLA_SCALE
        s_new = _mm_nt(q, kn_ref[0, h]) * MLA_SCALE
        m = jnp.maximum(jnp.max(s_past, axis=-1, keepdims=True), jnp.max(s_new, axis=-1, keepdims=True))
        p_past = jnp.exp(s_past - m)
        p_new = jnp.exp(s_new - m)
        l = jnp.sum(p_past, axis=-1, keepdims=True) + jnp.sum(p_new, axis=-1, keepdims=True)
        o = _mm(p_past, kvp[:, (MLA_HEADS + h) * LANES:(MLA_HEADS + h + 1) * LANES]) + _mm(p_new, vn_ref[0, h])
        o_ref[0, :, h * MLA_D_V:(h + 1) * MLA_D_V] = (o / l).astype(BF16)


def _attn_sample(q, k_new, v_new, past_lat, past_kr, wukv):
    b, nh, ts, _ = q.shape
    past = past_lat.shape[1]
    b4 = lambda w: pl.BlockSpec((1, nh, ts, w), lambda bi: (bi, 0, 0, 0))
    return pl.pallas_call(
        _attn_sample_kernel,
        grid=(b,),
        in_specs=[b4(MLA_QK_W), b4(MLA_QK_W), b4(MLA_D_V),
                  pl.BlockSpec((1, past, MLA_KV_RANK), lambda bi: (bi, 0, 0)),
                  pl.BlockSpec((1, past, MLA_D_ROPE), lambda bi: (bi, 0, 0)),
                  pl.BlockSpec((MLA_KV_RANK, 2 * MLA_HEADS * LANES), lambda bi: (0, 0))],
        out_specs=pl.BlockSpec((1, ts, MLA_HEADS * MLA_D_V), lambda bi: (bi, 0, 0)),
        out_shape=jax.ShapeDtypeStruct((b, ts, MLA_HEADS * MLA_D_V), BF16),
        compiler_params=_params("arbitrary"),
        name="attn_sample",
    )(q, k_new, v_new, past_lat, past_kr, wukv)


def _layernorm(y, g, b):
    mu = jnp.mean(y, axis=-1, keepdims=True)
    d = y - mu
    var = jnp.mean(d * d, axis=-1, keepdims=True)
    return d * lax.rsqrt(var + LN_EPS) * g + b


def _mixln_kernel(og_ref, om_ref, x_ref, w_ref, g_ref, b_ref, prev_ref, o_ref):
    del prev_ref
    mix = (jnp.dot(og_ref[...], w_ref[:GDN_V_W, :], preferred_element_type=F32)
           + jnp.dot(om_ref[...], w_ref[GDN_V_W:, :], preferred_element_type=F32))
    o_ref[...] = _layernorm(DEEPNORM_ALPHA * x_ref[...] + mix, g_ref[...], b_ref[...])


def _mixln(og, om, x, w_out, g, b, x1_all, row0, tm):
    n = x.shape[0]
    blk0 = row0 // tm
    rows = lambda w: pl.BlockSpec((tm, w), lambda i: (i, 0))
    const = lambda s: pl.BlockSpec(s, lambda i: (0, 0))
    return pl.pallas_call(
        _mixln_kernel,
        grid=(n // tm,),
        in_specs=[rows(GDN_V_W), rows(MLA_HEADS * MLA_D_V), rows(D_MODEL),
                  const((D_MODEL, D_MODEL)), const((1, D_MODEL)), const((1, D_MODEL)),
                  pl.BlockSpec(memory_space=pl.ANY)],
        out_specs=pl.BlockSpec((tm, D_MODEL), lambda i: (blk0 + i, 0)),
        out_shape=jax.ShapeDtypeStruct(x1_all.shape, F32),
        input_output_aliases={6: 0},
        compiler_params=_params("arbitrary"),
        name="mixln",
    )(og, om, x, w_out, g, b, x1_all)


def _router_kernel(x_ref, rw_ref, rb_ref, idx_ref, gate_ref, rank_ref, cnt_ref, carry_scr, *, tt):
    s = pl.program_id(0)

    @pl.when(s == 0)
    def _():
        carry_scr[...] = jnp.zeros(carry_scr.shape, F32)

    ninf = -jnp.inf
    big = float(2 * N_EXPERTS)
    scores = _sigmoid(_mm(x_ref[...], rw_ref[...]).T)
    biased = scores + rb_ref[...]
    eio = lax.broadcasted_iota(I32, (N_EXPERTS, tt), 0).astype(F32)

    def first_argmax(vals, io):
        m = jnp.max(vals, axis=0, keepdims=True)
        i = jnp.min(jnp.where(vals == m, io, big), axis=0, keepdims=True)
        return m, i

    gs = []
    for g in range(N_GROUPS):
        blk = biased[g * GROUP_SIZE:(g + 1) * GROUP_SIZE]
        io = (lax.broadcasted_iota(I32, (GROUP_SIZE, tt), 0) + g * GROUP_SIZE).astype(F32)
        m1, i1 = first_argmax(blk, io)
        m2 = jnp.max(jnp.where(io == i1, ninf, blk), axis=0, keepdims=True)
        gs.append(m1 + m2)
    gio = lax.broadcasted_iota(I32, (N_GROUPS, tt), 0).astype(F32)
    gsc = jnp.zeros((N_GROUPS, tt), F32)
    for g in range(N_GROUPS):
        gsc = jnp.where(gio == float(g), gs[g], gsc)
    gsel = jnp.zeros((N_GROUPS, tt), F32)
    for _ in range(TOPK_GROUPS):
        _, gi = first_argmax(gsc, gio)
        hit = gio == gi
        gsel = jnp.where(hit, 1.0, gsel)
        gsc = jnp.where(hit, ninf, gsc)
    masked = jnp.concatenate(
        [jnp.where(jnp.max(jnp.where(gio == float(g), gsel, 0.0), axis=0, keepdims=True) > 0.0,
                   biased[g * GROUP_SIZE:(g + 1) * GROUP_SIZE], ninf) for g in range(N_GROUPS)], axis=0)

    idx, wts = [], []
    sel = jnp.zeros((N_EXPERTS, tt), F32)
    for _ in range(TOP_K):
        _, ei = first_argmax(masked, eio)
        hit = eio == ei
        wts.append(jnp.sum(jnp.where(hit, scores, 0.0), axis=0, keepdims=True))
        masked = jnp.where(hit, ninf, masked)
        sel = jnp.where(hit, 1.0, sel)
        idx.append(ei)
    wsum = wts[0]
    for w in wts[1:]:
        wsum = wsum + w

    t0 = lax.broadcasted_iota(I32, (tt, tt), 0)
    t1 = lax.broadcasted_iota(I32, (tt, tt), 1)
    before = jnp.where(t0 < t1, 1.0, 0.0).astype(BF16)
    sel_b = sel.astype(BF16)
    base = carry_scr[:, :1] + jnp.dot(sel_b, before, preferred_element_type=F32)
    ranks = [jnp.sum(jnp.where(eio == ei, base, 0.0), axis=0, keepdims=True) for ei in idx]
    carry_scr[...] = carry_scr[...] + jnp.dot(sel_b, jnp.ones((tt, LANES), BF16), preferred_element_type=F32)

    for k in range(TOP_K):
        idx_ref[k:k + 1, :] = idx[k].astype(I32)
        gate_ref[k:k + 1, :] = wts[k] / wsum * ROUTED_SCALE
        rank_ref[k:k + 1, :] = ranks[k].astype(I32)
    cnt_ref[...] = carry_scr[...]


def _router(x1, rw, rb, tt):
    n = x1.shape[0]
    kt = lambda dt: jax.ShapeDtypeStruct((TOP_K, n), dt)
    kspec = pl.BlockSpec((TOP_K, tt), lambda i: (0, i))
    return pl.pallas_call(
        functools.partial(_router_kernel, tt=tt),
        grid=(n // tt,),
        in_specs=[pl.BlockSpec((tt, D_MODEL), lambda i: (i, 0)),
                  pl.BlockSpec((D_MODEL, N_EXPERTS), lambda i: (0, 0)),
                  pl.BlockSpec((N_EXPERTS, 1), lambda i: (0, 0))],
        out_specs=(kspec, kspec, kspec, pl.BlockSpec((N_EXPERTS, LANES), lambda i: (0, 0))),
        out_shape=(kt(I32), kt(F32), kt(I32), jax.ShapeDtypeStruct((N_EXPERTS, LANES), F32)),
        scratch_shapes=[pltpu.VMEM((N_EXPERTS, LANES), F32)],
        compiler_params=_params("arbitrary"),
        name="router",
    )(x1, rw, rb)


def _dest_kernel(idx_ref, rank_ref, pstart_ref, dest_ref, *, tt):
    eio = lax.broadcasted_iota(I32, (N_EXPERTS, tt), 0)
    pstart = pstart_ref[...]
    for k in range(TOP_K):
        start = jnp.sum(jnp.where(eio == idx_ref[k:k + 1, :], pstart, 0.0), axis=0, keepdims=True)
        dest_ref[k:k + 1, :] = start.astype(I32) + rank_ref[k:k + 1, :]


def _dest(idx, rank, pstart, tt):
    n = idx.shape[1]
    kspec = pl.BlockSpec((TOP_K, tt), lambda i: (0, i))
    return pl.pallas_call(
        functools.partial(_dest_kernel, tt=tt),
        grid=(n // tt,),
        in_specs=[kspec, kspec, pl.BlockSpec((N_EXPERTS, 1), lambda i: (0, 0))],
        out_specs=kspec,
        out_shape=jax.ShapeDtypeStruct((TOP_K, n), I32),
        compiler_params=_params("arbitrary"),
        name="dest",
    )(idx, rank, pstart)


def _dispatch_kernel(tail_ref, dest_ref, x_ref, xs_out, zbuf, sem, zsem, *, td):
    s = pl.program_id(0)

    @pl.when(s == 0)
    def _():
        zbuf[...] = jnp.zeros(zbuf.shape, F32)

        def zero_copy(e):
            return pltpu.make_async_copy(zbuf, xs_out.at[pl.ds(pl.multiple_of(tail_ref[e], MOE_BM), MOE_BM)], zsem)

        def zstart(e, carry):
            zero_copy(e).start()
            return carry

        def zwait(e, carry):
            zero_copy(e).wait()
            return carry

        lax.fori_loop(0, N_EXPERTS, zstart, 0)
        lax.fori_loop(0, N_EXPERTS, zwait, 0)

    def row_copy(i, d):
        return pltpu.make_async_copy(x_ref.at[pl.ds(i, 1)], xs_out.at[pl.ds(d, 1)], sem)

    def issue(i, carry):
        for k in range(TOP_K):
            row_copy(i, dest_ref[k, i]).start(priority=k % 2)
        return carry

    lax.fori_loop(0, td, issue, 0, unroll=4)

    def drain(i, carry):
        for k in range(TOP_K):
            row_copy(0, 0).wait()
        return carry

    lax.fori_loop(0, td, drain, 0)


def _dispatch(tail, dest, x1, n_rows, td):
    n = x1.shape[0]
    return pl.pallas_call(
        functools.partial(_dispatch_kernel, td=td),
        grid_spec=pltpu.PrefetchScalarGridSpec(
            num_scalar_prefetch=1,
            grid=(n // td,),
            in_specs=[pl.BlockSpec((TOP_K, td), lambda i, tl: (0, i), memory_space=pltpu.SMEM),
                      pl.BlockSpec((td, D_MODEL), lambda i, tl: (i, 0))],
            out_specs=pl.BlockSpec(memory_space=pl.ANY),
            scratch_shapes=[pltpu.VMEM((MOE_BM, D_MODEL), F32), pltpu.SemaphoreType.DMA(()),
                            pltpu.SemaphoreType.DMA(())]),
        out_shape=jax.ShapeDtypeStruct((n_rows, D_MODEL), F32),
        compiler_params=_params("arbitrary"),
        name="dispatch",
    )(tail, dest, x1)


def _expert_kernel(be_ref, nact_ref, xs_ref, wg_ref, wu_ref, wd_ref, y_ref):
    del be_ref

    @pl.when(pl.program_id(0) < nact_ref[0])
    def _():
        xb = xs_ref[...].astype(BF16)
        hg = _mm(xb, wg_ref[0])
        hu = _mm(xb, wu_ref[0])
        y_ref[...] = _mm(_silu(hg) * hu, wd_ref[0])


def _experts(block_e, nact, xs, wg, wu, wd):
    n_rows = xs.shape[0]
    nb = n_rows // MOE_BM
    act = lambda b, na: jnp.minimum(b, na[0] - 1)
    rows = pl.BlockSpec((MOE_BM, D_MODEL), lambda b, be, na: (act(b, na), 0))
    wspec = lambda s: pl.BlockSpec((1,) + s, lambda b, be, na: (be[act(b, na)], 0, 0))
    return pl.pallas_call(
        _expert_kernel,
        grid_spec=pltpu.PrefetchScalarGridSpec(
            num_scalar_prefetch=2,
            grid=(nb,),
            in_specs=[rows, wspec((D_MODEL, EXP_HIDDEN)), wspec((D_MODEL, EXP_HIDDEN)),
                      wspec((EXP_HIDDEN, D_MODEL))],
            out_specs=rows),
        out_shape=jax.ShapeDtypeStruct((n_rows, D_MODEL), F32),
        compiler_params=_params("arbitrary"),
        name="experts",
    )(block_e, nact, xs, wg, wu, wd)


def _combine_kernel(dcur_ref, dnxt_ref, y_hbm, gate_ref, x_ref, wsg_ref, wsu_ref, wsd_ref, g_ref, b_ref,
                    outp_ref, outs_ref, buf, sem, *, tc, np_tiles):
    s = pl.program_id(0)
    ns = pl.num_programs(0)
    slot = s % 2

    def row_copy(d, slot_, k, i):
        return pltpu.make_async_copy(y_hbm.at[pl.ds(d, 1)], buf.at[slot_, k, pl.ds(i, 1)], sem.at[slot_])

    def issue(dref, slot_):
        def body(i, carry):
            for k in range(TOP_K):
                row_copy(dref[k, i], slot_, k, i).start(priority=k % 2)
            return carry
        lax.fori_loop(0, tc, body, 0, unroll=4)

    @pl.when(s == 0)
    def _():
        issue(dcur_ref, 0)

    @pl.when(s + 1 < ns)
    def _():
        issue(dnxt_ref, 1 - slot)

    def drain(i, carry):
        for k in range(TOP_K):
            row_copy(0, slot, k, i).wait()
        return carry

    lax.fori_loop(0, tc, drain, 0)

    x = x_ref[...]
    gate = gate_ref[...]
    routed = buf[slot, 0] * gate[:, 0:1]
    for k in range(1, TOP_K):
        routed = routed + buf[slot, k] * gate[:, k:k + 1]
    xb = x.astype(BF16)
    shared = _mm(_silu(_mm(xb, wsg_ref[...])) * _mm(xb, wsu_ref[...]), wsd_ref[...])
    out = _layernorm(DEEPNORM_ALPHA * x + (routed + shared), g_ref[...], b_ref[...])

    @pl.when(s < np_tiles)
    def _():
        outp_ref[...] = out

    @pl.when(s >= np_tiles)
    def _():
        outs_ref[...] = out


def _combine(dest, y_sorted, gate, x1, wsg, wsu, wsd, g, b, n_prompt, tc):
    n = x1.shape[0]
    ns = n // tc
    np_tiles = n_prompt // tc
    const = lambda s: pl.BlockSpec(s, lambda i: (0, 0))
    dspec = lambda f: pl.BlockSpec((TOP_K, tc), f, memory_space=pltpu.SMEM)
    return pl.pallas_call(
        functools.partial(_combine_kernel, tc=tc, np_tiles=np_tiles),
        grid=(ns,),
        in_specs=[dspec(lambda i: (0, i)), dspec(lambda i: (0, jnp.minimum(i + 1, ns - 1))),
                  pl.BlockSpec(memory_space=pl.ANY),
                  pl.BlockSpec((tc, TOP_K), lambda i: (i, 0)),
                  pl.BlockSpec((tc, D_MODEL), lambda i: (i, 0)),
                  const((D_MODEL, EXP_HIDDEN)), const((D_MODEL, EXP_HIDDEN)), const((EXP_HIDDEN, D_MODEL)),
                  const((1, D_MODEL)), const((1, D_MODEL))],
        out_specs=(pl.BlockSpec((tc, D_MODEL), lambda i: (jnp.minimum(i, np_tiles - 1), 0)),
                   pl.BlockSpec((tc, D_MODEL), lambda i: (jnp.maximum(i - np_tiles, 0), 0))),
        out_shape=(jax.ShapeDtypeStruct((n_prompt, D_MODEL), F32),
                   jax.ShapeDtypeStruct((n - n_prompt, D_MODEL), F32)),
        scratch_shapes=[pltpu.VMEM((2, TOP_K, tc, D_MODEL), F32), pltpu.SemaphoreType.DMA((2,))],
        compiler_params=_params("arbitrary"),
        name="combine",
    )(dest, dest, y_sorted, gate, x1, wsg, wsu, wsd, g, b)


def _pack_w_in(w_in):
    d = w_in.shape[0]
    o_z = GDN_CONV_CH
    o_a = o_z + GDN_V_W
    o_b = o_a + GDN_HEADS
    o_q = o_b + GDN_HEADS
    o_c = o_q + MLA_HEADS * (MLA_D_NOPE + MLA_D_ROPE)
    o_kr = o_c + MLA_KV_RANK
    zeros = lambda w: jnp.zeros((d, w), w_in.dtype)
    wq = w_in[:, o_q:o_c].reshape(d, MLA_HEADS, MLA_D_NOPE + MLA_D_ROPE)
    q_nope = wq[:, :, :MLA_D_NOPE].reshape(d, MLA_HEADS * MLA_D_NOPE)
    q_rope = jnp.pad(wq[:, :, MLA_D_NOPE:], ((0, 0), (0, 0), (0, LANES - MLA_D_ROPE))).reshape(d, MLA_HEADS * LANES)
    cols = [w_in[:, :o_a], q_nope, q_rope, w_in[:, o_c:o_kr], w_in[:, o_kr:], zeros(LANES - MLA_D_ROPE),
            w_in[:, o_a:o_q], zeros(LANES - 2 * GDN_HEADS)]
    return jnp.concatenate(cols, axis=1).astype(BF16)


def _rope_tables(pos):
    inv_freq = ROPE_THETA ** (-jnp.arange(0, MLA_D_ROPE, 2, dtype=F32) / MLA_D_ROPE)
    ang = pos.astype(F32)[:, None] * inv_freq[None, :]
    cos, sin = jnp.cos(ang), jnp.sin(ang)
    pad = jnp.zeros((pos.shape[0], LANES - MLA_D_ROPE), F32)
    return jnp.concatenate([cos, cos, pad], axis=1), jnp.concatenate([-sin, sin, pad], axis=1)


def _pick(t, pref):
    return pref if t % pref == 0 else t


def _token_mixers(x, pos, conv_hist, s0, past, wts, x1_all, row0):
    b, t, _ = x.shape
    cs_tab, sn_tab = _rope_tables(pos)
    tt = _pick(t, 512)
    c = min(CHUNK, t)
    qkv, z, gb, q, k, v, latent, k_rope, conv_new = _front(
        x, wts["w_pack"], wts["conv_w"], conv_hist, wts["gpar"], wts["kvnw"], wts["wukv"], cs_tab, sn_tab, tt, c)
    og, s_new = _gdn(qkv, z, gb, s0, wts["gdn_nw"], _pick(t, 4 * CHUNK), c)
    if past is None:
        tb = _pick(t, 1024)
        om = _attn_prompt(q, k, v, tb, _pick(tb, 512))
    else:
        om = _attn_sample(q, k, v, past[0], past[1], wts["wukv"])
    n = b * t
    x1_all = _mixln(og.reshape(n, -1), om.reshape(n, -1), x.reshape(n, D_MODEL), wts["w_out"],
                    wts["ln1_g"], wts["ln1_b"], x1_all, row0, _pick(n, 256))
    return x1_all, latent, k_rope, s_new, conv_new


def _moe(x1_all, n_prompt, wts):
    n = x1_all.shape[0]
    idx, gate, rank, cnt = _router(x1_all, wts["router_w"], wts["router_b"], _pick(n, 256))
    counts = cnt[:, 0].astype(I32)
    padded = (counts + MOE_BM - 1) // MOE_BM * MOE_BM
    pend = jnp.cumsum(padded)
    pstart = pend - padded
    td = _pick(n, 256)
    dest = _dest(idx, rank, pstart.astype(F32).reshape(-1, 1), td)
    n_blocks = n * TOP_K // MOE_BM + N_EXPERTS
    block_start = jnp.arange(n_blocks, dtype=I32) * MOE_BM
    block_e = jnp.minimum(jnp.sum((pend[None, :] <= block_start[:, None]).astype(I32), axis=1), N_EXPERTS - 1)
    nact = (pend[-1:] // MOE_BM).astype(I32)
    tail = jnp.maximum(pend - MOE_BM, 0).astype(I32)

    xs = _dispatch(tail, dest, x1_all, n_blocks * MOE_BM, td)
    y_sorted = _experts(block_e, nact, xs, wts["exp_wg"], wts["exp_wu"], wts["exp_wd"])
    tc = _pick(math.gcd(n_prompt, n - n_prompt), 128)
    return _combine(dest, y_sorted, gate.T, x1_all,
                    wts["sh_wg"], wts["sh_wu"], wts["sh_wd"], wts["ln2_g"], wts["ln2_b"], n_prompt, tc)


def kernel(x_prompt, x_sample, cache_kv_latent, cache_k_rope, state_gdn, state_conv, w_in, gdn_conv_w, gdn_a_log, gdn_dt_bias, gdn_norm_w, mla_kv_norm_w, mla_w_uk, mla_w_uv, w_out, ln1_g, ln1_b, router_w, router_bias, exp_w_gate, exp_w_up, exp_w_down, shared_w_gate, shared_w_up, shared_w_down, ln2_g, ln2_b):
    assert w_in.shape[0] == 1, "single-layer stack"
    b_p, t_p, _ = x_prompt.shape
    b_s, t_s, _ = x_sample.shape
    past = cache_kv_latent.shape[2]
    l = 0
    pad4 = lambda a: jnp.pad(a.astype(F32), (0, LANES - GDN_HEADS))
    wts = {
        "w_pack": _pack_w_in(w_in[l]),
        "conv_w": gdn_conv_w[l],
        "gpar": jnp.stack([pad4(gdn_a_log[l]), pad4(gdn_dt_bias[l])]),
        "kvnw": mla_kv_norm_w[l].reshape(1, -1),
        "wukv": jnp.concatenate([mla_w_uk[l].reshape(MLA_KV_RANK, -1), mla_w_uv[l].reshape(MLA_KV_RANK, -1)],
                                axis=1).astype(BF16),
        "gdn_nw": gdn_norm_w[l].reshape(1, -1),
        "w_out": w_out[l].astype(BF16),
        "ln1_g": ln1_g[l].reshape(1, -1), "ln1_b": ln1_b[l].reshape(1, -1),
        "router_w": router_w[l].astype(BF16), "router_b": router_bias[l].reshape(-1, 1),
        "exp_wg": exp_w_gate[l], "exp_wu": exp_w_up[l], "exp_wd": exp_w_down[l],
        "sh_wg": shared_w_gate[l].astype(BF16), "sh_wu": shared_w_up[l].astype(BF16),
        "sh_wd": shared_w_down[l].astype(BF16),
        "ln2_g": ln2_g[l].reshape(1, -1), "ln2_b": ln2_b[l].reshape(1, -1),
    }
    n_p, n_s = b_p * t_p, b_s * t_s
    x1_all = jnp.zeros((n_p + n_s, D_MODEL), F32)
    conv0 = jnp.zeros((b_p, GDN_CONV - 1, GDN_CONV_CH), F32)
    s0 = jnp.zeros((b_p, GDN_HEADS, GDN_DK, GDN_DV), F32)
    x1_all, lat_p, kr_p, sg_p, cv_p = _token_mixers(x_prompt, jnp.arange(t_p), conv0, s0, None, wts, x1_all, 0)
    x1_all, lat_s, kr_s, sg_s, cv_s = _token_mixers(
        x_sample, past + jnp.arange(t_s), state_conv[l], state_gdn[l],
        (cache_kv_latent[l], cache_k_rope[l]), wts, x1_all, n_p)
    y_p, y_s = _moe(x1_all, n_p, wts)
    return (y_p.reshape(b_p, t_p, D_MODEL), y_s.reshape(b_s, t_s, D_MODEL),
            lat_p[None], kr_p[None], sg_p[None], cv_p[None],
            lat_s[None], kr_s[None], sg_s[None], cv_s[None])
```

```python
import functools
import math

import jax
import jax.numpy as jnp
from jax import lax
from jax.experimental import pallas as pl
from jax.experimental.pallas import tpu as pltpu

F32 = jnp.float32
BF16 = jnp.bfloat16
I32 = jnp.int32

D_MODEL = 1024
CHUNK = 64
GDN_HEADS = 4
GDN_DK = 128
GDN_DV = 128
GDN_CONV = 4
GDN_QK_W = GDN_HEADS * GDN_DK
GDN_V_W = GDN_HEADS * GDN_DV
GDN_CONV_CH = 2 * GDN_QK_W + GDN_V_W
MLA_HEADS = 4
MLA_D_NOPE = 128
MLA_D_ROPE = 64
MLA_D_V = 128
MLA_KV_RANK = 256
MLA_SCALE = (MLA_D_NOPE + MLA_D_ROPE) ** -0.5
ROPE_THETA = 10000.0
N_EXPERTS = 256
N_GROUPS = 8
GROUP_SIZE = N_EXPERTS // N_GROUPS
TOPK_GROUPS = 4
TOP_K = 8
EXP_HIDDEN = 256
ROUTED_SCALE = 2.5
DEPTH = 1
DEEPNORM_ALPHA = (2.0 * DEPTH) ** 0.25
LN_EPS = 1e-5
RMS_EPS = 1e-6
L2_EPS = 1e-6

LANES = 128
PK_QKV = 0
PK_Z = PK_QKV + GDN_CONV_CH
PK_QNOPE = PK_Z + GDN_V_W
PK_QROPE = PK_QNOPE + MLA_HEADS * MLA_D_NOPE
PK_CKV = PK_QROPE + MLA_HEADS * LANES
PK_KROPE = PK_CKV + MLA_KV_RANK
PK_AB = PK_KROPE + LANES
PK_W = PK_AB + LANES
MLA_QK_W = 2 * LANES

MOE_BM = 256
VMEM_LIMIT = 56 * 1024 * 1024


def _mm(a, b):
    return jnp.dot(a.astype(BF16), b.astype(BF16), preferred_element_type=F32)


def _mm_nt(a, b):
    return lax.dot_general(a.astype(BF16), b.astype(BF16), (((1,), (1,)), ((), ())),
                           preferred_element_type=F32)


def _mm_tn(a, b):
    return lax.dot_general(a.astype(BF16), b.astype(BF16), (((0,), (0,)), ((), ())),
                           preferred_element_type=F32)


def _split3(x):
    hi = x.astype(BF16)
    r = x - hi.astype(F32)
    mid = r.astype(BF16)
    lo = (r - mid.astype(F32)).astype(BF16)
    return hi, mid, lo


def _sigmoid(x):
    return 1.0 / (1.0 + jnp.exp(-x))


def _silu(x):
    return x * _sigmoid(x)


def _softplus(x):
    return jnp.maximum(x, 0.0) + jnp.log1p(jnp.exp(-jnp.abs(x)))


def _rope(x, cs, sn):
    w = x.shape[-1]
    n = w // LANES
    if n > 1:
        cs = jnp.concatenate([cs] * n, axis=1)
        sn = jnp.concatenate([sn] * n, axis=1)
    lane = lax.broadcasted_iota(I32, x.shape, 1) & (LANES - 1)
    half = MLA_D_ROPE // 2
    swapped = jnp.where(lane < half, pltpu.roll(x, w - half, 1), pltpu.roll(x, half, 1))
    return x * cs + swapped * sn


def _params(*sem):
    return pltpu.CompilerParams(dimension_semantics=sem, vmem_limit_bytes=VMEM_LIMIT)


def _front_kernel(x_ref, w_ref, convw_ref, hist_ref, gpar_ref, kvnw_ref, wukv_ref, cs_ref, sn_ref,
                  qkv_ref, z_ref, gb_ref, q_ref, k_ref, v_ref, lat_ref, kr_ref, convnew_ref,
                  xp_scr, *, tt, c):
    t = pl.program_id(1)
    hrow = 8 - (GDN_CONV - 1)

    @pl.when(t == 0)
    def _():
        xp_scr[hrow:8, :] = hist_ref[0]

    proj = _mm(x_ref[0], w_ref[...])

    raw = proj[:, PK_QKV:PK_Z]
    xp_scr[8:8 + tt, :] = raw
    cw = convw_ref[...]
    y = raw * cw[GDN_CONV - 1:GDN_CONV]
    for i in range(GDN_CONV - 1):
        y = y + xp_scr[hrow + i:hrow + i + tt, :] * cw[i:i + 1]
    tail = xp_scr[tt + hrow:tt + 8, :]
    convnew_ref[0] = tail
    xp_scr[hrow:8, :] = tail
    qkv = _silu(y)
    for h in range(2 * GDN_HEADS):
        xh = qkv[:, h * GDN_DK:(h + 1) * GDN_DK]
        xh = xh * lax.rsqrt(jnp.sum(xh * xh, axis=-1, keepdims=True) + L2_EPS)
        if h < GDN_HEADS:
            xh = xh * GDN_DK ** -0.5
        qkv_ref[0, :, h * GDN_DK:(h + 1) * GDN_DK] = xh
    qkv_ref[0, :, 2 * GDN_QK_W:] = qkv[:, 2 * GDN_QK_W:]
    z_ref[0] = proj[:, PK_Z:PK_QNOPE]

    ab = proj[:, PK_AB:PK_W]
    gpar = gpar_ref[...]
    gc = -jnp.exp(gpar[0:1]) * _softplus(ab + gpar[1:2])
    pos = lax.broadcasted_iota(I32, ab.shape, 0) & (c - 1)
    step = 1
    while step < c:
        gc = gc + jnp.where(pos >= step, pltpu.roll(gc, step, 0), 0.0)
        step *= 2
    beta = _sigmoid(ab)
    lane = lax.broadcasted_iota(I32, ab.shape, 1)
    gb_ref[0] = jnp.where(lane < GDN_HEADS, gc, jnp.where(lane < 2 * GDN_HEADS, beta, 0.0))

    cs = cs_ref[...]
    sn = sn_ref[...]
    q_nope = proj[:, PK_QNOPE:PK_QROPE]
    q_rope = _rope(proj[:, PK_QROPE:PK_CKV], cs, sn)
    c_raw = proj[:, PK_CKV:PK_KROPE]
    latent = c_raw * lax.rsqrt(jnp.mean(c_raw * c_raw, axis=-1, keepdims=True) + RMS_EPS) * kvnw_ref[...]
    lat_ref[0] = latent
    k_rope = _rope(proj[:, PK_KROPE:PK_AB], cs, sn)
    kr_ref[0] = k_rope[:, :MLA_D_ROPE]
    kv = _mm(latent, wukv_ref[...])
    k_rope_b = k_rope.astype(BF16)
    for h in range(MLA_HEADS):
        q_ref[0, h, :, :LANES] = q_nope[:, h * LANES:(h + 1) * LANES].astype(BF16)
        q_ref[0, h, :, LANES:] = q_rope[:, h * LANES:(h + 1) * LANES].astype(BF16)
        k_ref[0, h, :, :LANES] = kv[:, h * LANES:(h + 1) * LANES].astype(BF16)
        k_ref[0, h, :, LANES:] = k_rope_b
        v_ref[0, h] = kv[:, (MLA_HEADS + h) * LANES:(MLA_HEADS + h + 1) * LANES].astype(BF16)


def _front(x, w_pack, conv_w, hist, gpar, kvnw, wukv, cs_tab, sn_tab, tt, c):
    b, t, _ = x.shape
    nt = t // tt
    const2 = lambda bi, ti: (0, 0)
    out_shape = (
        jax.ShapeDtypeStruct((b, t, GDN_CONV_CH), F32),
        jax.ShapeDtypeStruct((b, t, GDN_V_W), F32),
        jax.ShapeDtypeStruct((b, t, LANES), F32),
        jax.ShapeDtypeStruct((b, MLA_HEADS, t, MLA_QK_W), BF16),
        jax.ShapeDtypeStruct((b, MLA_HEADS, t, MLA_QK_W), BF16),
        jax.ShapeDtypeStruct((b, MLA_HEADS, t, MLA_D_V), BF16),
        jax.ShapeDtypeStruct((b, t, MLA_KV_RANK), F32),
        jax.ShapeDtypeStruct((b, t, MLA_D_ROPE), F32),
        jax.ShapeDtypeStruct((b, GDN_CONV - 1, GDN_CONV_CH), F32),
    )
    row3 = lambda w: pl.BlockSpec((1, tt, w), lambda bi, ti: (bi, ti, 0))
    head4 = lambda w: pl.BlockSpec((1, MLA_HEADS, tt, w), lambda bi, ti: (bi, 0, ti, 0))
    return pl.pallas_call(
        functools.partial(_front_kernel, tt=tt, c=c),
        grid=(b, nt),
        in_specs=[
            row3(D_MODEL),
            pl.BlockSpec((D_MODEL, PK_W), const2),
            pl.BlockSpec((GDN_CONV, GDN_CONV_CH), const2),
            pl.BlockSpec((1, GDN_CONV - 1, GDN_CONV_CH), lambda bi, ti: (bi, 0, 0)),
            pl.BlockSpec((2, LANES), const2),
            pl.BlockSpec((1, MLA_KV_RANK), const2),
            pl.BlockSpec((MLA_KV_RANK, 2 * MLA_HEADS * LANES), const2),
            pl.BlockSpec((tt, LANES), lambda bi, ti: (ti, 0)),
            pl.BlockSpec((tt, LANES), lambda bi, ti: (ti, 0)),
        ],
        out_specs=(
            row3(GDN_CONV_CH), row3(GDN_V_W), row3(LANES),
            head4(MLA_QK_W), head4(MLA_QK_W), head4(MLA_D_V),
            row3(MLA_KV_RANK), row3(MLA_D_ROPE),
            pl.BlockSpec((1, GDN_CONV - 1, GDN_CONV_CH), lambda bi, ti: (bi, 0, 0)),
        ),
        out_shape=out_shape,
        scratch_shapes=[pltpu.VMEM((tt + 8, GDN_CONV_CH), F32)],
        compiler_params=_params("arbitrary", "arbitrary"),
        name="front",
    )(x, w_pack, conv_w, hist, gpar, kvnw, wukv, cs_tab, sn_tab)


def _gdn_kernel(qkv_ref, z_ref, gb_ref, s0_ref, nw_ref, og_ref, sout_ref, s_scr, *, tg, c):
    t = pl.program_id(1)
    nh = GDN_HEADS
    r = nh * c
    sh = c.bit_length() - 1

    @pl.when(t == 0)
    def _():
        s_scr[...] = s0_ref[0]

    row = lax.broadcasted_iota(I32, (r, r), 0)
    col = lax.broadcasted_iota(I32, (r, r), 1)
    same = (row >> sh) == (col >> sh)
    incl = same & (row >= col)
    strict = same & (row > col)
    eye = jnp.where(row == col, 1.0, 0.0)
    lane0 = jnp.where(lax.broadcasted_iota(I32, (r, LANES), 1) == 0, 1.0, 0.0).astype(BF16)
    nw = nw_ref[...]
    chunks = range(tg // c)

    def stacked(ref, ci, base):
        return jnp.concatenate(
            [ref[0, ci * c:(ci + 1) * c, base + h * LANES:base + (h + 1) * LANES] for h in range(nh)], axis=0)

    def col_bcast(ci, lane):
        gbc = gb_ref[0, ci * c:(ci + 1) * c, :]
        return jnp.concatenate(
            [jnp.broadcast_to(gbc[:, lane + h:lane + h + 1], (c, LANES)) for h in range(nh)], axis=0)

    def as_col(gc_b):
        return _lane_tile(gc_b, r) if r % LANES == 0 else gc_b[:, :r]

    def as_row(gc_b):
        if r % LANES == 0:
            return as_col(gc_b).T
        return sum(lax.dot_general(lane0, p, (((1,), (1,)), ((), ())), preferred_element_type=F32)
                   for p in _split3(gc_b))

    ks = [stacked(qkv_ref, ci, GDN_QK_W) for ci in chunks]
    gc = [col_bcast(ci, 0) for ci in chunks]
    beta = [col_bcast(ci, nh) for ci in chunks]
    decay, qk_kk = [], []
    for ci in chunks:
        decay.append(jnp.exp(jnp.where(incl, as_col(gc[ci]) - as_row(gc[ci]), -jnp.inf)))
        qk_kk.append(_mm_nt(jnp.concatenate([stacked(qkv_ref, ci, 0), ks[ci]], axis=0), ks[ci]))
    intra = [qk_kk[ci][:r] * decay[ci] for ci in chunks]
    n_pow = [jnp.where(strict, -as_col(beta[ci]) * qk_kk[ci][r:] * decay[ci], 0.0)
             for ci in chunks]
    t_inv = [eye + n_pow[ci] for ci in chunks]
    for _ in range(sh - 1):
        n_pow = [_mm(n_pow[ci], n_pow[ci]) for ci in chunks]
        t_inv = [t_inv[ci] + _mm(t_inv[ci], n_pow[ci]) for ci in chunks]
    egc = [jnp.exp(gc[ci]) for ci in chunks]
    uw = [_mm(t_inv[ci], jnp.concatenate([stacked(qkv_ref, ci, 2 * GDN_QK_W) * beta[ci],
                                          ks[ci] * beta[ci] * egc[ci]], axis=1)) for ci in chunks]

    for ci in chunks:
        u = uw[ci][:, :GDN_DV]
        w = uw[ci][:, GDN_DV:]
        qd = stacked(qkv_ref, ci, 0) * egc[ci]
        vn, qs_s = [], []
        for h in range(nh):
            hs = slice(h * c, (h + 1) * c)
            s_h = s_scr[h]
            wq = _mm(jnp.concatenate([w[hs], qd[hs]], axis=0), s_h)
            vn_h = u[hs] - wq[:c]
            g_last = gc[ci][h * c + c - 1:h * c + c, :]
            kd = ks[ci][hs] * jnp.exp(g_last - gc[ci][hs])
            s_scr[h] = s_h * jnp.exp(g_last) + _mm_tn(kd, vn_h)
            vn.append(vn_h)
            qs_s.append(wq[c:])
        o = jnp.concatenate(qs_s, axis=0) + _mm(intra[ci], jnp.concatenate(vn, axis=0))

        o = o * lax.rsqrt(jnp.mean(o * o, axis=-1, keepdims=True) + RMS_EPS) * nw
        o = o * _silu(stacked(z_ref, ci, 0))
        for h in range(nh):
            og_ref[0, ci * c:(ci + 1) * c, h * LANES:(h + 1) * LANES] = o[h * c:(h + 1) * c].astype(BF16)

    @pl.when(t == pl.num_programs(1) - 1)
    def _():
        sout_ref[0] = s_scr[...]


def _gdn(qkv, z, gb, s0, nw, tg, c):
    b, t, _ = qkv.shape
    row3 = lambda w: pl.BlockSpec((1, tg, w), lambda bi, ti: (bi, ti, 0))
    st = pl.BlockSpec((1, GDN_HEADS, GDN_DK, GDN_DV), lambda bi, ti: (bi, 0, 0, 0))
    return pl.pallas_call(
        functools.partial(_gdn_kernel, tg=tg, c=c),
        grid=(b, t // tg),
        in_specs=[row3(GDN_CONV_CH), row3(GDN_V_W), row3(LANES), st,
                  pl.BlockSpec((1, GDN_DV), lambda bi, ti: (0, 0))],
        out_specs=(row3(GDN_V_W), st),
        out_shape=(jax.ShapeDtypeStruct((b, t, GDN_V_W), BF16),
                   jax.ShapeDtypeStruct((b, GDN_HEADS, GDN_DK, GDN_DV), F32)),
        scratch_shapes=[pltpu.VMEM((GDN_HEADS, GDN_DK, GDN_DV), F32)],
        compiler_params=_params("arbitrary", "arbitrary"),
        name="gdn",
    )(qkv, z, gb, s0, nw)


ATTN_ROW_BLOCK = 32


def _lane_tile(x, width):
    return x if width == LANES else jnp.concatenate([x] * (width // LANES), axis=1)


def _attn_kernel(qi_ref, ki_ref, q_ref, k_ref, v_ref, o_ref, m_scr, l_scr, acc_scr, a_scr, s_scr, p_scr,
                 *, tb, sub):
    step = pl.program_id(2)
    qi = qi_ref[step]
    ki = ki_ref[step]
    csh = CHUNK.bit_length() - 1
    c2 = MLA_SCALE * math.log2(math.e)

    @pl.when(ki == 0)
    def _():
        m_scr[...] = jnp.full(m_scr.shape, -jnp.inf, F32)
        l_scr[...] = jnp.zeros(l_scr.shape, F32)
        acc_scr[...] = jnp.zeros(acc_scr.shape, F32)

    def update(r0, j, masked):
        rows = slice(r0, tb)
        keys = slice(j * sub, (j + 1) * sub)
        s_scr[rows, :] = lax.dot_general(q_ref[0, 0, rows, :], k_ref[0, 0, keys, :], (((1,), (1,)), ((), ())),
                                         preferred_element_type=F32)
        rb = min(ATTN_ROW_BLOCK, tb - r0)

        def block(i):
            rr = pl.ds(pl.multiple_of(r0 + i * rb, rb), rb)
            s = s_scr[rr, :]
            if masked:
                qc = (r0 + i * rb + lax.broadcasted_iota(I32, s.shape, 0)) >> csh
                kc = (j * sub + lax.broadcasted_iota(I32, s.shape, 1)) >> csh
                s = jnp.where(kc <= qc, s, -jnp.inf)
            return rr, s

        def row_max(i, carry):
            rr, s = block(i)
            m_prev = m_scr[rr, :]
            m_new = jnp.maximum(m_prev, jnp.max(s, axis=-1, keepdims=True))
            a_scr[rr, :] = jnp.exp2((m_prev - m_new) * c2)
            m_scr[rr, :] = m_new
            return carry

        def row_exp(i, carry):
            rr, s = block(i)
            p = jnp.exp2((s - _lane_tile(m_scr[rr, :], sub)) * c2)
            l_scr[rr, :] = a_scr[rr, :] * l_scr[rr, :] + jnp.sum(p, axis=-1, keepdims=True)
            p_scr[rr, :] = p.astype(BF16)
            return carry

        for i in range((tb - r0) // rb):
            row_max(i, 0)
        for i in range((tb - r0) // rb):
            row_exp(i, 0)
        acc_scr[rows, :] = a_scr[rows, :] * acc_scr[rows, :] + jnp.dot(
            p_scr[rows, :], v_ref[0, 0, keys, :], preferred_element_type=F32)

    @pl.when(ki < qi)
    def _():
        for j in range(tb // sub):
            update(0, j, False)

    @pl.when(ki == qi)
    def _():
        for j in range(tb // sub):
            update(j * sub, j, True)
        o_ref[0] = (acc_scr[...] / l_scr[...]).astype(BF16)


def _attn_prompt(q, k, v, tb, sub):
    b, nh, t, _ = q.shape
    nt = t // tb
    pairs = [(qi, ki) for qi in range(nt) for ki in range(qi + 1)]
    qi_of = jnp.asarray([p[0] for p in pairs], I32)
    ki_of = jnp.asarray([p[1] for p in pairs], I32)
    return pl.pallas_call(
        functools.partial(_attn_kernel, tb=tb, sub=sub),
        grid_spec=pltpu.PrefetchScalarGridSpec(
            num_scalar_prefetch=2,
            grid=(b, nh, len(pairs)),
            in_specs=[pl.BlockSpec((1, 1, tb, MLA_QK_W), lambda bi, hi, s, qo, ko: (bi, hi, qo[s], 0)),
                      pl.BlockSpec((1, 1, tb, MLA_QK_W), lambda bi, hi, s, qo, ko: (bi, hi, ko[s], 0)),
                      pl.BlockSpec((1, 1, tb, MLA_D_V), lambda bi, hi, s, qo, ko: (bi, hi, ko[s], 0))],
            out_specs=pl.BlockSpec((1, tb, MLA_D_V), lambda bi, hi, s, qo, ko: (bi, qo[s], hi)),
            scratch_shapes=[pltpu.VMEM((tb, LANES), F32), pltpu.VMEM((tb, LANES), F32),
                            pltpu.VMEM((tb, MLA_D_V), F32), pltpu.VMEM((tb, LANES), F32),
                            pltpu.VMEM((tb, sub), F32), pltpu.VMEM((tb, sub), BF16)]),
        out_shape=jax.ShapeDtypeStruct((b, t, MLA_HEADS * MLA_D_V), BF16),
        compiler_params=_params("arbitrary", "arbitrary", "arbitrary"),
        name="attn_prompt",
    )(qi_of, ki_of, q, k, v)


def _attn_sample_kernel(q_ref, kn_ref, vn_ref, plat_ref, pkr_ref, wukv_ref, o_ref):
    kvp = _mm(plat_ref[0], wukv_ref[...])
    pkr = pkr_ref[0].astype(BF16)
    for h in range(MLA_HEADS):
        q = q_ref[0, h]
        s_past = (_mm_nt(q[:, :MLA_D_NOPE], kvp[:, h * LANES:(h + 1) * LANES])
                  + _mm_nt(q[:, MLA_D_NOPE:MLA_D_NOPE + MLA_D_ROPE], pkr)) * MLA_SCALE
        s_new = _mm_nt(q, kn_ref[0, h]) * MLA_SCALE
        m = jnp.maximum(jnp.max(s_past, axis=-1, keepdims=True), jnp.max(s_new, axis=-1, keepdims=True))
        p_past = jnp.exp(s_past - m)
        p_new = jnp.exp(s_new - m)
        l = jnp.sum(p_past, axis=-1, keepdims=True) + jnp.sum(p_new, axis=-1, keepdims=True)
        o = _mm(p_past, kvp[:, (MLA_HEADS + h) * LANES:(MLA_HEADS + h + 1) * LANES]) + _mm(p_new, vn_ref[0, h])
        o_ref[0, :, h * MLA_D_V:(h + 1) * MLA_D_V] = (o / l).astype(BF16)


def _attn_sample(q, k_new, v_new, past_lat, past_kr, wukv):
    b, nh, ts, _ = q.shape
    past = past_lat.shape[1]
    b4 = lambda w: pl.BlockSpec((1, nh, ts, w), lambda bi: (bi, 0, 0, 0))
    return pl.pallas_call(
        _attn_sample_kernel,
        grid=(b,),
        in_specs=[b4(MLA_QK_W), b4(MLA_QK_W), b4(MLA_D_V),
                  pl.BlockSpec((1, past, MLA_KV_RANK), lambda bi: (bi, 0, 0)),
                  pl.BlockSpec((1, past, MLA_D_ROPE), lambda bi: (bi, 0, 0)),
                  pl.BlockSpec((MLA_KV_RANK, 2 * MLA_HEADS * LANES), lambda bi: (0, 0))],
        out_specs=pl.BlockSpec((1, ts, MLA_HEADS * MLA_D_V), lambda bi: (bi, 0, 0)),
        out_shape=jax.ShapeDtypeStruct((b, ts, MLA_HEADS * MLA_D_V), BF16),
        compiler_params=_params("arbitrary"),
        name="attn_sample",
    )(q, k_new, v_new, past_lat, past_kr, wukv)


def _layernorm(y, g, b):
    mu = jnp.mean(y, axis=-1, keepdims=True)
    d = y - mu
    var = jnp.mean(d * d, axis=-1, keepdims=True)
    return d * lax.rsqrt(var + LN_EPS) * g + b


def _mixln_kernel(og_ref, om_ref, x_ref, w_ref, g_ref, b_ref, prev_ref, o_ref):
    del prev_ref
    mix = (jnp.dot(og_ref[...], w_ref[:GDN_V_W, :], preferred_element_type=F32)
           + jnp.dot(om_ref[...], w_ref[GDN_V_W:, :], preferred_element_type=F32))
    o_ref[...] = _layernorm(DEEPNORM_ALPHA * x_ref[...] + mix, g_ref[...], b_ref[...])


def _mixln(og, om, x, w_out, g, b, x1_all, row0, tm):
    n = x.shape[0]
    blk0 = row0 // tm
    rows = lambda w: pl.BlockSpec((tm, w), lambda i: (i, 0))
    const = lambda s: pl.BlockSpec(s, lambda i: (0, 0))
    return pl.pallas_call(
        _mixln_kernel,
        grid=(n // tm,),
        in_specs=[rows(GDN_V_W), rows(MLA_HEADS * MLA_D_V), rows(D_MODEL),
                  const((D_MODEL, D_MODEL)), const((1, D_MODEL)), const((1, D_MODEL)),
                  pl.BlockSpec(memory_space=pl.ANY)],
        out_specs=pl.BlockSpec((tm, D_MODEL), lambda i: (blk0 + i, 0)),
        out_shape=jax.ShapeDtypeStruct(x1_all.shape, F32),
        input_output_aliases={6: 0},
        compiler_params=_params("arbitrary"),
        name="mixln",
    )(og, om, x, w_out, g, b, x1_all)


def _router_kernel(x_ref, rw_ref, rb_ref, idx_ref, gate_ref, rank_ref, cnt_ref, carry_scr, *, tt):
    s = pl.program_id(0)

    @pl.when(s == 0)
    def _():
        carry_scr[...] = jnp.zeros(carry_scr.shape, F32)

    ninf = -jnp.inf
    big = float(2 * N_EXPERTS)
    scores = _sigmoid(_mm(x_ref[...], rw_ref[...]).T)
    biased = scores + rb_ref[...]
    eio = lax.broadcasted_iota(I32, (N_EXPERTS, tt), 0).astype(F32)

    def first_argmax(vals, io):
        m = jnp.max(vals, axis=0, keepdims=True)
        i = jnp.min(jnp.where(vals == m, io, big), axis=0, keepdims=True)
        return m, i

    gs = []
    for g in range(N_GROUPS):
        blk = biased[g * GROUP_SIZE:(g + 1) * GROUP_SIZE]
        io = (lax.broadcasted_iota(I32, (GROUP_SIZE, tt), 0) + g * GROUP_SIZE).astype(F32)
        m1, i1 = first_argmax(blk, io)
        m2 = jnp.max(jnp.where(io == i1, ninf, blk), axis=0, keepdims=True)
        gs.append(m1 + m2)
    gio = lax.broadcasted_iota(I32, (N_GROUPS, tt), 0).astype(F32)
    gsc = jnp.zeros((N_GROUPS, tt), F32)
    for g in range(N_GROUPS):
        gsc = jnp.where(gio == float(g), gs[g], gsc)
    gsel = jnp.zeros((N_GROUPS, tt), F32)
    for _ in range(TOPK_GROUPS):
        _, gi = first_argmax(gsc, gio)
        hit = gio == gi
        gsel = jnp.where(hit, 1.0, gsel)
        gsc = jnp.where(hit, ninf, gsc)
    masked = jnp.concatenate(
        [jnp.where(jnp.max(jnp.where(gio == float(g), gsel, 0.0), axis=0, keepdims=True) > 0.0,
                   biased[g * GROUP_SIZE:(g + 1) * GROUP_SIZE], ninf) for g in range(N_GROUPS)], axis=0)

    idx, wts = [], []
    sel = jnp.zeros((N_EXPERTS, tt), F32)
    for _ in range(TOP_K):
        _, ei = first_argmax(masked, eio)
        hit = eio == ei
        wts.append(jnp.sum(jnp.where(hit, scores, 0.0), axis=0, keepdims=True))
        masked = jnp.where(hit, ninf, masked)
        sel = jnp.where(hit, 1.0, sel)
        idx.append(ei)
    wsum = wts[0]
    for w in wts[1:]:
        wsum = wsum + w

    t0 = lax.broadcasted_iota(I32, (tt, tt), 0)
    t1 = lax.broadcasted_iota(I32, (tt, tt), 1)
    before = jnp.where(t0 < t1, 1.0, 0.0).astype(BF16)
    sel_b = sel.astype(BF16)
    base = carry_scr[:, :1] + jnp.dot(sel_b, before, preferred_element_type=F32)
    ranks = [jnp.sum(jnp.where(eio == ei, base, 0.0), axis=0, keepdims=True) for ei in idx]
    carry_scr[...] = carry_scr[...] + jnp.dot(sel_b, jnp.ones((tt, LANES), BF16), preferred_element_type=F32)

    for k in range(TOP_K):
        idx_ref[k:k + 1, :] = idx[k].astype(I32)
        gate_ref[k:k + 1, :] = wts[k] / wsum * ROUTED_SCALE
        rank_ref[k:k + 1, :] = ranks[k].astype(I32)
    cnt_ref[...] = carry_scr[...]


def _router(x1, rw, rb, tt):
    n = x1.shape[0]
    kt = lambda dt: jax.ShapeDtypeStruct((TOP_K, n), dt)
    kspec = pl.BlockSpec((TOP_K, tt), lambda i: (0, i))
    return pl.pallas_call(
        functools.partial(_router_kernel, tt=tt),
        grid=(n // tt,),
        in_specs=[pl.BlockSpec((tt, D_MODEL), lambda i: (i, 0)),
                  pl.BlockSpec((D_MODEL, N_EXPERTS), lambda i: (0, 0)),
                  pl.BlockSpec((N_EXPERTS, 1), lambda i: (0, 0))],
        out_specs=(kspec, kspec, kspec, pl.BlockSpec((N_EXPERTS, LANES), lambda i: (0, 0))),
        out_shape=(kt(I32), kt(F32), kt(I32), jax.ShapeDtypeStruct((N_EXPERTS, LANES), F32)),
        scratch_shapes=[pltpu.VMEM((N_EXPERTS, LANES), F32)],
        compiler_params=_params("arbitrary"),
        name="router",
    )(x1, rw, rb)


def _dest_kernel(idx_ref, rank_ref, pstart_ref, dest_ref, *, tt):
    eio = lax.broadcasted_iota(I32, (N_EXPERTS, tt), 0)
    pstart = pstart_ref[...]
    for k in range(TOP_K):
        start = jnp.sum(jnp.where(eio == idx_ref[k:k + 1, :], pstart, 0.0), axis=0, keepdims=True)
        dest_ref[0, k:k + 1, :] = start.astype(I32) + rank_ref[k:k + 1, :]


def _dest(idx, rank, pstart, tt):
    n = idx.shape[1]
    kspec = pl.BlockSpec((TOP_K, tt), lambda i: (0, i))
    return pl.pallas_call(
        functools.partial(_dest_kernel, tt=tt),
        grid=(n // tt,),
        in_specs=[kspec, kspec, pl.BlockSpec((N_EXPERTS, 1), lambda i: (0, 0))],
        out_specs=pl.BlockSpec((1, TOP_K, tt), lambda i: (i, 0, 0)),
        out_shape=jax.ShapeDtypeStruct((n // tt, TOP_K, tt), I32),
        compiler_params=_params("arbitrary"),
        name="dest",
    )(idx, rank, pstart)


ROW_TILE = D_MODEL // LANES


def _rows_to_tiles(x, tiles_ref, base, n):
    for j in range(ROW_TILE):
        tiles_ref[pl.ds(base * ROW_TILE + j, n, stride=ROW_TILE), :] = x[:, j * LANES:(j + 1) * LANES]


def _tiles_to_rows(tiles_ref, base, n):
    return jnp.concatenate(
        [tiles_ref[pl.ds(base * ROW_TILE + j, n, stride=ROW_TILE), :] for j in range(ROW_TILE)], axis=1)


def _dispatch_kernel(tail_ref, dest_ref, x_ref, xs_out, zbuf, xt_scr, sem, zsem, *, td):
    s = pl.program_id(0)

    @pl.when(s == 0)
    def _():
        zbuf[...] = jnp.zeros(zbuf.shape, F32)

        def zero_copy(e):
            first = pl.multiple_of(tail_ref[e] * ROW_TILE, MOE_BM * ROW_TILE)
            return pltpu.make_async_copy(zbuf, xs_out.at[pl.ds(first, MOE_BM * ROW_TILE)], zsem)

        def zstart(e, carry):
            zero_copy(e).start()
            return carry

        def zwait(e, carry):
            zero_copy(e).wait()
            return carry

        lax.fori_loop(0, N_EXPERTS, zstart, 0)
        lax.fori_loop(0, N_EXPERTS, zwait, 0)

    _rows_to_tiles(x_ref[...], xt_scr, 0, td)

    def row_copy(i, d):
        return pltpu.make_async_copy(xt_scr.at[pl.ds(pl.multiple_of(i * ROW_TILE, ROW_TILE), ROW_TILE)],
                                     xs_out.at[pl.ds(pl.multiple_of(d * ROW_TILE, ROW_TILE), ROW_TILE)], sem)

    def issue(i, carry):
        for k in range(TOP_K):
            row_copy(i, dest_ref[0, 0, k * td + i]).start(priority=k % 2)
        return carry

    lax.fori_loop(0, td, issue, 0, unroll=4)

    def drain(i, carry):
        for k in range(TOP_K):
            row_copy(0, 0).wait()
        return carry

    lax.fori_loop(0, td, drain, 0)


def _dispatch(tail, dest, x1, n_rows, td):
    n = x1.shape[0]
    return pl.pallas_call(
        functools.partial(_dispatch_kernel, td=td),
        grid_spec=pltpu.PrefetchScalarGridSpec(
            num_scalar_prefetch=1,
            grid=(n // td,),
            in_specs=[pl.BlockSpec((1, 1, TOP_K * td), lambda i, tl: (i, 0, 0), memory_space=pltpu.SMEM),
                      pl.BlockSpec((td, D_MODEL), lambda i, tl: (i, 0))],
            out_specs=pl.BlockSpec(memory_space=pl.ANY),
            scratch_shapes=[pltpu.VMEM((MOE_BM * ROW_TILE, LANES), F32), pltpu.VMEM((td * ROW_TILE, LANES), F32),
                            pltpu.SemaphoreType.DMA(()), pltpu.SemaphoreType.DMA(())]),
        out_shape=jax.ShapeDtypeStruct((n_rows * ROW_TILE, LANES), F32),
        compiler_params=_params("arbitrary"),
        name="dispatch",
    )(tail, dest, x1)


def _expert_kernel(be_ref, nact_ref, xs_ref, wg_ref, wu_ref, wd_ref, y_ref):
    del be_ref

    @pl.when(pl.program_id(0) < nact_ref[0])
    def _():
        xb = _tiles_to_rows(xs_ref, 0, MOE_BM).astype(BF16)
        hg = _mm(xb, wg_ref[0])
        hu = _mm(xb, wu_ref[0])
        _rows_to_tiles(_mm(_silu(hg) * hu, wd_ref[0]), y_ref, 0, MOE_BM)


def _experts(block_e, nact, xs, wg, wu, wd):
    n_rows = xs.shape[0] // ROW_TILE
    nb = n_rows // MOE_BM
    act = lambda b, na: jnp.minimum(b, na[0] - 1)
    rows = pl.BlockSpec((MOE_BM * ROW_TILE, LANES), lambda b, be, na: (act(b, na), 0))
    wspec = lambda s: pl.BlockSpec((1,) + s, lambda b, be, na: (be[act(b, na)], 0, 0))
    return pl.pallas_call(
        _expert_kernel,
        grid_spec=pltpu.PrefetchScalarGridSpec(
            num_scalar_prefetch=2,
            grid=(nb,),
            in_specs=[rows, wspec((D_MODEL, EXP_HIDDEN)), wspec((D_MODEL, EXP_HIDDEN)),
                      wspec((EXP_HIDDEN, D_MODEL))],
            out_specs=rows),
        out_shape=jax.ShapeDtypeStruct((n_rows * ROW_TILE, LANES), F32),
        compiler_params=_params("arbitrary"),
        name="experts",
    )(block_e, nact, xs, wg, wu, wd)


def _combine_kernel(dcur_ref, dnxt_ref, y_hbm, gate_ref, x_ref, wsg_ref, wsu_ref, wsd_ref, g_ref, b_ref,
                    outp_ref, outs_ref, buf, sem, *, tc, np_tiles):
    s = pl.program_id(0)
    ns = pl.num_programs(0)
    slot = s % 2

    def row_copy(d, slot_, k, i):
        dst = pl.multiple_of(((slot_ * TOP_K + k) * tc + i) * ROW_TILE, ROW_TILE)
        return pltpu.make_async_copy(y_hbm.at[pl.ds(pl.multiple_of(d * ROW_TILE, ROW_TILE), ROW_TILE)],
                                     buf.at[pl.ds(dst, ROW_TILE)], sem.at[slot_])

    def issue(dref, slot_):
        def body(i, carry):
            for k in range(TOP_K):
                row_copy(dref[0, 0, k * tc + i], slot_, k, i).start(priority=k % 2)
            return carry
        lax.fori_loop(0, tc, body, 0, unroll=4)

    @pl.when(s == 0)
    def _():
        issue(dcur_ref, 0)

    @pl.when(s + 1 < ns)
    def _():
        issue(dnxt_ref, 1 - slot)

    def drain(i, carry):
        for k in range(TOP_K):
            row_copy(0, slot, k, i).wait()
        return carry

    lax.fori_loop(0, tc, drain, 0)

    x = x_ref[...]
    gate = gate_ref[...]
    routed = _tiles_to_rows(buf, slot * TOP_K * tc, tc) * gate[:, 0:1]
    for k in range(1, TOP_K):
        routed = routed + _tiles_to_rows(buf, (slot * TOP_K + k) * tc, tc) * gate[:, k:k + 1]
    xb = x.astype(BF16)
    shared = _mm(_silu(_mm(xb, wsg_ref[...])) * _mm(xb, wsu_ref[...]), wsd_ref[...])
    out = _layernorm(DEEPNORM_ALPHA * x + (routed + shared), g_ref[...], b_ref[...])

    @pl.when(s < np_tiles)
    def _():
        outp_ref[...] = out

    @pl.when(s >= np_tiles)
    def _():
        outs_ref[...] = out


def _combine(dest, y_sorted, gate, x1, wsg, wsu, wsd, g, b, n_prompt, tc):
    n = x1.shape[0]
    ns = n // tc
    np_tiles = n_prompt // tc
    const = lambda s: pl.BlockSpec(s, lambda i: (0, 0))
    dspec = lambda f: pl.BlockSpec((1, 1, TOP_K * tc), f, memory_space=pltpu.SMEM)
    return pl.pallas_call(
        functools.partial(_combine_kernel, tc=tc, np_tiles=np_tiles),
        grid=(ns,),
        in_specs=[dspec(lambda i: (i, 0, 0)), dspec(lambda i: (jnp.minimum(i + 1, ns - 1), 0, 0)),
                  pl.BlockSpec(memory_space=pl.ANY),
                  pl.BlockSpec((tc, TOP_K), lambda i: (i, 0)),
                  pl.BlockSpec((tc, D_MODEL), lambda i: (i, 0)),
                  const((D_MODEL, EXP_HIDDEN)), const((D_MODEL, EXP_HIDDEN)), const((EXP_HIDDEN, D_MODEL)),
                  const((1, D_MODEL)), const((1, D_MODEL))],
        out_specs=(pl.BlockSpec((tc, D_MODEL), lambda i: (jnp.minimum(i, np_tiles - 1), 0)),
                   pl.BlockSpec((tc, D_MODEL), lambda i: (jnp.maximum(i - np_tiles, 0), 0))),
        out_shape=(jax.ShapeDtypeStruct((n_prompt, D_MODEL), F32),
                   jax.ShapeDtypeStruct((n - n_prompt, D_MODEL), F32)),
        scratch_shapes=[pltpu.VMEM((2 * TOP_K * tc * ROW_TILE, LANES), F32), pltpu.SemaphoreType.DMA((2,))],
        compiler_params=_params("arbitrary"),
        name="combine",
    )(dest, dest, y_sorted, gate, x1, wsg, wsu, wsd, g, b)


def _pack_w_in(w_in):
    d = w_in.shape[0]
    o_z = GDN_CONV_CH
    o_a = o_z + GDN_V_W
    o_b = o_a + GDN_HEADS
    o_q = o_b + GDN_HEADS
    o_c = o_q + MLA_HEADS * (MLA_D_NOPE + MLA_D_ROPE)
    o_kr = o_c + MLA_KV_RANK
    zeros = lambda w: jnp.zeros((d, w), w_in.dtype)
    wq = w_in[:, o_q:o_c].reshape(d, MLA_HEADS, MLA_D_NOPE + MLA_D_ROPE)
    q_nope = wq[:, :, :MLA_D_NOPE].reshape(d, MLA_HEADS * MLA_D_NOPE)
    q_rope = jnp.pad(wq[:, :, MLA_D_NOPE:], ((0, 0), (0, 0), (0, LANES - MLA_D_ROPE))).reshape(d, MLA_HEADS * LANES)
    cols = [w_in[:, :o_a], q_nope, q_rope, w_in[:, o_c:o_kr], w_in[:, o_kr:], zeros(LANES - MLA_D_ROPE),
            w_in[:, o_a:o_q], zeros(LANES - 2 * GDN_HEADS)]
    return jnp.concatenate(cols, axis=1).astype(BF16)


def _rope_tables(pos):
    inv_freq = ROPE_THETA ** (-jnp.arange(0, MLA_D_ROPE, 2, dtype=F32) / MLA_D_ROPE)
    ang = pos.astype(F32)[:, None] * inv_freq[None, :]
    cos, sin = jnp.cos(ang), jnp.sin(ang)
    pad = jnp.zeros((pos.shape[0], LANES - MLA_D_ROPE), F32)
    return jnp.concatenate([cos, cos, pad], axis=1), jnp.concatenate([-sin, sin, pad], axis=1)


def _pick(t, pref):
    return pref if t % pref == 0 else t


def _token_mixers(x, pos, conv_hist, s0, past, wts, x1_all, row0):
    b, t, _ = x.shape
    cs_tab, sn_tab = _rope_tables(pos)
    tt = _pick(t, 512)
    c = min(CHUNK, t)
    qkv, z, gb, q, k, v, latent, k_rope, conv_new = _front(
        x, wts["w_pack"], wts["conv_w"], conv_hist, wts["gpar"], wts["kvnw"], wts["wukv"], cs_tab, sn_tab, tt, c)
    og, s_new = _gdn(qkv, z, gb, s0, wts["gdn_nw"], _pick(t, 4 * CHUNK), c)
    if past is None:
        tb = _pick(t, 1024)
        om = _attn_prompt(q, k, v, tb, _pick(tb, 512))
    else:
        om = _attn_sample(q, k, v, past[0], past[1], wts["wukv"])
    n = b * t
    x1_all = _mixln(og.reshape(n, -1), om.reshape(n, -1), x.reshape(n, D_MODEL), wts["w_out"],
                    wts["ln1_g"], wts["ln1_b"], x1_all, row0, _pick(n, 256))
    return x1_all, latent, k_rope, s_new, conv_new


def _moe(x1_all, n_prompt, wts):
    n = x1_all.shape[0]
    idx, gate, rank, cnt = _router(x1_all, wts["router_w"], wts["router_b"], _pick(n, 256))
    counts = cnt[:, 0].astype(I32)
    padded = (counts + MOE_BM - 1) // MOE_BM * MOE_BM
    pend = jnp.cumsum(padded)
    pstart = pend - padded
    td = _pick(math.gcd(n_prompt, n - n_prompt), 256)
    dest = _dest(idx, rank, pstart.astype(F32).reshape(-1, 1), td)
    dest = dest.reshape(n // td, 1, TOP_K * td)
    n_blocks = n * TOP_K // MOE_BM + N_EXPERTS
    block_start = jnp.arange(n_blocks, dtype=I32) * MOE_BM
    block_e = jnp.minimum(jnp.sum((pend[None, :] <= block_start[:, None]).astype(I32), axis=1), N_EXPERTS - 1)
    nact = (pend[-1:] // MOE_BM).astype(I32)
    tail = jnp.maximum(pend - MOE_BM, 0).astype(I32)

    xs = _dispatch(tail, dest, x1_all, n_blocks * MOE_BM, td)
    y_sorted = _experts(block_e, nact, xs, wts["exp_wg"], wts["exp_wu"], wts["exp_wd"])
    return _combine(dest, y_sorted, gate.T, x1_all,
                    wts["sh_wg"], wts["sh_wu"], wts["sh_wd"], wts["ln2_g"], wts["ln2_b"], n_prompt, td)


def kernel(x_prompt, x_sample, cache_kv_latent, cache_k_rope, state_gdn, state_conv, w_in, gdn_conv_w, gdn_a_log, gdn_dt_bias, gdn_norm_w, mla_kv_norm_w, mla_w_uk, mla_w_uv, w_out, ln1_g, ln1_b, router_w, router_bias, exp_w_gate, exp_w_up, exp_w_down, shared_w_gate, shared_w_up, shared_w_down, ln2_g, ln2_b):
    assert w_in.shape[0] == 1, "single-layer stack"
    b_p, t_p, _ = x_prompt.shape
    b_s, t_s, _ = x_sample.shape
    past = cache_kv_latent.shape[2]
    l = 0
    pad4 = lambda a: jnp.pad(a.astype(F32), (0, LANES - GDN_HEADS))
    wts = {
        "w_pack": _pack_w_in(w_in[l]),
        "conv_w": gdn_conv_w[l],
        "gpar": jnp.stack([pad4(gdn_a_log[l]), pad4(gdn_dt_bias[l])]),
        "kvnw": mla_kv_norm_w[l].reshape(1, -1),
        "wukv": jnp.concatenate([mla_w_uk[l].reshape(MLA_KV_RANK, -1), mla_w_uv[l].reshape(MLA_KV_RANK, -1)],
                                axis=1).astype(BF16),
        "gdn_nw": gdn_norm_w[l].reshape(1, -1),
        "w_out": w_out[l].astype(BF16),
        "ln1_g": ln1_g[l].reshape(1, -1), "ln1_b": ln1_b[l].reshape(1, -1),
        "router_w": router_w[l].astype(BF16), "router_b": router_bias[l].reshape(-1, 1),
        "exp_wg": exp_w_gate[l], "exp_wu": exp_w_up[l], "exp_wd": exp_w_down[l],
        "sh_wg": shared_w_gate[l].astype(BF16), "sh_wu": shared_w_up[l].astype(BF16),
        "sh_wd": shared_w_down[l].astype(BF16),
        "ln2_g": ln2_g[l].reshape(1, -1), "ln2_b": ln2_b[l].reshape(1, -1),
    }
    n_p, n_s = b_p * t_p, b_s * t_s
    x1_all = jnp.zeros((n_p + n_s, D_MODEL), F32)
    conv0 = jnp.zeros((b_p, GDN_CONV - 1, GDN_CONV_CH), F32)
    s0 = jnp.zeros((b_p, GDN_HEADS, GDN_DK, GDN_DV), F32)
    x1_all, lat_p, kr_p, sg_p, cv_p = _token_mixers(x_prompt, jnp.arange(t_p), conv0, s0, None, wts, x1_all, 0)
    x1_all, lat_s, kr_s, sg_s, cv_s = _token_mixers(
        x_sample, past + jnp.arange(t_s), state_conv[l], state_gdn[l],
        (cache_kv_latent[l], cache_k_rope[l]), wts, x1_all, n_p)
    y_p, y_s = _moe(x1_all, n_p, wts)
    return (y_p.reshape(b_p, t_p, D_MODEL), y_s.reshape(b_s, t_s, D_MODEL),
            lat_p[None], kr_p[None], sg_p[None], cv_p[None],
            lat_s[None], kr_s[None], sg_s[None], cv_s[None])
```

```python
import functools
import math

import jax
import jax.numpy as jnp
from jax import lax
from jax.experimental import pallas as pl
from jax.experimental.pallas import tpu as pltpu

F32 = jnp.float32
BF16 = jnp.bfloat16
I32 = jnp.int32

D_MODEL = 1024
CHUNK = 64
GDN_HEADS = 4
GDN_DK = 128
GDN_DV = 128
GDN_CONV = 4
GDN_QK_W = GDN_HEADS * GDN_DK
GDN_V_W = GDN_HEADS * GDN_DV
GDN_CONV_CH = 2 * GDN_QK_W + GDN_V_W
MLA_HEADS = 4
MLA_D_NOPE = 128
MLA_D_ROPE = 64
MLA_D_V = 128
MLA_KV_RANK = 256
MLA_SCALE = (MLA_D_NOPE + MLA_D_ROPE) ** -0.5
ROPE_THETA = 10000.0
N_EXPERTS = 256
N_GROUPS = 8
GROUP_SIZE = N_EXPERTS // N_GROUPS
TOPK_GROUPS = 4
TOP_K = 8
EXP_HIDDEN = 256
ROUTED_SCALE = 2.5
DEPTH = 1
DEEPNORM_ALPHA = (2.0 * DEPTH) ** 0.25
LN_EPS = 1e-5
RMS_EPS = 1e-6
L2_EPS = 1e-6

LANES = 128
PK_QKV = 0
PK_Z = PK_QKV + GDN_CONV_CH
PK_QNOPE = PK_Z + GDN_V_W
PK_QROPE = PK_QNOPE + MLA_HEADS * MLA_D_NOPE
PK_CKV = PK_QROPE + MLA_HEADS * LANES
PK_KROPE = PK_CKV + MLA_KV_RANK
PK_AB = PK_KROPE + LANES
PK_W = PK_AB + LANES
MLA_QK_W = 2 * LANES

MOE_BM = 256
VMEM_LIMIT = 56 * 1024 * 1024


def _mm(a, b):
    return jnp.dot(a.astype(BF16), b.astype(BF16), preferred_element_type=F32)


def _mm_nt(a, b):
    return lax.dot_general(a.astype(BF16), b.astype(BF16), (((1,), (1,)), ((), ())),
                           preferred_element_type=F32)


def _mm_tn(a, b):
    return lax.dot_general(a.astype(BF16), b.astype(BF16), (((0,), (0,)), ((), ())),
                           preferred_element_type=F32)


def _split3(x):
    hi = x.astype(BF16)
    r = x - hi.astype(F32)
    mid = r.astype(BF16)
    lo = (r - mid.astype(F32)).astype(BF16)
    return hi, mid, lo


def _sigmoid(x):
    return 1.0 / (1.0 + jnp.exp(-x))


def _silu(x):
    return x * _sigmoid(x)


def _softplus(x):
    return jnp.maximum(x, 0.0) + jnp.log1p(jnp.exp(-jnp.abs(x)))


def _rope(x, cs, sn):
    w = x.shape[-1]
    n = w // LANES
    if n > 1:
        cs = jnp.concatenate([cs] * n, axis=1)
        sn = jnp.concatenate([sn] * n, axis=1)
    lane = lax.broadcasted_iota(I32, x.shape, 1) & (LANES - 1)
    half = MLA_D_ROPE // 2
    swapped = jnp.where(lane < half, pltpu.roll(x, w - half, 1), pltpu.roll(x, half, 1))
    return x * cs + swapped * sn


def _params(*sem):
    return pltpu.CompilerParams(dimension_semantics=sem, vmem_limit_bytes=VMEM_LIMIT)


def _front_kernel(x_ref, w_ref, convw_ref, hist_ref, gpar_ref, kvnw_ref, wukv_ref, cs_ref, sn_ref,
                  qkv_ref, z_ref, gb_ref, q_ref, k_ref, v_ref, lat_ref, kr_ref, convnew_ref,
                  xp_scr, *, tt, c):
    t = pl.program_id(1)
    hrow = 8 - (GDN_CONV - 1)

    @pl.when(t == 0)
    def _():
        xp_scr[hrow:8, :] = hist_ref[0]

    proj = _mm(x_ref[0], w_ref[...])

    raw = proj[:, PK_QKV:PK_Z]
    xp_scr[8:8 + tt, :] = raw
    cw = convw_ref[...]
    y = raw * cw[GDN_CONV - 1:GDN_CONV]
    for i in range(GDN_CONV - 1):
        y = y + xp_scr[hrow + i:hrow + i + tt, :] * cw[i:i + 1]
    tail = xp_scr[tt + hrow:tt + 8, :]
    convnew_ref[0] = tail
    xp_scr[hrow:8, :] = tail
    qkv = _silu(y)
    for h in range(2 * GDN_HEADS):
        xh = qkv[:, h * GDN_DK:(h + 1) * GDN_DK]
        xh = xh * lax.rsqrt(jnp.sum(xh * xh, axis=-1, keepdims=True) + L2_EPS)
        if h < GDN_HEADS:
            xh = xh * GDN_DK ** -0.5
        qkv_ref[0, :, h * GDN_DK:(h + 1) * GDN_DK] = xh
    qkv_ref[0, :, 2 * GDN_QK_W:] = qkv[:, 2 * GDN_QK_W:]
    z_ref[0] = proj[:, PK_Z:PK_QNOPE]

    ab = proj[:, PK_AB:PK_W]
    gpar = gpar_ref[...]
    gc = -jnp.exp(gpar[0:1]) * _softplus(ab + gpar[1:2])
    pos = lax.broadcasted_iota(I32, ab.shape, 0) & (c - 1)
    step = 1
    while step < c:
        gc = gc + jnp.where(pos >= step, pltpu.roll(gc, step, 0), 0.0)
        step *= 2
    beta = _sigmoid(ab)
    lane = lax.broadcasted_iota(I32, ab.shape, 1)
    gb_ref[0] = jnp.where(lane < GDN_HEADS, gc, jnp.where(lane < 2 * GDN_HEADS, beta, 0.0))

    cs = cs_ref[...]
    sn = sn_ref[...]
    q_nope = proj[:, PK_QNOPE:PK_QROPE]
    q_rope = _rope(proj[:, PK_QROPE:PK_CKV], cs, sn)
    c_raw = proj[:, PK_CKV:PK_KROPE]
    latent = c_raw * lax.rsqrt(jnp.mean(c_raw * c_raw, axis=-1, keepdims=True) + RMS_EPS) * kvnw_ref[...]
    lat_ref[0] = latent
    k_rope = _rope(proj[:, PK_KROPE:PK_AB], cs, sn)
    kr_ref[0] = k_rope[:, :MLA_D_ROPE]
    kv = _mm(latent, wukv_ref[...])
    k_rope_b = k_rope.astype(BF16)
    for h in range(MLA_HEADS):
        q_ref[0, h, :, :LANES] = q_nope[:, h * LANES:(h + 1) * LANES].astype(BF16)
        q_ref[0, h, :, LANES:] = q_rope[:, h * LANES:(h + 1) * LANES].astype(BF16)
        k_ref[0, h, :, :LANES] = kv[:, h * LANES:(h + 1) * LANES].astype(BF16)
        k_ref[0, h, :, LANES:] = k_rope_b
        v_ref[0, h] = kv[:, (MLA_HEADS + h) * LANES:(MLA_HEADS + h + 1) * LANES].astype(BF16)


def _front(x, w_pack, conv_w, hist, gpar, kvnw, wukv, cs_tab, sn_tab, tt, c):
    b, t, _ = x.shape
    nt = t // tt
    const2 = lambda bi, ti: (0, 0)
    out_shape = (
        jax.ShapeDtypeStruct((b, t, GDN_CONV_CH), F32),
        jax.ShapeDtypeStruct((b, t, GDN_V_W), F32),
        jax.ShapeDtypeStruct((b, t, LANES), F32),
        jax.ShapeDtypeStruct((b, MLA_HEADS, t, MLA_QK_W), BF16),
        jax.ShapeDtypeStruct((b, MLA_HEADS, t, MLA_QK_W), BF16),
        jax.ShapeDtypeStruct((b, MLA_HEADS, t, MLA_D_V), BF16),
        jax.ShapeDtypeStruct((b, t, MLA_KV_RANK), F32),
        jax.ShapeDtypeStruct((b, t, MLA_D_ROPE), F32),
        jax.ShapeDtypeStruct((b, GDN_CONV - 1, GDN_CONV_CH), F32),
    )
    row3 = lambda w: pl.BlockSpec((1, tt, w), lambda bi, ti: (bi, ti, 0))
    head4 = lambda w: pl.BlockSpec((1, MLA_HEADS, tt, w), lambda bi, ti: (bi, 0, ti, 0))
    return pl.pallas_call(
        functools.partial(_front_kernel, tt=tt, c=c),
        grid=(b, nt),
        in_specs=[
            row3(D_MODEL),
            pl.BlockSpec((D_MODEL, PK_W), const2),
            pl.BlockSpec((GDN_CONV, GDN_CONV_CH), const2),
            pl.BlockSpec((1, GDN_CONV - 1, GDN_CONV_CH), lambda bi, ti: (bi, 0, 0)),
            pl.BlockSpec((2, LANES), const2),
            pl.BlockSpec((1, MLA_KV_RANK), const2),
            pl.BlockSpec((MLA_KV_RANK, 2 * MLA_HEADS * LANES), const2),
            pl.BlockSpec((tt, LANES), lambda bi, ti: (ti, 0)),
            pl.BlockSpec((tt, LANES), lambda bi, ti: (ti, 0)),
        ],
        out_specs=(
            row3(GDN_CONV_CH), row3(GDN_V_W), row3(LANES),
            head4(MLA_QK_W), head4(MLA_QK_W), head4(MLA_D_V),
            row3(MLA_KV_RANK), row3(MLA_D_ROPE),
            pl.BlockSpec((1, GDN_CONV - 1, GDN_CONV_CH), lambda bi, ti: (bi, 0, 0)),
        ),
        out_shape=out_shape,
        scratch_shapes=[pltpu.VMEM((tt + 8, GDN_CONV_CH), F32)],
        compiler_params=_params("arbitrary", "arbitrary"),
        name="front",
    )(x, w_pack, conv_w, hist, gpar, kvnw, wukv, cs_tab, sn_tab)


def _gdn_kernel(qkv_ref, z_ref, gb_ref, s0_ref, nw_ref, og_ref, sout_ref, s_scr, *, tg, c):
    t = pl.program_id(1)
    nh = GDN_HEADS
    r = nh * c
    sh = c.bit_length() - 1

    @pl.when(t == 0)
    def _():
        s_scr[...] = s0_ref[0]

    row = lax.broadcasted_iota(I32, (r, r), 0)
    col = lax.broadcasted_iota(I32, (r, r), 1)
    same = (row >> sh) == (col >> sh)
    incl = same & (row >= col)
    strict = same & (row > col)
    eye = jnp.where(row == col, 1.0, 0.0)
    lane0 = jnp.where(lax.broadcasted_iota(I32, (r, LANES), 1) == 0, 1.0, 0.0).astype(BF16)
    nw = nw_ref[...]
    chunks = range(tg // c)

    def stacked(ref, ci, base):
        return jnp.concatenate(
            [ref[0, ci * c:(ci + 1) * c, base + h * LANES:base + (h + 1) * LANES] for h in range(nh)], axis=0)

    def col_bcast(ci, lane):
        gbc = gb_ref[0, ci * c:(ci + 1) * c, :]
        return jnp.concatenate(
            [jnp.broadcast_to(gbc[:, lane + h:lane + h + 1], (c, LANES)) for h in range(nh)], axis=0)

    def as_col(gc_b):
        return _lane_tile(gc_b, r) if r % LANES == 0 else gc_b[:, :r]

    def as_row(gc_b):
        if r % LANES == 0:
            return as_col(gc_b).T
        return sum(lax.dot_general(lane0, p, (((1,), (1,)), ((), ())), preferred_element_type=F32)
                   for p in _split3(gc_b))

    ks = [stacked(qkv_ref, ci, GDN_QK_W) for ci in chunks]
    gc = [col_bcast(ci, 0) for ci in chunks]
    beta = [col_bcast(ci, nh) for ci in chunks]
    decay, qk_kk = [], []
    for ci in chunks:
        decay.append(jnp.exp(jnp.where(incl, as_col(gc[ci]) - as_row(gc[ci]), -jnp.inf)))
        qk_kk.append(_mm_nt(jnp.concatenate([stacked(qkv_ref, ci, 0), ks[ci]], axis=0), ks[ci]))
    intra = [qk_kk[ci][:r] * decay[ci] for ci in chunks]
    n_pow = [jnp.where(strict, -as_col(beta[ci]) * qk_kk[ci][r:] * decay[ci], 0.0)
             for ci in chunks]
    t_inv = [eye + n_pow[ci] for ci in chunks]
    for _ in range(sh - 1):
        n_pow = [_mm(n_pow[ci], n_pow[ci]) for ci in chunks]
        t_inv = [t_inv[ci] + _mm(t_inv[ci], n_pow[ci]) for ci in chunks]
    egc = [jnp.exp(gc[ci]) for ci in chunks]
    uw = [_mm(t_inv[ci], jnp.concatenate([stacked(qkv_ref, ci, 2 * GDN_QK_W) * beta[ci],
                                          ks[ci] * beta[ci] * egc[ci]], axis=1)) for ci in chunks]

    for ci in chunks:
        u = uw[ci][:, :GDN_DV]
        w = uw[ci][:, GDN_DV:]
        qd = stacked(qkv_ref, ci, 0) * egc[ci]
        vn, qs_s = [], []
        for h in range(nh):
            hs = slice(h * c, (h + 1) * c)
            s_h = s_scr[h]
            wq = _mm(jnp.concatenate([w[hs], qd[hs]], axis=0), s_h)
            vn_h = u[hs] - wq[:c]
            g_last = gc[ci][h * c + c - 1:h * c + c, :]
            kd = ks[ci][hs] * jnp.exp(g_last - gc[ci][hs])
            s_scr[h] = s_h * jnp.exp(g_last) + _mm_tn(kd, vn_h)
            vn.append(vn_h)
            qs_s.append(wq[c:])
        o = jnp.concatenate(qs_s, axis=0) + _mm(intra[ci], jnp.concatenate(vn, axis=0))

        o = o * lax.rsqrt(jnp.mean(o * o, axis=-1, keepdims=True) + RMS_EPS) * nw
        o = o * _silu(stacked(z_ref, ci, 0))
        for h in range(nh):
            og_ref[0, ci * c:(ci + 1) * c, h * LANES:(h + 1) * LANES] = o[h * c:(h + 1) * c].astype(BF16)

    @pl.when(t == pl.num_programs(1) - 1)
    def _():
        sout_ref[0] = s_scr[...]


def _gdn(qkv, z, gb, s0, nw, tg, c):
    b, t, _ = qkv.shape
    row3 = lambda w: pl.BlockSpec((1, tg, w), lambda bi, ti: (bi, ti, 0))
    st = pl.BlockSpec((1, GDN_HEADS, GDN_DK, GDN_DV), lambda bi, ti: (bi, 0, 0, 0))
    return pl.pallas_call(
        functools.partial(_gdn_kernel, tg=tg, c=c),
        grid=(b, t // tg),
        in_specs=[row3(GDN_CONV_CH), row3(GDN_V_W), row3(LANES), st,
                  pl.BlockSpec((1, GDN_DV), lambda bi, ti: (0, 0))],
        out_specs=(row3(GDN_V_W), st),
        out_shape=(jax.ShapeDtypeStruct((b, t, GDN_V_W), BF16),
                   jax.ShapeDtypeStruct((b, GDN_HEADS, GDN_DK, GDN_DV), F32)),
        scratch_shapes=[pltpu.VMEM((GDN_HEADS, GDN_DK, GDN_DV), F32)],
        compiler_params=_params("arbitrary", "arbitrary"),
        name="gdn",
    )(qkv, z, gb, s0, nw)


ATTN_ROW_BLOCK = 32


def _lane_tile(x, width):
    return x if width == LANES else jnp.concatenate([x] * (width // LANES), axis=1)


def _attn_kernel(qi_ref, ki_ref, q_ref, k_ref, v_ref, o_ref, m_scr, l_scr, acc_scr, a_scr, s_scr, p_scr,
                 *, tb, sub):
    step = pl.program_id(2)
    qi = qi_ref[step]
    ki = ki_ref[step]
    csh = CHUNK.bit_length() - 1
    c2 = MLA_SCALE * math.log2(math.e)

    @pl.when(ki == 0)
    def _():
        m_scr[...] = jnp.full(m_scr.shape, -jnp.inf, F32)
        l_scr[...] = jnp.zeros(l_scr.shape, F32)
        acc_scr[...] = jnp.zeros(acc_scr.shape, F32)

    def update(r0, j, masked):
        rows = slice(r0, tb)
        keys = slice(j * sub, (j + 1) * sub)
        s_scr[rows, :] = lax.dot_general(q_ref[0, 0, rows, :], k_ref[0, 0, keys, :], (((1,), (1,)), ((), ())),
                                         preferred_element_type=F32)
        rb = min(ATTN_ROW_BLOCK, tb - r0)

        def block(i):
            rr = pl.ds(pl.multiple_of(r0 + i * rb, rb), rb)
            s = s_scr[rr, :]
            if masked:
                qc = (r0 + i * rb + lax.broadcasted_iota(I32, s.shape, 0)) >> csh
                kc = (j * sub + lax.broadcasted_iota(I32, s.shape, 1)) >> csh
                s = jnp.where(kc <= qc, s, -jnp.inf)
            return rr, s

        def row_max(i, carry):
            rr, s = block(i)
            m_prev = m_scr[rr, :]
            m_new = jnp.maximum(m_prev, jnp.max(s, axis=-1, keepdims=True))
            a_scr[rr, :] = jnp.exp2((m_prev - m_new) * c2)
            m_scr[rr, :] = m_new
            return carry

        def row_exp(i, carry):
            rr, s = block(i)
            p = jnp.exp2((s - _lane_tile(m_scr[rr, :], sub)) * c2)
            l_scr[rr, :] = a_scr[rr, :] * l_scr[rr, :] + jnp.sum(p, axis=-1, keepdims=True)
            p_scr[rr, :] = p.astype(BF16)
            return carry

        for i in range((tb - r0) // rb):
            row_max(i, 0)
        for i in range((tb - r0) // rb):
            row_exp(i, 0)
        acc_scr[rows, :] = a_scr[rows, :] * acc_scr[rows, :] + jnp.dot(
            p_scr[rows, :], v_ref[0, 0, keys, :], preferred_element_type=F32)

    @pl.when(ki < qi)
    def _():
        for j in range(tb // sub):
            update(0, j, False)

    @pl.when(ki == qi)
    def _():
        for j in range(tb // sub):
            update(j * sub, j, True)
        o_ref[0] = (acc_scr[...] / l_scr[...]).astype(BF16)


def _attn_prompt(q, k, v, tb, sub):
    b, nh, t, _ = q.shape
    nt = t // tb
    pairs = [(qi, ki) for qi in range(nt) for ki in range(qi + 1)]
    qi_of = jnp.asarray([p[0] for p in pairs], I32)
    ki_of = jnp.asarray([p[1] for p in pairs], I32)
    return pl.pallas_call(
        functools.partial(_attn_kernel, tb=tb, sub=sub),
        grid_spec=pltpu.PrefetchScalarGridSpec(
            num_scalar_prefetch=2,
            grid=(b, nh, len(pairs)),
            in_specs=[pl.BlockSpec((1, 1, tb, MLA_QK_W), lambda bi, hi, s, qo, ko: (bi, hi, qo[s], 0)),
                      pl.BlockSpec((1, 1, tb, MLA_QK_W), lambda bi, hi, s, qo, ko: (bi, hi, ko[s], 0)),
                      pl.BlockSpec((1, 1, tb, MLA_D_V), lambda bi, hi, s, qo, ko: (bi, hi, ko[s], 0))],
            out_specs=pl.BlockSpec((1, tb, MLA_D_V), lambda bi, hi, s, qo, ko: (bi, qo[s], hi)),
            scratch_shapes=[pltpu.VMEM((tb, LANES), F32), pltpu.VMEM((tb, LANES), F32),
                            pltpu.VMEM((tb, MLA_D_V), F32), pltpu.VMEM((tb, LANES), F32),
                            pltpu.VMEM((tb, sub), F32), pltpu.VMEM((tb, sub), BF16)]),
        out_shape=jax.ShapeDtypeStruct((b, t, MLA_HEADS * MLA_D_V), BF16),
        compiler_params=_params("arbitrary", "arbitrary", "arbitrary"),
        name="attn_prompt",
    )(qi_of, ki_of, q, k, v)


def _attn_sample_kernel(q_ref, kn_ref, vn_ref, plat_ref, pkr_ref, wukv_ref, o_ref):
    kvp = _mm(plat_ref[0], wukv_ref[...])
    pkr = pkr_ref[0].astype(BF16)
    for h in range(MLA_HEADS):
        q = q_ref[0, h]
        s_past = (_mm_nt(q[:, :MLA_D_NOPE], kvp[:, h * LANES:(h + 1) * LANES])
                  + _mm_nt(q[:, MLA_D_NOPE:MLA_D_NOPE + MLA_D_ROPE], pkr)) * MLA_SCALE
        s_new = _mm_nt(q, kn_ref[0, h]) * MLA_SCALE
        m = jnp.maximum(jnp.max(s_past, axis=-1, keepdims=True), jnp.max(s_new, axis=-1, keepdims=True))
        p_past = jnp.exp(s_past - m)
        p_new = jnp.exp(s_new - m)
        l = jnp.sum(p_past, axis=-1, keepdims=True) + jnp.sum(p_new, axis=-1, keepdims=True)
        o = _mm(p_past, kvp[:, (MLA_HEADS + h) * LANES:(MLA_HEADS + h + 1) * LANES]) + _mm(p_new, vn_ref[0, h])
        o_ref[0, :, h * MLA_D_V:(h + 1) * MLA_D_V] = (o / l).astype(BF16)


def _attn_sample(q, k_new, v_new, past_lat, past_kr, wukv):
    b, nh, ts, _ = q.shape
    past = past_lat.shape[1]
    b4 = lambda w: pl.BlockSpec((1, nh, ts, w), lambda bi: (bi, 0, 0, 0))
    return pl.pallas_call(
        _attn_sample_kernel,
        grid=(b,),
        in_specs=[b4(MLA_QK_W), b4(MLA_QK_W), b4(MLA_D_V),
                  pl.BlockSpec((1, past, MLA_KV_RANK), lambda bi: (bi, 0, 0)),
                  pl.BlockSpec((1, past, MLA_D_ROPE), lambda bi: (bi, 0, 0)),
                  pl.BlockSpec((MLA_KV_RANK, 2 * MLA_HEADS * LANES), lambda bi: (0, 0))],
        out_specs=pl.BlockSpec((1, ts, MLA_HEADS * MLA_D_V), lambda bi: (bi, 0, 0)),
        out_shape=jax.ShapeDtypeStruct((b, ts, MLA_HEADS * MLA_D_V), BF16),
        compiler_params=_params("arbitrary"),
        name="attn_sample",
    )(q, k_new, v_new, past_lat, past_kr, wukv)


def _layernorm(y, g, b):
    mu = jnp.mean(y, axis=-1, keepdims=True)
    d = y - mu
    var = jnp.mean(d * d, axis=-1, keepdims=True)
    return d * lax.rsqrt(var + LN_EPS) * g + b


def _mixln_kernel(og_ref, om_ref, x_ref, w_ref, g_ref, b_ref, prev_ref, o_ref):
    del prev_ref
    mix = (jnp.dot(og_ref[...], w_ref[:GDN_V_W, :], preferred_element_type=F32)
           + jnp.dot(om_ref[...], w_ref[GDN_V_W:, :], preferred_element_type=F32))
    o_ref[...] = _layernorm(DEEPNORM_ALPHA * x_ref[...] + mix, g_ref[...], b_ref[...])


def _mixln(og, om, x, w_out, g, b, x1_all, row0, tm):
    n = x.shape[0]
    blk0 = row0 // tm
    rows = lambda w: pl.BlockSpec((tm, w), lambda i: (i, 0))
    const = lambda s: pl.BlockSpec(s, lambda i: (0, 0))
    return pl.pallas_call(
        _mixln_kernel,
        grid=(n // tm,),
        in_specs=[rows(GDN_V_W), rows(MLA_HEADS * MLA_D_V), rows(D_MODEL),
                  const((D_MODEL, D_MODEL)), const((1, D_MODEL)), const((1, D_MODEL)),
                  pl.BlockSpec(memory_space=pl.ANY)],
        out_specs=pl.BlockSpec((tm, D_MODEL), lambda i: (blk0 + i, 0)),
        out_shape=jax.ShapeDtypeStruct(x1_all.shape, F32),
        input_output_aliases={6: 0},
        compiler_params=_params("arbitrary"),
        name="mixln",
    )(og, om, x, w_out, g, b, x1_all)


def _router_kernel(x_ref, rw_ref, rb_ref, idx_ref, gate_ref, rank_ref, cnt_ref, carry_scr, *, tt):
    s = pl.program_id(0)

    @pl.when(s == 0)
    def _():
        carry_scr[...] = jnp.zeros(carry_scr.shape, F32)

    ninf = -jnp.inf
    big = float(2 * N_EXPERTS)
    scores = _sigmoid(_mm(x_ref[...], rw_ref[...]).T)
    biased = scores + rb_ref[...]
    eio = lax.broadcasted_iota(I32, (N_EXPERTS, tt), 0).astype(F32)

    def first_argmax(vals, io):
        m = jnp.max(vals, axis=0, keepdims=True)
        i = jnp.min(jnp.where(vals == m, io, big), axis=0, keepdims=True)
        return m, i

    gs = []
    for g in range(N_GROUPS):
        blk = biased[g * GROUP_SIZE:(g + 1) * GROUP_SIZE]
        io = (lax.broadcasted_iota(I32, (GROUP_SIZE, tt), 0) + g * GROUP_SIZE).astype(F32)
        m1, i1 = first_argmax(blk, io)
        m2 = jnp.max(jnp.where(io == i1, ninf, blk), axis=0, keepdims=True)
        gs.append(m1 + m2)
    gio = lax.broadcasted_iota(I32, (N_GROUPS, tt), 0).astype(F32)
    gsc = jnp.zeros((N_GROUPS, tt), F32)
    for g in range(N_GROUPS):
        gsc = jnp.where(gio == float(g), gs[g], gsc)
    gsel = jnp.zeros((N_GROUPS, tt), F32)
    for _ in range(TOPK_GROUPS):
        _, gi = first_argmax(gsc, gio)
        hit = gio == gi
        gsel = jnp.where(hit, 1.0, gsel)
        gsc = jnp.where(hit, ninf, gsc)
    masked = jnp.concatenate(
        [jnp.where(jnp.max(jnp.where(gio == float(g), gsel, 0.0), axis=0, keepdims=True) > 0.0,
                   biased[g * GROUP_SIZE:(g + 1) * GROUP_SIZE], ninf) for g in range(N_GROUPS)], axis=0)

    idx, wts = [], []
    sel = jnp.zeros((N_EXPERTS, tt), F32)
    for _ in range(TOP_K):
        _, ei = first_argmax(masked, eio)
        hit = eio == ei
        wts.append(jnp.sum(jnp.where(hit, scores, 0.0), axis=0, keepdims=True))
        masked = jnp.where(hit, ninf, masked)
        sel = jnp.where(hit, 1.0, sel)
        idx.append(ei)
    wsum = wts[0]
    for w in wts[1:]:
        wsum = wsum + w

    t0 = lax.broadcasted_iota(I32, (tt, tt), 0)
    t1 = lax.broadcasted_iota(I32, (tt, tt), 1)
    before = jnp.where(t0 < t1, 1.0, 0.0).astype(BF16)
    sel_b = sel.astype(BF16)
    base = carry_scr[:, :1] + jnp.dot(sel_b, before, preferred_element_type=F32)
    ranks = [jnp.sum(jnp.where(eio == ei, base, 0.0), axis=0, keepdims=True) for ei in idx]
    carry_scr[...] = carry_scr[...] + jnp.dot(sel_b, jnp.ones((tt, LANES), BF16), preferred_element_type=F32)

    for k in range(TOP_K):
        idx_ref[k:k + 1, :] = idx[k].astype(I32)
        gate_ref[k:k + 1, :] = wts[k] / wsum * ROUTED_SCALE
        rank_ref[k:k + 1, :] = ranks[k].astype(I32)
    cnt_ref[...] = carry_scr[...]


def _router(x1, rw, rb, tt):
    n = x1.shape[0]
    kt = lambda dt: jax.ShapeDtypeStruct((TOP_K, n), dt)
    kspec = pl.BlockSpec((TOP_K, tt), lambda i: (0, i))
    return pl.pallas_call(
        functools.partial(_router_kernel, tt=tt),
        grid=(n // tt,),
        in_specs=[pl.BlockSpec((tt, D_MODEL), lambda i: (i, 0)),
                  pl.BlockSpec((D_MODEL, N_EXPERTS), lambda i: (0, 0)),
                  pl.BlockSpec((N_EXPERTS, 1), lambda i: (0, 0))],
        out_specs=(kspec, kspec, kspec, pl.BlockSpec((N_EXPERTS, LANES), lambda i: (0, 0))),
        out_shape=(kt(I32), kt(F32), kt(I32), jax.ShapeDtypeStruct((N_EXPERTS, LANES), F32)),
        scratch_shapes=[pltpu.VMEM((N_EXPERTS, LANES), F32)],
        compiler_params=_params("arbitrary"),
        name="router",
    )(x1, rw, rb)


def _dest_kernel(idx_ref, rank_ref, pstart_ref, dest_ref, *, tt):
    eio = lax.broadcasted_iota(I32, (N_EXPERTS, tt), 0)
    pstart = pstart_ref[...]
    for k in range(TOP_K):
        start = jnp.sum(jnp.where(eio == idx_ref[k:k + 1, :], pstart, 0.0), axis=0, keepdims=True)
        dest_ref[0, k:k + 1, :] = start.astype(I32) + rank_ref[k:k + 1, :]


def _dest(idx, rank, pstart, tt):
    n = idx.shape[1]
    kspec = pl.BlockSpec((TOP_K, tt), lambda i: (0, i))
    return pl.pallas_call(
        functools.partial(_dest_kernel, tt=tt),
        grid=(n // tt,),
        in_specs=[kspec, kspec, pl.BlockSpec((N_EXPERTS, 1), lambda i: (0, 0))],
        out_specs=pl.BlockSpec((1, TOP_K, tt), lambda i: (i, 0, 0)),
        out_shape=jax.ShapeDtypeStruct((n // tt, TOP_K, tt), I32),
        compiler_params=_params("arbitrary"),
        name="dest",
    )(idx, rank, pstart)


ROW_TILE = D_MODEL // LANES
XROW_TILE = ROW_TILE // 2


def _rows_to_tiles(x, tiles_ref, base, n, rt=ROW_TILE):
    for j in range(rt):
        tiles_ref[pl.ds(base * rt + j, n, stride=rt), :] = x[:, j * LANES:(j + 1) * LANES]


def _tiles_to_rows(tiles_ref, base, n, rt=ROW_TILE):
    return jnp.concatenate(
        [tiles_ref[pl.ds(base * rt + j, n, stride=rt), :] for j in range(rt)], axis=1)


def _pack_bf16_pairs(x):
    half = x.shape[1] // 2
    lo = pltpu.bitcast(x[:, :half].astype(BF16).astype(F32), jnp.uint32) >> 16
    hi = pltpu.bitcast(x[:, half:].astype(BF16).astype(F32), jnp.uint32) & jnp.uint32(0xFFFF0000)
    return lo | hi


def _unpack_bf16_pairs(w):
    lo = pltpu.bitcast(w << 16, F32)
    hi = pltpu.bitcast(w & jnp.uint32(0xFFFF0000), F32)
    return jnp.concatenate([lo, hi], axis=1).astype(BF16)


def _dispatch_kernel(tail_ref, dest_ref, x_ref, xs_out, zbuf, xt_scr, sem, zsem, *, td):
    s = pl.program_id(0)

    @pl.when(s == 0)
    def _():
        zbuf[...] = jnp.zeros(zbuf.shape, zbuf.dtype)

        def zero_copy(e):
            first = pl.multiple_of(tail_ref[e] * XROW_TILE, MOE_BM * XROW_TILE)
            return pltpu.make_async_copy(zbuf, xs_out.at[pl.ds(first, MOE_BM * XROW_TILE)], zsem)

        def zstart(e, carry):
            zero_copy(e).start()
            return carry

        def zwait(e, carry):
            zero_copy(e).wait()
            return carry

        lax.fori_loop(0, N_EXPERTS, zstart, 0)
        lax.fori_loop(0, N_EXPERTS, zwait, 0)

    _rows_to_tiles(_pack_bf16_pairs(x_ref[...]), xt_scr, 0, td, XROW_TILE)

    def row_copy(i, d):
        return pltpu.make_async_copy(xt_scr.at[pl.ds(pl.multiple_of(i * XROW_TILE, XROW_TILE), XROW_TILE)],
                                     xs_out.at[pl.ds(pl.multiple_of(d * XROW_TILE, XROW_TILE), XROW_TILE)], sem)

    def issue(i, carry):
        for k in range(TOP_K):
            row_copy(i, dest_ref[0, 0, k * td + i]).start(priority=k % 2)
        return carry

    lax.fori_loop(0, td, issue, 0, unroll=4)

    def drain(i, carry):
        for k in range(TOP_K):
            row_copy(0, 0).wait()
        return carry

    lax.fori_loop(0, td, drain, 0)


def _dispatch(tail, dest, x1, n_rows, td):
    n = x1.shape[0]
    return pl.pallas_call(
        functools.partial(_dispatch_kernel, td=td),
        grid_spec=pltpu.PrefetchScalarGridSpec(
            num_scalar_prefetch=1,
            grid=(n // td,),
            in_specs=[pl.BlockSpec((1, 1, TOP_K * td), lambda i, tl: (i, 0, 0), memory_space=pltpu.SMEM),
                      pl.BlockSpec((td, D_MODEL), lambda i, tl: (i, 0))],
            out_specs=pl.BlockSpec(memory_space=pl.ANY),
            scratch_shapes=[pltpu.VMEM((MOE_BM * XROW_TILE, LANES), jnp.uint32),
                            pltpu.VMEM((td * XROW_TILE, LANES), jnp.uint32),
                            pltpu.SemaphoreType.DMA(()), pltpu.SemaphoreType.DMA(())]),
        out_shape=jax.ShapeDtypeStruct((n_rows * XROW_TILE, LANES), jnp.uint32),
        compiler_params=_params("arbitrary"),
        name="dispatch",
    )(tail, dest, x1)


def _expert_kernel(first_ref, nblk_ref, nact_ref, xs_hbm, wg_ref, wu_ref, wd_ref, y_hbm,
                   xbuf, ybuf, wgb, wub, wdb, xsem, ysem):
    e = pl.program_id(0)
    nact = nact_ref[0]
    xr = MOE_BM * XROW_TILE
    yr = MOE_BM * ROW_TILE

    def x_copy(g, slot):
        return pltpu.make_async_copy(xs_hbm.at[pl.ds(pl.multiple_of(g * xr, xr), xr)],
                                     xbuf.at[pl.ds(pl.multiple_of(slot * xr, xr), xr)], xsem.at[slot])

    def y_copy(g, slot):
        return pltpu.make_async_copy(ybuf.at[pl.ds(pl.multiple_of(slot * yr, yr), yr)],
                                     y_hbm.at[pl.ds(pl.multiple_of(g * yr, yr), yr)], ysem.at[slot])

    @pl.when(e == 0)
    def _():
        x_copy(0, 0).start()

    @pl.when(nblk_ref[e] > 0)
    def _():
        wgb[...] = wg_ref[0].astype(BF16)
        wub[...] = wu_ref[0].astype(BF16)
        wdb[...] = wd_ref[0].astype(BF16)

    def block(b, carry):
        g = first_ref[e] + b
        slot = g & 1
        x_copy(g, slot).wait()

        @pl.when(g + 1 < nact)
        def _():
            x_copy(g + 1, 1 - slot).start()

        @pl.when(g >= 2)
        def _():
            y_copy(g - 2, slot).wait()

        xb = _unpack_bf16_pairs(_tiles_to_rows(xbuf, slot * MOE_BM, MOE_BM, XROW_TILE))
        hg = jnp.dot(xb, wgb[...], preferred_element_type=F32)
        hu = jnp.dot(xb, wub[...], preferred_element_type=F32)
        y = jnp.dot((_silu(hg) * hu).astype(BF16), wdb[...], preferred_element_type=F32)
        _rows_to_tiles(y, ybuf, slot * MOE_BM, MOE_BM)
        y_copy(g, slot).start()

        @pl.when(g == nact - 1)
        def _():
            y_copy(g, slot).wait()

            @pl.when(g >= 1)
            def _():
                y_copy(g - 1, 1 - slot).wait()

        return carry

    lax.fori_loop(0, nblk_ref[e], block, 0)


def _experts(first, nblk, nact, xs, wg, wu, wd):
    n_rows = xs.shape[0] // XROW_TILE
    wspec = lambda s: pl.BlockSpec((1,) + s, lambda e, fr, nb, na: (e, 0, 0))
    return pl.pallas_call(
        _expert_kernel,
        grid_spec=pltpu.PrefetchScalarGridSpec(
            num_scalar_prefetch=3,
            grid=(N_EXPERTS,),
            in_specs=[pl.BlockSpec(memory_space=pl.ANY),
                      wspec((D_MODEL, EXP_HIDDEN)), wspec((D_MODEL, EXP_HIDDEN)), wspec((EXP_HIDDEN, D_MODEL))],
            out_specs=pl.BlockSpec(memory_space=pl.ANY),
            scratch_shapes=[pltpu.VMEM((2 * MOE_BM * XROW_TILE, LANES), jnp.uint32),
                            pltpu.VMEM((2 * MOE_BM * ROW_TILE, LANES), F32),
                            pltpu.VMEM((D_MODEL, EXP_HIDDEN), BF16), pltpu.VMEM((D_MODEL, EXP_HIDDEN), BF16),
                            pltpu.VMEM((EXP_HIDDEN, D_MODEL), BF16),
                            pltpu.SemaphoreType.DMA((2,)), pltpu.SemaphoreType.DMA((2,))]),
        out_shape=jax.ShapeDtypeStruct((n_rows * ROW_TILE, LANES), F32),
        compiler_params=_params("arbitrary"),
        name="experts",
    )(first, nblk, nact, xs, wg, wu, wd)


def _combine_kernel(dcur_ref, dnxt_ref, y_hbm, gate_ref, x_ref, wsg_ref, wsu_ref, wsd_ref, g_ref, b_ref,
                    outp_ref, outs_ref, buf, sem, *, tc, np_tiles):
    s = pl.program_id(0)
    ns = pl.num_programs(0)
    slot = s % 2

    def row_copy(d, slot_, k, i):
        dst = pl.multiple_of(((slot_ * TOP_K + k) * tc + i) * ROW_TILE, ROW_TILE)
        return pltpu.make_async_copy(y_hbm.at[pl.ds(pl.multiple_of(d * ROW_TILE, ROW_TILE), ROW_TILE)],
                                     buf.at[pl.ds(dst, ROW_TILE)], sem.at[slot_])

    def issue(dref, slot_):
        def body(i, carry):
            for k in range(TOP_K):
                row_copy(dref[0, 0, k * tc + i], slot_, k, i).start(priority=k % 2)
            return carry
        lax.fori_loop(0, tc, body, 0, unroll=4)

    @pl.when(s == 0)
    def _():
        issue(dcur_ref, 0)

    @pl.when(s + 1 < ns)
    def _():
        issue(dnxt_ref, 1 - slot)

    def drain(i, carry):
        for k in range(TOP_K):
            row_copy(0, slot, k, i).wait()
        return carry

    lax.fori_loop(0, tc, drain, 0)

    x = x_ref[...]
    gate = gate_ref[...]
    routed = _tiles_to_rows(buf, slot * TOP_K * tc, tc) * gate[:, 0:1]
    for k in range(1, TOP_K):
        routed = routed + _tiles_to_rows(buf, (slot * TOP_K + k) * tc, tc) * gate[:, k:k + 1]
    xb = x.astype(BF16)
    shared = _mm(_silu(_mm(xb, wsg_ref[...])) * _mm(xb, wsu_ref[...]), wsd_ref[...])
    out = _layernorm(DEEPNORM_ALPHA * x + (routed + shared), g_ref[...], b_ref[...])

    @pl.when(s < np_tiles)
    def _():
        outp_ref[...] = out

    @pl.when(s >= np_tiles)
    def _():
        outs_ref[...] = out


def _combine(dest, y_sorted, gate, x1, wsg, wsu, wsd, g, b, n_prompt, tc):
    n = x1.shape[0]
    ns = n // tc
    np_tiles = n_prompt // tc
    const = lambda s: pl.BlockSpec(s, lambda i: (0, 0))
    dspec = lambda f: pl.BlockSpec((1, 1, TOP_K * tc), f, memory_space=pltpu.SMEM)
    return pl.pallas_call(
        functools.partial(_combine_kernel, tc=tc, np_tiles=np_tiles),
        grid=(ns,),
        in_specs=[dspec(lambda i: (i, 0, 0)), dspec(lambda i: (jnp.minimum(i + 1, ns - 1), 0, 0)),
                  pl.BlockSpec(memory_space=pl.ANY),
                  pl.BlockSpec((tc, TOP_K), lambda i: (i, 0)),
                  pl.BlockSpec((tc, D_MODEL), lambda i: (i, 0)),
                  const((D_MODEL, EXP_HIDDEN)), const((D_MODEL, EXP_HIDDEN)), const((EXP_HIDDEN, D_MODEL)),
                  const((1, D_MODEL)), const((1, D_MODEL))],
        out_specs=(pl.BlockSpec((tc, D_MODEL), lambda i: (jnp.minimum(i, np_tiles - 1), 0)),
                   pl.BlockSpec((tc, D_MODEL), lambda i: (jnp.maximum(i - np_tiles, 0), 0))),
        out_shape=(jax.ShapeDtypeStruct((n_prompt, D_MODEL), F32),
                   jax.ShapeDtypeStruct((n - n_prompt, D_MODEL), F32)),
        scratch_shapes=[pltpu.VMEM((2 * TOP_K * tc * ROW_TILE, LANES), F32), pltpu.SemaphoreType.DMA((2,))],
        compiler_params=_params("arbitrary"),
        name="combine",
    )(dest, dest, y_sorted, gate, x1, wsg, wsu, wsd, g, b)


def _pack_w_in(w_in):
    d = w_in.shape[0]
    o_z = GDN_CONV_CH
    o_a = o_z + GDN_V_W
    o_b = o_a + GDN_HEADS
    o_q = o_b + GDN_HEADS
    o_c = o_q + MLA_HEADS * (MLA_D_NOPE + MLA_D_ROPE)
    o_kr = o_c + MLA_KV_RANK
    zeros = lambda w: jnp.zeros((d, w), w_in.dtype)
    wq = w_in[:, o_q:o_c].reshape(d, MLA_HEADS, MLA_D_NOPE + MLA_D_ROPE)
    q_nope = wq[:, :, :MLA_D_NOPE].reshape(d, MLA_HEADS * MLA_D_NOPE)
    q_rope = jnp.pad(wq[:, :, MLA_D_NOPE:], ((0, 0), (0, 0), (0, LANES - MLA_D_ROPE))).reshape(d, MLA_HEADS * LANES)
    cols = [w_in[:, :o_a], q_nope, q_rope, w_in[:, o_c:o_kr], w_in[:, o_kr:], zeros(LANES - MLA_D_ROPE),
            w_in[:, o_a:o_q], zeros(LANES - 2 * GDN_HEADS)]
    return jnp.concatenate(cols, axis=1).astype(BF16)


def _rope_tables(pos):
    inv_freq = ROPE_THETA ** (-jnp.arange(0, MLA_D_ROPE, 2, dtype=F32) / MLA_D_ROPE)
    ang = pos.astype(F32)[:, None] * inv_freq[None, :]
    cos, sin = jnp.cos(ang), jnp.sin(ang)
    pad = jnp.zeros((pos.shape[0], LANES - MLA_D_ROPE), F32)
    return jnp.concatenate([cos, cos, pad], axis=1), jnp.concatenate([-sin, sin, pad], axis=1)


def _pick(t, pref):
    return pref if t % pref == 0 else t


def _token_mixers(x, pos, conv_hist, s0, past, wts, x1_all, row0):
    b, t, _ = x.shape
    cs_tab, sn_tab = _rope_tables(pos)
    tt = _pick(t, 512)
    c = min(CHUNK, t)
    qkv, z, gb, q, k, v, latent, k_rope, conv_new = _front(
        x, wts["w_pack"], wts["conv_w"], conv_hist, wts["gpar"], wts["kvnw"], wts["wukv"], cs_tab, sn_tab, tt, c)
    og, s_new = _gdn(qkv, z, gb, s0, wts["gdn_nw"], _pick(t, 4 * CHUNK), c)
    if past is None:
        tb = _pick(t, 1024)
        om = _attn_prompt(q, k, v, tb, _pick(tb, 512))
    else:
        om = _attn_sample(q, k, v, past[0], past[1], wts["wukv"])
    n = b * t
    x1_all = _mixln(og.reshape(n, -1), om.reshape(n, -1), x.reshape(n, D_MODEL), wts["w_out"],
                    wts["ln1_g"], wts["ln1_b"], x1_all, row0, _pick(n, 256))
    return x1_all, latent, k_rope, s_new, conv_new


def _moe(x1_all, n_prompt, wts):
    n = x1_all.shape[0]
    idx, gate, rank, cnt = _router(x1_all, wts["router_w"], wts["router_b"], _pick(n, 256))
    counts = cnt[:, 0].astype(I32)
    padded = (counts + MOE_BM - 1) // MOE_BM * MOE_BM
    pend = jnp.cumsum(padded)
    pstart = pend - padded
    td = _pick(math.gcd(n_prompt, n - n_prompt), 256)
    dest = _dest(idx, rank, pstart.astype(F32).reshape(-1, 1), td)
    dest = dest.reshape(n // td, 1, TOP_K * td)
    n_blocks = n * TOP_K // MOE_BM + N_EXPERTS
    nact = (pend[-1:] // MOE_BM).astype(I32)
    tail = jnp.maximum(pend - MOE_BM, 0).astype(I32)

    xs = _dispatch(tail, dest, x1_all, n_blocks * MOE_BM, td)
    y_sorted = _experts((pstart // MOE_BM).astype(I32), (padded // MOE_BM).astype(I32), nact, xs,
                        wts["exp_wg"], wts["exp_wu"], wts["exp_wd"])
    return _combine(dest, y_sorted, gate.T, x1_all,
                    wts["sh_wg"], wts["sh_wu"], wts["sh_wd"], wts["ln2_g"], wts["ln2_b"], n_prompt, td)


def kernel(x_prompt, x_sample, cache_kv_latent, cache_k_rope, state_gdn, state_conv, w_in, gdn_conv_w, gdn_a_log, gdn_dt_bias, gdn_norm_w, mla_kv_norm_w, mla_w_uk, mla_w_uv, w_out, ln1_g, ln1_b, router_w, router_bias, exp_w_gate, exp_w_up, exp_w_down, shared_w_gate, shared_w_up, shared_w_down, ln2_g, ln2_b):
    assert w_in.shape[0] == 1, "single-layer stack"
    b_p, t_p, _ = x_prompt.shape
    b_s, t_s, _ = x_sample.shape
    past = cache_kv_latent.shape[2]
    l = 0
    pad4 = lambda a: jnp.pad(a.astype(F32), (0, LANES - GDN_HEADS))
    wts = {
        "w_pack": _pack_w_in(w_in[l]),
        "conv_w": gdn_conv_w[l],
        "gpar": jnp.stack([pad4(gdn_a_log[l]), pad4(gdn_dt_bias[l])]),
        "kvnw": mla_kv_norm_w[l].reshape(1, -1),
        "wukv": jnp.concatenate([mla_w_uk[l].reshape(MLA_KV_RANK, -1), mla_w_uv[l].reshape(MLA_KV_RANK, -1)],
                                axis=1).astype(BF16),
        "gdn_nw": gdn_norm_w[l].reshape(1, -1),
        "w_out": w_out[l].astype(BF16),
        "ln1_g": ln1_g[l].reshape(1, -1), "ln1_b": ln1_b[l].reshape(1, -1),
        "router_w": router_w[l].astype(BF16), "router_b": router_bias[l].reshape(-1, 1),
        "exp_wg": exp_w_gate[l], "exp_wu": exp_w_up[l], "exp_wd": exp_w_down[l],
        "sh_wg": shared_w_gate[l].astype(BF16), "sh_wu": shared_w_up[l].astype(BF16),
        "sh_wd": shared_w_down[l].astype(BF16),
        "ln2_g": ln2_g[l].reshape(1, -1), "ln2_b": ln2_b[l].reshape(1, -1),
    }
    n_p, n_s = b_p * t_p, b_s * t_s
    x1_all = jnp.zeros((n_p + n_s, D_MODEL), F32)
    conv0 = jnp.zeros((b_p, GDN_CONV - 1, GDN_CONV_CH), F32)
    s0 = jnp.zeros((b_p, GDN_HEADS, GDN_DK, GDN_DV), F32)
    x1_all, lat_p, kr_p, sg_p, cv_p = _token_mixers(x_prompt, jnp.arange(t_p), conv0, s0, None, wts, x1_all, 0)
    x1_all, lat_s, kr_s, sg_s, cv_s = _token_mixers(
        x_sample, past + jnp.arange(t_s), state_conv[l], state_gdn[l],
        (cache_kv_latent[l], cache_k_rope[l]), wts, x1_all, n_p)
    y_p, y_s = _moe(x1_all, n_p, wts)
    return (y_p.reshape(b_p, t_p, D_MODEL), y_s.reshape(b_s, t_s, D_MODEL),
            lat_p[None], kr_p[None], sg_p[None], cv_p[None],
            lat_s[None], kr_s[None], sg_s[None], cv_s[None])
```

```python
import functools
import math

import jax
import jax.numpy as jnp
from jax import lax
from jax.experimental import pallas as pl
from jax.experimental.pallas import tpu as pltpu

F32 = jnp.float32
BF16 = jnp.bfloat16
I32 = jnp.int32

D_MODEL = 1024
CHUNK = 64
GDN_HEADS = 4
GDN_DK = 128
GDN_DV = 128
GDN_CONV = 4
GDN_QK_W = GDN_HEADS * GDN_DK
GDN_V_W = GDN_HEADS * GDN_DV
GDN_CONV_CH = 2 * GDN_QK_W + GDN_V_W
MLA_HEADS = 4
MLA_D_NOPE = 128
MLA_D_ROPE = 64
MLA_D_V = 128
MLA_KV_RANK = 256
MLA_SCALE = (MLA_D_NOPE + MLA_D_ROPE) ** -0.5
ROPE_THETA = 10000.0
N_EXPERTS = 256
N_GROUPS = 8
GROUP_SIZE = N_EXPERTS // N_GROUPS
TOPK_GROUPS = 4
TOP_K = 8
EXP_HIDDEN = 256
ROUTED_SCALE = 2.5
DEPTH = 1
DEEPNORM_ALPHA = (2.0 * DEPTH) ** 0.25
LN_EPS = 1e-5
RMS_EPS = 1e-6
L2_EPS = 1e-6

LANES = 128
PK_QKV = 0
PK_Z = PK_QKV + GDN_CONV_CH
PK_QNOPE = PK_Z + GDN_V_W
PK_QROPE = PK_QNOPE + MLA_HEADS * MLA_D_NOPE
PK_CKV = PK_QROPE + MLA_HEADS * LANES
PK_KROPE = PK_CKV + MLA_KV_RANK
PK_AB = PK_KROPE + LANES
PK_W = PK_AB + LANES
MLA_QK_W = 2 * LANES

MOE_BM = 256
EXPERT_RING = 4
VMEM_LIMIT = 56 * 1024 * 1024


def _mm(a, b):
    return jnp.dot(a.astype(BF16), b.astype(BF16), preferred_element_type=F32)


def _mm_nt(a, b):
    return lax.dot_general(a.astype(BF16), b.astype(BF16), (((1,), (1,)), ((), ())),
                           preferred_element_type=F32)


def _mm_tn(a, b):
    return lax.dot_general(a.astype(BF16), b.astype(BF16), (((0,), (0,)), ((), ())),
                           preferred_element_type=F32)


def _split3(x):
    hi = x.astype(BF16)
    r = x - hi.astype(F32)
    mid = r.astype(BF16)
    lo = (r - mid.astype(F32)).astype(BF16)
    return hi, mid, lo


def _sigmoid(x):
    return 1.0 / (1.0 + jnp.exp(-x))


def _silu(x):
    return x * _sigmoid(x)


def _softplus(x):
    return jnp.maximum(x, 0.0) + jnp.log1p(jnp.exp(-jnp.abs(x)))


def _rope(x, cs, sn):
    w = x.shape[-1]
    n = w // LANES
    if n > 1:
        cs = jnp.concatenate([cs] * n, axis=1)
        sn = jnp.concatenate([sn] * n, axis=1)
    lane = lax.broadcasted_iota(I32, x.shape, 1) & (LANES - 1)
    half = MLA_D_ROPE // 2
    swapped = jnp.where(lane < half, pltpu.roll(x, w - half, 1), pltpu.roll(x, half, 1))
    return x * cs + swapped * sn


def _params(*sem):
    return pltpu.CompilerParams(dimension_semantics=sem, vmem_limit_bytes=VMEM_LIMIT)


def _front_kernel(x_ref, w_ref, convw_ref, hist_ref, gpar_ref, kvnw_ref, wukv_ref, cs_ref, sn_ref,
                  qkv_ref, z_ref, gb_ref, q_ref, k_ref, v_ref, lat_ref, kr_ref, convnew_ref,
                  xp_scr, *, tt, c):
    t = pl.program_id(1)
    hrow = 8 - (GDN_CONV - 1)

    @pl.when(t == 0)
    def _():
        xp_scr[hrow:8, :] = hist_ref[0]

    proj = _mm(x_ref[0], w_ref[...])

    raw = proj[:, PK_QKV:PK_Z]
    xp_scr[8:8 + tt, :] = raw
    cw = convw_ref[...]
    y = raw * cw[GDN_CONV - 1:GDN_CONV]
    for i in range(GDN_CONV - 1):
        y = y + xp_scr[hrow + i:hrow + i + tt, :] * cw[i:i + 1]
    tail = xp_scr[tt + hrow:tt + 8, :]
    convnew_ref[0] = tail
    xp_scr[hrow:8, :] = tail
    qkv = _silu(y)
    for h in range(2 * GDN_HEADS):
        xh = qkv[:, h * GDN_DK:(h + 1) * GDN_DK]
        xh = xh * lax.rsqrt(jnp.sum(xh * xh, axis=-1, keepdims=True) + L2_EPS)
        if h < GDN_HEADS:
            xh = xh * GDN_DK ** -0.5
        qkv_ref[0, :, h * GDN_DK:(h + 1) * GDN_DK] = xh
    qkv_ref[0, :, 2 * GDN_QK_W:] = qkv[:, 2 * GDN_QK_W:]
    z_ref[0] = proj[:, PK_Z:PK_QNOPE]

    ab = proj[:, PK_AB:PK_W]
    gpar = gpar_ref[...]
    gc = -jnp.exp(gpar[0:1]) * _softplus(ab + gpar[1:2])
    pos = lax.broadcasted_iota(I32, ab.shape, 0) & (c - 1)
    step = 1
    while step < c:
        gc = gc + jnp.where(pos >= step, pltpu.roll(gc, step, 0), 0.0)
        step *= 2
    beta = _sigmoid(ab)
    lane = lax.broadcasted_iota(I32, ab.shape, 1)
    gb_ref[0] = jnp.where(lane < GDN_HEADS, gc, jnp.where(lane < 2 * GDN_HEADS, beta, 0.0))

    cs = cs_ref[...]
    sn = sn_ref[...]
    q_nope = proj[:, PK_QNOPE:PK_QROPE]
    q_rope = _rope(proj[:, PK_QROPE:PK_CKV], cs, sn)
    c_raw = proj[:, PK_CKV:PK_KROPE]
    latent = c_raw * lax.rsqrt(jnp.mean(c_raw * c_raw, axis=-1, keepdims=True) + RMS_EPS) * kvnw_ref[...]
    lat_ref[0] = latent
    k_rope = _rope(proj[:, PK_KROPE:PK_AB], cs, sn)
    kr_ref[0] = k_rope[:, :MLA_D_ROPE]
    kv = _mm(latent, wukv_ref[...])
    k_rope_b = k_rope.astype(BF16)
    for h in range(MLA_HEADS):
        q_ref[0, h, :, :LANES] = q_nope[:, h * LANES:(h + 1) * LANES].astype(BF16)
        q_ref[0, h, :, LANES:] = q_rope[:, h * LANES:(h + 1) * LANES].astype(BF16)
        k_ref[0, h, :, :LANES] = kv[:, h * LANES:(h + 1) * LANES].astype(BF16)
        k_ref[0, h, :, LANES:] = k_rope_b
        v_ref[0, h] = kv[:, (MLA_HEADS + h) * LANES:(MLA_HEADS + h + 1) * LANES].astype(BF16)


def _front(x, w_pack, conv_w, hist, gpar, kvnw, wukv, cs_tab, sn_tab, tt, c):
    b, t, _ = x.shape
    nt = t // tt
    const2 = lambda bi, ti: (0, 0)
    out_shape = (
        jax.ShapeDtypeStruct((b, t, GDN_CONV_CH), F32),
        jax.ShapeDtypeStruct((b, t, GDN_V_W), F32),
        jax.ShapeDtypeStruct((b, t, LANES), F32),
        jax.ShapeDtypeStruct((b, MLA_HEADS, t, MLA_QK_W), BF16),
        jax.ShapeDtypeStruct((b, MLA_HEADS, t, MLA_QK_W), BF16),
        jax.ShapeDtypeStruct((b, MLA_HEADS, t, MLA_D_V), BF16),
        jax.ShapeDtypeStruct((b, t, MLA_KV_RANK), F32),
        jax.ShapeDtypeStruct((b, t, MLA_D_ROPE), F32),
        jax.ShapeDtypeStruct((b, GDN_CONV - 1, GDN_CONV_CH), F32),
    )
    row3 = lambda w: pl.BlockSpec((1, tt, w), lambda bi, ti: (bi, ti, 0))
    head4 = lambda w: pl.BlockSpec((1, MLA_HEADS, tt, w), lambda bi, ti: (bi, 0, ti, 0))
    return pl.pallas_call(
        functools.partial(_front_kernel, tt=tt, c=c),
        grid=(b, nt),
        in_specs=[
            row3(D_MODEL),
            pl.BlockSpec((D_MODEL, PK_W), const2),
            pl.BlockSpec((GDN_CONV, GDN_CONV_CH), const2),
            pl.BlockSpec((1, GDN_CONV - 1, GDN_CONV_CH), lambda bi, ti: (bi, 0, 0)),
            pl.BlockSpec((2, LANES), const2),
            pl.BlockSpec((1, MLA_KV_RANK), const2),
            pl.BlockSpec((MLA_KV_RANK, 2 * MLA_HEADS * LANES), const2),
            pl.BlockSpec((tt, LANES), lambda bi, ti: (ti, 0)),
            pl.BlockSpec((tt, LANES), lambda bi, ti: (ti, 0)),
        ],
        out_specs=(
            row3(GDN_CONV_CH), row3(GDN_V_W), row3(LANES),
            head4(MLA_QK_W), head4(MLA_QK_W), head4(MLA_D_V),
            row3(MLA_KV_RANK), row3(MLA_D_ROPE),
            pl.BlockSpec((1, GDN_CONV - 1, GDN_CONV_CH), lambda bi, ti: (bi, 0, 0)),
        ),
        out_shape=out_shape,
        scratch_shapes=[pltpu.VMEM((tt + 8, GDN_CONV_CH), F32)],
        compiler_params=_params("arbitrary", "arbitrary"),
        name="front",
    )(x, w_pack, conv_w, hist, gpar, kvnw, wukv, cs_tab, sn_tab)


def _gdn_kernel(qkv_ref, z_ref, gb_ref, s0_ref, nw_ref, og_ref, sout_ref, s_scr, *, tg, c):
    t = pl.program_id(1)
    nh = GDN_HEADS
    r = nh * c
    sh = c.bit_length() - 1

    @pl.when(t == 0)
    def _():
        s_scr[...] = s0_ref[0]

    row = lax.broadcasted_iota(I32, (r, r), 0)
    col = lax.broadcasted_iota(I32, (r, r), 1)
    same = (row >> sh) == (col >> sh)
    incl = same & (row >= col)
    strict = same & (row > col)
    eye = jnp.where(row == col, 1.0, 0.0)
    lane0 = jnp.where(lax.broadcasted_iota(I32, (r, LANES), 1) == 0, 1.0, 0.0).astype(BF16)
    nw = nw_ref[...]
    chunks = range(tg // c)

    def stacked(ref, ci, base):
        return jnp.concatenate(
            [ref[0, ci * c:(ci + 1) * c, base + h * LANES:base + (h + 1) * LANES] for h in range(nh)], axis=0)

    def col_bcast(ci, lane):
        gbc = gb_ref[0, ci * c:(ci + 1) * c, :]
        return jnp.concatenate(
            [jnp.broadcast_to(gbc[:, lane + h:lane + h + 1], (c, LANES)) for h in range(nh)], axis=0)

    def as_col(gc_b):
        return _lane_tile(gc_b, r) if r % LANES == 0 else gc_b[:, :r]

    def as_row(gc_b):
        if r % LANES == 0:
            return as_col(gc_b).T
        return sum(lax.dot_general(lane0, p, (((1,), (1,)), ((), ())), preferred_element_type=F32)
                   for p in _split3(gc_b))

    ks = [stacked(qkv_ref, ci, GDN_QK_W) for ci in chunks]
    gc = [col_bcast(ci, 0) for ci in chunks]
    beta = [col_bcast(ci, nh) for ci in chunks]
    decay, qk_kk = [], []
    for ci in chunks:
        decay.append(jnp.exp(jnp.where(incl, as_col(gc[ci]) - as_row(gc[ci]), -jnp.inf)))
        qk_kk.append(_mm_nt(jnp.concatenate([stacked(qkv_ref, ci, 0), ks[ci]], axis=0), ks[ci]))
    intra = [qk_kk[ci][:r] * decay[ci] for ci in chunks]
    n_pow = [jnp.where(strict, -as_col(beta[ci]) * qk_kk[ci][r:] * decay[ci], 0.0)
             for ci in chunks]
    t_inv = [eye + n_pow[ci] for ci in chunks]
    for _ in range(sh - 1):
        n_pow = [_mm(n_pow[ci], n_pow[ci]) for ci in chunks]
        t_inv = [t_inv[ci] + _mm(t_inv[ci], n_pow[ci]) for ci in chunks]
    egc = [jnp.exp(gc[ci]) for ci in chunks]
    uw = [_mm(t_inv[ci], jnp.concatenate([stacked(qkv_ref, ci, 2 * GDN_QK_W) * beta[ci],
                                          ks[ci] * beta[ci] * egc[ci]], axis=1)) for ci in chunks]

    for ci in chunks:
        u = uw[ci][:, :GDN_DV]
        w = uw[ci][:, GDN_DV:]
        qd = stacked(qkv_ref, ci, 0) * egc[ci]
        vn, qs_s = [], []
        for h in range(nh):
            hs = slice(h * c, (h + 1) * c)
            s_h = s_scr[h]
            wq = _mm(jnp.concatenate([w[hs], qd[hs]], axis=0), s_h)
            vn_h = u[hs] - wq[:c]
            g_last = gc[ci][h * c + c - 1:h * c + c, :]
            kd = ks[ci][hs] * jnp.exp(g_last - gc[ci][hs])
            s_scr[h] = s_h * jnp.exp(g_last) + _mm_tn(kd, vn_h)
            vn.append(vn_h)
            qs_s.append(wq[c:])
        o = jnp.concatenate(qs_s, axis=0) + _mm(intra[ci], jnp.concatenate(vn, axis=0))

        o = o * lax.rsqrt(jnp.mean(o * o, axis=-1, keepdims=True) + RMS_EPS) * nw
        o = o * _silu(stacked(z_ref, ci, 0))
        for h in range(nh):
            og_ref[0, ci * c:(ci + 1) * c, h * LANES:(h + 1) * LANES] = o[h * c:(h + 1) * c].astype(BF16)

    @pl.when(t == pl.num_programs(1) - 1)
    def _():
        sout_ref[0] = s_scr[...]


def _gdn(qkv, z, gb, s0, nw, tg, c):
    b, t, _ = qkv.shape
    row3 = lambda w: pl.BlockSpec((1, tg, w), lambda bi, ti: (bi, ti, 0))
    st = pl.BlockSpec((1, GDN_HEADS, GDN_DK, GDN_DV), lambda bi, ti: (bi, 0, 0, 0))
    return pl.pallas_call(
        functools.partial(_gdn_kernel, tg=tg, c=c),
        grid=(b, t // tg),
        in_specs=[row3(GDN_CONV_CH), row3(GDN_V_W), row3(LANES), st,
                  pl.BlockSpec((1, GDN_DV), lambda bi, ti: (0, 0))],
        out_specs=(row3(GDN_V_W), st),
        out_shape=(jax.ShapeDtypeStruct((b, t, GDN_V_W), BF16),
                   jax.ShapeDtypeStruct((b, GDN_HEADS, GDN_DK, GDN_DV), F32)),
        scratch_shapes=[pltpu.VMEM((GDN_HEADS, GDN_DK, GDN_DV), F32)],
        compiler_params=_params("arbitrary", "arbitrary"),
        name="gdn",
    )(qkv, z, gb, s0, nw)


ATTN_ROW_BLOCK = 32


def _lane_tile(x, width):
    return x if width == LANES else jnp.concatenate([x] * (width // LANES), axis=1)


def _attn_kernel(qi_ref, ki_ref, q_ref, k_ref, v_ref, o_ref, m_scr, l_scr, acc_scr, a_scr, s_scr, p_scr,
                 *, tb, sub):
    step = pl.program_id(2)
    qi = qi_ref[step]
    ki = ki_ref[step]
    csh = CHUNK.bit_length() - 1
    c2 = MLA_SCALE * math.log2(math.e)

    @pl.when(ki == 0)
    def _():
        m_scr[...] = jnp.full(m_scr.shape, -jnp.inf, F32)
        l_scr[...] = jnp.zeros(l_scr.shape, F32)
        acc_scr[...] = jnp.zeros(acc_scr.shape, F32)

    def update(r0, j, masked):
        rows = slice(r0, tb)
        keys = slice(j * sub, (j + 1) * sub)
        s_scr[rows, :] = lax.dot_general(q_ref[0, 0, rows, :], k_ref[0, 0, keys, :], (((1,), (1,)), ((), ())),
                                         preferred_element_type=F32)
        rb = min(ATTN_ROW_BLOCK, tb - r0)

        def block(i):
            rr = pl.ds(pl.multiple_of(r0 + i * rb, rb), rb)
            s = s_scr[rr, :]
            if masked:
                qc = (r0 + i * rb + lax.broadcasted_iota(I32, s.shape, 0)) >> csh
                kc = (j * sub + lax.broadcasted_iota(I32, s.shape, 1)) >> csh
                s = jnp.where(kc <= qc, s, -jnp.inf)
            return rr, s

        def row_max(i, carry):
            rr, s = block(i)
            m_prev = m_scr[rr, :]
            m_new = jnp.maximum(m_prev, jnp.max(s, axis=-1, keepdims=True))
            a_scr[rr, :] = jnp.exp2((m_prev - m_new) * c2)
            m_scr[rr, :] = m_new
            return carry

        def row_exp(i, carry):
            rr, s = block(i)
            p = jnp.exp2((s - _lane_tile(m_scr[rr, :], sub)) * c2)
            l_scr[rr, :] = a_scr[rr, :] * l_scr[rr, :] + jnp.sum(p, axis=-1, keepdims=True)
            p_scr[rr, :] = p.astype(BF16)
            return carry

        for i in range((tb - r0) // rb):
            row_max(i, 0)
        for i in range((tb - r0) // rb):
            row_exp(i, 0)
        acc_scr[rows, :] = a_scr[rows, :] * acc_scr[rows, :] + jnp.dot(
            p_scr[rows, :], v_ref[0, 0, keys, :], preferred_element_type=F32)

    @pl.when(ki < qi)
    def _():
        for j in range(tb // sub):
            update(0, j, False)

    @pl.when(ki == qi)
    def _():
        for j in range(tb // sub):
            update(j * sub, j, True)
        o_ref[0] = (acc_scr[...] / l_scr[...]).astype(BF16)


def _attn_prompt(q, k, v, tb, sub):
    b, nh, t, _ = q.shape
    nt = t // tb
    pairs = [(qi, ki) for qi in range(nt) for ki in range(qi + 1)]
    qi_of = jnp.asarray([p[0] for p in pairs], I32)
    ki_of = jnp.asarray([p[1] for p in pairs], I32)
    return pl.pallas_call(
        functools.partial(_attn_kernel, tb=tb, sub=sub),
        grid_spec=pltpu.PrefetchScalarGridSpec(
            num_scalar_prefetch=2,
            grid=(b, nh, len(pairs)),
            in_specs=[pl.BlockSpec((1, 1, tb, MLA_QK_W), lambda bi, hi, s, qo, ko: (bi, hi, qo[s], 0)),
                      pl.BlockSpec((1, 1, tb, MLA_QK_W), lambda bi, hi, s, qo, ko: (bi, hi, ko[s], 0)),
                      pl.BlockSpec((1, 1, tb, MLA_D_V), lambda bi, hi, s, qo, ko: (bi, hi, ko[s], 0))],
            out_specs=pl.BlockSpec((1, tb, MLA_D_V), lambda bi, hi, s, qo, ko: (bi, qo[s], hi)),
            scratch_shapes=[pltpu.VMEM((tb, LANES), F32), pltpu.VMEM((tb, LANES), F32),
                            pltpu.VMEM((tb, MLA_D_V), F32), pltpu.VMEM((tb, LANES), F32),
                            pltpu.VMEM((tb, sub), F32), pltpu.VMEM((tb, sub), BF16)]),
        out_shape=jax.ShapeDtypeStruct((b, t, MLA_HEADS * MLA_D_V), BF16),
        compiler_params=_params("arbitrary", "arbitrary", "arbitrary"),
        name="attn_prompt",
    )(qi_of, ki_of, q, k, v)


def _attn_sample_kernel(q_ref, kn_ref, vn_ref, plat_ref, pkr_ref, wukv_ref, o_ref):
    kvp = _mm(plat_ref[0], wukv_ref[...])
    pkr = pkr_ref[0].astype(BF16)
    for h in range(MLA_HEADS):
        q = q_ref[0, h]
        s_past = (_mm_nt(q[:, :MLA_D_NOPE], kvp[:, h * LANES:(h + 1) * LANES])
                  + _mm_nt(q[:, MLA_D_NOPE:MLA_D_NOPE + MLA_D_ROPE], pkr)) * MLA_SCALE
        s_new = _mm_nt(q, kn_ref[0, h]) * MLA_SCALE
        m = jnp.maximum(jnp.max(s_past, axis=-1, keepdims=True), jnp.max(s_new, axis=-1, keepdims=True))
        p_past = jnp.exp(s_past - m)
        p_new = jnp.exp(s_new - m)
        l = jnp.sum(p_past, axis=-1, keepdims=True) + jnp.sum(p_new, axis=-1, keepdims=True)
        o = _mm(p_past, kvp[:, (MLA_HEADS + h) * LANES:(MLA_HEADS + h + 1) * LANES]) + _mm(p_new, vn_ref[0, h])
        o_ref[0, :, h * MLA_D_V:(h + 1) * MLA_D_V] = (o / l).astype(BF16)


def _attn_sample(q, k_new, v_new, past_lat, past_kr, wukv):
    b, nh, ts, _ = q.shape
    past = past_lat.shape[1]
    b4 = lambda w: pl.BlockSpec((1, nh, ts, w), lambda bi: (bi, 0, 0, 0))
    return pl.pallas_call(
        _attn_sample_kernel,
        grid=(b,),
        in_specs=[b4(MLA_QK_W), b4(MLA_QK_W), b4(MLA_D_V),
                  pl.BlockSpec((1, past, MLA_KV_RANK), lambda bi: (bi, 0, 0)),
                  pl.BlockSpec((1, past, MLA_D_ROPE), lambda bi: (bi, 0, 0)),
                  pl.BlockSpec((MLA_KV_RANK, 2 * MLA_HEADS * LANES), lambda bi: (0, 0))],
        out_specs=pl.BlockSpec((1, ts, MLA_HEADS * MLA_D_V), lambda bi: (bi, 0, 0)),
        out_shape=jax.ShapeDtypeStruct((b, ts, MLA_HEADS * MLA_D_V), BF16),
        compiler_params=_params("arbitrary"),
        name="attn_sample",
    )(q, k_new, v_new, past_lat, past_kr, wukv)


def _layernorm(y, g, b):
    mu = jnp.mean(y, axis=-1, keepdims=True)
    d = y - mu
    var = jnp.mean(d * d, axis=-1, keepdims=True)
    return d * lax.rsqrt(var + LN_EPS) * g + b


def _mixln_kernel(og_ref, om_ref, x_ref, w_ref, g_ref, b_ref, prev_ref, o_ref):
    del prev_ref
    mix = (jnp.dot(og_ref[...], w_ref[:GDN_V_W, :], preferred_element_type=F32)
           + jnp.dot(om_ref[...], w_ref[GDN_V_W:, :], preferred_element_type=F32))
    o_ref[...] = _layernorm(DEEPNORM_ALPHA * x_ref[...] + mix, g_ref[...], b_ref[...])


def _mixln(og, om, x, w_out, g, b, x1_all, row0, tm):
    n = x.shape[0]
    blk0 = row0 // tm
    rows = lambda w: pl.BlockSpec((tm, w), lambda i: (i, 0))
    const = lambda s: pl.BlockSpec(s, lambda i: (0, 0))
    return pl.pallas_call(
        _mixln_kernel,
        grid=(n // tm,),
        in_specs=[rows(GDN_V_W), rows(MLA_HEADS * MLA_D_V), rows(D_MODEL),
                  const((D_MODEL, D_MODEL)), const((1, D_MODEL)), const((1, D_MODEL)),
                  pl.BlockSpec(memory_space=pl.ANY)],
        out_specs=pl.BlockSpec((tm, D_MODEL), lambda i: (blk0 + i, 0)),
        out_shape=jax.ShapeDtypeStruct(x1_all.shape, F32),
        input_output_aliases={6: 0},
        compiler_params=_params("arbitrary"),
        name="mixln",
    )(og, om, x, w_out, g, b, x1_all)


def _router_kernel(x_ref, rw_ref, rb_ref, idx_ref, gate_ref, rank_ref, cnt_ref, carry_scr, *, tt):
    s = pl.program_id(0)

    @pl.when(s == 0)
    def _():
        carry_scr[...] = jnp.zeros(carry_scr.shape, F32)

    ninf = -jnp.inf
    big = float(2 * N_EXPERTS)
    scores = _sigmoid(_mm(x_ref[...], rw_ref[...]).T)
    biased = scores + rb_ref[...]
    eio = lax.broadcasted_iota(I32, (N_EXPERTS, tt), 0).astype(F32)

    def first_argmax(vals, io):
        m = jnp.max(vals, axis=0, keepdims=True)
        i = jnp.min(jnp.where(vals == m, io, big), axis=0, keepdims=True)
        return m, i

    gs = []
    for g in range(N_GROUPS):
        blk = biased[g * GROUP_SIZE:(g + 1) * GROUP_SIZE]
        io = (lax.broadcasted_iota(I32, (GROUP_SIZE, tt), 0) + g * GROUP_SIZE).astype(F32)
        m1, i1 = first_argmax(blk, io)
        m2 = jnp.max(jnp.where(io == i1, ninf, blk), axis=0, keepdims=True)
        gs.append(m1 + m2)
    gio = lax.broadcasted_iota(I32, (N_GROUPS, tt), 0).astype(F32)
    gsc = jnp.zeros((N_GROUPS, tt), F32)
    for g in range(N_GROUPS):
        gsc = jnp.where(gio == float(g), gs[g], gsc)
    gsel = jnp.zeros((N_GROUPS, tt), F32)
    for _ in range(TOPK_GROUPS):
        _, gi = first_argmax(gsc, gio)
        hit = gio == gi
        gsel = jnp.where(hit, 1.0, gsel)
        gsc = jnp.where(hit, ninf, gsc)
    masked = jnp.concatenate(
        [jnp.where(jnp.max(jnp.where(gio == float(g), gsel, 0.0), axis=0, keepdims=True) > 0.0,
                   biased[g * GROUP_SIZE:(g + 1) * GROUP_SIZE], ninf) for g in range(N_GROUPS)], axis=0)

    idx, wts = [], []
    sel = jnp.zeros((N_EXPERTS, tt), F32)
    for _ in range(TOP_K):
        _, ei = first_argmax(masked, eio)
        hit = eio == ei
        wts.append(jnp.sum(jnp.where(hit, scores, 0.0), axis=0, keepdims=True))
        masked = jnp.where(hit, ninf, masked)
        sel = jnp.where(hit, 1.0, sel)
        idx.append(ei)
    wsum = wts[0]
    for w in wts[1:]:
        wsum = wsum + w

    t0 = lax.broadcasted_iota(I32, (tt, tt), 0)
    t1 = lax.broadcasted_iota(I32, (tt, tt), 1)
    before = jnp.where(t0 < t1, 1.0, 0.0).astype(BF16)
    sel_b = sel.astype(BF16)
    base = carry_scr[:, :1] + jnp.dot(sel_b, before, preferred_element_type=F32)
    ranks = [jnp.sum(jnp.where(eio == ei, base, 0.0), axis=0, keepdims=True) for ei in idx]
    carry_scr[...] = carry_scr[...] + jnp.dot(sel_b, jnp.ones((tt, LANES), BF16), preferred_element_type=F32)

    for k in range(TOP_K):
        idx_ref[k:k + 1, :] = idx[k].astype(I32)
        gate_ref[k:k + 1, :] = wts[k] / wsum * ROUTED_SCALE
        rank_ref[k:k + 1, :] = ranks[k].astype(I32)
    cnt_ref[...] = carry_scr[...]


def _router(x1, rw, rb, tt):
    n = x1.shape[0]
    kt = lambda dt: jax.ShapeDtypeStruct((TOP_K, n), dt)
    kspec = pl.BlockSpec((TOP_K, tt), lambda i: (0, i))
    return pl.pallas_call(
        functools.partial(_router_kernel, tt=tt),
        grid=(n // tt,),
        in_specs=[pl.BlockSpec((tt, D_MODEL), lambda i: (i, 0)),
                  pl.BlockSpec((D_MODEL, N_EXPERTS), lambda i: (0, 0)),
                  pl.BlockSpec((N_EXPERTS, 1), lambda i: (0, 0))],
        out_specs=(kspec, kspec, kspec, pl.BlockSpec((N_EXPERTS, LANES), lambda i: (0, 0))),
        out_shape=(kt(I32), kt(F32), kt(I32), jax.ShapeDtypeStruct((N_EXPERTS, LANES), F32)),
        scratch_shapes=[pltpu.VMEM((N_EXPERTS, LANES), F32)],
        compiler_params=_params("arbitrary"),
        name="router",
    )(x1, rw, rb)


def _dest_kernel(idx_ref, rank_ref, pstart_ref, dest_ref, *, tt):
    eio = lax.broadcasted_iota(I32, (N_EXPERTS, tt), 0)
    pstart = pstart_ref[...]
    for k in range(TOP_K):
        start = jnp.sum(jnp.where(eio == idx_ref[k:k + 1, :], pstart, 0.0), axis=0, keepdims=True)
        dest_ref[0, k:k + 1, :] = start.astype(I32) + rank_ref[k:k + 1, :]


def _dest(idx, rank, pstart, tt):
    n = idx.shape[1]
    kspec = pl.BlockSpec((TOP_K, tt), lambda i: (0, i))
    return pl.pallas_call(
        functools.partial(_dest_kernel, tt=tt),
        grid=(n // tt,),
        in_specs=[kspec, kspec, pl.BlockSpec((N_EXPERTS, 1), lambda i: (0, 0))],
        out_specs=pl.BlockSpec((1, TOP_K, tt), lambda i: (i, 0, 0)),
        out_shape=jax.ShapeDtypeStruct((n // tt, TOP_K, tt), I32),
        compiler_params=_params("arbitrary"),
        name="dest",
    )(idx, rank, pstart)


ROW_TILE = D_MODEL // LANES
XROW_TILE = ROW_TILE // 2


def _rows_to_tiles(x, tiles_ref, base, n, rt=ROW_TILE):
    for j in range(rt):
        tiles_ref[pl.ds(base * rt + j, n, stride=rt), :] = x[:, j * LANES:(j + 1) * LANES]


def _tiles_to_rows(tiles_ref, base, n, rt=ROW_TILE):
    return jnp.concatenate(
        [tiles_ref[pl.ds(base * rt + j, n, stride=rt), :] for j in range(rt)], axis=1)


def _pack_bf16_pairs(x):
    half = x.shape[1] // 2
    lo = pltpu.bitcast(x[:, :half].astype(BF16).astype(F32), jnp.uint32) >> 16
    hi = pltpu.bitcast(x[:, half:].astype(BF16).astype(F32), jnp.uint32) & jnp.uint32(0xFFFF0000)
    return lo | hi


def _unpack_bf16_pairs(w):
    lo = pltpu.bitcast(w << 16, F32)
    hi = pltpu.bitcast(w & jnp.uint32(0xFFFF0000), F32)
    return jnp.concatenate([lo, hi], axis=1)


def _dispatch_kernel(tail_ref, dest_ref, x_ref, xs_out, zbuf, xt_scr, sem, zsem, *, td):
    s = pl.program_id(0)

    @pl.when(s == 0)
    def _():
        zbuf[...] = jnp.zeros(zbuf.shape, zbuf.dtype)

        def zero_copy(e):
            first = pl.multiple_of(tail_ref[e] * XROW_TILE, MOE_BM * XROW_TILE)
            return pltpu.make_async_copy(zbuf, xs_out.at[pl.ds(first, MOE_BM * XROW_TILE)], zsem)

        def zstart(e, carry):
            zero_copy(e).start()
            return carry

        def zwait(e, carry):
            zero_copy(e).wait()
            return carry

        lax.fori_loop(0, N_EXPERTS, zstart, 0)
        lax.fori_loop(0, N_EXPERTS, zwait, 0)

    slot = s & 1

    def row_copy(slot_, i, d):
        src = pl.multiple_of((slot_ * td + i) * XROW_TILE, XROW_TILE)
        return pltpu.make_async_copy(xt_scr.at[pl.ds(src, XROW_TILE)],
                                     xs_out.at[pl.ds(pl.multiple_of(d * XROW_TILE, XROW_TILE), XROW_TILE)],
                                     sem.at[slot_])

    def drain(slot_):
        def body(i, carry):
            for k in range(TOP_K):
                row_copy(slot_, 0, 0).wait()
            return carry
        lax.fori_loop(0, td, body, 0)

    @pl.when(s >= 2)
    def _():
        drain(slot)

    _rows_to_tiles(_pack_bf16_pairs(x_ref[...]), xt_scr, slot * td, td, XROW_TILE)

    def issue(i, carry):
        for k in range(TOP_K):
            row_copy(slot, i, dest_ref[0, 0, k * td + i]).start(priority=k % 2)
        return carry

    lax.fori_loop(0, td, issue, 0, unroll=4)

    @pl.when(s == pl.num_programs(0) - 1)
    def _():
        drain(slot)

        @pl.when(s >= 1)
        def _():
            drain(1 - slot)


def _dispatch(tail, dest, x1, n_rows, td):
    n = x1.shape[0]
    return pl.pallas_call(
        functools.partial(_dispatch_kernel, td=td),
        grid_spec=pltpu.PrefetchScalarGridSpec(
            num_scalar_prefetch=1,
            grid=(n // td,),
            in_specs=[pl.BlockSpec((1, 1, TOP_K * td), lambda i, tl: (i, 0, 0), memory_space=pltpu.SMEM),
                      pl.BlockSpec((td, D_MODEL), lambda i, tl: (i, 0))],
            out_specs=pl.BlockSpec(memory_space=pl.ANY),
            scratch_shapes=[pltpu.VMEM((MOE_BM * XROW_TILE, LANES), jnp.uint32),
                            pltpu.VMEM((2 * td * XROW_TILE, LANES), jnp.uint32),
                            pltpu.SemaphoreType.DMA((2,)), pltpu.SemaphoreType.DMA(())]),
        out_shape=jax.ShapeDtypeStruct((n_rows * XROW_TILE, LANES), jnp.uint32),
        compiler_params=_params("arbitrary"),
        name="dispatch",
    )(tail, dest, x1)


def _expert_kernel(first_ref, nblk_ref, nact_ref, xs_hbm, wg_ref, wu_ref, wd_ref, y_hbm,
                   xbuf, ybuf, wgb, wub, wdb, xsem, ysem):
    e = pl.program_id(0)
    nact = nact_ref[0]
    ring = EXPERT_RING
    xr = MOE_BM * XROW_TILE
    yr = MOE_BM * XROW_TILE

    def x_copy(g):
        slot = g & (ring - 1)
        return pltpu.make_async_copy(xs_hbm.at[pl.ds(pl.multiple_of(g * xr, xr), xr)],
                                     xbuf.at[pl.ds(pl.multiple_of(slot * xr, xr), xr)], xsem.at[slot])

    def y_copy(g):
        slot = g & (ring - 1)
        return pltpu.make_async_copy(ybuf.at[pl.ds(pl.multiple_of(slot * yr, yr), yr)],
                                     y_hbm.at[pl.ds(pl.multiple_of(g * yr, yr), yr)], ysem.at[slot])

    @pl.when(e == 0)
    def _():
        for g0 in range(ring - 1):
            @pl.when(g0 < nact)
            def _():
                x_copy(g0).start()

    @pl.when(nblk_ref[e] > 0)
    def _():
        wgb[...] = wg_ref[0].astype(BF16)
        wub[...] = wu_ref[0].astype(BF16)
        wdb[...] = wd_ref[0].astype(BF16)

    def block(b, carry):
        g = first_ref[e] + b
        slot = g & (ring - 1)
        x_copy(g).wait()

        @pl.when(g + ring - 1 < nact)
        def _():
            x_copy(g + ring - 1).start()

        @pl.when(g >= ring)
        def _():
            y_copy(g - ring).wait()

        xb = _unpack_bf16_pairs(_tiles_to_rows(xbuf, slot * MOE_BM, MOE_BM, XROW_TILE)).astype(BF16)
        hg = jnp.dot(xb, wgb[...], preferred_element_type=F32)
        hu = jnp.dot(xb, wub[...], preferred_element_type=F32)
        y = jnp.dot((_silu(hg) * hu).astype(BF16), wdb[...], preferred_element_type=F32)
        _rows_to_tiles(_pack_bf16_pairs(y), ybuf, slot * MOE_BM, MOE_BM, XROW_TILE)
        y_copy(g).start()

        @pl.when(g == nact - 1)
        def _():
            for back in range(ring):
                @pl.when(g >= back)
                def _():
                    y_copy(g - back).wait()

        return carry

    lax.fori_loop(0, nblk_ref[e], block, 0)


def _experts(first, nblk, nact, xs, wg, wu, wd):
    n_rows = xs.shape[0] // XROW_TILE
    wspec = lambda s: pl.BlockSpec((1,) + s, lambda e, fr, nb, na: (e, 0, 0))
    return pl.pallas_call(
        _expert_kernel,
        grid_spec=pltpu.PrefetchScalarGridSpec(
            num_scalar_prefetch=3,
            grid=(N_EXPERTS,),
            in_specs=[pl.BlockSpec(memory_space=pl.ANY),
                      wspec((D_MODEL, EXP_HIDDEN)), wspec((D_MODEL, EXP_HIDDEN)), wspec((EXP_HIDDEN, D_MODEL))],
            out_specs=pl.BlockSpec(memory_space=pl.ANY),
            scratch_shapes=[pltpu.VMEM((EXPERT_RING * MOE_BM * XROW_TILE, LANES), jnp.uint32),
                            pltpu.VMEM((EXPERT_RING * MOE_BM * XROW_TILE, LANES), jnp.uint32),
                            pltpu.VMEM((D_MODEL, EXP_HIDDEN), BF16), pltpu.VMEM((D_MODEL, EXP_HIDDEN), BF16),
                            pltpu.VMEM((EXP_HIDDEN, D_MODEL), BF16),
                            pltpu.SemaphoreType.DMA((EXPERT_RING,)), pltpu.SemaphoreType.DMA((EXPERT_RING,))]),
        out_shape=jax.ShapeDtypeStruct((n_rows * XROW_TILE, LANES), jnp.uint32),
        compiler_params=_params("arbitrary"),
        name="experts",
    )(first, nblk, nact, xs, wg, wu, wd)


def _combine_kernel(dcur_ref, dnxt_ref, y_hbm, gate_ref, x_ref, wsg_ref, wsu_ref, wsd_ref, g_ref, b_ref,
                    outp_ref, outs_ref, buf, sem, *, tc, np_tiles):
    s = pl.program_id(0)
    ns = pl.num_programs(0)
    slot = s % 2

    def row_copy(d, slot_, k, i):
        dst = pl.multiple_of(((slot_ * TOP_K + k) * tc + i) * XROW_TILE, XROW_TILE)
        return pltpu.make_async_copy(y_hbm.at[pl.ds(pl.multiple_of(d * XROW_TILE, XROW_TILE), XROW_TILE)],
                                     buf.at[pl.ds(dst, XROW_TILE)], sem.at[slot_])

    def issue(dref, slot_):
        def body(i, carry):
            for k in range(TOP_K):
                row_copy(dref[0, 0, k * tc + i], slot_, k, i).start(priority=k % 2)
            return carry
        lax.fori_loop(0, tc, body, 0, unroll=4)

    @pl.when(s == 0)
    def _():
        issue(dcur_ref, 0)

    @pl.when(s + 1 < ns)
    def _():
        issue(dnxt_ref, 1 - slot)

    def drain(i, carry):
        for k in range(TOP_K):
            row_copy(0, slot, k, i).wait()
        return carry

    lax.fori_loop(0, tc, drain, 0)

    x = x_ref[...]
    gate = gate_ref[...]
    def expert_rows(k):
        return _unpack_bf16_pairs(_tiles_to_rows(buf, (slot * TOP_K + k) * tc, tc, XROW_TILE))

    routed = expert_rows(0) * gate[:, 0:1]
    for k in range(1, TOP_K):
        routed = routed + expert_rows(k) * gate[:, k:k + 1]
    xb = x.astype(BF16)
    shared = _mm(_silu(_mm(xb, wsg_ref[...])) * _mm(xb, wsu_ref[...]), wsd_ref[...])
    out = _layernorm(DEEPNORM_ALPHA * x + (routed + shared), g_ref[...], b_ref[...])

    @pl.when(s < np_tiles)
    def _():
        outp_ref[...] = out

    @pl.when(s >= np_tiles)
    def _():
        outs_ref[...] = out


def _combine(dest, y_sorted, gate, x1, wsg, wsu, wsd, g, b, n_prompt, tc):
    n = x1.shape[0]
    ns = n // tc
    np_tiles = n_prompt // tc
    const = lambda s: pl.BlockSpec(s, lambda i: (0, 0))
    dspec = lambda f: pl.BlockSpec((1, 1, TOP_K * tc), f, memory_space=pltpu.SMEM)
    return pl.pallas_call(
        functools.partial(_combine_kernel, tc=tc, np_tiles=np_tiles),
        grid=(ns,),
        in_specs=[dspec(lambda i: (i, 0, 0)), dspec(lambda i: (jnp.minimum(i + 1, ns - 1), 0, 0)),
                  pl.BlockSpec(memory_space=pl.ANY),
                  pl.BlockSpec((tc, TOP_K), lambda i: (i, 0)),
                  pl.BlockSpec((tc, D_MODEL), lambda i: (i, 0)),
                  const((D_MODEL, EXP_HIDDEN)), const((D_MODEL, EXP_HIDDEN)), const((EXP_HIDDEN, D_MODEL)),
                  const((1, D_MODEL)), const((1, D_MODEL))],
        out_specs=(pl.BlockSpec((tc, D_MODEL), lambda i: (jnp.minimum(i, np_tiles - 1), 0)),
                   pl.BlockSpec((tc, D_MODEL), lambda i: (jnp.maximum(i - np_tiles, 0), 0))),
        out_shape=(jax.ShapeDtypeStruct((n_prompt, D_MODEL), F32),
                   jax.ShapeDtypeStruct((n - n_prompt, D_MODEL), F32)),
        scratch_shapes=[pltpu.VMEM((2 * TOP_K * tc * XROW_TILE, LANES), jnp.uint32),
                        pltpu.SemaphoreType.DMA((2,))],
        compiler_params=_params("arbitrary"),
        name="combine",
    )(dest, dest, y_sorted, gate, x1, wsg, wsu, wsd, g, b)


def _pack_w_in(w_in):
    d = w_in.shape[0]
    o_z = GDN_CONV_CH
    o_a = o_z + GDN_V_W
    o_b = o_a + GDN_HEADS
    o_q = o_b + GDN_HEADS
    o_c = o_q + MLA_HEADS * (MLA_D_NOPE + MLA_D_ROPE)
    o_kr = o_c + MLA_KV_RANK
    zeros = lambda w: jnp.zeros((d, w), w_in.dtype)
    wq = w_in[:, o_q:o_c].reshape(d, MLA_HEADS, MLA_D_NOPE + MLA_D_ROPE)
    q_nope = wq[:, :, :MLA_D_NOPE].reshape(d, MLA_HEADS * MLA_D_NOPE)
    q_rope = jnp.pad(wq[:, :, MLA_D_NOPE:], ((0, 0), (0, 0), (0, LANES - MLA_D_ROPE))).reshape(d, MLA_HEADS * LANES)
    cols = [w_in[:, :o_a], q_nope, q_rope, w_in[:, o_c:o_kr], w_in[:, o_kr:], zeros(LANES - MLA_D_ROPE),
            w_in[:, o_a:o_q], zeros(LANES - 2 * GDN_HEADS)]
    return jnp.concatenate(cols, axis=1).astype(BF16)


def _rope_tables(pos):
    inv_freq = ROPE_THETA ** (-jnp.arange(0, MLA_D_ROPE, 2, dtype=F32) / MLA_D_ROPE)
    ang = pos.astype(F32)[:, None] * inv_freq[None, :]
    cos, sin = jnp.cos(ang), jnp.sin(ang)
    pad = jnp.zeros((pos.shape[0], LANES - MLA_D_ROPE), F32)
    return jnp.concatenate([cos, cos, pad], axis=1), jnp.concatenate([-sin, sin, pad], axis=1)


def _pick(t, pref):
    return pref if t % pref == 0 else t


def _token_mixers(x, pos, conv_hist, s0, past, wts, x1_all, row0):
    b, t, _ = x.shape
    cs_tab, sn_tab = _rope_tables(pos)
    tt = _pick(t, 512)
    c = min(CHUNK, t)
    qkv, z, gb, q, k, v, latent, k_rope, conv_new = _front(
        x, wts["w_pack"], wts["conv_w"], conv_hist, wts["gpar"], wts["kvnw"], wts["wukv"], cs_tab, sn_tab, tt, c)
    og, s_new = _gdn(qkv, z, gb, s0, wts["gdn_nw"], _pick(t, 4 * CHUNK), c)
    if past is None:
        tb = _pick(t, 1024)
        om = _attn_prompt(q, k, v, tb, _pick(tb, 512))
    else:
        om = _attn_sample(q, k, v, past[0], past[1], wts["wukv"])
    n = b * t
    x1_all = _mixln(og.reshape(n, -1), om.reshape(n, -1), x.reshape(n, D_MODEL), wts["w_out"],
                    wts["ln1_g"], wts["ln1_b"], x1_all, row0, _pick(n, 256))
    return x1_all, latent, k_rope, s_new, conv_new


def _moe(x1_all, n_prompt, wts):
    n = x1_all.shape[0]
    idx, gate, rank, cnt = _router(x1_all, wts["router_w"], wts["router_b"], _pick(n, 256))
    counts = cnt[:, 0].astype(I32)
    padded = (counts + MOE_BM - 1) // MOE_BM * MOE_BM
    pend = jnp.cumsum(padded)
    pstart = pend - padded
    td = _pick(math.gcd(n_prompt, n - n_prompt), 256)
    dest = _dest(idx, rank, pstart.astype(F32).reshape(-1, 1), td)
    dest = dest.reshape(n // td, 1, TOP_K * td)
    n_blocks = n * TOP_K // MOE_BM + N_EXPERTS
    nact = (pend[-1:] // MOE_BM).astype(I32)
    tail = jnp.maximum(pend - MOE_BM, 0).astype(I32)

    xs = _dispatch(tail, dest, x1_all, n_blocks * MOE_BM, td)
    y_sorted = _experts((pstart // MOE_BM).astype(I32), (padded // MOE_BM).astype(I32), nact, xs,
                        wts["exp_wg"], wts["exp_wu"], wts["exp_wd"])
    return _combine(dest, y_sorted, gate.T, x1_all,
                    wts["sh_wg"], wts["sh_wu"], wts["sh_wd"], wts["ln2_g"], wts["ln2_b"], n_prompt, td)


def kernel(x_prompt, x_sample, cache_kv_latent, cache_k_rope, state_gdn, state_conv, w_in, gdn_conv_w, gdn_a_log, gdn_dt_bias, gdn_norm_w, mla_kv_norm_w, mla_w_uk, mla_w_uv, w_out, ln1_g, ln1_b, router_w, router_bias, exp_w_gate, exp_w_up, exp_w_down, shared_w_gate, shared_w_up, shared_w_down, ln2_g, ln2_b):
    assert w_in.shape[0] == 1, "single-layer stack"
    b_p, t_p, _ = x_prompt.shape
    b_s, t_s, _ = x_sample.shape
    past = cache_kv_latent.shape[2]
    l = 0
    pad4 = lambda a: jnp.pad(a.astype(F32), (0, LANES - GDN_HEADS))
    wts = {
        "w_pack": _pack_w_in(w_in[l]),
        "conv_w": gdn_conv_w[l],
        "gpar": jnp.stack([pad4(gdn_a_log[l]), pad4(gdn_dt_bias[l])]),
        "kvnw": mla_kv_norm_w[l].reshape(1, -1),
        "wukv": jnp.concatenate([mla_w_uk[l].reshape(MLA_KV_RANK, -1), mla_w_uv[l].reshape(MLA_KV_RANK, -1)],
                                axis=1).astype(BF16),
        "gdn_nw": gdn_norm_w[l].reshape(1, -1),
        "w_out": w_out[l].astype(BF16),
        "ln1_g": ln1_g[l].reshape(1, -1), "ln1_b": ln1_b[l].reshape(1, -1),
        "router_w": router_w[l].astype(BF16), "router_b": router_bias[l].reshape(-1, 1),
        "exp_wg": exp_w_gate[l], "exp_wu": exp_w_up[l], "exp_wd": exp_w_down[l],
        "sh_wg": shared_w_gate[l].astype(BF16), "sh_wu": shared_w_up[l].astype(BF16),
        "sh_wd": shared_w_down[l].astype(BF16),
        "ln2_g": ln2_g[l].reshape(1, -1), "ln2_b": ln2_b[l].reshape(1, -1),
    }
    n_p, n_s = b_p * t_p, b_s * t_s
    x1_all = jnp.zeros((n_p + n_s, D_MODEL), F32)
    conv0 = jnp.zeros((b_p, GDN_CONV - 1, GDN_CONV_CH), F32)
    s0 = jnp.zeros((b_p, GDN_HEADS, GDN_DK, GDN_DV), F32)
    x1_all, lat_p, kr_p, sg_p, cv_p = _token_mixers(x_prompt, jnp.arange(t_p), conv0, s0, None, wts, x1_all, 0)
    x1_all, lat_s, kr_s, sg_s, cv_s = _token_mixers(
        x_sample, past + jnp.arange(t_s), state_conv[l], state_gdn[l],
        (cache_kv_latent[l], cache_k_rope[l]), wts, x1_all, n_p)
    y_p, y_s = _moe(x1_all, n_p, wts)
    return (y_p.reshape(b_p, t_p, D_MODEL), y_s.reshape(b_s, t_s, D_MODEL),
            lat_p[None], kr_p[None], sg_p[None], cv_p[None],
            lat_s[None], kr_s[None], sg_s[None], cv_s[None])
```

```python
import functools
import math

import jax
import jax.numpy as jnp
from jax import lax
from jax.experimental import pallas as pl
from jax.experimental.pallas import tpu as pltpu

F32 = jnp.float32
BF16 = jnp.bfloat16
I32 = jnp.int32

D_MODEL = 1024
CHUNK = 64
GDN_HEADS = 4
GDN_DK = 128
GDN_DV = 128
GDN_CONV = 4
GDN_QK_W = GDN_HEADS * GDN_DK
GDN_V_W = GDN_HEADS * GDN_DV
GDN_CONV_CH = 2 * GDN_QK_W + GDN_V_W
MLA_HEADS = 4
MLA_D_NOPE = 128
MLA_D_ROPE = 64
MLA_D_V = 128
MLA_KV_RANK = 256
MLA_SCALE = (MLA_D_NOPE + MLA_D_ROPE) ** -0.5
ROPE_THETA = 10000.0
N_EXPERTS = 256
N_GROUPS = 8
GROUP_SIZE = N_EXPERTS // N_GROUPS
TOPK_GROUPS = 4
TOP_K = 8
EXP_HIDDEN = 256
ROUTED_SCALE = 2.5
DEPTH = 1
DEEPNORM_ALPHA = (2.0 * DEPTH) ** 0.25
LN_EPS = 1e-5
RMS_EPS = 1e-6
L2_EPS = 1e-6

LANES = 128
PK_QKV = 0
PK_Z = PK_QKV + GDN_CONV_CH
PK_QNOPE = PK_Z + GDN_V_W
PK_QROPE = PK_QNOPE + MLA_HEADS * MLA_D_NOPE
PK_CKV = PK_QROPE + MLA_HEADS * LANES
PK_KROPE = PK_CKV + MLA_KV_RANK
PK_AB = PK_KROPE + LANES
PK_W = PK_AB + LANES
MLA_QK_W = 2 * LANES

MOE_BM = 256
EXPERT_RING = 4
VMEM_LIMIT = 56 * 1024 * 1024


def _mm(a, b):
    return jnp.dot(a.astype(BF16), b.astype(BF16), preferred_element_type=F32)


def _mm_nt(a, b):
    return lax.dot_general(a.astype(BF16), b.astype(BF16), (((1,), (1,)), ((), ())),
                           preferred_element_type=F32)


def _mm_tn(a, b):
    return lax.dot_general(a.astype(BF16), b.astype(BF16), (((0,), (0,)), ((), ())),
                           preferred_element_type=F32)


def _split3(x):
    hi = x.astype(BF16)
    r = x - hi.astype(F32)
    mid = r.astype(BF16)
    lo = (r - mid.astype(F32)).astype(BF16)
    return hi, mid, lo


def _sigmoid(x):
    return 1.0 / (1.0 + jnp.exp(-x))


def _silu(x):
    return x * _sigmoid(x)


def _softplus(x):
    return jnp.maximum(x, 0.0) + jnp.log1p(jnp.exp(-jnp.abs(x)))


def _rope(x, cs, sn):
    w = x.shape[-1]
    n = w // LANES
    if n > 1:
        cs = jnp.concatenate([cs] * n, axis=1)
        sn = jnp.concatenate([sn] * n, axis=1)
    lane = lax.broadcasted_iota(I32, x.shape, 1) & (LANES - 1)
    half = MLA_D_ROPE // 2
    swapped = jnp.where(lane < half, pltpu.roll(x, w - half, 1), pltpu.roll(x, half, 1))
    return x * cs + swapped * sn


def _params(*sem):
    return pltpu.CompilerParams(dimension_semantics=sem, vmem_limit_bytes=VMEM_LIMIT)


def _front_kernel(x_ref, w_ref, convw_ref, hist_ref, gpar_ref, kvnw_ref, wukv_ref, cs_ref, sn_ref,
                  qkv_ref, z_ref, gb_ref, q_ref, k_ref, v_ref, lat_ref, kr_ref, convnew_ref,
                  xp_scr, *, tt, c):
    t = pl.program_id(1)
    hrow = 8 - (GDN_CONV - 1)

    @pl.when(t == 0)
    def _():
        xp_scr[hrow:8, :] = hist_ref[0]

    proj = _mm(x_ref[0], w_ref[...])

    raw = proj[:, PK_QKV:PK_Z]
    xp_scr[8:8 + tt, :] = raw
    cw = convw_ref[...]
    y = raw * cw[GDN_CONV - 1:GDN_CONV]
    for i in range(GDN_CONV - 1):
        y = y + xp_scr[hrow + i:hrow + i + tt, :] * cw[i:i + 1]
    tail = xp_scr[tt + hrow:tt + 8, :]
    convnew_ref[0] = tail
    xp_scr[hrow:8, :] = tail
    qkv = _silu(y)
    for h in range(2 * GDN_HEADS):
        xh = qkv[:, h * GDN_DK:(h + 1) * GDN_DK]
        xh = xh * lax.rsqrt(jnp.sum(xh * xh, axis=-1, keepdims=True) + L2_EPS)
        if h < GDN_HEADS:
            xh = xh * GDN_DK ** -0.5
        qkv_ref[0, :, h * GDN_DK:(h + 1) * GDN_DK] = xh
    qkv_ref[0, :, 2 * GDN_QK_W:] = qkv[:, 2 * GDN_QK_W:]
    z_ref[0] = proj[:, PK_Z:PK_QNOPE]

    ab = proj[:, PK_AB:PK_W]
    gpar = gpar_ref[...]
    gc = -jnp.exp(gpar[0:1]) * _softplus(ab + gpar[1:2])
    pos = lax.broadcasted_iota(I32, ab.shape, 0) & (c - 1)
    step = 1
    while step < c:
        gc = gc + jnp.where(pos >= step, pltpu.roll(gc, step, 0), 0.0)
        step *= 2
    beta = _sigmoid(ab)
    lane = lax.broadcasted_iota(I32, ab.shape, 1)
    gb_ref[0] = jnp.where(lane < GDN_HEADS, gc, jnp.where(lane < 2 * GDN_HEADS, beta, 0.0))

    cs = cs_ref[...]
    sn = sn_ref[...]
    q_nope = proj[:, PK_QNOPE:PK_QROPE]
    q_rope = _rope(proj[:, PK_QROPE:PK_CKV], cs, sn)
    c_raw = proj[:, PK_CKV:PK_KROPE]
    latent = c_raw * lax.rsqrt(jnp.mean(c_raw * c_raw, axis=-1, keepdims=True) + RMS_EPS) * kvnw_ref[...]
    lat_ref[0] = latent
    k_rope = _rope(proj[:, PK_KROPE:PK_AB], cs, sn)
    kr_ref[0] = k_rope[:, :MLA_D_ROPE]
    kv = _mm(latent, wukv_ref[...])
    k_rope_b = k_rope.astype(BF16)
    for h in range(MLA_HEADS):
        q_ref[0, h, :, :LANES] = q_nope[:, h * LANES:(h + 1) * LANES].astype(BF16)
        q_ref[0, h, :, LANES:] = q_rope[:, h * LANES:(h + 1) * LANES].astype(BF16)
        k_ref[0, h, :, :LANES] = kv[:, h * LANES:(h + 1) * LANES].astype(BF16)
        k_ref[0, h, :, LANES:] = k_rope_b
        v_ref[0, h] = kv[:, (MLA_HEADS + h) * LANES:(MLA_HEADS + h + 1) * LANES].astype(BF16)


def _front(x, w_pack, conv_w, hist, gpar, kvnw, wukv, cs_tab, sn_tab, tt, c):
    b, t, _ = x.shape
    nt = t // tt
    const2 = lambda bi, ti: (0, 0)
    out_shape = (
        jax.ShapeDtypeStruct((b, t, GDN_CONV_CH), F32),
        jax.ShapeDtypeStruct((b, t, GDN_V_W), F32),
        jax.ShapeDtypeStruct((b, t, LANES), F32),
        jax.ShapeDtypeStruct((b, MLA_HEADS, t, MLA_QK_W), BF16),
        jax.ShapeDtypeStruct((b, MLA_HEADS, t, MLA_QK_W), BF16),
        jax.ShapeDtypeStruct((b, MLA_HEADS, t, MLA_D_V), BF16),
        jax.ShapeDtypeStruct((b, t, MLA_KV_RANK), F32),
        jax.ShapeDtypeStruct((b, t, MLA_D_ROPE), F32),
        jax.ShapeDtypeStruct((b, GDN_CONV - 1, GDN_CONV_CH), F32),
    )
    row3 = lambda w: pl.BlockSpec((1, tt, w), lambda bi, ti: (bi, ti, 0))
    head4 = lambda w: pl.BlockSpec((1, MLA_HEADS, tt, w), lambda bi, ti: (bi, 0, ti, 0))
    return pl.pallas_call(
        functools.partial(_front_kernel, tt=tt, c=c),
        grid=(b, nt),
        in_specs=[
            row3(D_MODEL),
            pl.BlockSpec((D_MODEL, PK_W), const2),
            pl.BlockSpec((GDN_CONV, GDN_CONV_CH), const2),
            pl.BlockSpec((1, GDN_CONV - 1, GDN_CONV_CH), lambda bi, ti: (bi, 0, 0)),
            pl.BlockSpec((2, LANES), const2),
            pl.BlockSpec((1, MLA_KV_RANK), const2),
            pl.BlockSpec((MLA_KV_RANK, 2 * MLA_HEADS * LANES), const2),
            pl.BlockSpec((tt, LANES), lambda bi, ti: (ti, 0)),
            pl.BlockSpec((tt, LANES), lambda bi, ti: (ti, 0)),
        ],
        out_specs=(
            row3(GDN_CONV_CH), row3(GDN_V_W), row3(LANES),
            head4(MLA_QK_W), head4(MLA_QK_W), head4(MLA_D_V),
            row3(MLA_KV_RANK), row3(MLA_D_ROPE),
            pl.BlockSpec((1, GDN_CONV - 1, GDN_CONV_CH), lambda bi, ti: (bi, 0, 0)),
        ),
        out_shape=out_shape,
        scratch_shapes=[pltpu.VMEM((tt + 8, GDN_CONV_CH), F32)],
        compiler_params=_params("arbitrary", "arbitrary"),
        name="front",
    )(x, w_pack, conv_w, hist, gpar, kvnw, wukv, cs_tab, sn_tab)


def _gdn_kernel(qkv_ref, z_ref, gb_ref, s0_ref, nw_ref, og_ref, sout_ref, s_scr, *, tg, c, bb):
    t = pl.program_id(1)
    nh = GDN_HEADS
    r = nh * c
    sh = c.bit_length() - 1

    @pl.when(t == 0)
    def _():
        s_scr[...] = s0_ref[...]

    row = lax.broadcasted_iota(I32, (r, r), 0)
    col = lax.broadcasted_iota(I32, (r, r), 1)
    same = (row >> sh) == (col >> sh)
    incl = same & (row >= col)
    strict = same & (row > col)
    eye = jnp.where(row == col, 1.0, 0.0)
    lane0 = jnp.where(lax.broadcasted_iota(I32, (r, LANES), 1) == 0, 1.0, 0.0).astype(BF16)
    nw = nw_ref[...]
    items = [(bi, ci) for ci in range(tg // c) for bi in range(bb)]
    chunks = range(len(items))

    def stacked(ref, it, base):
        bi, ci = items[it]
        return jnp.concatenate(
            [ref[bi, ci * c:(ci + 1) * c, base + h * LANES:base + (h + 1) * LANES] for h in range(nh)], axis=0)

    def col_bcast(it, lane):
        bi, ci = items[it]
        gbc = gb_ref[bi, ci * c:(ci + 1) * c, :]
        return jnp.concatenate(
            [jnp.broadcast_to(gbc[:, lane + h:lane + h + 1], (c, LANES)) for h in range(nh)], axis=0)

    def as_col(gc_b):
        return _lane_tile(gc_b, r) if r % LANES == 0 else gc_b[:, :r]

    def as_row(gc_b):
        if r % LANES == 0:
            return as_col(gc_b).T
        return sum(lax.dot_general(lane0, p, (((1,), (1,)), ((), ())), preferred_element_type=F32)
                   for p in _split3(gc_b))

    ks = [stacked(qkv_ref, ci, GDN_QK_W) for ci in chunks]
    gc = [col_bcast(ci, 0) for ci in chunks]
    beta = [col_bcast(ci, nh) for ci in chunks]
    decay, qk_kk = [], []
    for ci in chunks:
        decay.append(jnp.exp(jnp.where(incl, as_col(gc[ci]) - as_row(gc[ci]), -jnp.inf)))
        qk_kk.append(_mm_nt(jnp.concatenate([stacked(qkv_ref, ci, 0), ks[ci]], axis=0), ks[ci]))
    intra = [qk_kk[ci][:r] * decay[ci] for ci in chunks]
    n_pow = [jnp.where(strict, -as_col(beta[ci]) * qk_kk[ci][r:] * decay[ci], 0.0)
             for ci in chunks]
    t_inv = [eye + n_pow[ci] for ci in chunks]
    for _ in range(sh - 1):
        n_pow = [_mm(n_pow[ci], n_pow[ci]) for ci in chunks]
        t_inv = [t_inv[ci] + _mm(t_inv[ci], n_pow[ci]) for ci in chunks]
    egc = [jnp.exp(gc[ci]) for ci in chunks]
    uw = [_mm(t_inv[ci], jnp.concatenate([stacked(qkv_ref, ci, 2 * GDN_QK_W) * beta[ci],
                                          ks[ci] * beta[ci] * egc[ci]], axis=1)) for ci in chunks]

    for ci in chunks:
        bi, cpos = items[ci]
        u = uw[ci][:, :GDN_DV]
        w = uw[ci][:, GDN_DV:]
        qd = stacked(qkv_ref, ci, 0) * egc[ci]
        vn, qs_s = [], []
        for h in range(nh):
            hs = slice(h * c, (h + 1) * c)
            s_h = s_scr[bi, h]
            wq = _mm(jnp.concatenate([w[hs], qd[hs]], axis=0), s_h)
            vn_h = u[hs] - wq[:c]
            g_last = gc[ci][h * c + c - 1:h * c + c, :]
            kd = ks[ci][hs] * jnp.exp(g_last - gc[ci][hs])
            s_scr[bi, h] = s_h * jnp.exp(g_last) + _mm_tn(kd, vn_h)
            vn.append(vn_h)
            qs_s.append(wq[c:])
        o = jnp.concatenate(qs_s, axis=0) + _mm(intra[ci], jnp.concatenate(vn, axis=0))

        o = o * lax.rsqrt(jnp.mean(o * o, axis=-1, keepdims=True) + RMS_EPS) * nw
        o = o * _silu(stacked(z_ref, ci, 0))
        for h in range(nh):
            og_ref[bi, cpos * c:(cpos + 1) * c, h * LANES:(h + 1) * LANES] = o[h * c:(h + 1) * c].astype(BF16)

    @pl.when(t == pl.num_programs(1) - 1)
    def _():
        sout_ref[...] = s_scr[...]


def _gdn(qkv, z, gb, s0, nw, tg, c):
    b, t, _ = qkv.shape
    bb = 2 if b % 2 == 0 else 1
    row3 = lambda w: pl.BlockSpec((bb, tg, w), lambda bi, ti: (bi, ti, 0))
    st = pl.BlockSpec((bb, GDN_HEADS, GDN_DK, GDN_DV), lambda bi, ti: (bi, 0, 0, 0))
    return pl.pallas_call(
        functools.partial(_gdn_kernel, tg=tg, c=c, bb=bb),
        grid=(b // bb, t // tg),
        in_specs=[row3(GDN_CONV_CH), row3(GDN_V_W), row3(LANES), st,
                  pl.BlockSpec((1, GDN_DV), lambda bi, ti: (0, 0))],
        out_specs=(row3(GDN_V_W), st),
        out_shape=(jax.ShapeDtypeStruct((b, t, GDN_V_W), BF16),
                   jax.ShapeDtypeStruct((b, GDN_HEADS, GDN_DK, GDN_DV), F32)),
        scratch_shapes=[pltpu.VMEM((bb, GDN_HEADS, GDN_DK, GDN_DV), F32)],
        compiler_params=_params("arbitrary", "arbitrary"),
        name="gdn",
    )(qkv, z, gb, s0, nw)


ATTN_ROW_BLOCK = 32


def _lane_tile(x, width):
    return x if width == LANES else jnp.concatenate([x] * (width // LANES), axis=1)


def _attn_kernel(qi_ref, ki_ref, q_ref, k_ref, v_ref, o_ref, m_scr, l_scr, acc_scr, a_scr, s_scr, p_scr,
                 *, tb, sub):
    step = pl.program_id(2)
    qi = qi_ref[step]
    ki = ki_ref[step]
    csh = CHUNK.bit_length() - 1
    c2 = MLA_SCALE * math.log2(math.e)

    @pl.when(ki == 0)
    def _():
        m_scr[...] = jnp.full(m_scr.shape, -jnp.inf, F32)
        l_scr[...] = jnp.zeros(l_scr.shape, F32)
        acc_scr[...] = jnp.zeros(acc_scr.shape, F32)

    half = tb // 2
    rb = min(ATTN_ROW_BLOCK, half)

    def scores(h, j):
        rows = slice(h * half, (h + 1) * half)
        s_scr[rows, :] = lax.dot_general(q_ref[0, 0, rows, :], k_ref[0, 0, j * sub:(j + 1) * sub, :],
                                         (((1,), (1,)), ((), ())), preferred_element_type=F32)

    def softmax(h, j, masked):
        r0 = h * half

        def block(i):
            rr = pl.ds(pl.multiple_of(r0 + i * rb, rb), rb)
            s = s_scr[rr, :]
            if masked:
                qc = (r0 + i * rb + lax.broadcasted_iota(I32, s.shape, 0)) >> csh
                kc = (j * sub + lax.broadcasted_iota(I32, s.shape, 1)) >> csh
                s = jnp.where(kc <= qc, s, -jnp.inf)
            return rr, s

        def row_max(i, carry):
            rr, s = block(i)
            m_prev = m_scr[rr, :]
            m_new = jnp.maximum(m_prev, jnp.max(s, axis=-1, keepdims=True))
            a_scr[rr, :] = jnp.exp2((m_prev - m_new) * c2)
            m_scr[rr, :] = m_new
            return carry

        def row_exp(i, carry):
            rr, s = block(i)
            p = jnp.exp2((s - _lane_tile(m_scr[rr, :], sub)) * c2)
            l_scr[rr, :] = a_scr[rr, :] * l_scr[rr, :] + jnp.sum(p, axis=-1, keepdims=True)
            p_scr[rr, :] = p.astype(BF16)
            return carry

        for i in range(half // rb):
            row_max(i, 0)
        for i in range(half // rb):
            row_exp(i, 0)

    def weighted_values(h, j):
        rows = slice(h * half, (h + 1) * half)
        acc_scr[rows, :] = a_scr[rows, :] * acc_scr[rows, :] + jnp.dot(
            p_scr[rows, :], v_ref[0, 0, j * sub:(j + 1) * sub, :], preferred_element_type=F32)

    def run(tasks):
        for h in range(2):
            if tasks[h]:
                scores(h, tasks[h][0][0])
        for n in range(max(len(t) for t in tasks)):
            for h in range(2):
                if n < len(tasks[h]):
                    j, masked = tasks[h][n]
                    softmax(h, j, masked)
                    weighted_values(h, j)
                    if n + 1 < len(tasks[h]):
                        scores(h, tasks[h][n + 1][0])

    def diagonal_tasks(h):
        r0 = h * half
        out = []
        for j in range(tb // sub):
            k0, k1 = j * sub, (j + 1) * sub
            if k0 >= r0 + half:
                continue
            out.append((j, k1 > r0 + CHUNK))
        return out

    @pl.when(ki < qi)
    def _():
        run([[(j, False) for j in range(tb // sub)]] * 2)

    @pl.when(ki == qi)
    def _():
        run([diagonal_tasks(0), diagonal_tasks(1)])
        o_ref[0] = (acc_scr[...] / l_scr[...]).astype(BF16)


def _attn_prompt(q, k, v, tb, sub):
    b, nh, t, _ = q.shape
    nt = t // tb
    pairs = [(qi, ki) for qi in range(nt) for ki in range(qi + 1)]
    qi_of = jnp.asarray([p[0] for p in pairs], I32)
    ki_of = jnp.asarray([p[1] for p in pairs], I32)
    return pl.pallas_call(
        functools.partial(_attn_kernel, tb=tb, sub=sub),
        grid_spec=pltpu.PrefetchScalarGridSpec(
            num_scalar_prefetch=2,
            grid=(b, nh, len(pairs)),
            in_specs=[pl.BlockSpec((1, 1, tb, MLA_QK_W), lambda bi, hi, s, qo, ko: (bi, hi, qo[s], 0)),
                      pl.BlockSpec((1, 1, tb, MLA_QK_W), lambda bi, hi, s, qo, ko: (bi, hi, ko[s], 0)),
                      pl.BlockSpec((1, 1, tb, MLA_D_V), lambda bi, hi, s, qo, ko: (bi, hi, ko[s], 0))],
            out_specs=pl.BlockSpec((1, tb, MLA_D_V), lambda bi, hi, s, qo, ko: (bi, qo[s], hi)),
            scratch_shapes=[pltpu.VMEM((tb, LANES), F32), pltpu.VMEM((tb, LANES), F32),
                            pltpu.VMEM((tb, MLA_D_V), F32), pltpu.VMEM((tb, LANES), F32),
                            pltpu.VMEM((tb, sub), F32), pltpu.VMEM((tb, sub), BF16)]),
        out_shape=jax.ShapeDtypeStruct((b, t, MLA_HEADS * MLA_D_V), BF16),
        compiler_params=_params("arbitrary", "arbitrary", "arbitrary"),
        name="attn_prompt",
    )(qi_of, ki_of, q, k, v)


def _attn_sample_kernel(q_ref, kn_ref, vn_ref, plat_ref, pkr_ref, wukv_ref, o_ref):
    kvp = _mm(plat_ref[0], wukv_ref[...])
    pkr = pkr_ref[0].astype(BF16)
    for h in range(MLA_HEADS):
        q = q_ref[0, h]
        s_past = (_mm_nt(q[:, :MLA_D_NOPE], kvp[:, h * LANES:(h + 1) * LANES])
                  + _mm_nt(q[:, MLA_D_NOPE:MLA_D_NOPE + MLA_D_ROPE], pkr)) * MLA_SCALE
        s_new = _mm_nt(q, kn_ref[0, h]) * MLA_SCALE
        m = jnp.maximum(jnp.max(s_past, axis=-1, keepdims=True), jnp.max(s_new, axis=-1, keepdims=True))
        p_past = jnp.exp(s_past - m)
        p_new = jnp.exp(s_new - m)
        l = jnp.sum(p_past, axis=-1, keepdims=True) + jnp.sum(p_new, axis=-1, keepdims=True)
        o = _mm(p_past, kvp[:, (MLA_HEADS + h) * LANES:(MLA_HEADS + h + 1) * LANES]) + _mm(p_new, vn_ref[0, h])
        o_ref[0, :, h * MLA_D_V:(h + 1) * MLA_D_V] = (o / l).astype(BF16)


def _attn_sample(q, k_new, v_new, past_lat, past_kr, wukv):
    b, nh, ts, _ = q.shape
    past = past_lat.shape[1]
    b4 = lambda w: pl.BlockSpec((1, nh, ts, w), lambda bi: (bi, 0, 0, 0))
    return pl.pallas_call(
        _attn_sample_kernel,
        grid=(b,),
        in_specs=[b4(MLA_QK_W), b4(MLA_QK_W), b4(MLA_D_V),
                  pl.BlockSpec((1, past, MLA_KV_RANK), lambda bi: (bi, 0, 0)),
                  pl.BlockSpec((1, past, MLA_D_ROPE), lambda bi: (bi, 0, 0)),
                  pl.BlockSpec((MLA_KV_RANK, 2 * MLA_HEADS * LANES), lambda bi: (0, 0))],
        out_specs=pl.BlockSpec((1, ts, MLA_HEADS * MLA_D_V), lambda bi: (bi, 0, 0)),
        out_shape=jax.ShapeDtypeStruct((b, ts, MLA_HEADS * MLA_D_V), BF16),
        compiler_params=_params("arbitrary"),
        name="attn_sample",
    )(q, k_new, v_new, past_lat, past_kr, wukv)


def _layernorm(y, g, b):
    mu = jnp.mean(y, axis=-1, keepdims=True)
    d = y - mu
    var = jnp.mean(d * d, axis=-1, keepdims=True)
    return d * lax.rsqrt(var + LN_EPS) * g + b


def _mixln_kernel(og_ref, om_ref, x_ref, w_ref, g_ref, b_ref, *rest):
    o_ref = rest[-1]
    mix = (jnp.dot(og_ref[...], w_ref[:GDN_V_W, :], preferred_element_type=F32)
           + jnp.dot(om_ref[...], w_ref[GDN_V_W:, :], preferred_element_type=F32))
    o_ref[...] = _layernorm(DEEPNORM_ALPHA * x_ref[...] + mix, g_ref[...], b_ref[...])


def _mixln(og, om, x, w_out, g, b, x1_all, n_all, row0, tm):
    n = x.shape[0]
    blk0 = row0 // tm
    rows = lambda w: pl.BlockSpec((tm, w), lambda i: (i, 0))
    const = lambda s: pl.BlockSpec(s, lambda i: (0, 0))
    first = x1_all is None
    return pl.pallas_call(
        _mixln_kernel,
        grid=(n // tm,),
        in_specs=[rows(GDN_V_W), rows(MLA_HEADS * MLA_D_V), rows(D_MODEL),
                  const((D_MODEL, D_MODEL)), const((1, D_MODEL)), const((1, D_MODEL))]
                 + ([] if first else [pl.BlockSpec(memory_space=pl.ANY)]),
        out_specs=pl.BlockSpec((tm, D_MODEL), lambda i: (blk0 + i, 0)),
        out_shape=jax.ShapeDtypeStruct((n_all, D_MODEL), F32),
        input_output_aliases={} if first else {6: 0},
        compiler_params=_params("arbitrary"),
        name="mixln",
    )(*((og, om, x, w_out, g, b) + (() if first else (x1_all,))))


def _router_kernel(x_ref, rw_ref, rb_ref, idx_ref, gate_ref, rank_ref, cnt_ref, carry_scr, *, tt):
    s = pl.program_id(0)

    @pl.when(s == 0)
    def _():
        carry_scr[...] = jnp.zeros(carry_scr.shape, F32)

    ninf = -jnp.inf
    big = float(2 * N_EXPERTS)
    scores = _sigmoid(_mm(x_ref[...], rw_ref[...]).T)
    biased = scores + rb_ref[...]
    eio = lax.broadcasted_iota(I32, (N_EXPERTS, tt), 0).astype(F32)

    def first_argmax(vals, io):
        m = jnp.max(vals, axis=0, keepdims=True)
        i = jnp.min(jnp.where(vals == m, io, big), axis=0, keepdims=True)
        return m, i

    gs = []
    for g in range(N_GROUPS):
        blk = biased[g * GROUP_SIZE:(g + 1) * GROUP_SIZE]
        io = (lax.broadcasted_iota(I32, (GROUP_SIZE, tt), 0) + g * GROUP_SIZE).astype(F32)
        m1, i1 = first_argmax(blk, io)
        m2 = jnp.max(jnp.where(io == i1, ninf, blk), axis=0, keepdims=True)
        gs.append(m1 + m2)
    gio = lax.broadcasted_iota(I32, (N_GROUPS, tt), 0).astype(F32)
    gsc = jnp.zeros((N_GROUPS, tt), F32)
    for g in range(N_GROUPS):
        gsc = jnp.where(gio == float(g), gs[g], gsc)
    gsel = jnp.zeros((N_GROUPS, tt), F32)
    for _ in range(TOPK_GROUPS):
        _, gi = first_argmax(gsc, gio)
        hit = gio == gi
        gsel = jnp.where(hit, 1.0, gsel)
        gsc = jnp.where(hit, ninf, gsc)
    masked = jnp.concatenate(
        [jnp.where(jnp.max(jnp.where(gio == float(g), gsel, 0.0), axis=0, keepdims=True) > 0.0,
                   biased[g * GROUP_SIZE:(g + 1) * GROUP_SIZE], ninf) for g in range(N_GROUPS)], axis=0)

    idx, wts = [], []
    sel = jnp.zeros((N_EXPERTS, tt), F32)
    for _ in range(TOP_K):
        _, ei = first_argmax(masked, eio)
        hit = eio == ei
        wts.append(jnp.sum(jnp.where(hit, scores, 0.0), axis=0, keepdims=True))
        masked = jnp.where(hit, ninf, masked)
        sel = jnp.where(hit, 1.0, sel)
        idx.append(ei)
    wsum = wts[0]
    for w in wts[1:]:
        wsum = wsum + w

    t0 = lax.broadcasted_iota(I32, (tt, tt), 0)
    t1 = lax.broadcasted_iota(I32, (tt, tt), 1)
    before = jnp.where(t0 < t1, 1.0, 0.0).astype(BF16)
    sel_b = sel.astype(BF16)
    base = carry_scr[:, :1] + jnp.dot(sel_b, before, preferred_element_type=F32)
    ranks = [jnp.sum(jnp.where(eio == ei, base, 0.0), axis=0, keepdims=True) for ei in idx]
    carry_scr[...] = carry_scr[...] + jnp.dot(sel_b, jnp.ones((tt, LANES), BF16), preferred_element_type=F32)

    for k in range(TOP_K):
        idx_ref[k:k + 1, :] = idx[k].astype(I32)
        gate_ref[k:k + 1, :] = wts[k] / wsum * ROUTED_SCALE
        rank_ref[k:k + 1, :] = ranks[k].astype(I32)
    cnt_ref[...] = carry_scr[...]


def _router(x1, rw, rb, tt):
    n = x1.shape[0]
    kt = lambda dt: jax.ShapeDtypeStruct((TOP_K, n), dt)
    kspec = pl.BlockSpec((TOP_K, tt), lambda i: (0, i))
    return pl.pallas_call(
        functools.partial(_router_kernel, tt=tt),
        grid=(n // tt,),
        in_specs=[pl.BlockSpec((tt, D_MODEL), lambda i: (i, 0)),
                  pl.BlockSpec((D_MODEL, N_EXPERTS), lambda i: (0, 0)),
                  pl.BlockSpec((N_EXPERTS, 1), lambda i: (0, 0))],
        out_specs=(kspec, kspec, kspec, pl.BlockSpec((N_EXPERTS, LANES), lambda i: (0, 0))),
        out_shape=(kt(I32), kt(F32), kt(I32), jax.ShapeDtypeStruct((N_EXPERTS, LANES), F32)),
        scratch_shapes=[pltpu.VMEM((N_EXPERTS, LANES), F32)],
        compiler_params=_params("arbitrary"),
        name="router",
    )(x1, rw, rb)


def _dest_kernel(idx_ref, rank_ref, pstart_ref, dest_ref, *, tt):
    eio = lax.broadcasted_iota(I32, (N_EXPERTS, tt), 0)
    pstart = pstart_ref[...]
    for k in range(TOP_K):
        start = jnp.sum(jnp.where(eio == idx_ref[k:k + 1, :], pstart, 0.0), axis=0, keepdims=True)
        dest_ref[0, k:k + 1, :] = start.astype(I32) + rank_ref[k:k + 1, :]


def _dest(idx, rank, pstart, tt):
    n = idx.shape[1]
    kspec = pl.BlockSpec((TOP_K, tt), lambda i: (0, i))
    return pl.pallas_call(
        functools.partial(_dest_kernel, tt=tt),
        grid=(n // tt,),
        in_specs=[kspec, kspec, pl.BlockSpec((N_EXPERTS, 1), lambda i: (0, 0))],
        out_specs=pl.BlockSpec((1, TOP_K, tt), lambda i: (i, 0, 0)),
        out_shape=jax.ShapeDtypeStruct((n // tt, TOP_K, tt), I32),
        compiler_params=_params("arbitrary"),
        name="dest",
    )(idx, rank, pstart)


ROW_TILE = D_MODEL // LANES
XROW_TILE = ROW_TILE // 2


def _rows_to_tiles(x, tiles_ref, base, n, rt=ROW_TILE):
    for j in range(rt):
        tiles_ref[pl.ds(base * rt + j, n, stride=rt), :] = x[:, j * LANES:(j + 1) * LANES]


def _tiles_to_rows(tiles_ref, base, n, rt=ROW_TILE):
    return jnp.concatenate(
        [tiles_ref[pl.ds(base * rt + j, n, stride=rt), :] for j in range(rt)], axis=1)


def _pack_bf16_pairs(x):
    half = x.shape[1] // 2
    lo = pltpu.bitcast(x[:, :half].astype(BF16).astype(F32), jnp.uint32) >> 16
    hi = pltpu.bitcast(x[:, half:].astype(BF16).astype(F32), jnp.uint32) & jnp.uint32(0xFFFF0000)
    return lo | hi


def _unpack_bf16_pairs(w):
    lo = pltpu.bitcast(w << 16, F32)
    hi = pltpu.bitcast(w & jnp.uint32(0xFFFF0000), F32)
    return jnp.concatenate([lo, hi], axis=1)


def _dispatch_kernel(tail_ref, dest_ref, x_ref, xs_out, zbuf, xt_scr, sem, zsem, *, td):
    s = pl.program_id(0)

    @pl.when(s == 0)
    def _():
        zbuf[...] = jnp.zeros(zbuf.shape, zbuf.dtype)

        def zero_copy(e):
            first = pl.multiple_of(tail_ref[e] * XROW_TILE, MOE_BM * XROW_TILE)
            return pltpu.make_async_copy(zbuf, xs_out.at[pl.ds(first, MOE_BM * XROW_TILE)], zsem)

        def zstart(e, carry):
            zero_copy(e).start()
            return carry

        def zwait(e, carry):
            zero_copy(e).wait()
            return carry

        lax.fori_loop(0, N_EXPERTS, zstart, 0)
        lax.fori_loop(0, N_EXPERTS, zwait, 0)

    slot = s & 1

    def row_copy(slot_, i, d):
        src = pl.multiple_of((slot_ * td + i) * XROW_TILE, XROW_TILE)
        return pltpu.make_async_copy(xt_scr.at[pl.ds(src, XROW_TILE)],
                                     xs_out.at[pl.ds(pl.multiple_of(d * XROW_TILE, XROW_TILE), XROW_TILE)],
                                     sem.at[slot_])

    def drain(slot_):
        def body(i, carry):
            for k in range(TOP_K):
                row_copy(slot_, 0, 0).wait()
            return carry
        lax.fori_loop(0, td, body, 0)

    @pl.when(s >= 2)
    def _():
        drain(slot)

    _rows_to_tiles(_pack_bf16_pairs(x_ref[...]), xt_scr, slot * td, td, XROW_TILE)

    def issue(i, carry):
        for k in range(TOP_K):
            row_copy(slot, i, dest_ref[0, 0, k * td + i]).start(priority=k % 2)
        return carry

    lax.fori_loop(0, td, issue, 0, unroll=4)

    @pl.when(s == pl.num_programs(0) - 1)
    def _():
        drain(slot)

        @pl.when(s >= 1)
        def _():
            drain(1 - slot)


def _dispatch(tail, dest, x1, n_rows, td):
    n = x1.shape[0]
    return pl.pallas_call(
        functools.partial(_dispatch_kernel, td=td),
        grid_spec=pltpu.PrefetchScalarGridSpec(
            num_scalar_prefetch=1,
            grid=(n // td,),
            in_specs=[pl.BlockSpec((1, 1, TOP_K * td), lambda i, tl: (i, 0, 0), memory_space=pltpu.SMEM),
                      pl.BlockSpec((td, D_MODEL), lambda i, tl: (i, 0))],
            out_specs=pl.BlockSpec(memory_space=pl.ANY),
            scratch_shapes=[pltpu.VMEM((MOE_BM * XROW_TILE, LANES), jnp.uint32),
                            pltpu.VMEM((2 * td * XROW_TILE, LANES), jnp.uint32),
                            pltpu.SemaphoreType.DMA((2,)), pltpu.SemaphoreType.DMA(())]),
        out_shape=jax.ShapeDtypeStruct((n_rows * XROW_TILE, LANES), jnp.uint32),
        compiler_params=_params("arbitrary"),
        name="dispatch",
    )(tail, dest, x1)


def _expert_kernel(first_ref, nblk_ref, nact_ref, xs_hbm, wg_ref, wu_ref, wd_ref, y_hbm,
                   xbuf, ybuf, wgb, wub, wdb, xsem, ysem):
    e = pl.program_id(0)
    nact = nact_ref[0]
    ring = EXPERT_RING
    xr = MOE_BM * XROW_TILE
    yr = MOE_BM * XROW_TILE

    def x_copy(g):
        slot = g & (ring - 1)
        return pltpu.make_async_copy(xs_hbm.at[pl.ds(pl.multiple_of(g * xr, xr), xr)],
                                     xbuf.at[pl.ds(pl.multiple_of(slot * xr, xr), xr)], xsem.at[slot])

    def y_copy(g):
        slot = g & (ring - 1)
        return pltpu.make_async_copy(ybuf.at[pl.ds(pl.multiple_of(slot * yr, yr), yr)],
                                     y_hbm.at[pl.ds(pl.multiple_of(g * yr, yr), yr)], ysem.at[slot])

    @pl.when(e == 0)
    def _():
        for g0 in range(ring - 1):
            @pl.when(g0 < nact)
            def _():
                x_copy(g0).start()

    @pl.when(nblk_ref[e] > 0)
    def _():
        wgb[...] = wg_ref[0].astype(BF16)
        wub[...] = wu_ref[0].astype(BF16)
        wdb[...] = wd_ref[0].astype(BF16)

    def block(b, carry):
        g = first_ref[e] + b
        slot = g & (ring - 1)
        x_copy(g).wait()

        @pl.when(g + ring - 1 < nact)
        def _():
            x_copy(g + ring - 1).start()

        @pl.when(g >= ring)
        def _():
            y_copy(g - ring).wait()

        xb = _unpack_bf16_pairs(_tiles_to_rows(xbuf, slot * MOE_BM, MOE_BM, XROW_TILE)).astype(BF16)
        hg = jnp.dot(xb, wgb[...], preferred_element_type=F32)
        hu = jnp.dot(xb, wub[...], preferred_element_type=F32)
        y = jnp.dot((_silu(hg) * hu).astype(BF16), wdb[...], preferred_element_type=F32)
        _rows_to_tiles(_pack_bf16_pairs(y), ybuf, slot * MOE_BM, MOE_BM, XROW_TILE)
        y_copy(g).start()

        @pl.when(g == nact - 1)
        def _():
            for back in range(ring):
                @pl.when(g >= back)
                def _():
                    y_copy(g - back).wait()

        return carry

    lax.fori_loop(0, nblk_ref[e], block, 0)


def _experts(first, nblk, nact, xs, wg, wu, wd):
    n_rows = xs.shape[0] // XROW_TILE
    wspec = lambda s: pl.BlockSpec((1,) + s, lambda e, fr, nb, na: (e, 0, 0))
    return pl.pallas_call(
        _expert_kernel,
        grid_spec=pltpu.PrefetchScalarGridSpec(
            num_scalar_prefetch=3,
            grid=(N_EXPERTS,),
            in_specs=[pl.BlockSpec(memory_space=pl.ANY),
                      wspec((D_MODEL, EXP_HIDDEN)), wspec((D_MODEL, EXP_HIDDEN)), wspec((EXP_HIDDEN, D_MODEL))],
            out_specs=pl.BlockSpec(memory_space=pl.ANY),
            scratch_shapes=[pltpu.VMEM((EXPERT_RING * MOE_BM * XROW_TILE, LANES), jnp.uint32),
                            pltpu.VMEM((EXPERT_RING * MOE_BM * XROW_TILE, LANES), jnp.uint32),
                            pltpu.VMEM((D_MODEL, EXP_HIDDEN), BF16), pltpu.VMEM((D_MODEL, EXP_HIDDEN), BF16),
                            pltpu.VMEM((EXP_HIDDEN, D_MODEL), BF16),
                            pltpu.SemaphoreType.DMA((EXPERT_RING,)), pltpu.SemaphoreType.DMA((EXPERT_RING,))]),
        out_shape=jax.ShapeDtypeStruct((n_rows * XROW_TILE, LANES), jnp.uint32),
        compiler_params=_params("arbitrary"),
        name="experts",
    )(first, nblk, nact, xs, wg, wu, wd)


def _combine_kernel(dcur_ref, dnxt_ref, y_hbm, gate_ref, x_ref, wsg_ref, wsu_ref, wsd_ref, g_ref, b_ref,
                    outp_ref, outs_ref, buf, sem, *, tc, np_tiles):
    s = pl.program_id(0)
    ns = pl.num_programs(0)
    slot = s % 2

    def row_copy(d, slot_, k, i):
        dst = pl.multiple_of(((slot_ * TOP_K + k) * tc + i) * XROW_TILE, XROW_TILE)
        return pltpu.make_async_copy(y_hbm.at[pl.ds(pl.multiple_of(d * XROW_TILE, XROW_TILE), XROW_TILE)],
                                     buf.at[pl.ds(dst, XROW_TILE)], sem.at[slot_])

    def issue(dref, slot_):
        def body(i, carry):
            for k in range(TOP_K):
                row_copy(dref[0, 0, k * tc + i], slot_, k, i).start(priority=k % 2)
            return carry
        lax.fori_loop(0, tc, body, 0, unroll=4)

    @pl.when(s == 0)
    def _():
        issue(dcur_ref, 0)

    @pl.when(s + 1 < ns)
    def _():
        issue(dnxt_ref, 1 - slot)

    def drain(i, carry):
        for k in range(TOP_K):
            row_copy(0, slot, k, i).wait()
        return carry

    lax.fori_loop(0, tc, drain, 0)

    x = x_ref[...]
    gate = gate_ref[...]
    def expert_rows(k):
        return _unpack_bf16_pairs(_tiles_to_rows(buf, (slot * TOP_K + k) * tc, tc, XROW_TILE))

    routed = expert_rows(0) * gate[:, 0:1]
    for k in range(1, TOP_K):
        routed = routed + expert_rows(k) * gate[:, k:k + 1]
    xb = x.astype(BF16)
    shared = _mm(_silu(_mm(xb, wsg_ref[...])) * _mm(xb, wsu_ref[...]), wsd_ref[...])
    out = _layernorm(DEEPNORM_ALPHA * x + (routed + shared), g_ref[...], b_ref[...])

    @pl.when(s < np_tiles)
    def _():
        outp_ref[...] = out

    @pl.when(s >= np_tiles)
    def _():
        outs_ref[...] = out


def _combine(dest, y_sorted, gate, x1, wsg, wsu, wsd, g, b, n_prompt, tc):
    n = x1.shape[0]
    ns = n // tc
    np_tiles = n_prompt // tc
    const = lambda s: pl.BlockSpec(s, lambda i: (0, 0))
    dspec = lambda f: pl.BlockSpec((1, 1, TOP_K * tc), f, memory_space=pltpu.SMEM)
    return pl.pallas_call(
        functools.partial(_combine_kernel, tc=tc, np_tiles=np_tiles),
        grid=(ns,),
        in_specs=[dspec(lambda i: (i, 0, 0)), dspec(lambda i: (jnp.minimum(i + 1, ns - 1), 0, 0)),
                  pl.BlockSpec(memory_space=pl.ANY),
                  pl.BlockSpec((tc, TOP_K), lambda i: (i, 0)),
                  pl.BlockSpec((tc, D_MODEL), lambda i: (i, 0)),
                  const((D_MODEL, EXP_HIDDEN)), const((D_MODEL, EXP_HIDDEN)), const((EXP_HIDDEN, D_MODEL)),
                  const((1, D_MODEL)), const((1, D_MODEL))],
        out_specs=(pl.BlockSpec((tc, D_MODEL), lambda i: (jnp.minimum(i, np_tiles - 1), 0)),
                   pl.BlockSpec((tc, D_MODEL), lambda i: (jnp.maximum(i - np_tiles, 0), 0))),
        out_shape=(jax.ShapeDtypeStruct((n_prompt, D_MODEL), F32),
                   jax.ShapeDtypeStruct((n - n_prompt, D_MODEL), F32)),
        scratch_shapes=[pltpu.VMEM((2 * TOP_K * tc * XROW_TILE, LANES), jnp.uint32),
                        pltpu.SemaphoreType.DMA((2,))],
        compiler_params=_params("arbitrary"),
        name="combine",
    )(dest, dest, y_sorted, gate, x1, wsg, wsu, wsd, g, b)


def _pack_w_in(w_in):
    d = w_in.shape[0]
    o_z = GDN_CONV_CH
    o_a = o_z + GDN_V_W
    o_b = o_a + GDN_HEADS
    o_q = o_b + GDN_HEADS
    o_c = o_q + MLA_HEADS * (MLA_D_NOPE + MLA_D_ROPE)
    o_kr = o_c + MLA_KV_RANK
    zeros = lambda w: jnp.zeros((d, w), w_in.dtype)
    wq = w_in[:, o_q:o_c].reshape(d, MLA_HEADS, MLA_D_NOPE + MLA_D_ROPE)
    q_nope = wq[:, :, :MLA_D_NOPE].reshape(d, MLA_HEADS * MLA_D_NOPE)
    q_rope = jnp.pad(wq[:, :, MLA_D_NOPE:], ((0, 0), (0, 0), (0, LANES - MLA_D_ROPE))).reshape(d, MLA_HEADS * LANES)
    cols = [w_in[:, :o_a], q_nope, q_rope, w_in[:, o_c:o_kr], w_in[:, o_kr:], zeros(LANES - MLA_D_ROPE),
            w_in[:, o_a:o_q], zeros(LANES - 2 * GDN_HEADS)]
    return jnp.concatenate(cols, axis=1).astype(BF16)


def _rope_tables(pos):
    inv_freq = ROPE_THETA ** (-jnp.arange(0, MLA_D_ROPE, 2, dtype=F32) / MLA_D_ROPE)
    ang = pos.astype(F32)[:, None] * inv_freq[None, :]
    cos, sin = jnp.cos(ang), jnp.sin(ang)
    pad = jnp.zeros((pos.shape[0], LANES - MLA_D_ROPE), F32)
    return jnp.concatenate([cos, cos, pad], axis=1), jnp.concatenate([-sin, sin, pad], axis=1)


def _pick(t, pref):
    return pref if t % pref == 0 else t


def _token_mixers(x, pos, conv_hist, s0, past, wts, x1_all, n_all, row0):
    b, t, _ = x.shape
    cs_tab, sn_tab = _rope_tables(pos)
    tt = _pick(t, 512)
    c = min(CHUNK, t)
    qkv, z, gb, q, k, v, latent, k_rope, conv_new = _front(
        x, wts["w_pack"], wts["conv_w"], conv_hist, wts["gpar"], wts["kvnw"], wts["wukv"], cs_tab, sn_tab, tt, c)
    og, s_new = _gdn(qkv, z, gb, s0, wts["gdn_nw"], _pick(t, 4 * CHUNK), c)
    if past is None:
        tb = _pick(t, 1024)
        om = _attn_prompt(q, k, v, tb, _pick(tb, 512))
    else:
        om = _attn_sample(q, k, v, past[0], past[1], wts["wukv"])
    n = b * t
    x1_all = _mixln(og.reshape(n, -1), om.reshape(n, -1), x.reshape(n, D_MODEL), wts["w_out"],
                    wts["ln1_g"], wts["ln1_b"], x1_all, n_all, row0, _pick(n, 256))
    return x1_all, latent, k_rope, s_new, conv_new


def _moe(x1_all, n_prompt, wts):
    n = x1_all.shape[0]
    idx, gate, rank, cnt = _router(x1_all, wts["router_w"], wts["router_b"], _pick(n, 256))
    counts = cnt[:, 0].astype(I32)
    padded = (counts + MOE_BM - 1) // MOE_BM * MOE_BM
    pend = jnp.cumsum(padded)
    pstart = pend - padded
    td = _pick(math.gcd(n_prompt, n - n_prompt), 256)
    dest = _dest(idx, rank, pstart.astype(F32).reshape(-1, 1), td)
    dest = dest.reshape(n // td, 1, TOP_K * td)
    n_blocks = n * TOP_K // MOE_BM + N_EXPERTS
    nact = (pend[-1:] // MOE_BM).astype(I32)
    tail = jnp.maximum(pend - MOE_BM, 0).astype(I32)

    xs = _dispatch(tail, dest, x1_all, n_blocks * MOE_BM, td)
    y_sorted = _experts((pstart // MOE_BM).astype(I32), (padded // MOE_BM).astype(I32), nact, xs,
                        wts["exp_wg"], wts["exp_wu"], wts["exp_wd"])
    return _combine(dest, y_sorted, gate.T, x1_all,
                    wts["sh_wg"], wts["sh_wu"], wts["sh_wd"], wts["ln2_g"], wts["ln2_b"], n_prompt, td)


def kernel(x_prompt, x_sample, cache_kv_latent, cache_k_rope, state_gdn, state_conv, w_in, gdn_conv_w, gdn_a_log, gdn_dt_bias, gdn_norm_w, mla_kv_norm_w, mla_w_uk, mla_w_uv, w_out, ln1_g, ln1_b, router_w, router_bias, exp_w_gate, exp_w_up, exp_w_down, shared_w_gate, shared_w_up, shared_w_down, ln2_g, ln2_b):
    assert w_in.shape[0] == 1, "single-layer stack"
    b_p, t_p, _ = x_prompt.shape
    b_s, t_s, _ = x_sample.shape
    past = cache_kv_latent.shape[2]
    l = 0
    pad4 = lambda a: jnp.pad(a.astype(F32), (0, LANES - GDN_HEADS))
    wts = {
        "w_pack": _pack_w_in(w_in[l]),
        "conv_w": gdn_conv_w[l],
        "gpar": jnp.stack([pad4(gdn_a_log[l]), pad4(gdn_dt_bias[l])]),
        "kvnw": mla_kv_norm_w[l].reshape(1, -1),
        "wukv": jnp.concatenate([mla_w_uk[l].reshape(MLA_KV_RANK, -1), mla_w_uv[l].reshape(MLA_KV_RANK, -1)],
                                axis=1).astype(BF16),
        "gdn_nw": gdn_norm_w[l].reshape(1, -1),
        "w_out": w_out[l].astype(BF16),
        "ln1_g": ln1_g[l].reshape(1, -1), "ln1_b": ln1_b[l].reshape(1, -1),
        "router_w": router_w[l].astype(BF16), "router_b": router_bias[l].reshape(-1, 1),
        "exp_wg": exp_w_gate[l], "exp_wu": exp_w_up[l], "exp_wd": exp_w_down[l],
        "sh_wg": shared_w_gate[l].astype(BF16), "sh_wu": shared_w_up[l].astype(BF16),
        "sh_wd": shared_w_down[l].astype(BF16),
        "ln2_g": ln2_g[l].reshape(1, -1), "ln2_b": ln2_b[l].reshape(1, -1),
    }
    n_p, n_s = b_p * t_p, b_s * t_s
    conv0 = jnp.zeros((b_p, GDN_CONV - 1, GDN_CONV_CH), F32)
    s0 = jnp.zeros((b_p, GDN_HEADS, GDN_DK, GDN_DV), F32)
    x1_all, lat_p, kr_p, sg_p, cv_p = _token_mixers(
        x_prompt, jnp.arange(t_p), conv0, s0, None, wts, None, n_p + n_s, 0)
    x1_all, lat_s, kr_s, sg_s, cv_s = _token_mixers(
        x_sample, past + jnp.arange(t_s), state_conv[l], state_gdn[l],
        (cache_kv_latent[l], cache_k_rope[l]), wts, x1_all, n_p + n_s, n_p)
    y_p, y_s = _moe(x1_all, n_p, wts)
    return (y_p.reshape(b_p, t_p, D_MODEL), y_s.reshape(b_s, t_s, D_MODEL),
            lat_p[None], kr_p[None], sg_p[None], cv_p[None],
            lat_s[None], kr_s[None], sg_s[None], cv_s[None])
```

```python
import functools
import math

import jax
import jax.numpy as jnp
from jax import lax
from jax.experimental import pallas as pl
from jax.experimental.pallas import tpu as pltpu

F32 = jnp.float32
BF16 = jnp.bfloat16
I32 = jnp.int32

D_MODEL = 1024
CHUNK = 64
GDN_HEADS = 4
GDN_DK = 128
GDN_DV = 128
GDN_CONV = 4
GDN_QK_W = GDN_HEADS * GDN_DK
GDN_V_W = GDN_HEADS * GDN_DV
GDN_CONV_CH = 2 * GDN_QK_W + GDN_V_W
MLA_HEADS = 4
MLA_D_NOPE = 128
MLA_D_ROPE = 64
MLA_D_V = 128
MLA_KV_RANK = 256
MLA_SCALE = (MLA_D_NOPE + MLA_D_ROPE) ** -0.5
ROPE_THETA = 10000.0
N_EXPERTS = 256
N_GROUPS = 8
GROUP_SIZE = N_EXPERTS // N_GROUPS
TOPK_GROUPS = 4
TOP_K = 8
EXP_HIDDEN = 256
ROUTED_SCALE = 2.5
DEPTH = 1
DEEPNORM_ALPHA = (2.0 * DEPTH) ** 0.25
LN_EPS = 1e-5
RMS_EPS = 1e-6
L2_EPS = 1e-6

LANES = 128
PK_QKV = 0
PK_Z = PK_QKV + GDN_CONV_CH
PK_QNOPE = PK_Z + GDN_V_W
PK_QROPE = PK_QNOPE + MLA_HEADS * MLA_D_NOPE
PK_CKV = PK_QROPE + MLA_HEADS * LANES
PK_KROPE = PK_CKV + MLA_KV_RANK
PK_AB = PK_KROPE + LANES
PK_W = PK_AB + LANES
MLA_QK_W = 2 * LANES

MOE_BM = 256
EXPERT_RING = 4
VMEM_LIMIT = 56 * 1024 * 1024


def _mm(a, b):
    return jnp.dot(a.astype(BF16), b.astype(BF16), preferred_element_type=F32)


def _mm_nt(a, b):
    return lax.dot_general(a.astype(BF16), b.astype(BF16), (((1,), (1,)), ((), ())),
                           preferred_element_type=F32)


def _mm_tn(a, b):
    return lax.dot_general(a.astype(BF16), b.astype(BF16), (((0,), (0,)), ((), ())),
                           preferred_element_type=F32)


def _split3(x):
    hi = x.astype(BF16)
    r = x - hi.astype(F32)
    mid = r.astype(BF16)
    lo = (r - mid.astype(F32)).astype(BF16)
    return hi, mid, lo


def _sigmoid(x):
    return 1.0 / (1.0 + jnp.exp(-x))


def _silu(x):
    return x * _sigmoid(x)


def _softplus(x):
    return jnp.maximum(x, 0.0) + jnp.log1p(jnp.exp(-jnp.abs(x)))


def _rope(x, cs, sn):
    w = x.shape[-1]
    n = w // LANES
    if n > 1:
        cs = jnp.concatenate([cs] * n, axis=1)
        sn = jnp.concatenate([sn] * n, axis=1)
    lane = lax.broadcasted_iota(I32, x.shape, 1) & (LANES - 1)
    half = MLA_D_ROPE // 2
    swapped = jnp.where(lane < half, pltpu.roll(x, w - half, 1), pltpu.roll(x, half, 1))
    return x * cs + swapped * sn


def _params(*sem):
    return pltpu.CompilerParams(dimension_semantics=sem, vmem_limit_bytes=VMEM_LIMIT)


def _front_kernel(x_ref, w_ref, convw_ref, hist_ref, gpar_ref, kvnw_ref, wukv_ref, cs_ref, sn_ref,
                  qkv_ref, z_ref, gb_ref, q_ref, k_ref, v_ref, lat_ref, kr_ref, convnew_ref,
                  xp_scr, *, tt, c):
    t = pl.program_id(1)
    hrow = 8 - (GDN_CONV - 1)

    @pl.when(t == 0)
    def _():
        xp_scr[hrow:8, :] = hist_ref[0]

    proj = _mm(x_ref[0], w_ref[...])

    raw = proj[:, PK_QKV:PK_Z]
    xp_scr[8:8 + tt, :] = raw
    cw = convw_ref[...]
    y = raw * cw[GDN_CONV - 1:GDN_CONV]
    for i in range(GDN_CONV - 1):
        y = y + xp_scr[hrow + i:hrow + i + tt, :] * cw[i:i + 1]
    tail = xp_scr[tt + hrow:tt + 8, :]
    convnew_ref[0] = tail
    xp_scr[hrow:8, :] = tail
    qkv = _silu(y)
    for h in range(2 * GDN_HEADS):
        xh = qkv[:, h * GDN_DK:(h + 1) * GDN_DK]
        xh = xh * lax.rsqrt(jnp.sum(xh * xh, axis=-1, keepdims=True) + L2_EPS)
        if h < GDN_HEADS:
            xh = xh * GDN_DK ** -0.5
        qkv_ref[0, :, h * GDN_DK:(h + 1) * GDN_DK] = xh
    qkv_ref[0, :, 2 * GDN_QK_W:] = qkv[:, 2 * GDN_QK_W:]
    z_ref[0] = proj[:, PK_Z:PK_QNOPE]

    ab = proj[:, PK_AB:PK_W]
    gpar = gpar_ref[...]
    gc = -jnp.exp(gpar[0:1]) * _softplus(ab + gpar[1:2])
    pos = lax.broadcasted_iota(I32, ab.shape, 0) & (c - 1)
    step = 1
    while step < c:
        gc = gc + jnp.where(pos >= step, pltpu.roll(gc, step, 0), 0.0)
        step *= 2
    beta = _sigmoid(ab)
    lane = lax.broadcasted_iota(I32, ab.shape, 1)
    gb_ref[0] = jnp.where(lane < GDN_HEADS, gc, jnp.where(lane < 2 * GDN_HEADS, beta, 0.0))

    cs = cs_ref[...]
    sn = sn_ref[...]
    q_nope = proj[:, PK_QNOPE:PK_QROPE]
    q_rope = _rope(proj[:, PK_QROPE:PK_CKV], cs, sn)
    c_raw = proj[:, PK_CKV:PK_KROPE]
    latent = c_raw * lax.rsqrt(jnp.mean(c_raw * c_raw, axis=-1, keepdims=True) + RMS_EPS) * kvnw_ref[...]
    lat_ref[0] = latent
    k_rope = _rope(proj[:, PK_KROPE:PK_AB], cs, sn)
    kr_ref[0] = k_rope[:, :MLA_D_ROPE]
    kv = _mm(latent, wukv_ref[...])
    k_rope_b = k_rope.astype(BF16)
    for h in range(MLA_HEADS):
        q_ref[0, h, :, :LANES] = q_nope[:, h * LANES:(h + 1) * LANES].astype(BF16)
        q_ref[0, h, :, LANES:] = q_rope[:, h * LANES:(h + 1) * LANES].astype(BF16)
        k_ref[0, h, :, :LANES] = kv[:, h * LANES:(h + 1) * LANES].astype(BF16)
        k_ref[0, h, :, LANES:] = k_rope_b
        v_ref[0, h] = kv[:, (MLA_HEADS + h) * LANES:(MLA_HEADS + h + 1) * LANES].astype(BF16)


def _front(x, w_pack, conv_w, hist, gpar, kvnw, wukv, cs_tab, sn_tab, tt, c):
    b, t, _ = x.shape
    nt = t // tt
    const2 = lambda bi, ti: (0, 0)
    out_shape = (
        jax.ShapeDtypeStruct((b, t, GDN_CONV_CH), F32),
        jax.ShapeDtypeStruct((b, t, GDN_V_W), F32),
        jax.ShapeDtypeStruct((b, t, LANES), F32),
        jax.ShapeDtypeStruct((b, MLA_HEADS, t, MLA_QK_W), BF16),
        jax.ShapeDtypeStruct((b, MLA_HEADS, t, MLA_QK_W), BF16),
        jax.ShapeDtypeStruct((b, MLA_HEADS, t, MLA_D_V), BF16),
        jax.ShapeDtypeStruct((b, t, MLA_KV_RANK), F32),
        jax.ShapeDtypeStruct((b, t, MLA_D_ROPE), F32),
        jax.ShapeDtypeStruct((b, GDN_CONV - 1, GDN_CONV_CH), F32),
    )
    row3 = lambda w: pl.BlockSpec((1, tt, w), lambda bi, ti: (bi, ti, 0))
    head4 = lambda w: pl.BlockSpec((1, MLA_HEADS, tt, w), lambda bi, ti: (bi, 0, ti, 0))
    return pl.pallas_call(
        functools.partial(_front_kernel, tt=tt, c=c),
        grid=(b, nt),
        in_specs=[
            row3(D_MODEL),
            pl.BlockSpec((D_MODEL, PK_W), const2),
            pl.BlockSpec((GDN_CONV, GDN_CONV_CH), const2),
            pl.BlockSpec((1, GDN_CONV - 1, GDN_CONV_CH), lambda bi, ti: (bi, 0, 0)),
            pl.BlockSpec((2, LANES), const2),
            pl.BlockSpec((1, MLA_KV_RANK), const2),
            pl.BlockSpec((MLA_KV_RANK, 2 * MLA_HEADS * LANES), const2),
            pl.BlockSpec((tt, LANES), lambda bi, ti: (ti, 0)),
            pl.BlockSpec((tt, LANES), lambda bi, ti: (ti, 0)),
        ],
        out_specs=(
            row3(GDN_CONV_CH), row3(GDN_V_W), row3(LANES),
            head4(MLA_QK_W), head4(MLA_QK_W), head4(MLA_D_V),
            row3(MLA_KV_RANK), row3(MLA_D_ROPE),
            pl.BlockSpec((1, GDN_CONV - 1, GDN_CONV_CH), lambda bi, ti: (bi, 0, 0)),
        ),
        out_shape=out_shape,
        scratch_shapes=[pltpu.VMEM((tt + 8, GDN_CONV_CH), F32)],
        compiler_params=_params("arbitrary", "arbitrary"),
        name="front",
    )(x, w_pack, conv_w, hist, gpar, kvnw, wukv, cs_tab, sn_tab)


def _gdn_kernel(qkv_ref, z_ref, gb_ref, s0_ref, nw_ref, og_ref, sout_ref, s_scr, *, tg, c, bb):
    t = pl.program_id(1)
    nh = GDN_HEADS
    r = nh * c
    sh = c.bit_length() - 1

    @pl.when(t == 0)
    def _():
        s_scr[...] = s0_ref[...]

    row = lax.broadcasted_iota(I32, (r, r), 0)
    col = lax.broadcasted_iota(I32, (r, r), 1)
    same = (row >> sh) == (col >> sh)
    incl = same & (row >= col)
    strict = same & (row > col)
    eye = jnp.where(row == col, 1.0, 0.0)
    lane0 = jnp.where(lax.broadcasted_iota(I32, (r, LANES), 1) == 0, 1.0, 0.0).astype(BF16)
    nw = nw_ref[...]
    items = [(bi, ci) for ci in range(tg // c) for bi in range(bb)]
    chunks = range(len(items))

    def stacked(ref, it, base):
        bi, ci = items[it]
        return jnp.concatenate(
            [ref[bi, ci * c:(ci + 1) * c, base + h * LANES:base + (h + 1) * LANES] for h in range(nh)], axis=0)

    def col_bcast(it, lane):
        bi, ci = items[it]
        gbc = gb_ref[bi, ci * c:(ci + 1) * c, :]
        return jnp.concatenate(
            [jnp.broadcast_to(gbc[:, lane + h:lane + h + 1], (c, LANES)) for h in range(nh)], axis=0)

    def as_col(gc_b):
        return _lane_tile(gc_b, r) if r % LANES == 0 else gc_b[:, :r]

    def as_row(gc_b):
        if r % LANES == 0:
            return as_col(gc_b).T
        return sum(lax.dot_general(lane0, p, (((1,), (1,)), ((), ())), preferred_element_type=F32)
                   for p in _split3(gc_b))

    ks = [stacked(qkv_ref, ci, GDN_QK_W) for ci in chunks]
    gc = [col_bcast(ci, 0) for ci in chunks]
    beta = [col_bcast(ci, nh) for ci in chunks]
    decay, qk_kk = [], []
    for ci in chunks:
        decay.append(jnp.exp(jnp.where(incl, as_col(gc[ci]) - as_row(gc[ci]), -jnp.inf)))
        qk_kk.append(_mm_nt(jnp.concatenate([stacked(qkv_ref, ci, 0), ks[ci]], axis=0), ks[ci]))
    intra = [qk_kk[ci][:r] * decay[ci] for ci in chunks]
    n_pow = [jnp.where(strict, -as_col(beta[ci]) * qk_kk[ci][r:] * decay[ci], 0.0)
             for ci in chunks]
    t_inv = [eye + n_pow[ci] for ci in chunks]
    for _ in range(sh - 1):
        n_pow = [_mm(n_pow[ci], n_pow[ci]) for ci in chunks]
        t_inv = [t_inv[ci] + _mm(t_inv[ci], n_pow[ci]) for ci in chunks]
    egc = [jnp.exp(gc[ci]) for ci in chunks]
    uw = [_mm(t_inv[ci], jnp.concatenate([stacked(qkv_ref, ci, 2 * GDN_QK_W) * beta[ci],
                                          ks[ci] * beta[ci] * egc[ci]], axis=1)) for ci in chunks]

    for ci in chunks:
        bi, cpos = items[ci]
        u = uw[ci][:, :GDN_DV]
        w = uw[ci][:, GDN_DV:]
        qd = stacked(qkv_ref, ci, 0) * egc[ci]
        vn, qs_s = [], []
        for h in range(nh):
            hs = slice(h * c, (h + 1) * c)
            s_h = s_scr[bi, h]
            wq = _mm(jnp.concatenate([w[hs], qd[hs]], axis=0), s_h)
            vn_h = u[hs] - wq[:c]
            g_last = gc[ci][h * c + c - 1:h * c + c, :]
            kd = ks[ci][hs] * jnp.exp(g_last - gc[ci][hs])
            s_scr[bi, h] = s_h * jnp.exp(g_last) + _mm_tn(kd, vn_h)
            vn.append(vn_h)
            qs_s.append(wq[c:])
        o = jnp.concatenate(qs_s, axis=0) + _mm(intra[ci], jnp.concatenate(vn, axis=0))

        o = o * lax.rsqrt(jnp.mean(o * o, axis=-1, keepdims=True) + RMS_EPS) * nw
        o = o * _silu(stacked(z_ref, ci, 0))
        for h in range(nh):
            og_ref[bi, cpos * c:(cpos + 1) * c, h * LANES:(h + 1) * LANES] = o[h * c:(h + 1) * c].astype(BF16)

    @pl.when(t == pl.num_programs(1) - 1)
    def _():
        sout_ref[...] = s_scr[...]


def _gdn(qkv, z, gb, s0, nw, tg, c):
    b, t, _ = qkv.shape
    bb = 2 if b % 2 == 0 else 1
    row3 = lambda w: pl.BlockSpec((bb, tg, w), lambda bi, ti: (bi, ti, 0))
    st = pl.BlockSpec((bb, GDN_HEADS, GDN_DK, GDN_DV), lambda bi, ti: (bi, 0, 0, 0))
    return pl.pallas_call(
        functools.partial(_gdn_kernel, tg=tg, c=c, bb=bb),
        grid=(b // bb, t // tg),
        in_specs=[row3(GDN_CONV_CH), row3(GDN_V_W), row3(LANES), st,
                  pl.BlockSpec((1, GDN_DV), lambda bi, ti: (0, 0))],
        out_specs=(row3(GDN_V_W), st),
        out_shape=(jax.ShapeDtypeStruct((b, t, GDN_V_W), BF16),
                   jax.ShapeDtypeStruct((b, GDN_HEADS, GDN_DK, GDN_DV), F32)),
        scratch_shapes=[pltpu.VMEM((bb, GDN_HEADS, GDN_DK, GDN_DV), F32)],
        compiler_params=_params("arbitrary", "arbitrary"),
        name="gdn",
    )(qkv, z, gb, s0, nw)


ATTN_ROW_BLOCK = 32


def _lane_tile(x, width):
    return x if width == LANES else jnp.concatenate([x] * (width // LANES), axis=1)


def _attn_kernel(qi_ref, ki_ref, q_ref, k_ref, v_ref, o_ref, m_scr, l_scr, acc_scr, a_scr, s_scr, p_scr,
                 *, tb, sub):
    step = pl.program_id(2)
    qi = qi_ref[step]
    ki = ki_ref[step]
    csh = CHUNK.bit_length() - 1
    c2 = MLA_SCALE * math.log2(math.e)

    @pl.when(ki == 0)
    def _():
        m_scr[...] = jnp.full(m_scr.shape, -jnp.inf, F32)
        l_scr[...] = jnp.zeros(l_scr.shape, F32)
        acc_scr[...] = jnp.zeros(acc_scr.shape, F32)

    half = tb // 2
    rb = min(ATTN_ROW_BLOCK, half)

    def scores(h, j):
        rows = slice(h * half, (h + 1) * half)
        s_scr[rows, :] = lax.dot_general(q_ref[0, 0, rows, :], k_ref[0, 0, j * sub:(j + 1) * sub, :],
                                         (((1,), (1,)), ((), ())), preferred_element_type=F32)

    def softmax(h, j, masked):
        r0 = h * half

        def block(i):
            rr = pl.ds(pl.multiple_of(r0 + i * rb, rb), rb)
            s = s_scr[rr, :]
            if masked:
                qc = (r0 + i * rb + lax.broadcasted_iota(I32, s.shape, 0)) >> csh
                kc = (j * sub + lax.broadcasted_iota(I32, s.shape, 1)) >> csh
                s = jnp.where(kc <= qc, s, -jnp.inf)
            return rr, s

        def row_max(i, carry):
            rr, s = block(i)
            m_prev = m_scr[rr, :]
            m_new = jnp.maximum(m_prev, jnp.max(s, axis=-1, keepdims=True))
            a_scr[rr, :] = jnp.exp2((m_prev - m_new) * c2)
            m_scr[rr, :] = m_new
            return carry

        def row_exp(i, carry):
            rr, s = block(i)
            p = jnp.exp2((s - _lane_tile(m_scr[rr, :], sub)) * c2)
            l_scr[rr, :] = a_scr[rr, :] * l_scr[rr, :] + jnp.sum(p, axis=-1, keepdims=True)
            p_scr[rr, :] = p.astype(BF16)
            return carry

        for i in range(half // rb):
            row_max(i, 0)
        for i in range(half // rb):
            row_exp(i, 0)

    def weighted_values(h, j):
        rows = slice(h * half, (h + 1) * half)
        acc_scr[rows, :] = a_scr[rows, :] * acc_scr[rows, :] + jnp.dot(
            p_scr[rows, :], v_ref[0, 0, j * sub:(j + 1) * sub, :], preferred_element_type=F32)

    def run(tasks):
        for h in range(2):
            if tasks[h]:
                scores(h, tasks[h][0][0])
        for n in range(max(len(t) for t in tasks)):
            for h in range(2):
                if n < len(tasks[h]):
                    j, masked = tasks[h][n]
                    softmax(h, j, masked)
                    weighted_values(h, j)
                    if n + 1 < len(tasks[h]):
                        scores(h, tasks[h][n + 1][0])

    def diagonal_tasks(h):
        r0 = h * half
        out = []
        for j in range(tb // sub):
            k0, k1 = j * sub, (j + 1) * sub
            if k0 >= r0 + half:
                continue
            out.append((j, k1 > r0 + CHUNK))
        return out

    @pl.when(ki < qi)
    def _():
        run([[(j, False) for j in range(tb // sub)]] * 2)

    @pl.when(ki == qi)
    def _():
        run([diagonal_tasks(0), diagonal_tasks(1)])
        o_ref[0] = (acc_scr[...] / l_scr[...]).astype(BF16)


def _attn_prompt(q, k, v, tb, sub):
    b, nh, t, _ = q.shape
    nt = t // tb
    pairs = [(qi, ki) for qi in range(nt) for ki in range(qi + 1)]
    qi_of = jnp.asarray([p[0] for p in pairs], I32)
    ki_of = jnp.asarray([p[1] for p in pairs], I32)
    return pl.pallas_call(
        functools.partial(_attn_kernel, tb=tb, sub=sub),
        grid_spec=pltpu.PrefetchScalarGridSpec(
            num_scalar_prefetch=2,
            grid=(b, nh, len(pairs)),
            in_specs=[pl.BlockSpec((1, 1, tb, MLA_QK_W), lambda bi, hi, s, qo, ko: (bi, hi, qo[s], 0)),
                      pl.BlockSpec((1, 1, tb, MLA_QK_W), lambda bi, hi, s, qo, ko: (bi, hi, ko[s], 0)),
                      pl.BlockSpec((1, 1, tb, MLA_D_V), lambda bi, hi, s, qo, ko: (bi, hi, ko[s], 0))],
            out_specs=pl.BlockSpec((1, tb, MLA_D_V), lambda bi, hi, s, qo, ko: (bi, qo[s], hi)),
            scratch_shapes=[pltpu.VMEM((tb, LANES), F32), pltpu.VMEM((tb, LANES), F32),
                            pltpu.VMEM((tb, MLA_D_V), F32), pltpu.VMEM((tb, LANES), F32),
                            pltpu.VMEM((tb, sub), F32), pltpu.VMEM((tb, sub), BF16)]),
        out_shape=jax.ShapeDtypeStruct((b, t, MLA_HEADS * MLA_D_V), BF16),
        compiler_params=_params("arbitrary", "arbitrary", "arbitrary"),
        name="attn_prompt",
    )(qi_of, ki_of, q, k, v)


def _attn_sample_kernel(q_ref, kn_ref, vn_ref, plat_ref, pkr_ref, wukv_ref, o_ref):
    kvp = _mm(plat_ref[0], wukv_ref[...])
    pkr = pkr_ref[0].astype(BF16)
    for h in range(MLA_HEADS):
        q = q_ref[0, h]
        s_past = (_mm_nt(q[:, :MLA_D_NOPE], kvp[:, h * LANES:(h + 1) * LANES])
                  + _mm_nt(q[:, MLA_D_NOPE:MLA_D_NOPE + MLA_D_ROPE], pkr)) * MLA_SCALE
        s_new = _mm_nt(q, kn_ref[0, h]) * MLA_SCALE
        m = jnp.maximum(jnp.max(s_past, axis=-1, keepdims=True), jnp.max(s_new, axis=-1, keepdims=True))
        p_past = jnp.exp(s_past - m)
        p_new = jnp.exp(s_new - m)
        l = jnp.sum(p_past, axis=-1, keepdims=True) + jnp.sum(p_new, axis=-1, keepdims=True)
        o = _mm(p_past, kvp[:, (MLA_HEADS + h) * LANES:(MLA_HEADS + h + 1) * LANES]) + _mm(p_new, vn_ref[0, h])
        o_ref[0, :, h * MLA_D_V:(h + 1) * MLA_D_V] = (o / l).astype(BF16)


def _attn_sample(q, k_new, v_new, past_lat, past_kr, wukv):
    b, nh, ts, _ = q.shape
    past = past_lat.shape[1]
    b4 = lambda w: pl.BlockSpec((1, nh, ts, w), lambda bi: (bi, 0, 0, 0))
    return pl.pallas_call(
        _attn_sample_kernel,
        grid=(b,),
        in_specs=[b4(MLA_QK_W), b4(MLA_QK_W), b4(MLA_D_V),
                  pl.BlockSpec((1, past, MLA_KV_RANK), lambda bi: (bi, 0, 0)),
                  pl.BlockSpec((1, past, MLA_D_ROPE), lambda bi: (bi, 0, 0)),
                  pl.BlockSpec((MLA_KV_RANK, 2 * MLA_HEADS * LANES), lambda bi: (0, 0))],
        out_specs=pl.BlockSpec((1, ts, MLA_HEADS * MLA_D_V), lambda bi: (bi, 0, 0)),
        out_shape=jax.ShapeDtypeStruct((b, ts, MLA_HEADS * MLA_D_V), BF16),
        compiler_params=_params("arbitrary"),
        name="attn_sample",
    )(q, k_new, v_new, past_lat, past_kr, wukv)


def _layernorm(y, g, b):
    mu = jnp.mean(y, axis=-1, keepdims=True)
    d = y - mu
    var = jnp.mean(d * d, axis=-1, keepdims=True)
    return d * lax.rsqrt(var + LN_EPS) * g + b


N_SHARED = 4


def _mixln_kernel(og_ref, om_ref, x_ref, w_ref, g_ref, b_ref, rw_ref, rb_ref, cnt0_ref, *rest, tm):
    o_ref, idx_ref, gate_ref, rank_ref, cnt_ref, carry_scr = rest[-6:]

    @pl.when(pl.program_id(0) == 0)
    def _():
        carry_scr[...] = cnt0_ref[...]

    mix = (jnp.dot(og_ref[...], w_ref[:GDN_V_W, :], preferred_element_type=F32)
           + jnp.dot(om_ref[...], w_ref[GDN_V_W:, :], preferred_element_type=F32))
    x1 = _layernorm(DEEPNORM_ALPHA * x_ref[...] + mix, g_ref[...], b_ref[...])
    o_ref[...] = x1
    _route_tile(x1, rw_ref, rb_ref, carry_scr, idx_ref, gate_ref, rank_ref, tm)
    cnt_ref[...] = carry_scr[...]


def _mixln(og, om, x, w_out, g, b, rw, rb, cnt0, shared, n_all, row0, tm):
    n = x.shape[0]
    blk0 = row0 // tm
    rows = lambda w: pl.BlockSpec((tm, w), lambda i: (i, 0))
    const = lambda s: pl.BlockSpec(s, lambda i: (0, 0))
    kspec = pl.BlockSpec((TOP_K, tm), lambda i: (0, blk0 + i))
    first = shared is None
    n_in = 9
    kt = lambda dt: jax.ShapeDtypeStruct((TOP_K, n_all), dt)
    return pl.pallas_call(
        functools.partial(_mixln_kernel, tm=tm),
        grid=(n // tm,),
        in_specs=[rows(GDN_V_W), rows(MLA_HEADS * MLA_D_V), rows(D_MODEL),
                  const((D_MODEL, D_MODEL)), const((1, D_MODEL)), const((1, D_MODEL)),
                  const((D_MODEL, N_EXPERTS)), const((N_EXPERTS, 1)), const((N_EXPERTS, LANES))]
                 + ([] if first else [pl.BlockSpec(memory_space=pl.ANY)] * N_SHARED),
        out_specs=(pl.BlockSpec((tm, D_MODEL), lambda i: (blk0 + i, 0)), kspec, kspec, kspec,
                   const((N_EXPERTS, LANES))),
        out_shape=(jax.ShapeDtypeStruct((n_all, D_MODEL), F32), kt(I32), kt(F32), kt(I32),
                   jax.ShapeDtypeStruct((N_EXPERTS, LANES), F32)),
        scratch_shapes=[pltpu.VMEM((N_EXPERTS, LANES), F32)],
        input_output_aliases={} if first else {n_in + j: j for j in range(N_SHARED)},
        compiler_params=_params("arbitrary"),
        name="mixln",
    )(*((og, om, x, w_out, g, b, rw, rb, cnt0) + (() if first else tuple(shared))))


def _route_tile(x, rw_ref, rb_ref, carry_scr, idx_ref, gate_ref, rank_ref, tt):
    ninf = -jnp.inf
    big = float(2 * N_EXPERTS)
    scores = _sigmoid(_mm(x, rw_ref[...]).T)
    biased = scores + rb_ref[...]
    eio = lax.broadcasted_iota(I32, (N_EXPERTS, tt), 0).astype(F32)

    def first_argmax(vals, io):
        m = jnp.max(vals, axis=0, keepdims=True)
        i = jnp.min(jnp.where(vals == m, io, big), axis=0, keepdims=True)
        return m, i

    gs = []
    for g in range(N_GROUPS):
        blk = biased[g * GROUP_SIZE:(g + 1) * GROUP_SIZE]
        io = (lax.broadcasted_iota(I32, (GROUP_SIZE, tt), 0) + g * GROUP_SIZE).astype(F32)
        m1, i1 = first_argmax(blk, io)
        m2 = jnp.max(jnp.where(io == i1, ninf, blk), axis=0, keepdims=True)
        gs.append(m1 + m2)
    gio = lax.broadcasted_iota(I32, (N_GROUPS, tt), 0).astype(F32)
    gsc = jnp.zeros((N_GROUPS, tt), F32)
    for g in range(N_GROUPS):
        gsc = jnp.where(gio == float(g), gs[g], gsc)
    gsel = jnp.zeros((N_GROUPS, tt), F32)
    for _ in range(TOPK_GROUPS):
        _, gi = first_argmax(gsc, gio)
        hit = gio == gi
        gsel = jnp.where(hit, 1.0, gsel)
        gsc = jnp.where(hit, ninf, gsc)
    masked = jnp.concatenate(
        [jnp.where(jnp.max(jnp.where(gio == float(g), gsel, 0.0), axis=0, keepdims=True) > 0.0,
                   biased[g * GROUP_SIZE:(g + 1) * GROUP_SIZE], ninf) for g in range(N_GROUPS)], axis=0)

    idx, wts = [], []
    sel = jnp.zeros((N_EXPERTS, tt), F32)
    for _ in range(TOP_K):
        _, ei = first_argmax(masked, eio)
        hit = eio == ei
        wts.append(jnp.sum(jnp.where(hit, scores, 0.0), axis=0, keepdims=True))
        masked = jnp.where(hit, ninf, masked)
        sel = jnp.where(hit, 1.0, sel)
        idx.append(ei)
    wsum = wts[0]
    for w in wts[1:]:
        wsum = wsum + w

    t0 = lax.broadcasted_iota(I32, (tt, tt), 0)
    t1 = lax.broadcasted_iota(I32, (tt, tt), 1)
    before = jnp.where(t0 < t1, 1.0, 0.0).astype(BF16)
    sel_b = sel.astype(BF16)
    base = carry_scr[:, :1] + jnp.dot(sel_b, before, preferred_element_type=F32)
    ranks = [jnp.sum(jnp.where(eio == ei, base, 0.0), axis=0, keepdims=True) for ei in idx]
    carry_scr[...] = carry_scr[...] + jnp.dot(sel_b, jnp.ones((tt, LANES), BF16), preferred_element_type=F32)

    for k in range(TOP_K):
        idx_ref[k:k + 1, :] = idx[k].astype(I32)
        gate_ref[k:k + 1, :] = wts[k] / wsum * ROUTED_SCALE
        rank_ref[k:k + 1, :] = ranks[k].astype(I32)


def _dest_kernel(idx_ref, rank_ref, pstart_ref, dest_ref, *, tt):
    eio = lax.broadcasted_iota(I32, (N_EXPERTS, tt), 0)
    pstart = pstart_ref[...]
    for k in range(TOP_K):
        start = jnp.sum(jnp.where(eio == idx_ref[k:k + 1, :], pstart, 0.0), axis=0, keepdims=True)
        dest_ref[0, k:k + 1, :] = start.astype(I32) + rank_ref[k:k + 1, :]


def _dest(idx, rank, pstart, tt):
    n = idx.shape[1]
    kspec = pl.BlockSpec((TOP_K, tt), lambda i: (0, i))
    return pl.pallas_call(
        functools.partial(_dest_kernel, tt=tt),
        grid=(n // tt,),
        in_specs=[kspec, kspec, pl.BlockSpec((N_EXPERTS, 1), lambda i: (0, 0))],
        out_specs=pl.BlockSpec((1, TOP_K, tt), lambda i: (i, 0, 0)),
        out_shape=jax.ShapeDtypeStruct((n // tt, TOP_K, tt), I32),
        compiler_params=_params("arbitrary"),
        name="dest",
    )(idx, rank, pstart)


ROW_TILE = D_MODEL // LANES
XROW_TILE = ROW_TILE // 2


def _rows_to_tiles(x, tiles_ref, base, n, rt=ROW_TILE):
    for j in range(rt):
        tiles_ref[pl.ds(base * rt + j, n, stride=rt), :] = x[:, j * LANES:(j + 1) * LANES]


def _tiles_to_rows(tiles_ref, base, n, rt=ROW_TILE):
    return jnp.concatenate(
        [tiles_ref[pl.ds(base * rt + j, n, stride=rt), :] for j in range(rt)], axis=1)


def _pack_bf16_pairs(x):
    half = x.shape[1] // 2
    lo = pltpu.bitcast(x[:, :half].astype(BF16).astype(F32), jnp.uint32) >> 16
    hi = pltpu.bitcast(x[:, half:].astype(BF16).astype(F32), jnp.uint32) & jnp.uint32(0xFFFF0000)
    return lo | hi


def _unpack_bf16_pairs(w):
    lo = pltpu.bitcast(w << 16, F32)
    hi = pltpu.bitcast(w & jnp.uint32(0xFFFF0000), F32)
    return jnp.concatenate([lo, hi], axis=1)


def _dispatch_kernel(tail_ref, dest_ref, x_ref, xs_out, zbuf, xt_scr, sem, zsem, *, td):
    s = pl.program_id(0)

    @pl.when(s == 0)
    def _():
        zbuf[...] = jnp.zeros(zbuf.shape, zbuf.dtype)

        def zero_copy(e):
            first = pl.multiple_of(tail_ref[e] * XROW_TILE, MOE_BM * XROW_TILE)
            return pltpu.make_async_copy(zbuf, xs_out.at[pl.ds(first, MOE_BM * XROW_TILE)], zsem)

        def zstart(e, carry):
            zero_copy(e).start()
            return carry

        def zwait(e, carry):
            zero_copy(e).wait()
            return carry

        lax.fori_loop(0, N_EXPERTS, zstart, 0)
        lax.fori_loop(0, N_EXPERTS, zwait, 0)

    slot = s & 1

    def row_copy(slot_, i, d):
        src = pl.multiple_of((slot_ * td + i) * XROW_TILE, XROW_TILE)
        return pltpu.make_async_copy(xt_scr.at[pl.ds(src, XROW_TILE)],
                                     xs_out.at[pl.ds(pl.multiple_of(d * XROW_TILE, XROW_TILE), XROW_TILE)],
                                     sem.at[slot_])

    def drain(slot_):
        step_rows = TOP_K * td * XROW_TILE
        pltpu.make_async_copy(xs_out.at[pl.ds(0, step_rows)], xs_out.at[pl.ds(0, step_rows)],
                              sem.at[slot_]).wait()

    @pl.when(s >= 2)
    def _():
        drain(slot)

    _rows_to_tiles(_pack_bf16_pairs(x_ref[...]), xt_scr, slot * td, td, XROW_TILE)

    def issue(i, carry):
        for k in range(TOP_K):
            row_copy(slot, i, dest_ref[0, 0, k * td + i]).start(priority=k % 2)
        return carry

    lax.fori_loop(0, td, issue, 0, unroll=4)

    @pl.when(s == pl.num_programs(0) - 1)
    def _():
        drain(slot)

        @pl.when(s >= 1)
        def _():
            drain(1 - slot)


def _dispatch(tail, dest, x1, n_rows, td):
    n = x1.shape[0]
    return pl.pallas_call(
        functools.partial(_dispatch_kernel, td=td),
        grid_spec=pltpu.PrefetchScalarGridSpec(
            num_scalar_prefetch=1,
            grid=(n // td,),
            in_specs=[pl.BlockSpec((1, 1, TOP_K * td), lambda i, tl: (i, 0, 0), memory_space=pltpu.SMEM),
                      pl.BlockSpec((td, D_MODEL), lambda i, tl: (i, 0))],
            out_specs=pl.BlockSpec(memory_space=pl.ANY),
            scratch_shapes=[pltpu.VMEM((MOE_BM * XROW_TILE, LANES), jnp.uint32),
                            pltpu.VMEM((2 * td * XROW_TILE, LANES), jnp.uint32),
                            pltpu.SemaphoreType.DMA((2,)), pltpu.SemaphoreType.DMA(())]),
        out_shape=jax.ShapeDtypeStruct((n_rows * XROW_TILE, LANES), jnp.uint32),
        compiler_params=_params("arbitrary"),
        name="dispatch",
    )(tail, dest, x1)


def _expert_kernel(first_ref, nblk_ref, nact_ref, xs_hbm, wg_ref, wu_ref, wd_ref, y_hbm,
                   xbuf, ybuf, wgb, wub, wdb, xsem, ysem):
    e = pl.program_id(0)
    nact = nact_ref[0]
    ring = EXPERT_RING
    xr = MOE_BM * XROW_TILE
    yr = MOE_BM * XROW_TILE

    def x_copy(g):
        slot = g & (ring - 1)
        return pltpu.make_async_copy(xs_hbm.at[pl.ds(pl.multiple_of(g * xr, xr), xr)],
                                     xbuf.at[pl.ds(pl.multiple_of(slot * xr, xr), xr)], xsem.at[slot])

    def y_copy(g):
        slot = g & (ring - 1)
        return pltpu.make_async_copy(ybuf.at[pl.ds(pl.multiple_of(slot * yr, yr), yr)],
                                     y_hbm.at[pl.ds(pl.multiple_of(g * yr, yr), yr)], ysem.at[slot])

    @pl.when(e == 0)
    def _():
        for g0 in range(ring - 1):
            @pl.when(g0 < nact)
            def _():
                x_copy(g0).start()

    @pl.when(nblk_ref[e] > 0)
    def _():
        wgb[...] = wg_ref[0].astype(BF16)
        wub[...] = wu_ref[0].astype(BF16)
        wdb[...] = wd_ref[0].astype(BF16)

    def block(b, carry):
        g = first_ref[e] + b
        slot = g & (ring - 1)
        x_copy(g).wait()

        @pl.when(g + ring - 1 < nact)
        def _():
            x_copy(g + ring - 1).start()

        @pl.when(g >= ring)
        def _():
            y_copy(g - ring).wait()

        xb = _unpack_bf16_pairs(_tiles_to_rows(xbuf, slot * MOE_BM, MOE_BM, XROW_TILE)).astype(BF16)
        hg = jnp.dot(xb, wgb[...], preferred_element_type=F32)
        hu = jnp.dot(xb, wub[...], preferred_element_type=F32)
        y = jnp.dot((_silu(hg) * hu).astype(BF16), wdb[...], preferred_element_type=F32)
        _rows_to_tiles(_pack_bf16_pairs(y), ybuf, slot * MOE_BM, MOE_BM, XROW_TILE)
        y_copy(g).start()

        @pl.when(g == nact - 1)
        def _():
            for back in range(ring):
                @pl.when(g >= back)
                def _():
                    y_copy(g - back).wait()

        return carry

    lax.fori_loop(0, nblk_ref[e], block, 0)


def _experts(first, nblk, nact, xs, wg, wu, wd):
    n_rows = xs.shape[0] // XROW_TILE
    wspec = lambda s: pl.BlockSpec((1,) + s, lambda e, fr, nb, na: (e, 0, 0))
    return pl.pallas_call(
        _expert_kernel,
        grid_spec=pltpu.PrefetchScalarGridSpec(
            num_scalar_prefetch=3,
            grid=(N_EXPERTS,),
            in_specs=[pl.BlockSpec(memory_space=pl.ANY),
                      wspec((D_MODEL, EXP_HIDDEN)), wspec((D_MODEL, EXP_HIDDEN)), wspec((EXP_HIDDEN, D_MODEL))],
            out_specs=pl.BlockSpec(memory_space=pl.ANY),
            scratch_shapes=[pltpu.VMEM((EXPERT_RING * MOE_BM * XROW_TILE, LANES), jnp.uint32),
                            pltpu.VMEM((EXPERT_RING * MOE_BM * XROW_TILE, LANES), jnp.uint32),
                            pltpu.VMEM((D_MODEL, EXP_HIDDEN), BF16), pltpu.VMEM((D_MODEL, EXP_HIDDEN), BF16),
                            pltpu.VMEM((EXP_HIDDEN, D_MODEL), BF16),
                            pltpu.SemaphoreType.DMA((EXPERT_RING,)), pltpu.SemaphoreType.DMA((EXPERT_RING,))]),
        out_shape=jax.ShapeDtypeStruct((n_rows * XROW_TILE, LANES), jnp.uint32),
        compiler_params=_params("arbitrary"),
        name="experts",
    )(first, nblk, nact, xs, wg, wu, wd)


def _combine_kernel(dcur_ref, dnxt_ref, y_hbm, gate_ref, x_ref, wsg_ref, wsu_ref, wsd_ref, g_ref, b_ref,
                    outp_ref, outs_ref, buf, sem, *, tc, np_tiles):
    s = pl.program_id(0)
    ns = pl.num_programs(0)
    slot = s % 2

    def row_copy(d, slot_, k, i):
        dst = pl.multiple_of(((slot_ * TOP_K + k) * tc + i) * XROW_TILE, XROW_TILE)
        return pltpu.make_async_copy(y_hbm.at[pl.ds(pl.multiple_of(d * XROW_TILE, XROW_TILE), XROW_TILE)],
                                     buf.at[pl.ds(dst, XROW_TILE)], sem.at[slot_])

    def issue(dref, slot_):
        def body(i, carry):
            for k in range(TOP_K):
                row_copy(dref[0, 0, k * tc + i], slot_, k, i).start(priority=k % 2)
            return carry
        lax.fori_loop(0, tc, body, 0, unroll=4)

    @pl.when(s == 0)
    def _():
        issue(dcur_ref, 0)

    @pl.when(s + 1 < ns)
    def _():
        issue(dnxt_ref, 1 - slot)

    slot_rows = TOP_K * tc * XROW_TILE
    pltpu.make_async_copy(y_hbm.at[pl.ds(0, slot_rows)],
                          buf.at[pl.ds(pl.multiple_of(slot * slot_rows, slot_rows), slot_rows)],
                          sem.at[slot]).wait()

    x = x_ref[...]
    gate = gate_ref[...]
    def expert_rows(k):
        return _unpack_bf16_pairs(_tiles_to_rows(buf, (slot * TOP_K + k) * tc, tc, XROW_TILE))

    routed = expert_rows(0) * gate[:, 0:1]
    for k in range(1, TOP_K):
        routed = routed + expert_rows(k) * gate[:, k:k + 1]
    xb = x.astype(BF16)
    shared = _mm(_silu(_mm(xb, wsg_ref[...])) * _mm(xb, wsu_ref[...]), wsd_ref[...])
    out = _layernorm(DEEPNORM_ALPHA * x + (routed + shared), g_ref[...], b_ref[...])

    @pl.when(s < np_tiles)
    def _():
        outp_ref[...] = out

    @pl.when(s >= np_tiles)
    def _():
        outs_ref[...] = out


def _combine(dest, y_sorted, gate, x1, wsg, wsu, wsd, g, b, n_prompt, tc):
    n = x1.shape[0]
    ns = n // tc
    np_tiles = n_prompt // tc
    const = lambda s: pl.BlockSpec(s, lambda i: (0, 0))
    dspec = lambda f: pl.BlockSpec((1, 1, TOP_K * tc), f, memory_space=pltpu.SMEM)
    return pl.pallas_call(
        functools.partial(_combine_kernel, tc=tc, np_tiles=np_tiles),
        grid=(ns,),
        in_specs=[dspec(lambda i: (i, 0, 0)), dspec(lambda i: (jnp.minimum(i + 1, ns - 1), 0, 0)),
                  pl.BlockSpec(memory_space=pl.ANY),
                  pl.BlockSpec((tc, TOP_K), lambda i: (i, 0)),
                  pl.BlockSpec((tc, D_MODEL), lambda i: (i, 0)),
                  const((D_MODEL, EXP_HIDDEN)), const((D_MODEL, EXP_HIDDEN)), const((EXP_HIDDEN, D_MODEL)),
                  const((1, D_MODEL)), const((1, D_MODEL))],
        out_specs=(pl.BlockSpec((tc, D_MODEL), lambda i: (jnp.minimum(i, np_tiles - 1), 0)),
                   pl.BlockSpec((tc, D_MODEL), lambda i: (jnp.maximum(i - np_tiles, 0), 0))),
        out_shape=(jax.ShapeDtypeStruct((n_prompt, D_MODEL), F32),
                   jax.ShapeDtypeStruct((n - n_prompt, D_MODEL), F32)),
        scratch_shapes=[pltpu.VMEM((2 * TOP_K * tc * XROW_TILE, LANES), jnp.uint32),
                        pltpu.SemaphoreType.DMA((2,))],
        compiler_params=_params("arbitrary"),
        name="combine",
    )(dest, dest, y_sorted, gate, x1, wsg, wsu, wsd, g, b)


def _pack_w_in(w_in):
    d = w_in.shape[0]
    o_z = GDN_CONV_CH
    o_a = o_z + GDN_V_W
    o_b = o_a + GDN_HEADS
    o_q = o_b + GDN_HEADS
    o_c = o_q + MLA_HEADS * (MLA_D_NOPE + MLA_D_ROPE)
    o_kr = o_c + MLA_KV_RANK
    zeros = lambda w: jnp.zeros((d, w), w_in.dtype)
    wq = w_in[:, o_q:o_c].reshape(d, MLA_HEADS, MLA_D_NOPE + MLA_D_ROPE)
    q_nope = wq[:, :, :MLA_D_NOPE].reshape(d, MLA_HEADS * MLA_D_NOPE)
    q_rope = jnp.pad(wq[:, :, MLA_D_NOPE:], ((0, 0), (0, 0), (0, LANES - MLA_D_ROPE))).reshape(d, MLA_HEADS * LANES)
    cols = [w_in[:, :o_a], q_nope, q_rope, w_in[:, o_c:o_kr], w_in[:, o_kr:], zeros(LANES - MLA_D_ROPE),
            w_in[:, o_a:o_q], zeros(LANES - 2 * GDN_HEADS)]
    return jnp.concatenate(cols, axis=1).astype(BF16)


def _rope_tables(pos):
    inv_freq = ROPE_THETA ** (-jnp.arange(0, MLA_D_ROPE, 2, dtype=F32) / MLA_D_ROPE)
    ang = pos.astype(F32)[:, None] * inv_freq[None, :]
    cos, sin = jnp.cos(ang), jnp.sin(ang)
    pad = jnp.zeros((pos.shape[0], LANES - MLA_D_ROPE), F32)
    return jnp.concatenate([cos, cos, pad], axis=1), jnp.concatenate([-sin, sin, pad], axis=1)


def _pick(t, pref):
    return pref if t % pref == 0 else t


def _token_mixers(x, pos, conv_hist, s0, past, wts, shared, cnt0, n_all, row0):
    b, t, _ = x.shape
    cs_tab, sn_tab = _rope_tables(pos)
    tt = _pick(t, 512)
    c = min(CHUNK, t)
    qkv, z, gb, q, k, v, latent, k_rope, conv_new = _front(
        x, wts["w_pack"], wts["conv_w"], conv_hist, wts["gpar"], wts["kvnw"], wts["wukv"], cs_tab, sn_tab, tt, c)
    og, s_new = _gdn(qkv, z, gb, s0, wts["gdn_nw"], _pick(t, 4 * CHUNK), c)
    if past is None:
        tb = _pick(t, 2048)
        om = _attn_prompt(q, k, v, tb, _pick(tb, 512))
    else:
        om = _attn_sample(q, k, v, past[0], past[1], wts["wukv"])
    n = b * t
    *shared, cnt = _mixln(og.reshape(n, -1), om.reshape(n, -1), x.reshape(n, D_MODEL), wts["w_out"],
                          wts["ln1_g"], wts["ln1_b"], wts["router_w"], wts["router_b"], cnt0,
                          shared, n_all, row0, _pick(n, 256))
    return shared, cnt, latent, k_rope, s_new, conv_new


def _moe(x1_all, idx, gate, rank, cnt, n_prompt, wts):
    n = x1_all.shape[0]
    counts = cnt[:, 0].astype(I32)
    padded = (counts + MOE_BM - 1) // MOE_BM * MOE_BM
    pend = jnp.cumsum(padded)
    pstart = pend - padded
    td = _pick(math.gcd(n_prompt, n - n_prompt), 256)
    dest = _dest(idx, rank, pstart.astype(F32).reshape(-1, 1), td)
    dest = dest.reshape(n // td, 1, TOP_K * td)
    n_blocks = n * TOP_K // MOE_BM + N_EXPERTS
    nact = (pend[-1:] // MOE_BM).astype(I32)
    tail = jnp.maximum(pend - MOE_BM, 0).astype(I32)

    xs = _dispatch(tail, dest, x1_all, n_blocks * MOE_BM, td)
    y_sorted = _experts((pstart // MOE_BM).astype(I32), (padded // MOE_BM).astype(I32), nact, xs,
                        wts["exp_wg"], wts["exp_wu"], wts["exp_wd"])
    return _combine(dest, y_sorted, gate.T, x1_all,
                    wts["sh_wg"], wts["sh_wu"], wts["sh_wd"], wts["ln2_g"], wts["ln2_b"], n_prompt, td)


def kernel(x_prompt, x_sample, cache_kv_latent, cache_k_rope, state_gdn, state_conv, w_in, gdn_conv_w, gdn_a_log, gdn_dt_bias, gdn_norm_w, mla_kv_norm_w, mla_w_uk, mla_w_uv, w_out, ln1_g, ln1_b, router_w, router_bias, exp_w_gate, exp_w_up, exp_w_down, shared_w_gate, shared_w_up, shared_w_down, ln2_g, ln2_b):
    assert w_in.shape[0] == 1, "single-layer stack"
    b_p, t_p, _ = x_prompt.shape
    b_s, t_s, _ = x_sample.shape
    past = cache_kv_latent.shape[2]
    l = 0
    pad4 = lambda a: jnp.pad(a.astype(F32), (0, LANES - GDN_HEADS))
    wts = {
        "w_pack": _pack_w_in(w_in[l]),
        "conv_w": gdn_conv_w[l],
        "gpar": jnp.stack([pad4(gdn_a_log[l]), pad4(gdn_dt_bias[l])]),
        "kvnw": mla_kv_norm_w[l].reshape(1, -1),
        "wukv": jnp.concatenate([mla_w_uk[l].reshape(MLA_KV_RANK, -1), mla_w_uv[l].reshape(MLA_KV_RANK, -1)],
                                axis=1).astype(BF16),
        "gdn_nw": gdn_norm_w[l].reshape(1, -1),
        "w_out": w_out[l].astype(BF16),
        "ln1_g": ln1_g[l].reshape(1, -1), "ln1_b": ln1_b[l].reshape(1, -1),
        "router_w": router_w[l].astype(BF16), "router_b": router_bias[l].reshape(-1, 1),
        "exp_wg": exp_w_gate[l], "exp_wu": exp_w_up[l], "exp_wd": exp_w_down[l],
        "sh_wg": shared_w_gate[l].astype(BF16), "sh_wu": shared_w_up[l].astype(BF16),
        "sh_wd": shared_w_down[l].astype(BF16),
        "ln2_g": ln2_g[l].reshape(1, -1), "ln2_b": ln2_b[l].reshape(1, -1),
    }
    n_p, n_s = b_p * t_p, b_s * t_s
    conv0 = jnp.zeros((b_p, GDN_CONV - 1, GDN_CONV_CH), F32)
    s0 = jnp.zeros((b_p, GDN_HEADS, GDN_DK, GDN_DV), F32)
    cnt0 = jnp.zeros((N_EXPERTS, LANES), F32)
    shared, cnt, lat_p, kr_p, sg_p, cv_p = _token_mixers(
        x_prompt, jnp.arange(t_p), conv0, s0, None, wts, None, cnt0, n_p + n_s, 0)
    shared, cnt, lat_s, kr_s, sg_s, cv_s = _token_mixers(
        x_sample, past + jnp.arange(t_s), state_conv[l], state_gdn[l],
        (cache_kv_latent[l], cache_k_rope[l]), wts, shared, cnt, n_p + n_s, n_p)
    y_p, y_s = _moe(*shared, cnt, n_p, wts)
    return (y_p.reshape(b_p, t_p, D_MODEL), y_s.reshape(b_s, t_s, D_MODEL),
            lat_p[None], kr_p[None], sg_p[None], cv_p[None],
            lat_s[None], kr_s[None], sg_s[None], cv_s[None])
```

```python
import functools
import math

import jax
import jax.numpy as jnp
from jax import lax
from jax.experimental import pallas as pl
from jax.experimental.pallas import tpu as pltpu

F32 = jnp.float32
BF16 = jnp.bfloat16
I32 = jnp.int32

D_MODEL = 1024
CHUNK = 64
GDN_HEADS = 4
GDN_DK = 128
GDN_DV = 128
GDN_CONV = 4
GDN_QK_W = GDN_HEADS * GDN_DK
GDN_V_W = GDN_HEADS * GDN_DV
GDN_CONV_CH = 2 * GDN_QK_W + GDN_V_W
MLA_HEADS = 4
MLA_D_NOPE = 128
MLA_D_ROPE = 64
MLA_D_V = 128
MLA_KV_RANK = 256
MLA_SCALE = (MLA_D_NOPE + MLA_D_ROPE) ** -0.5
QK_PRESCALE = MLA_SCALE * math.log2(math.e)
ROPE_THETA = 10000.0
N_EXPERTS = 256
N_GROUPS = 8
GROUP_SIZE = N_EXPERTS // N_GROUPS
TOPK_GROUPS = 4
TOP_K = 8
EXP_HIDDEN = 256
ROUTED_SCALE = 2.5
DEPTH = 1
DEEPNORM_ALPHA = (2.0 * DEPTH) ** 0.25
LN_EPS = 1e-5
RMS_EPS = 1e-6
L2_EPS = 1e-6

LANES = 128
PK_QKV = 0
PK_Z = PK_QKV + GDN_CONV_CH
PK_QNOPE = PK_Z + GDN_V_W
PK_QROPE = PK_QNOPE + MLA_HEADS * MLA_D_NOPE
PK_CKV = PK_QROPE + MLA_HEADS * LANES
PK_KROPE = PK_CKV + MLA_KV_RANK
PK_AB = PK_KROPE + LANES
PK_W = PK_AB + LANES
MLA_QK_W = 2 * LANES

MOE_BM = 256
EXPERT_RING = 4
VMEM_LIMIT = 56 * 1024 * 1024


def _mm(a, b):
    return jnp.dot(a.astype(BF16), b.astype(BF16), preferred_element_type=F32)


def _mm_nt(a, b):
    return lax.dot_general(a.astype(BF16), b.astype(BF16), (((1,), (1,)), ((), ())),
                           preferred_element_type=F32)


def _mm_tn(a, b):
    return lax.dot_general(a.astype(BF16), b.astype(BF16), (((0,), (0,)), ((), ())),
                           preferred_element_type=F32)


def _split3(x):
    hi = x.astype(BF16)
    r = x - hi.astype(F32)
    mid = r.astype(BF16)
    lo = (r - mid.astype(F32)).astype(BF16)
    return hi, mid, lo


def _sigmoid(x):
    return 1.0 / (1.0 + jnp.exp(-x))


def _silu(x):
    return x * _sigmoid(x)


def _softplus(x):
    return jnp.maximum(x, 0.0) + jnp.log1p(jnp.exp(-jnp.abs(x)))


def _rope(x, cs, sn):
    w = x.shape[-1]
    n = w // LANES
    if n > 1:
        cs = jnp.concatenate([cs] * n, axis=1)
        sn = jnp.concatenate([sn] * n, axis=1)
    lane = lax.broadcasted_iota(I32, x.shape, 1) & (LANES - 1)
    half = MLA_D_ROPE // 2
    swapped = jnp.where(lane < half, pltpu.roll(x, w - half, 1), pltpu.roll(x, half, 1))
    return x * cs + swapped * sn


def _params(*sem):
    return pltpu.CompilerParams(dimension_semantics=sem, vmem_limit_bytes=VMEM_LIMIT)


def _front_kernel(x_ref, w_ref, convw_ref, hist_ref, gpar_ref, kvnw_ref, wukv_ref, cs_ref, sn_ref,
                  qkv_ref, z_ref, gb_ref, q_ref, k_ref, v_ref, lat_ref, kr_ref, convnew_ref,
                  xp_scr, *, tt, c):
    t = pl.program_id(1)
    hrow = 8 - (GDN_CONV - 1)

    @pl.when(t == 0)
    def _():
        xp_scr[hrow:8, :] = hist_ref[0]

    proj = _mm(x_ref[0], w_ref[...])

    raw = proj[:, PK_QKV:PK_Z]
    xp_scr[8:8 + tt, :] = raw
    cw = convw_ref[...]
    y = raw * cw[GDN_CONV - 1:GDN_CONV]
    for i in range(GDN_CONV - 1):
        y = y + xp_scr[hrow + i:hrow + i + tt, :] * cw[i:i + 1]
    tail = xp_scr[tt + hrow:tt + 8, :]
    convnew_ref[0] = tail
    xp_scr[hrow:8, :] = tail
    qkv = _silu(y)
    for h in range(2 * GDN_HEADS):
        xh = qkv[:, h * GDN_DK:(h + 1) * GDN_DK]
        xh = xh * lax.rsqrt(jnp.sum(xh * xh, axis=-1, keepdims=True) + L2_EPS)
        if h < GDN_HEADS:
            xh = xh * GDN_DK ** -0.5
        qkv_ref[0, :, h * GDN_DK:(h + 1) * GDN_DK] = xh
    qkv_ref[0, :, 2 * GDN_QK_W:] = qkv[:, 2 * GDN_QK_W:]
    z_ref[0] = proj[:, PK_Z:PK_QNOPE]

    ab = proj[:, PK_AB:PK_W]
    gpar = gpar_ref[...]
    gc = -jnp.exp(gpar[0:1]) * _softplus(ab + gpar[1:2])
    pos = lax.broadcasted_iota(I32, ab.shape, 0) & (c - 1)
    step = 1
    while step < c:
        gc = gc + jnp.where(pos >= step, pltpu.roll(gc, step, 0), 0.0)
        step *= 2
    beta = _sigmoid(ab)
    lane = lax.broadcasted_iota(I32, ab.shape, 1)
    gb_ref[0] = jnp.where(lane < GDN_HEADS, gc, jnp.where(lane < 2 * GDN_HEADS, beta, 0.0))

    cs = cs_ref[...]
    sn = sn_ref[...]
    q_nope = proj[:, PK_QNOPE:PK_QROPE]
    q_rope = _rope(proj[:, PK_QROPE:PK_CKV], cs, sn)
    c_raw = proj[:, PK_CKV:PK_KROPE]
    latent = c_raw * lax.rsqrt(jnp.mean(c_raw * c_raw, axis=-1, keepdims=True) + RMS_EPS) * kvnw_ref[...]
    lat_ref[0] = latent
    k_rope = _rope(proj[:, PK_KROPE:PK_AB], cs, sn)
    kr_ref[0] = k_rope[:, :MLA_D_ROPE]
    kv = _mm(latent, wukv_ref[...])
    k_rope_b = k_rope.astype(BF16)
    for h in range(MLA_HEADS):
        q_ref[0, h, :, :LANES] = (q_nope[:, h * LANES:(h + 1) * LANES] * QK_PRESCALE).astype(BF16)
        q_ref[0, h, :, LANES:] = (q_rope[:, h * LANES:(h + 1) * LANES] * QK_PRESCALE).astype(BF16)
        k_ref[0, h, :, :LANES] = kv[:, h * LANES:(h + 1) * LANES].astype(BF16)
        k_ref[0, h, :, LANES:] = k_rope_b
        v_ref[0, h] = kv[:, (MLA_HEADS + h) * LANES:(MLA_HEADS + h + 1) * LANES].astype(BF16)


def _front(x, w_pack, conv_w, hist, gpar, kvnw, wukv, cs_tab, sn_tab, tt, c):
    b, t, _ = x.shape
    nt = t // tt
    const2 = lambda bi, ti: (0, 0)
    out_shape = (
        jax.ShapeDtypeStruct((b, t, GDN_CONV_CH), F32),
        jax.ShapeDtypeStruct((b, t, GDN_V_W), F32),
        jax.ShapeDtypeStruct((b, t, LANES), F32),
        jax.ShapeDtypeStruct((b, MLA_HEADS, t, MLA_QK_W), BF16),
        jax.ShapeDtypeStruct((b, MLA_HEADS, t, MLA_QK_W), BF16),
        jax.ShapeDtypeStruct((b, MLA_HEADS, t, MLA_D_V), BF16),
        jax.ShapeDtypeStruct((b, t, MLA_KV_RANK), F32),
        jax.ShapeDtypeStruct((b, t, MLA_D_ROPE), F32),
        jax.ShapeDtypeStruct((b, GDN_CONV - 1, GDN_CONV_CH), F32),
    )
    row3 = lambda w: pl.BlockSpec((1, tt, w), lambda bi, ti: (bi, ti, 0))
    head4 = lambda w: pl.BlockSpec((1, MLA_HEADS, tt, w), lambda bi, ti: (bi, 0, ti, 0))
    return pl.pallas_call(
        functools.partial(_front_kernel, tt=tt, c=c),
        grid=(b, nt),
        in_specs=[
            row3(D_MODEL),
            pl.BlockSpec((D_MODEL, PK_W), const2),
            pl.BlockSpec((GDN_CONV, GDN_CONV_CH), const2),
            pl.BlockSpec((1, GDN_CONV - 1, GDN_CONV_CH), lambda bi, ti: (bi, 0, 0)),
            pl.BlockSpec((2, LANES), const2),
            pl.BlockSpec((1, MLA_KV_RANK), const2),
            pl.BlockSpec((MLA_KV_RANK, 2 * MLA_HEADS * LANES), const2),
            pl.BlockSpec((tt, LANES), lambda bi, ti: (ti, 0)),
            pl.BlockSpec((tt, LANES), lambda bi, ti: (ti, 0)),
        ],
        out_specs=(
            row3(GDN_CONV_CH), row3(GDN_V_W), row3(LANES),
            head4(MLA_QK_W), head4(MLA_QK_W), head4(MLA_D_V),
            row3(MLA_KV_RANK), row3(MLA_D_ROPE),
            pl.BlockSpec((1, GDN_CONV - 1, GDN_CONV_CH), lambda bi, ti: (bi, 0, 0)),
        ),
        out_shape=out_shape,
        scratch_shapes=[pltpu.VMEM((tt + 8, GDN_CONV_CH), F32)],
        compiler_params=_params("arbitrary", "arbitrary"),
        name="front",
    )(x, w_pack, conv_w, hist, gpar, kvnw, wukv, cs_tab, sn_tab)


def _gdn_kernel(qkv_ref, z_ref, gb_ref, s0_ref, nw_ref, og_ref, sout_ref, s_scr, *, tg, c, bb):
    t = pl.program_id(1)
    nh = GDN_HEADS
    r = nh * c
    sh = c.bit_length() - 1

    @pl.when(t == 0)
    def _():
        s_scr[...] = s0_ref[...]

    row = lax.broadcasted_iota(I32, (r, r), 0)
    col = lax.broadcasted_iota(I32, (r, r), 1)
    same = (row >> sh) == (col >> sh)
    incl = same & (row >= col)
    strict = same & (row > col)
    eye = jnp.where(row == col, 1.0, 0.0)
    lane0 = jnp.where(lax.broadcasted_iota(I32, (r, LANES), 1) == 0, 1.0, 0.0).astype(BF16)
    nw = nw_ref[...]
    items = [(bi, ci) for ci in range(tg // c) for bi in range(bb)]
    chunks = range(len(items))

    def stacked(ref, it, base):
        bi, ci = items[it]
        return jnp.concatenate(
            [ref[bi, ci * c:(ci + 1) * c, base + h * LANES:base + (h + 1) * LANES] for h in range(nh)], axis=0)

    def col_bcast(it, lane):
        bi, ci = items[it]
        gbc = gb_ref[bi, ci * c:(ci + 1) * c, :]
        return jnp.concatenate(
            [jnp.broadcast_to(gbc[:, lane + h:lane + h + 1], (c, LANES)) for h in range(nh)], axis=0)

    def as_col(gc_b):
        return _lane_tile(gc_b, r) if r % LANES == 0 else gc_b[:, :r]

    def as_row(gc_b):
        if r % LANES == 0:
            return as_col(gc_b).T
        return sum(lax.dot_general(lane0, p, (((1,), (1,)), ((), ())), preferred_element_type=F32)
                   for p in _split3(gc_b))

    ks = [stacked(qkv_ref, ci, GDN_QK_W) for ci in chunks]
    gc = [col_bcast(ci, 0) for ci in chunks]
    beta = [col_bcast(ci, nh) for ci in chunks]
    decay, qk_kk = [], []
    for ci in chunks:
        decay.append(jnp.exp(jnp.where(incl, as_col(gc[ci]) - as_row(gc[ci]), -jnp.inf)))
        qk_kk.append(_mm_nt(jnp.concatenate([stacked(qkv_ref, ci, 0), ks[ci]], axis=0), ks[ci]))
    intra = [qk_kk[ci][:r] * decay[ci] for ci in chunks]
    n_pow = [jnp.where(strict, -as_col(beta[ci]) * qk_kk[ci][r:] * decay[ci], 0.0)
             for ci in chunks]
    t_inv = [eye + n_pow[ci] for ci in chunks]
    for _ in range(sh - 1):
        n_pow = [_mm(n_pow[ci], n_pow[ci]) for ci in chunks]
        t_inv = [t_inv[ci] + _mm(t_inv[ci], n_pow[ci]) for ci in chunks]
    egc = [jnp.exp(gc[ci]) for ci in chunks]
    uw = [_mm(t_inv[ci], jnp.concatenate([stacked(qkv_ref, ci, 2 * GDN_QK_W) * beta[ci],
                                          ks[ci] * beta[ci] * egc[ci]], axis=1)) for ci in chunks]

    for ci in chunks:
        bi, cpos = items[ci]
        u = uw[ci][:, :GDN_DV]
        w = uw[ci][:, GDN_DV:]
        qd = stacked(qkv_ref, ci, 0) * egc[ci]
        vn, qs_s = [], []
        for h in range(nh):
            hs = slice(h * c, (h + 1) * c)
            s_h = s_scr[bi, h]
            wq = _mm(jnp.concatenate([w[hs], qd[hs]], axis=0), s_h)
            vn_h = u[hs] - wq[:c]
            g_last = gc[ci][h * c + c - 1:h * c + c, :]
            kd = ks[ci][hs] * jnp.exp(g_last - gc[ci][hs])
            s_scr[bi, h] = s_h * jnp.exp(g_last) + _mm_tn(kd, vn_h)
            vn.append(vn_h)
            qs_s.append(wq[c:])
        o = jnp.concatenate(qs_s, axis=0) + _mm(intra[ci], jnp.concatenate(vn, axis=0))

        o = o * lax.rsqrt(jnp.mean(o * o, axis=-1, keepdims=True) + RMS_EPS) * nw
        o = o * _silu(stacked(z_ref, ci, 0))
        for h in range(nh):
            og_ref[bi, cpos * c:(cpos + 1) * c, h * LANES:(h + 1) * LANES] = o[h * c:(h + 1) * c].astype(BF16)

    @pl.when(t == pl.num_programs(1) - 1)
    def _():
        sout_ref[...] = s_scr[...]


def _gdn(qkv, z, gb, s0, nw, tg, c):
    b, t, _ = qkv.shape
    bb = 2 if b % 2 == 0 else 1
    row3 = lambda w: pl.BlockSpec((bb, tg, w), lambda bi, ti: (bi, ti, 0))
    st = pl.BlockSpec((bb, GDN_HEADS, GDN_DK, GDN_DV), lambda bi, ti: (bi, 0, 0, 0))
    return pl.pallas_call(
        functools.partial(_gdn_kernel, tg=tg, c=c, bb=bb),
        grid=(b // bb, t // tg),
        in_specs=[row3(GDN_CONV_CH), row3(GDN_V_W), row3(LANES), st,
                  pl.BlockSpec((1, GDN_DV), lambda bi, ti: (0, 0))],
        out_specs=(row3(GDN_V_W), st),
        out_shape=(jax.ShapeDtypeStruct((b, t, GDN_V_W), BF16),
                   jax.ShapeDtypeStruct((b, GDN_HEADS, GDN_DK, GDN_DV), F32)),
        scratch_shapes=[pltpu.VMEM((bb, GDN_HEADS, GDN_DK, GDN_DV), F32)],
        compiler_params=_params("arbitrary", "arbitrary"),
        name="gdn",
    )(qkv, z, gb, s0, nw)


ATTN_ROW_BLOCK = 32


def _lane_tile(x, width):
    return x if width == LANES else jnp.concatenate([x] * (width // LANES), axis=1)


def _attn_kernel(qi_ref, ki_ref, q_ref, k_ref, v_ref, o_ref, m_scr, l_scr, acc_scr, a_scr, s_scr, p_scr,
                 *, tb, sub):
    step = pl.program_id(2)
    qi = qi_ref[step]
    ki = ki_ref[step]
    csh = CHUNK.bit_length() - 1

    @pl.when(ki == 0)
    def _():
        m_scr[...] = jnp.full(m_scr.shape, -jnp.inf, F32)
        l_scr[...] = jnp.zeros(l_scr.shape, F32)
        acc_scr[...] = jnp.zeros(acc_scr.shape, F32)

    half = tb // 2
    rb = min(ATTN_ROW_BLOCK, half)

    def scores(h, j):
        rows = slice(h * half, (h + 1) * half)
        s_scr[rows, :] = lax.dot_general(q_ref[0, 0, rows, :], k_ref[0, 0, j * sub:(j + 1) * sub, :],
                                         (((1,), (1,)), ((), ())), preferred_element_type=F32)

    def softmax(h, j, masked):
        r0 = h * half

        def block(i):
            rr = pl.ds(pl.multiple_of(r0 + i * rb, rb), rb)
            s = s_scr[rr, :]
            if masked:
                qc = (r0 + i * rb + lax.broadcasted_iota(I32, s.shape, 0)) >> csh
                kc = (j * sub + lax.broadcasted_iota(I32, s.shape, 1)) >> csh
                s = jnp.where(kc <= qc, s, -jnp.inf)
            return rr, s

        def row_max(i, carry):
            rr, s = block(i)
            m_prev = m_scr[rr, :]
            m_new = jnp.maximum(m_prev, jnp.max(s, axis=-1, keepdims=True))
            a_scr[rr, :] = jnp.exp2(m_prev - m_new)
            m_scr[rr, :] = m_new
            return carry

        def row_exp(i, carry):
            rr, s = block(i)
            p = jnp.exp2(s - _lane_tile(m_scr[rr, :], sub))
            l_scr[rr, :] = a_scr[rr, :] * l_scr[rr, :] + jnp.sum(p, axis=-1, keepdims=True)
            p_scr[rr, :] = p.astype(BF16)
            return carry

        for i in range(half // rb):
            row_max(i, 0)
        for i in range(half // rb):
            row_exp(i, 0)

    def weighted_values(h, j):
        rows = slice(h * half, (h + 1) * half)
        acc_scr[rows, :] = a_scr[rows, :] * acc_scr[rows, :] + jnp.dot(
            p_scr[rows, :], v_ref[0, 0, j * sub:(j + 1) * sub, :], preferred_element_type=F32)

    def run(tasks):
        for h in range(2):
            if tasks[h]:
                scores(h, tasks[h][0][0])
        for n in range(max(len(t) for t in tasks)):
            for h in range(2):
                if n < len(tasks[h]):
                    j, masked = tasks[h][n]
                    softmax(h, j, masked)
                    weighted_values(h, j)
                    if n + 1 < len(tasks[h]):
                        scores(h, tasks[h][n + 1][0])

    def diagonal_tasks(h):
        r0 = h * half
        out = []
        for j in range(tb // sub):
            k0, k1 = j * sub, (j + 1) * sub
            if k0 >= r0 + half:
                continue
            out.append((j, k1 > r0 + CHUNK))
        return out

    @pl.when(ki < qi)
    def _():
        run([[(j, False) for j in range(tb // sub)]] * 2)

    @pl.when(ki == qi)
    def _():
        run([diagonal_tasks(0), diagonal_tasks(1)])
        o_ref[0] = (acc_scr[...] / l_scr[...]).astype(BF16)


def _attn_prompt(q, k, v, tb, sub):
    b, nh, t, _ = q.shape
    nt = t // tb
    pairs = [(qi, ki) for qi in range(nt) for ki in range(qi + 1)]
    qi_of = jnp.asarray([p[0] for p in pairs], I32)
    ki_of = jnp.asarray([p[1] for p in pairs], I32)
    return pl.pallas_call(
        functools.partial(_attn_kernel, tb=tb, sub=sub),
        grid_spec=pltpu.PrefetchScalarGridSpec(
            num_scalar_prefetch=2,
            grid=(b, nh, len(pairs)),
            in_specs=[pl.BlockSpec((1, 1, tb, MLA_QK_W), lambda bi, hi, s, qo, ko: (bi, hi, qo[s], 0)),
                      pl.BlockSpec((1, 1, tb, MLA_QK_W), lambda bi, hi, s, qo, ko: (bi, hi, ko[s], 0)),
                      pl.BlockSpec((1, 1, tb, MLA_D_V), lambda bi, hi, s, qo, ko: (bi, hi, ko[s], 0))],
            out_specs=pl.BlockSpec((1, tb, MLA_D_V), lambda bi, hi, s, qo, ko: (bi, qo[s], hi)),
            scratch_shapes=[pltpu.VMEM((tb, LANES), F32), pltpu.VMEM((tb, LANES), F32),
                            pltpu.VMEM((tb, MLA_D_V), F32), pltpu.VMEM((tb, LANES), F32),
                            pltpu.VMEM((tb, sub), F32), pltpu.VMEM((tb, sub), BF16)]),
        out_shape=jax.ShapeDtypeStruct((b, t, MLA_HEADS * MLA_D_V), BF16),
        compiler_params=_params("arbitrary", "arbitrary", "arbitrary"),
        name="attn_prompt",
    )(qi_of, ki_of, q, k, v)


def _attn_sample_kernel(q_ref, kn_ref, vn_ref, plat_ref, pkr_ref, wukv_ref, o_ref):
    kvp = _mm(plat_ref[0], wukv_ref[...])
    pkr = pkr_ref[0].astype(BF16)
    for h in range(MLA_HEADS):
        q = q_ref[0, h]
        s_past = (_mm_nt(q[:, :MLA_D_NOPE], kvp[:, h * LANES:(h + 1) * LANES])
                  + _mm_nt(q[:, MLA_D_NOPE:MLA_D_NOPE + MLA_D_ROPE], pkr))
        s_new = _mm_nt(q, kn_ref[0, h])
        m = jnp.maximum(jnp.max(s_past, axis=-1, keepdims=True), jnp.max(s_new, axis=-1, keepdims=True))
        p_past = jnp.exp2(s_past - m)
        p_new = jnp.exp2(s_new - m)
        l = jnp.sum(p_past, axis=-1, keepdims=True) + jnp.sum(p_new, axis=-1, keepdims=True)
        o = _mm(p_past, kvp[:, (MLA_HEADS + h) * LANES:(MLA_HEADS + h + 1) * LANES]) + _mm(p_new, vn_ref[0, h])
        o_ref[0, :, h * MLA_D_V:(h + 1) * MLA_D_V] = (o / l).astype(BF16)


def _attn_sample(q, k_new, v_new, past_lat, past_kr, wukv):
    b, nh, ts, _ = q.shape
    past = past_lat.shape[1]
    b4 = lambda w: pl.BlockSpec((1, nh, ts, w), lambda bi: (bi, 0, 0, 0))
    return pl.pallas_call(
        _attn_sample_kernel,
        grid=(b,),
        in_specs=[b4(MLA_QK_W), b4(MLA_QK_W), b4(MLA_D_V),
                  pl.BlockSpec((1, past, MLA_KV_RANK), lambda bi: (bi, 0, 0)),
                  pl.BlockSpec((1, past, MLA_D_ROPE), lambda bi: (bi, 0, 0)),
                  pl.BlockSpec((MLA_KV_RANK, 2 * MLA_HEADS * LANES), lambda bi: (0, 0))],
        out_specs=pl.BlockSpec((1, ts, MLA_HEADS * MLA_D_V), lambda bi: (bi, 0, 0)),
        out_shape=jax.ShapeDtypeStruct((b, ts, MLA_HEADS * MLA_D_V), BF16),
        compiler_params=_params("arbitrary"),
        name="attn_sample",
    )(q, k_new, v_new, past_lat, past_kr, wukv)


def _layernorm(y, g, b):
    mu = jnp.mean(y, axis=-1, keepdims=True)
    d = y - mu
    var = jnp.mean(d * d, axis=-1, keepdims=True)
    return d * lax.rsqrt(var + LN_EPS) * g + b


N_SHARED = 4


def _mixln_kernel(og_ref, om_ref, x_ref, w_ref, g_ref, b_ref, rw_ref, rb_ref, cnt0_ref, *rest, tm):
    o_ref, idx_ref, gate_ref, rank_ref, cnt_ref, carry_scr = rest[-6:]

    @pl.when(pl.program_id(0) == 0)
    def _():
        carry_scr[...] = cnt0_ref[...]

    mix = (jnp.dot(og_ref[...], w_ref[:GDN_V_W, :], preferred_element_type=F32)
           + jnp.dot(om_ref[...], w_ref[GDN_V_W:, :], preferred_element_type=F32))
    x1 = _layernorm(DEEPNORM_ALPHA * x_ref[...] + mix, g_ref[...], b_ref[...])
    o_ref[...] = x1
    _route_tile(x1, rw_ref, rb_ref, carry_scr, idx_ref, gate_ref, rank_ref, tm)
    cnt_ref[...] = carry_scr[...]


def _mixln(og, om, x, w_out, g, b, rw, rb, cnt0, shared, n_all, row0, tm):
    n = x.shape[0]
    blk0 = row0 // tm
    rows = lambda w: pl.BlockSpec((tm, w), lambda i: (i, 0))
    const = lambda s: pl.BlockSpec(s, lambda i: (0, 0))
    kspec = pl.BlockSpec((TOP_K, tm), lambda i: (0, blk0 + i))
    first = shared is None
    n_in = 9
    kt = lambda dt: jax.ShapeDtypeStruct((TOP_K, n_all), dt)
    return pl.pallas_call(
        functools.partial(_mixln_kernel, tm=tm),
        grid=(n // tm,),
        in_specs=[rows(GDN_V_W), rows(MLA_HEADS * MLA_D_V), rows(D_MODEL),
                  const((D_MODEL, D_MODEL)), const((1, D_MODEL)), const((1, D_MODEL)),
                  const((D_MODEL, N_EXPERTS)), const((N_EXPERTS, 1)), const((N_EXPERTS, LANES))]
                 + ([] if first else [pl.BlockSpec(memory_space=pl.ANY)] * N_SHARED),
        out_specs=(pl.BlockSpec((tm, D_MODEL), lambda i: (blk0 + i, 0)), kspec, kspec, kspec,
                   const((N_EXPERTS, LANES))),
        out_shape=(jax.ShapeDtypeStruct((n_all, D_MODEL), F32), kt(I32), kt(F32), kt(I32),
                   jax.ShapeDtypeStruct((N_EXPERTS, LANES), F32)),
        scratch_shapes=[pltpu.VMEM((N_EXPERTS, LANES), F32)],
        input_output_aliases={} if first else {n_in + j: j for j in range(N_SHARED)},
        compiler_params=_params("arbitrary"),
        name="mixln",
    )(*((og, om, x, w_out, g, b, rw, rb, cnt0) + (() if first else tuple(shared))))


def _route_tile(x, rw_ref, rb_ref, carry_scr, idx_ref, gate_ref, rank_ref, tt):
    ninf = -jnp.inf
    big = float(2 * N_EXPERTS)
    scores = _sigmoid(_mm(x, rw_ref[...]).T)
    biased = scores + rb_ref[...]
    eio = lax.broadcasted_iota(I32, (N_EXPERTS, tt), 0).astype(F32)

    def first_argmax(vals, io):
        m = jnp.max(vals, axis=0, keepdims=True)
        i = jnp.min(jnp.where(vals == m, io, big), axis=0, keepdims=True)
        return m, i

    gs = []
    for g in range(N_GROUPS):
        blk = biased[g * GROUP_SIZE:(g + 1) * GROUP_SIZE]
        io = (lax.broadcasted_iota(I32, (GROUP_SIZE, tt), 0) + g * GROUP_SIZE).astype(F32)
        m1, i1 = first_argmax(blk, io)
        m2 = jnp.max(jnp.where(io == i1, ninf, blk), axis=0, keepdims=True)
        gs.append(m1 + m2)
    gio = lax.broadcasted_iota(I32, (N_GROUPS, tt), 0).astype(F32)
    gsc = jnp.zeros((N_GROUPS, tt), F32)
    for g in range(N_GROUPS):
        gsc = jnp.where(gio == float(g), gs[g], gsc)
    gsel = jnp.zeros((N_GROUPS, tt), F32)
    for _ in range(TOPK_GROUPS):
        _, gi = first_argmax(gsc, gio)
        hit = gio == gi
        gsel = jnp.where(hit, 1.0, gsel)
        gsc = jnp.where(hit, ninf, gsc)
    masked = jnp.concatenate(
        [jnp.where(jnp.max(jnp.where(gio == float(g), gsel, 0.0), axis=0, keepdims=True) > 0.0,
                   biased[g * GROUP_SIZE:(g + 1) * GROUP_SIZE], ninf) for g in range(N_GROUPS)], axis=0)

    idx, wts = [], []
    sel = jnp.zeros((N_EXPERTS, tt), F32)
    for _ in range(TOP_K):
        _, ei = first_argmax(masked, eio)
        hit = eio == ei
        wts.append(jnp.sum(jnp.where(hit, scores, 0.0), axis=0, keepdims=True))
        masked = jnp.where(hit, ninf, masked)
        sel = jnp.where(hit, 1.0, sel)
        idx.append(ei)
    wsum = wts[0]
    for w in wts[1:]:
        wsum = wsum + w

    t0 = lax.broadcasted_iota(I32, (tt, tt), 0)
    t1 = lax.broadcasted_iota(I32, (tt, tt), 1)
    before = jnp.where(t0 < t1, 1.0, 0.0).astype(BF16)
    sel_b = sel.astype(BF16)
    base = carry_scr[:, :1] + jnp.dot(sel_b, before, preferred_element_type=F32)
    ranks = [jnp.sum(jnp.where(eio == ei, base, 0.0), axis=0, keepdims=True) for ei in idx]
    carry_scr[...] = carry_scr[...] + jnp.dot(sel_b, jnp.ones((tt, LANES), BF16), preferred_element_type=F32)

    for k in range(TOP_K):
        idx_ref[k:k + 1, :] = idx[k].astype(I32)
        gate_ref[k:k + 1, :] = wts[k] / wsum * ROUTED_SCALE
        rank_ref[k:k + 1, :] = ranks[k].astype(I32)


def _dest_kernel(idx_ref, rank_ref, pstart_ref, dest_ref, *, tt):
    eio = lax.broadcasted_iota(I32, (N_EXPERTS, tt), 0)
    pstart = pstart_ref[...]
    for k in range(TOP_K):
        start = jnp.sum(jnp.where(eio == idx_ref[k:k + 1, :], pstart, 0.0), axis=0, keepdims=True)
        dest_ref[0, k:k + 1, :] = start.astype(I32) + rank_ref[k:k + 1, :]


def _dest(idx, rank, pstart, tt):
    n = idx.shape[1]
    kspec = pl.BlockSpec((TOP_K, tt), lambda i: (0, i))
    return pl.pallas_call(
        functools.partial(_dest_kernel, tt=tt),
        grid=(n // tt,),
        in_specs=[kspec, kspec, pl.BlockSpec((N_EXPERTS, 1), lambda i: (0, 0))],
        out_specs=pl.BlockSpec((1, TOP_K, tt), lambda i: (i, 0, 0)),
        out_shape=jax.ShapeDtypeStruct((n // tt, TOP_K, tt), I32),
        compiler_params=_params("arbitrary"),
        name="dest",
    )(idx, rank, pstart)


ROW_TILE = D_MODEL // LANES
XROW_TILE = ROW_TILE // 2


def _rows_to_tiles(x, tiles_ref, base, n, rt=ROW_TILE):
    for j in range(rt):
        tiles_ref[pl.ds(base * rt + j, n, stride=rt), :] = x[:, j * LANES:(j + 1) * LANES]


def _tiles_to_rows(tiles_ref, base, n, rt=ROW_TILE):
    return jnp.concatenate(
        [tiles_ref[pl.ds(base * rt + j, n, stride=rt), :] for j in range(rt)], axis=1)


def _pack_bf16_pairs(x):
    half = x.shape[1] // 2
    lo = pltpu.bitcast(x[:, :half].astype(BF16).astype(F32), jnp.uint32) >> 16
    hi = pltpu.bitcast(x[:, half:].astype(BF16).astype(F32), jnp.uint32) & jnp.uint32(0xFFFF0000)
    return lo | hi


def _unpack_bf16_pairs(w):
    lo = pltpu.bitcast(w << 16, F32)
    hi = pltpu.bitcast(w & jnp.uint32(0xFFFF0000), F32)
    return jnp.concatenate([lo, hi], axis=1)


def _dispatch_kernel(tail_ref, dest_ref, x_ref, xs_out, zbuf, xt_scr, sem, zsem, *, td):
    s = pl.program_id(0)

    @pl.when(s == 0)
    def _():
        zbuf[...] = jnp.zeros(zbuf.shape, zbuf.dtype)

        def zero_copy(e):
            first = pl.multiple_of(tail_ref[e] * XROW_TILE, MOE_BM * XROW_TILE)
            return pltpu.make_async_copy(zbuf, xs_out.at[pl.ds(first, MOE_BM * XROW_TILE)], zsem)

        def zstart(e, carry):
            zero_copy(e).start()
            return carry

        def zwait(e, carry):
            zero_copy(e).wait()
            return carry

        lax.fori_loop(0, N_EXPERTS, zstart, 0)
        lax.fori_loop(0, N_EXPERTS, zwait, 0)

    slot = s & 1

    def row_copy(slot_, i, d):
        src = pl.multiple_of((slot_ * td + i) * XROW_TILE, XROW_TILE)
        return pltpu.make_async_copy(xt_scr.at[pl.ds(src, XROW_TILE)],
                                     xs_out.at[pl.ds(pl.multiple_of(d * XROW_TILE, XROW_TILE), XROW_TILE)],
                                     sem.at[slot_])

    def drain(slot_):
        step_rows = TOP_K * td * XROW_TILE
        pltpu.make_async_copy(xs_out.at[pl.ds(0, step_rows)], xs_out.at[pl.ds(0, step_rows)],
                              sem.at[slot_]).wait()

    @pl.when(s >= 2)
    def _():
        drain(slot)

    _rows_to_tiles(_pack_bf16_pairs(x_ref[...]), xt_scr, slot * td, td, XROW_TILE)

    def issue(i, carry):
        for k in range(TOP_K):
            row_copy(slot, i, dest_ref[0, 0, k * td + i]).start(priority=k % 2)
        return carry

    lax.fori_loop(0, td, issue, 0, unroll=4)

    @pl.when(s == pl.num_programs(0) - 1)
    def _():
        drain(slot)

        @pl.when(s >= 1)
        def _():
            drain(1 - slot)


def _dispatch(tail, dest, x1, n_rows, td):
    n = x1.shape[0]
    return pl.pallas_call(
        functools.partial(_dispatch_kernel, td=td),
        grid_spec=pltpu.PrefetchScalarGridSpec(
            num_scalar_prefetch=1,
            grid=(n // td,),
            in_specs=[pl.BlockSpec((1, 1, TOP_K * td), lambda i, tl: (i, 0, 0), memory_space=pltpu.SMEM),
                      pl.BlockSpec((td, D_MODEL), lambda i, tl: (i, 0))],
            out_specs=pl.BlockSpec(memory_space=pl.ANY),
            scratch_shapes=[pltpu.VMEM((MOE_BM * XROW_TILE, LANES), jnp.uint32),
                            pltpu.VMEM((2 * td * XROW_TILE, LANES), jnp.uint32),
                            pltpu.SemaphoreType.DMA((2,)), pltpu.SemaphoreType.DMA(())]),
        out_shape=jax.ShapeDtypeStruct((n_rows * XROW_TILE, LANES), jnp.uint32),
        compiler_params=_params("arbitrary"),
        name="dispatch",
    )(tail, dest, x1)


def _expert_kernel(first_ref, nblk_ref, nact_ref, xs_hbm, wg_ref, wu_ref, wd_ref, y_hbm,
                   xbuf, ybuf, wgb, wub, wdb, xsem, ysem):
    e = pl.program_id(0)
    nact = nact_ref[0]
    ring = EXPERT_RING
    xr = MOE_BM * XROW_TILE
    yr = MOE_BM * XROW_TILE

    def x_copy(g):
        slot = g & (ring - 1)
        return pltpu.make_async_copy(xs_hbm.at[pl.ds(pl.multiple_of(g * xr, xr), xr)],
                                     xbuf.at[pl.ds(pl.multiple_of(slot * xr, xr), xr)], xsem.at[slot])

    def y_copy(g):
        slot = g & (ring - 1)
        return pltpu.make_async_copy(ybuf.at[pl.ds(pl.multiple_of(slot * yr, yr), yr)],
                                     y_hbm.at[pl.ds(pl.multiple_of(g * yr, yr), yr)], ysem.at[slot])

    @pl.when(e == 0)
    def _():
        for g0 in range(ring - 1):
            @pl.when(g0 < nact)
            def _():
                x_copy(g0).start()

    @pl.when(nblk_ref[e] > 0)
    def _():
        wgb[...] = wg_ref[0].astype(BF16)
        wub[...] = wu_ref[0].astype(BF16)
        wdb[...] = wd_ref[0].astype(BF16)

    def block(b, carry):
        g = first_ref[e] + b
        slot = g & (ring - 1)
        x_copy(g).wait()

        @pl.when(g + ring - 1 < nact)
        def _():
            x_copy(g + ring - 1).start()

        @pl.when(g >= ring)
        def _():
            y_copy(g - ring).wait()

        xb = _unpack_bf16_pairs(_tiles_to_rows(xbuf, slot * MOE_BM, MOE_BM, XROW_TILE)).astype(BF16)
        hg = jnp.dot(xb, wgb[...], preferred_element_type=F32)
        hu = jnp.dot(xb, wub[...], preferred_element_type=F32)
        y = jnp.dot((_silu(hg) * hu).astype(BF16), wdb[...], preferred_element_type=F32)
        _rows_to_tiles(_pack_bf16_pairs(y), ybuf, slot * MOE_BM, MOE_BM, XROW_TILE)
        y_copy(g).start()

        @pl.when(g == nact - 1)
        def _():
            for back in range(ring):
                @pl.when(g >= back)
                def _():
                    y_copy(g - back).wait()

        return carry

    lax.fori_loop(0, nblk_ref[e], block, 0)


def _experts(first, nblk, nact, xs, wg, wu, wd):
    n_rows = xs.shape[0] // XROW_TILE
    wspec = lambda s: pl.BlockSpec((1,) + s, lambda e, fr, nb, na: (e, 0, 0))
    return pl.pallas_call(
        _expert_kernel,
        grid_spec=pltpu.PrefetchScalarGridSpec(
            num_scalar_prefetch=3,
            grid=(N_EXPERTS,),
            in_specs=[pl.BlockSpec(memory_space=pl.ANY),
                      wspec((D_MODEL, EXP_HIDDEN)), wspec((D_MODEL, EXP_HIDDEN)), wspec((EXP_HIDDEN, D_MODEL))],
            out_specs=pl.BlockSpec(memory_space=pl.ANY),
            scratch_shapes=[pltpu.VMEM((EXPERT_RING * MOE_BM * XROW_TILE, LANES), jnp.uint32),
                            pltpu.VMEM((EXPERT_RING * MOE_BM * XROW_TILE, LANES), jnp.uint32),
                            pltpu.VMEM((D_MODEL, EXP_HIDDEN), BF16), pltpu.VMEM((D_MODEL, EXP_HIDDEN), BF16),
                            pltpu.VMEM((EXP_HIDDEN, D_MODEL), BF16),
                            pltpu.SemaphoreType.DMA((EXPERT_RING,)), pltpu.SemaphoreType.DMA((EXPERT_RING,))]),
        out_shape=jax.ShapeDtypeStruct((n_rows * XROW_TILE, LANES), jnp.uint32),
        compiler_params=_params("arbitrary"),
        name="experts",
    )(first, nblk, nact, xs, wg, wu, wd)


def _combine_kernel(dcur_ref, dnxt_ref, y_hbm, gate_ref, x_ref, wsg_ref, wsu_ref, wsd_ref, g_ref, b_ref,
                    outp_ref, outs_ref, buf, routed_scr, sem, *, tc, np_tiles):
    s = pl.program_id(0)
    ns = pl.num_programs(0)
    slot = s % 2

    def row_copy(d, slot_, k, i):
        dst = pl.multiple_of(((slot_ * TOP_K + k) * tc + i) * XROW_TILE, XROW_TILE)
        return pltpu.make_async_copy(y_hbm.at[pl.ds(pl.multiple_of(d * XROW_TILE, XROW_TILE), XROW_TILE)],
                                     buf.at[pl.ds(dst, XROW_TILE)], sem.at[slot_])

    def issue(dref, slot_):
        def body(i, carry):
            for k in range(TOP_K):
                row_copy(dref[0, 0, k * tc + i], slot_, k, i).start(priority=k % 2)
            return carry
        lax.fori_loop(0, tc, body, 0, unroll=4)

    @pl.when(s == 0)
    def _():
        issue(dcur_ref, 0)

    slot_rows = TOP_K * tc * XROW_TILE
    pltpu.make_async_copy(y_hbm.at[pl.ds(0, slot_rows)],
                          buf.at[pl.ds(pl.multiple_of(slot * slot_rows, slot_rows), slot_rows)],
                          sem.at[slot]).wait()

    grp = 16

    def gated_sum(gi):
        base = pl.multiple_of(gi * grp, grp)
        gate = gate_ref[pl.ds(base, grp), :]
        acc = None
        for k in range(TOP_K):
            rows = _unpack_bf16_pairs(_tiles_to_rows(buf, (slot * TOP_K + k) * tc + base, grp, XROW_TILE))
            term = rows * gate[:, k:k + 1]
            acc = term if acc is None else acc + term
        routed_scr[pl.ds(base, grp), :] = acc

    def sum_and_issue(gi, carry):
        for i in range(grp):
            for k in range(TOP_K):
                row_copy(dnxt_ref[0, 0, k * tc + gi * grp + i], 1 - slot, k, gi * grp + i).start(priority=k % 2)
        gated_sum(gi)
        return carry

    def sum_only(gi, carry):
        gated_sum(gi)
        return carry

    @pl.when(s + 1 < ns)
    def _():
        lax.fori_loop(0, tc // grp, sum_and_issue, 0)

    @pl.when(s + 1 == ns)
    def _():
        lax.fori_loop(0, tc // grp, sum_only, 0)

    x = x_ref[...]
    routed = routed_scr[...]
    xb = x.astype(BF16)
    shared = _mm(_silu(_mm(xb, wsg_ref[...])) * _mm(xb, wsu_ref[...]), wsd_ref[...])
    out = _layernorm(DEEPNORM_ALPHA * x + (routed + shared), g_ref[...], b_ref[...])

    @pl.when(s < np_tiles)
    def _():
        outp_ref[...] = out

    @pl.when(s >= np_tiles)
    def _():
        outs_ref[...] = out


def _combine(dest, y_sorted, gate, x1, wsg, wsu, wsd, g, b, n_prompt, tc):
    n = x1.shape[0]
    ns = n // tc
    np_tiles = n_prompt // tc
    const = lambda s: pl.BlockSpec(s, lambda i: (0, 0))
    dspec = lambda f: pl.BlockSpec((1, 1, TOP_K * tc), f, memory_space=pltpu.SMEM)
    return pl.pallas_call(
        functools.partial(_combine_kernel, tc=tc, np_tiles=np_tiles),
        grid=(ns,),
        in_specs=[dspec(lambda i: (i, 0, 0)), dspec(lambda i: (jnp.minimum(i + 1, ns - 1), 0, 0)),
                  pl.BlockSpec(memory_space=pl.ANY),
                  pl.BlockSpec((tc, TOP_K), lambda i: (i, 0)),
                  pl.BlockSpec((tc, D_MODEL), lambda i: (i, 0)),
                  const((D_MODEL, EXP_HIDDEN)), const((D_MODEL, EXP_HIDDEN)), const((EXP_HIDDEN, D_MODEL)),
                  const((1, D_MODEL)), const((1, D_MODEL))],
        out_specs=(pl.BlockSpec((tc, D_MODEL), lambda i: (jnp.minimum(i, np_tiles - 1), 0)),
                   pl.BlockSpec((tc, D_MODEL), lambda i: (jnp.maximum(i - np_tiles, 0), 0))),
        out_shape=(jax.ShapeDtypeStruct((n_prompt, D_MODEL), F32),
                   jax.ShapeDtypeStruct((n - n_prompt, D_MODEL), F32)),
        scratch_shapes=[pltpu.VMEM((2 * TOP_K * tc * XROW_TILE, LANES), jnp.uint32),
                        pltpu.VMEM((tc, D_MODEL), F32),
                        pltpu.SemaphoreType.DMA((2,))],
        compiler_params=_params("arbitrary"),
        name="combine",
    )(dest, dest, y_sorted, gate, x1, wsg, wsu, wsd, g, b)


def _pack_w_in(w_in):
    d = w_in.shape[0]
    o_z = GDN_CONV_CH
    o_a = o_z + GDN_V_W
    o_b = o_a + GDN_HEADS
    o_q = o_b + GDN_HEADS
    o_c = o_q + MLA_HEADS * (MLA_D_NOPE + MLA_D_ROPE)
    o_kr = o_c + MLA_KV_RANK
    zeros = lambda w: jnp.zeros((d, w), w_in.dtype)
    wq = w_in[:, o_q:o_c].reshape(d, MLA_HEADS, MLA_D_NOPE + MLA_D_ROPE)
    q_nope = wq[:, :, :MLA_D_NOPE].reshape(d, MLA_HEADS * MLA_D_NOPE)
    q_rope = jnp.pad(wq[:, :, MLA_D_NOPE:], ((0, 0), (0, 0), (0, LANES - MLA_D_ROPE))).reshape(d, MLA_HEADS * LANES)
    cols = [w_in[:, :o_a], q_nope, q_rope, w_in[:, o_c:o_kr], w_in[:, o_kr:], zeros(LANES - MLA_D_ROPE),
            w_in[:, o_a:o_q], zeros(LANES - 2 * GDN_HEADS)]
    return jnp.concatenate(cols, axis=1).astype(BF16)


def _rope_tables(pos):
    inv_freq = ROPE_THETA ** (-jnp.arange(0, MLA_D_ROPE, 2, dtype=F32) / MLA_D_ROPE)
    ang = pos.astype(F32)[:, None] * inv_freq[None, :]
    cos, sin = jnp.cos(ang), jnp.sin(ang)
    pad = jnp.zeros((pos.shape[0], LANES - MLA_D_ROPE), F32)
    return jnp.concatenate([cos, cos, pad], axis=1), jnp.concatenate([-sin, sin, pad], axis=1)


def _pick(t, pref):
    return pref if t % pref == 0 else t


def _token_mixers(x, pos, conv_hist, s0, past, wts, shared, cnt0, n_all, row0):
    b, t, _ = x.shape
    cs_tab, sn_tab = _rope_tables(pos)
    tt = _pick(t, 512)
    c = min(CHUNK, t)
    qkv, z, gb, q, k, v, latent, k_rope, conv_new = _front(
        x, wts["w_pack"], wts["conv_w"], conv_hist, wts["gpar"], wts["kvnw"], wts["wukv"], cs_tab, sn_tab, tt, c)
    og, s_new = _gdn(qkv, z, gb, s0, wts["gdn_nw"], _pick(t, 4 * CHUNK), c)
    if past is None:
        tb = _pick(t, 2048)
        om = _attn_prompt(q, k, v, tb, _pick(tb, 512))
    else:
        om = _attn_sample(q, k, v, past[0], past[1], wts["wukv"])
    n = b * t
    *shared, cnt = _mixln(og.reshape(n, -1), om.reshape(n, -1), x.reshape(n, D_MODEL), wts["w_out"],
                          wts["ln1_g"], wts["ln1_b"], wts["router_w"], wts["router_b"], cnt0,
                          shared, n_all, row0, _pick(n, 256))
    return shared, cnt, latent, k_rope, s_new, conv_new


def _moe(x1_all, idx, gate, rank, cnt, n_prompt, wts):
    n = x1_all.shape[0]
    counts = cnt[:, 0].astype(I32)
    padded = (counts + MOE_BM - 1) // MOE_BM * MOE_BM
    pend = jnp.cumsum(padded)
    pstart = pend - padded
    td = _pick(math.gcd(n_prompt, n - n_prompt), 256)
    dest = _dest(idx, rank, pstart.astype(F32).reshape(-1, 1), td)
    dest = dest.reshape(n // td, 1, TOP_K * td)
    n_blocks = n * TOP_K // MOE_BM + N_EXPERTS
    nact = (pend[-1:] // MOE_BM).astype(I32)
    tail = jnp.maximum(pend - MOE_BM, 0).astype(I32)

    xs = _dispatch(tail, dest, x1_all, n_blocks * MOE_BM, td)
    y_sorted = _experts((pstart // MOE_BM).astype(I32), (padded // MOE_BM).astype(I32), nact, xs,
                        wts["exp_wg"], wts["exp_wu"], wts["exp_wd"])
    return _combine(dest, y_sorted, gate.T, x1_all,
                    wts["sh_wg"], wts["sh_wu"], wts["sh_wd"], wts["ln2_g"], wts["ln2_b"], n_prompt, td)


def kernel(x_prompt, x_sample, cache_kv_latent, cache_k_rope, state_gdn, state_conv, w_in, gdn_conv_w, gdn_a_log, gdn_dt_bias, gdn_norm_w, mla_kv_norm_w, mla_w_uk, mla_w_uv, w_out, ln1_g, ln1_b, router_w, router_bias, exp_w_gate, exp_w_up, exp_w_down, shared_w_gate, shared_w_up, shared_w_down, ln2_g, ln2_b):
    assert w_in.shape[0] == 1, "single-layer stack"
    b_p, t_p, _ = x_prompt.shape
    b_s, t_s, _ = x_sample.shape
    past = cache_kv_latent.shape[2]
    l = 0
    pad4 = lambda a: jnp.pad(a.astype(F32), (0, LANES - GDN_HEADS))
    wts = {
        "w_pack": _pack_w_in(w_in[l]),
        "conv_w": gdn_conv_w[l],
        "gpar": jnp.stack([pad4(gdn_a_log[l]), pad4(gdn_dt_bias[l])]),
        "kvnw": mla_kv_norm_w[l].reshape(1, -1),
        "wukv": jnp.concatenate([mla_w_uk[l].reshape(MLA_KV_RANK, -1), mla_w_uv[l].reshape(MLA_KV_RANK, -1)],
                                axis=1).astype(BF16),
        "gdn_nw": gdn_norm_w[l].reshape(1, -1),
        "w_out": w_out[l].astype(BF16),
        "ln1_g": ln1_g[l].reshape(1, -1), "ln1_b": ln1_b[l].reshape(1, -1),
        "router_w": router_w[l].astype(BF16), "router_b": router_bias[l].reshape(-1, 1),
        "exp_wg": exp_w_gate[l], "exp_wu": exp_w_up[l], "exp_wd": exp_w_down[l],
        "sh_wg": shared_w_gate[l].astype(BF16), "sh_wu": shared_w_up[l].astype(BF16),
        "sh_wd": shared_w_down[l].astype(BF16),
        "ln2_g": ln2_g[l].reshape(1, -1), "ln2_b": ln2_b[l].reshape(1, -1),
    }
    n_p, n_s = b_p * t_p, b_s * t_s
    conv0 = jnp.zeros((b_p, GDN_CONV - 1, GDN_CONV_CH), F32)
    s0 = jnp.zeros((b_p, GDN_HEADS, GDN_DK, GDN_DV), F32)
    cnt0 = jnp.zeros((N_EXPERTS, LANES), F32)
    shared, cnt, lat_p, kr_p, sg_p, cv_p = _token_mixers(
        x_prompt, jnp.arange(t_p), conv0, s0, None, wts, None, cnt0, n_p + n_s, 0)
    shared, cnt, lat_s, kr_s, sg_s, cv_s = _token_mixers(
        x_sample, past + jnp.arange(t_s), state_conv[l], state_gdn[l],
        (cache_kv_latent[l], cache_k_rope[l]), wts, shared, cnt, n_p + n_s, n_p)
    y_p, y_s = _moe(*shared, cnt, n_p, wts)
    return (y_p.reshape(b_p, t_p, D_MODEL), y_s.reshape(b_s, t_s, D_MODEL),
            lat_p[None], kr_p[None], sg_p[None], cv_p[None],
            lat_s[None], kr_s[None], sg_s[None], cv_s[None])
```

```python
import functools
import math

import jax
import jax.numpy as jnp
from jax import lax
from jax.experimental import pallas as pl
from jax.experimental.pallas import tpu as pltpu

F32 = jnp.float32
BF16 = jnp.bfloat16
I32 = jnp.int32

D_MODEL = 1024
CHUNK = 64
GDN_HEADS = 4
GDN_DK = 128
GDN_DV = 128
GDN_CONV = 4
GDN_QK_W = GDN_HEADS * GDN_DK
GDN_V_W = GDN_HEADS * GDN_DV
GDN_CONV_CH = 2 * GDN_QK_W + GDN_V_W
MLA_HEADS = 4
MLA_D_NOPE = 128
MLA_D_ROPE = 64
MLA_D_V = 128
MLA_KV_RANK = 256
MLA_SCALE = (MLA_D_NOPE + MLA_D_ROPE) ** -0.5
QK_PRESCALE = MLA_SCALE * math.log2(math.e)
ROPE_THETA = 10000.0
N_EXPERTS = 256
N_GROUPS = 8
GROUP_SIZE = N_EXPERTS // N_GROUPS
TOPK_GROUPS = 4
TOP_K = 8
EXP_HIDDEN = 256
ROUTED_SCALE = 2.5
DEPTH = 1
DEEPNORM_ALPHA = (2.0 * DEPTH) ** 0.25
LN_EPS = 1e-5
RMS_EPS = 1e-6
L2_EPS = 1e-6

LANES = 128
PK_QKV = 0
PK_Z = PK_QKV + GDN_CONV_CH
PK_QNOPE = PK_Z + GDN_V_W
PK_QROPE = PK_QNOPE + MLA_HEADS * MLA_D_NOPE
PK_CKV = PK_QROPE + MLA_HEADS * LANES
PK_KROPE = PK_CKV + MLA_KV_RANK
PK_AB = PK_KROPE + LANES
PK_W = PK_AB + LANES
MLA_QK_W = 2 * LANES

MOE_BM = 256
EXPERT_RING = 8
VMEM_LIMIT = 56 * 1024 * 1024


def _mm(a, b):
    return jnp.dot(a.astype(BF16), b.astype(BF16), preferred_element_type=F32)


def _mm_nt(a, b):
    return lax.dot_general(a.astype(BF16), b.astype(BF16), (((1,), (1,)), ((), ())),
                           preferred_element_type=F32)


def _mm_tn(a, b):
    return lax.dot_general(a.astype(BF16), b.astype(BF16), (((0,), (0,)), ((), ())),
                           preferred_element_type=F32)


def _split3(x):
    hi = x.astype(BF16)
    r = x - hi.astype(F32)
    mid = r.astype(BF16)
    lo = (r - mid.astype(F32)).astype(BF16)
    return hi, mid, lo


def _sigmoid(x):
    return 1.0 / (1.0 + jnp.exp(-x))


def _silu(x):
    return x * _sigmoid(x)


def _softplus(x):
    return jnp.maximum(x, 0.0) + jnp.log1p(jnp.exp(-jnp.abs(x)))


def _rope(x, cs, sn):
    w = x.shape[-1]
    n = w // LANES
    if n > 1:
        cs = jnp.concatenate([cs] * n, axis=1)
        sn = jnp.concatenate([sn] * n, axis=1)
    lane = lax.broadcasted_iota(I32, x.shape, 1) & (LANES - 1)
    half = MLA_D_ROPE // 2
    swapped = jnp.where(lane < half, pltpu.roll(x, w - half, 1), pltpu.roll(x, half, 1))
    return x * cs + swapped * sn


def _params(*sem):
    return pltpu.CompilerParams(dimension_semantics=sem, vmem_limit_bytes=VMEM_LIMIT)


def _front_kernel(x_ref, w_ref, convw_ref, hist_ref, gpar_ref, kvnw_ref, wukv_ref, cs_ref, sn_ref,
                  qkv_ref, z_ref, gb_ref, q_ref, k_ref, v_ref, lat_ref, kr_ref, convnew_ref,
                  xp_scr, *, tt, c):
    t = pl.program_id(1)
    hrow = 8 - (GDN_CONV - 1)

    @pl.when(t == 0)
    def _():
        xp_scr[hrow:8, :] = hist_ref[0]

    proj = _mm(x_ref[0], w_ref[...])

    raw = proj[:, PK_QKV:PK_Z]
    xp_scr[8:8 + tt, :] = raw
    cw = convw_ref[...]
    y = raw * cw[GDN_CONV - 1:GDN_CONV]
    for i in range(GDN_CONV - 1):
        y = y + xp_scr[hrow + i:hrow + i + tt, :] * cw[i:i + 1]
    tail = xp_scr[tt + hrow:tt + 8, :]
    convnew_ref[0] = tail
    xp_scr[hrow:8, :] = tail
    qkv = _silu(y)
    for h in range(2 * GDN_HEADS):
        xh = qkv[:, h * GDN_DK:(h + 1) * GDN_DK]
        xh = xh * lax.rsqrt(jnp.sum(xh * xh, axis=-1, keepdims=True) + L2_EPS)
        if h < GDN_HEADS:
            xh = xh * GDN_DK ** -0.5
        qkv_ref[0, :, h * GDN_DK:(h + 1) * GDN_DK] = xh
    qkv_ref[0, :, 2 * GDN_QK_W:] = qkv[:, 2 * GDN_QK_W:]
    z_ref[0] = proj[:, PK_Z:PK_QNOPE]

    ab = proj[:, PK_AB:PK_W]
    gpar = gpar_ref[...]
    gc = -jnp.exp(gpar[0:1]) * _softplus(ab + gpar[1:2])
    pos = lax.broadcasted_iota(I32, ab.shape, 0) & (c - 1)
    step = 1
    while step < c:
        gc = gc + jnp.where(pos >= step, pltpu.roll(gc, step, 0), 0.0)
        step *= 2
    beta = _sigmoid(ab)
    lane = lax.broadcasted_iota(I32, ab.shape, 1)
    gb_ref[0] = jnp.where(lane < GDN_HEADS, gc, jnp.where(lane < 2 * GDN_HEADS, beta, 0.0))

    cs = cs_ref[...]
    sn = sn_ref[...]
    q_nope = proj[:, PK_QNOPE:PK_QROPE]
    q_rope = _rope(proj[:, PK_QROPE:PK_CKV], cs, sn)
    c_raw = proj[:, PK_CKV:PK_KROPE]
    latent = c_raw * lax.rsqrt(jnp.mean(c_raw * c_raw, axis=-1, keepdims=True) + RMS_EPS) * kvnw_ref[...]
    lat_ref[0] = latent
    k_rope = _rope(proj[:, PK_KROPE:PK_AB], cs, sn)
    kr_ref[0] = k_rope[:, :MLA_D_ROPE]
    kv = _mm(latent, wukv_ref[...])
    k_rope_b = k_rope.astype(BF16)
    for h in range(MLA_HEADS):
        q_ref[0, h, :, :LANES] = (q_nope[:, h * LANES:(h + 1) * LANES] * QK_PRESCALE).astype(BF16)
        q_ref[0, h, :, LANES:] = (q_rope[:, h * LANES:(h + 1) * LANES] * QK_PRESCALE).astype(BF16)
        k_ref[0, h, :, :LANES] = kv[:, h * LANES:(h + 1) * LANES].astype(BF16)
        k_ref[0, h, :, LANES:] = k_rope_b
        v_ref[0, h] = kv[:, (MLA_HEADS + h) * LANES:(MLA_HEADS + h + 1) * LANES].astype(BF16)


def _front(x, w_pack, conv_w, hist, gpar, kvnw, wukv, cs_tab, sn_tab, tt, c):
    b, t, _ = x.shape
    nt = t // tt
    const2 = lambda bi, ti: (0, 0)
    out_shape = (
        jax.ShapeDtypeStruct((b, t, GDN_CONV_CH), F32),
        jax.ShapeDtypeStruct((b, t, GDN_V_W), F32),
        jax.ShapeDtypeStruct((b, t, LANES), F32),
        jax.ShapeDtypeStruct((b, MLA_HEADS, t, MLA_QK_W), BF16),
        jax.ShapeDtypeStruct((b, MLA_HEADS, t, MLA_QK_W), BF16),
        jax.ShapeDtypeStruct((b, MLA_HEADS, t, MLA_D_V), BF16),
        jax.ShapeDtypeStruct((b, t, MLA_KV_RANK), F32),
        jax.ShapeDtypeStruct((b, t, MLA_D_ROPE), F32),
        jax.ShapeDtypeStruct((b, GDN_CONV - 1, GDN_CONV_CH), F32),
    )
    row3 = lambda w: pl.BlockSpec((1, tt, w), lambda bi, ti: (bi, ti, 0))
    head4 = lambda w: pl.BlockSpec((1, MLA_HEADS, tt, w), lambda bi, ti: (bi, 0, ti, 0))
    return pl.pallas_call(
        functools.partial(_front_kernel, tt=tt, c=c),
        grid=(b, nt),
        in_specs=[
            row3(D_MODEL),
            pl.BlockSpec((D_MODEL, PK_W), const2),
            pl.BlockSpec((GDN_CONV, GDN_CONV_CH), const2),
            pl.BlockSpec((1, GDN_CONV - 1, GDN_CONV_CH), lambda bi, ti: (bi, 0, 0)),
            pl.BlockSpec((2, LANES), const2),
            pl.BlockSpec((1, MLA_KV_RANK), const2),
            pl.BlockSpec((MLA_KV_RANK, 2 * MLA_HEADS * LANES), const2),
            pl.BlockSpec((tt, LANES), lambda bi, ti: (ti, 0)),
            pl.BlockSpec((tt, LANES), lambda bi, ti: (ti, 0)),
        ],
        out_specs=(
            row3(GDN_CONV_CH), row3(GDN_V_W), row3(LANES),
            head4(MLA_QK_W), head4(MLA_QK_W), head4(MLA_D_V),
            row3(MLA_KV_RANK), row3(MLA_D_ROPE),
            pl.BlockSpec((1, GDN_CONV - 1, GDN_CONV_CH), lambda bi, ti: (bi, 0, 0)),
        ),
        out_shape=out_shape,
        scratch_shapes=[pltpu.VMEM((tt + 8, GDN_CONV_CH), F32)],
        compiler_params=_params("arbitrary", "arbitrary"),
        name="front",
    )(x, w_pack, conv_w, hist, gpar, kvnw, wukv, cs_tab, sn_tab)


def _gdn_kernel(qkv_ref, z_ref, gb_ref, s0_ref, nw_ref, og_ref, sout_ref, s_scr, *, tg, c, bb):
    t = pl.program_id(1)
    nh = GDN_HEADS
    r = nh * c
    sh = c.bit_length() - 1

    @pl.when(t == 0)
    def _():
        s_scr[...] = s0_ref[...]

    row = lax.broadcasted_iota(I32, (r, r), 0)
    col = lax.broadcasted_iota(I32, (r, r), 1)
    same = (row >> sh) == (col >> sh)
    incl = same & (row >= col)
    strict = same & (row > col)
    eye = jnp.where(row == col, 1.0, 0.0)
    lane0 = jnp.where(lax.broadcasted_iota(I32, (r, LANES), 1) == 0, 1.0, 0.0).astype(BF16)
    nw = nw_ref[...]
    items = [(bi, ci) for ci in range(tg // c) for bi in range(bb)]
    chunks = range(len(items))

    def stacked(ref, it, base):
        bi, ci = items[it]
        return jnp.concatenate(
            [ref[bi, ci * c:(ci + 1) * c, base + h * LANES:base + (h + 1) * LANES] for h in range(nh)], axis=0)

    def col_bcast(it, lane):
        bi, ci = items[it]
        gbc = gb_ref[bi, ci * c:(ci + 1) * c, :]
        return jnp.concatenate(
            [jnp.broadcast_to(gbc[:, lane + h:lane + h + 1], (c, LANES)) for h in range(nh)], axis=0)

    def as_col(gc_b):
        return _lane_tile(gc_b, r) if r % LANES == 0 else gc_b[:, :r]

    def as_row(gc_b):
        if r % LANES == 0:
            return as_col(gc_b).T
        return sum(lax.dot_general(lane0, p, (((1,), (1,)), ((), ())), preferred_element_type=F32)
                   for p in _split3(gc_b))

    ks = [stacked(qkv_ref, ci, GDN_QK_W) for ci in chunks]
    gc = [col_bcast(ci, 0) for ci in chunks]
    beta = [col_bcast(ci, nh) for ci in chunks]
    decay, qk_kk = [], []
    for ci in chunks:
        decay.append(jnp.exp(jnp.where(incl, as_col(gc[ci]) - as_row(gc[ci]), -jnp.inf)))
        qk_kk.append(_mm_nt(jnp.concatenate([stacked(qkv_ref, ci, 0), ks[ci]], axis=0), ks[ci]))
    intra = [qk_kk[ci][:r] * decay[ci] for ci in chunks]
    n_pow = [jnp.where(strict, -as_col(beta[ci]) * qk_kk[ci][r:] * decay[ci], 0.0)
             for ci in chunks]
    t_inv = [eye + n_pow[ci] for ci in chunks]
    for _ in range(sh - 1):
        n_pow = [_mm(n_pow[ci], n_pow[ci]) for ci in chunks]
        t_inv = [t_inv[ci] + _mm(t_inv[ci], n_pow[ci]) for ci in chunks]
    egc = [jnp.exp(gc[ci]) for ci in chunks]
    uw = [_mm(t_inv[ci], jnp.concatenate([stacked(qkv_ref, ci, 2 * GDN_QK_W) * beta[ci],
                                          ks[ci] * beta[ci] * egc[ci]], axis=1)) for ci in chunks]

    for ci in chunks:
        bi, cpos = items[ci]
        u = uw[ci][:, :GDN_DV]
        w = uw[ci][:, GDN_DV:]
        qd = stacked(qkv_ref, ci, 0) * egc[ci]
        vn, qs_s = [], []
        for h in range(nh):
            hs = slice(h * c, (h + 1) * c)
            s_h = s_scr[bi, h]
            wq = _mm(jnp.concatenate([w[hs], qd[hs]], axis=0), s_h)
            vn_h = u[hs] - wq[:c]
            g_last = gc[ci][h * c + c - 1:h * c + c, :]
            kd = ks[ci][hs] * jnp.exp(g_last - gc[ci][hs])
            s_scr[bi, h] = s_h * jnp.exp(g_last) + _mm_tn(kd, vn_h)
            vn.append(vn_h)
            qs_s.append(wq[c:])
        o = jnp.concatenate(qs_s, axis=0) + _mm(intra[ci], jnp.concatenate(vn, axis=0))

        o = o * lax.rsqrt(jnp.mean(o * o, axis=-1, keepdims=True) + RMS_EPS) * nw
        o = o * _silu(stacked(z_ref, ci, 0))
        for h in range(nh):
            og_ref[bi, cpos * c:(cpos + 1) * c, h * LANES:(h + 1) * LANES] = o[h * c:(h + 1) * c].astype(BF16)

    @pl.when(t == pl.num_programs(1) - 1)
    def _():
        sout_ref[...] = s_scr[...]


def _gdn(qkv, z, gb, s0, nw, tg, c):
    b, t, _ = qkv.shape
    bb = 2 if b % 2 == 0 else 1
    row3 = lambda w: pl.BlockSpec((bb, tg, w), lambda bi, ti: (bi, ti, 0))
    st = pl.BlockSpec((bb, GDN_HEADS, GDN_DK, GDN_DV), lambda bi, ti: (bi, 0, 0, 0))
    return pl.pallas_call(
        functools.partial(_gdn_kernel, tg=tg, c=c, bb=bb),
        grid=(b // bb, t // tg),
        in_specs=[row3(GDN_CONV_CH), row3(GDN_V_W), row3(LANES), st,
                  pl.BlockSpec((1, GDN_DV), lambda bi, ti: (0, 0))],
        out_specs=(row3(GDN_V_W), st),
        out_shape=(jax.ShapeDtypeStruct((b, t, GDN_V_W), BF16),
                   jax.ShapeDtypeStruct((b, GDN_HEADS, GDN_DK, GDN_DV), F32)),
        scratch_shapes=[pltpu.VMEM((bb, GDN_HEADS, GDN_DK, GDN_DV), F32)],
        compiler_params=_params("arbitrary", "arbitrary"),
        name="gdn",
    )(qkv, z, gb, s0, nw)


ATTN_ROW_BLOCK = 32


def _lane_tile(x, width):
    return x if width == LANES else jnp.concatenate([x] * (width // LANES), axis=1)


def _attn_kernel(qi_ref, ki_ref, q_ref, k_ref, v_ref, o_ref, m_scr, l_scr, acc_scr, a_scr, s_scr, p_scr,
                 *, tb, sub):
    step = pl.program_id(2)
    qi = qi_ref[step]
    ki = ki_ref[step]
    csh = CHUNK.bit_length() - 1

    @pl.when(ki == 0)
    def _():
        m_scr[...] = jnp.full(m_scr.shape, -jnp.inf, F32)
        l_scr[...] = jnp.zeros(l_scr.shape, F32)
        acc_scr[...] = jnp.zeros(acc_scr.shape, F32)

    half = tb // 2
    rb = min(ATTN_ROW_BLOCK, half)

    def scores(h, j):
        rows = slice(h * half, (h + 1) * half)
        s_scr[rows, :] = lax.dot_general(q_ref[0, 0, rows, :], k_ref[0, 0, j * sub:(j + 1) * sub, :],
                                         (((1,), (1,)), ((), ())), preferred_element_type=F32)

    def softmax(h, j, masked):
        r0 = h * half

        def block(i):
            rr = pl.ds(pl.multiple_of(r0 + i * rb, rb), rb)
            s = s_scr[rr, :]
            if masked:
                qc = (r0 + i * rb + lax.broadcasted_iota(I32, s.shape, 0)) >> csh
                kc = (j * sub + lax.broadcasted_iota(I32, s.shape, 1)) >> csh
                s = jnp.where(kc <= qc, s, -jnp.inf)
            return rr, s

        def row_max(i, carry):
            rr, s = block(i)
            m_prev = m_scr[rr, :]
            m_new = jnp.maximum(m_prev, jnp.max(s, axis=-1, keepdims=True))
            a_scr[rr, :] = jnp.exp2(m_prev - m_new)
            m_scr[rr, :] = m_new
            return carry

        def row_exp(i, carry):
            rr, s = block(i)
            p = jnp.exp2(s - _lane_tile(m_scr[rr, :], sub))
            l_scr[rr, :] = a_scr[rr, :] * l_scr[rr, :] + jnp.sum(p, axis=-1, keepdims=True)
            p_scr[rr, :] = p.astype(BF16)
            return carry

        for i in range(half // rb):
            row_max(i, 0)
        for i in range(half // rb):
            row_exp(i, 0)

    def weighted_values(h, j):
        rows = slice(h * half, (h + 1) * half)
        acc_scr[rows, :] = a_scr[rows, :] * acc_scr[rows, :] + jnp.dot(
            p_scr[rows, :], v_ref[0, 0, j * sub:(j + 1) * sub, :], preferred_element_type=F32)

    def run(tasks):
        for h in range(2):
            if tasks[h]:
                scores(h, tasks[h][0][0])
        for n in range(max(len(t) for t in tasks)):
            for h in range(2):
                if n < len(tasks[h]):
                    j, masked = tasks[h][n]
                    softmax(h, j, masked)
                    weighted_values(h, j)
                    if n + 1 < len(tasks[h]):
                        scores(h, tasks[h][n + 1][0])

    def diagonal_tasks(h):
        r0 = h * half
        out = []
        for j in range(tb // sub):
            k0, k1 = j * sub, (j + 1) * sub
            if k0 >= r0 + half:
                continue
            out.append((j, k1 > r0 + CHUNK))
        return out

    @pl.when(ki < qi)
    def _():
        run([[(j, False) for j in range(tb // sub)]] * 2)

    @pl.when(ki == qi)
    def _():
        run([diagonal_tasks(0), diagonal_tasks(1)])
        o_ref[0] = (acc_scr[...] / l_scr[...]).astype(BF16)


def _attn_prompt(q, k, v, tb, sub):
    b, nh, t, _ = q.shape
    nt = t // tb
    pairs = [(qi, ki) for qi in range(nt) for ki in range(qi + 1)]
    qi_of = jnp.asarray([p[0] for p in pairs], I32)
    ki_of = jnp.asarray([p[1] for p in pairs], I32)
    return pl.pallas_call(
        functools.partial(_attn_kernel, tb=tb, sub=sub),
        grid_spec=pltpu.PrefetchScalarGridSpec(
            num_scalar_prefetch=2,
            grid=(b, nh, len(pairs)),
            in_specs=[pl.BlockSpec((1, 1, tb, MLA_QK_W), lambda bi, hi, s, qo, ko: (bi, hi, qo[s], 0)),
                      pl.BlockSpec((1, 1, tb, MLA_QK_W), lambda bi, hi, s, qo, ko: (bi, hi, ko[s], 0)),
                      pl.BlockSpec((1, 1, tb, MLA_D_V), lambda bi, hi, s, qo, ko: (bi, hi, ko[s], 0))],
            out_specs=pl.BlockSpec((1, tb, MLA_D_V), lambda bi, hi, s, qo, ko: (bi, qo[s], hi)),
            scratch_shapes=[pltpu.VMEM((tb, LANES), F32), pltpu.VMEM((tb, LANES), F32),
                            pltpu.VMEM((tb, MLA_D_V), F32), pltpu.VMEM((tb, LANES), F32),
                            pltpu.VMEM((tb, sub), F32), pltpu.VMEM((tb, sub), BF16)]),
        out_shape=jax.ShapeDtypeStruct((b, t, MLA_HEADS * MLA_D_V), BF16),
        compiler_params=_params("arbitrary", "arbitrary", "arbitrary"),
        name="attn_prompt",
    )(qi_of, ki_of, q, k, v)


def _attn_sample_kernel(q_ref, kn_ref, vn_ref, plat_ref, pkr_ref, wukv_ref, o_ref):
    kvp = _mm(plat_ref[0], wukv_ref[...])
    pkr = pkr_ref[0].astype(BF16)
    for h in range(MLA_HEADS):
        q = q_ref[0, h]
        s_past = (_mm_nt(q[:, :MLA_D_NOPE], kvp[:, h * LANES:(h + 1) * LANES])
                  + _mm_nt(q[:, MLA_D_NOPE:MLA_D_NOPE + MLA_D_ROPE], pkr))
        s_new = _mm_nt(q, kn_ref[0, h])
        m = jnp.maximum(jnp.max(s_past, axis=-1, keepdims=True), jnp.max(s_new, axis=-1, keepdims=True))
        p_past = jnp.exp2(s_past - m)
        p_new = jnp.exp2(s_new - m)
        l = jnp.sum(p_past, axis=-1, keepdims=True) + jnp.sum(p_new, axis=-1, keepdims=True)
        o = _mm(p_past, kvp[:, (MLA_HEADS + h) * LANES:(MLA_HEADS + h + 1) * LANES]) + _mm(p_new, vn_ref[0, h])
        o_ref[0, :, h * MLA_D_V:(h + 1) * MLA_D_V] = (o / l).astype(BF16)


def _attn_sample(q, k_new, v_new, past_lat, past_kr, wukv):
    b, nh, ts, _ = q.shape
    past = past_lat.shape[1]
    b4 = lambda w: pl.BlockSpec((1, nh, ts, w), lambda bi: (bi, 0, 0, 0))
    return pl.pallas_call(
        _attn_sample_kernel,
        grid=(b,),
        in_specs=[b4(MLA_QK_W), b4(MLA_QK_W), b4(MLA_D_V),
                  pl.BlockSpec((1, past, MLA_KV_RANK), lambda bi: (bi, 0, 0)),
                  pl.BlockSpec((1, past, MLA_D_ROPE), lambda bi: (bi, 0, 0)),
                  pl.BlockSpec((MLA_KV_RANK, 2 * MLA_HEADS * LANES), lambda bi: (0, 0))],
        out_specs=pl.BlockSpec((1, ts, MLA_HEADS * MLA_D_V), lambda bi: (bi, 0, 0)),
        out_shape=jax.ShapeDtypeStruct((b, ts, MLA_HEADS * MLA_D_V), BF16),
        compiler_params=_params("arbitrary"),
        name="attn_sample",
    )(q, k_new, v_new, past_lat, past_kr, wukv)


def _layernorm(y, g, b):
    mu = jnp.mean(y, axis=-1, keepdims=True)
    d = y - mu
    var = jnp.mean(d * d, axis=-1, keepdims=True)
    return d * lax.rsqrt(var + LN_EPS) * g + b


N_SHARED = 4


def _mixln_kernel(og_ref, om_ref, x_ref, w_ref, g_ref, b_ref, rw_ref, rb_ref, cnt0_ref, *rest, tm):
    o_ref, idx_ref, gate_ref, rank_ref, cnt_ref, carry_scr = rest[-6:]

    @pl.when(pl.program_id(0) == 0)
    def _():
        carry_scr[...] = cnt0_ref[...]

    mix = (jnp.dot(og_ref[...], w_ref[:GDN_V_W, :], preferred_element_type=F32)
           + jnp.dot(om_ref[...], w_ref[GDN_V_W:, :], preferred_element_type=F32))
    x1 = _layernorm(DEEPNORM_ALPHA * x_ref[...] + mix, g_ref[...], b_ref[...])
    o_ref[...] = x1
    _route_tile(x1, rw_ref, rb_ref, carry_scr, idx_ref, gate_ref, rank_ref, tm)
    cnt_ref[...] = carry_scr[...]


def _mixln(og, om, x, w_out, g, b, rw, rb, cnt0, shared, n_all, row0, tm):
    n = x.shape[0]
    blk0 = row0 // tm
    rows = lambda w: pl.BlockSpec((tm, w), lambda i: (i, 0))
    const = lambda s: pl.BlockSpec(s, lambda i: (0, 0))
    kspec = pl.BlockSpec((TOP_K, tm), lambda i: (0, blk0 + i))
    first = shared is None
    n_in = 9
    kt = lambda dt: jax.ShapeDtypeStruct((TOP_K, n_all), dt)
    return pl.pallas_call(
        functools.partial(_mixln_kernel, tm=tm),
        grid=(n // tm,),
        in_specs=[rows(GDN_V_W), rows(MLA_HEADS * MLA_D_V), rows(D_MODEL),
                  const((D_MODEL, D_MODEL)), const((1, D_MODEL)), const((1, D_MODEL)),
                  const((D_MODEL, N_EXPERTS)), const((N_EXPERTS, 1)), const((N_EXPERTS, LANES))]
                 + ([] if first else [pl.BlockSpec(memory_space=pl.ANY)] * N_SHARED),
        out_specs=(pl.BlockSpec((tm, D_MODEL), lambda i: (blk0 + i, 0)), kspec, kspec, kspec,
                   const((N_EXPERTS, LANES))),
        out_shape=(jax.ShapeDtypeStruct((n_all, D_MODEL), F32), kt(I32), kt(F32), kt(I32),
                   jax.ShapeDtypeStruct((N_EXPERTS, LANES), F32)),
        scratch_shapes=[pltpu.VMEM((N_EXPERTS, LANES), F32)],
        input_output_aliases={} if first else {n_in + j: j for j in range(N_SHARED)},
        compiler_params=_params("arbitrary"),
        name="mixln",
    )(*((og, om, x, w_out, g, b, rw, rb, cnt0) + (() if first else tuple(shared))))


def _route_tile(x, rw_ref, rb_ref, carry_scr, idx_ref, gate_ref, rank_ref, tt):
    ninf = -jnp.inf
    big = float(2 * N_EXPERTS)
    scores = _sigmoid(_mm(x, rw_ref[...]).T)
    biased = scores + rb_ref[...]
    eio = lax.broadcasted_iota(I32, (N_EXPERTS, tt), 0).astype(F32)

    def first_argmax(vals, io):
        m = jnp.max(vals, axis=0, keepdims=True)
        i = jnp.min(jnp.where(vals == m, io, big), axis=0, keepdims=True)
        return m, i

    gs = []
    for g in range(N_GROUPS):
        blk = biased[g * GROUP_SIZE:(g + 1) * GROUP_SIZE]
        io = (lax.broadcasted_iota(I32, (GROUP_SIZE, tt), 0) + g * GROUP_SIZE).astype(F32)
        m1, i1 = first_argmax(blk, io)
        m2 = jnp.max(jnp.where(io == i1, ninf, blk), axis=0, keepdims=True)
        gs.append(m1 + m2)
    gio = lax.broadcasted_iota(I32, (N_GROUPS, tt), 0).astype(F32)
    gsc = jnp.zeros((N_GROUPS, tt), F32)
    for g in range(N_GROUPS):
        gsc = jnp.where(gio == float(g), gs[g], gsc)
    gsel = jnp.zeros((N_GROUPS, tt), F32)
    for _ in range(TOPK_GROUPS):
        _, gi = first_argmax(gsc, gio)
        hit = gio == gi
        gsel = jnp.where(hit, 1.0, gsel)
        gsc = jnp.where(hit, ninf, gsc)
    masked = jnp.concatenate(
        [jnp.where(jnp.max(jnp.where(gio == float(g), gsel, 0.0), axis=0, keepdims=True) > 0.0,
                   biased[g * GROUP_SIZE:(g + 1) * GROUP_SIZE], ninf) for g in range(N_GROUPS)], axis=0)

    idx, wts = [], []
    sel = jnp.zeros((N_EXPERTS, tt), F32)
    for _ in range(TOP_K):
        _, ei = first_argmax(masked, eio)
        hit = eio == ei
        wts.append(jnp.sum(jnp.where(hit, scores, 0.0), axis=0, keepdims=True))
        masked = jnp.where(hit, ninf, masked)
        sel = jnp.where(hit, 1.0, sel)
        idx.append(ei)
    wsum = wts[0]
    for w in wts[1:]:
        wsum = wsum + w

    t0 = lax.broadcasted_iota(I32, (tt, tt), 0)
    t1 = lax.broadcasted_iota(I32, (tt, tt), 1)
    before = jnp.where(t0 < t1, 1.0, 0.0).astype(BF16)
    sel_b = sel.astype(BF16)
    base = carry_scr[:, :1] + jnp.dot(sel_b, before, preferred_element_type=F32)
    ranks = [jnp.sum(jnp.where(eio == ei, base, 0.0), axis=0, keepdims=True) for ei in idx]
    carry_scr[...] = carry_scr[...] + jnp.dot(sel_b, jnp.ones((tt, LANES), BF16), preferred_element_type=F32)

    for k in range(TOP_K):
        idx_ref[k:k + 1, :] = idx[k].astype(I32)
        gate_ref[k:k + 1, :] = wts[k] / wsum * ROUTED_SCALE
        rank_ref[k:k + 1, :] = ranks[k].astype(I32)


def _dest_kernel(idx_ref, rank_ref, pstart_ref, dest_ref, *, tt):
    eio = lax.broadcasted_iota(I32, (N_EXPERTS, tt), 0)
    pstart = pstart_ref[...]
    for k in range(TOP_K):
        start = jnp.sum(jnp.where(eio == idx_ref[k:k + 1, :], pstart, 0.0), axis=0, keepdims=True)
        dest_ref[0, k:k + 1, :] = start.astype(I32) + rank_ref[k:k + 1, :]


def _dest(idx, rank, pstart, tt):
    n = idx.shape[1]
    kspec = pl.BlockSpec((TOP_K, tt), lambda i: (0, i))
    return pl.pallas_call(
        functools.partial(_dest_kernel, tt=tt),
        grid=(n // tt,),
        in_specs=[kspec, kspec, pl.BlockSpec((N_EXPERTS, 1), lambda i: (0, 0))],
        out_specs=pl.BlockSpec((1, TOP_K, tt), lambda i: (i, 0, 0)),
        out_shape=jax.ShapeDtypeStruct((n // tt, TOP_K, tt), I32),
        compiler_params=_params("arbitrary"),
        name="dest",
    )(idx, rank, pstart)


ROW_TILE = D_MODEL // LANES
XROW_TILE = ROW_TILE // 2


def _rows_to_tiles(x, tiles_ref, base, n, rt=ROW_TILE):
    for j in range(rt):
        tiles_ref[pl.ds(base * rt + j, n, stride=rt), :] = x[:, j * LANES:(j + 1) * LANES]


def _tiles_to_rows(tiles_ref, base, n, rt=ROW_TILE):
    return jnp.concatenate(
        [tiles_ref[pl.ds(base * rt + j, n, stride=rt), :] for j in range(rt)], axis=1)


def _pack_bf16_pairs(x):
    half = x.shape[1] // 2
    lo = pltpu.bitcast(x[:, :half].astype(BF16).astype(F32), jnp.uint32) >> 16
    hi = pltpu.bitcast(x[:, half:].astype(BF16).astype(F32), jnp.uint32) & jnp.uint32(0xFFFF0000)
    return lo | hi


def _unpack_bf16_pairs(w):
    lo = pltpu.bitcast(w << 16, F32)
    hi = pltpu.bitcast(w & jnp.uint32(0xFFFF0000), F32)
    return jnp.concatenate([lo, hi], axis=1)


def _dispatch_kernel(tail_ref, dest_ref, x_ref, xs_out, zbuf, xt_scr, sem, zsem, *, td):
    s = pl.program_id(0)

    @pl.when(s == 0)
    def _():
        zbuf[...] = jnp.zeros(zbuf.shape, zbuf.dtype)

        def zero_copy(e):
            first = pl.multiple_of(tail_ref[e] * XROW_TILE, MOE_BM * XROW_TILE)
            return pltpu.make_async_copy(zbuf, xs_out.at[pl.ds(first, MOE_BM * XROW_TILE)], zsem)

        def zstart(e, carry):
            zero_copy(e).start()
            return carry

        def zwait(e, carry):
            zero_copy(e).wait()
            return carry

        lax.fori_loop(0, N_EXPERTS, zstart, 0)
        lax.fori_loop(0, N_EXPERTS, zwait, 0)

    slot = s & 1

    def row_copy(slot_, i, d):
        src = pl.multiple_of((slot_ * td + i) * XROW_TILE, XROW_TILE)
        return pltpu.make_async_copy(xt_scr.at[pl.ds(src, XROW_TILE)],
                                     xs_out.at[pl.ds(pl.multiple_of(d * XROW_TILE, XROW_TILE), XROW_TILE)],
                                     sem.at[slot_])

    def drain(slot_):
        step_rows = TOP_K * td * XROW_TILE
        pltpu.make_async_copy(xs_out.at[pl.ds(0, step_rows)], xs_out.at[pl.ds(0, step_rows)],
                              sem.at[slot_]).wait()

    @pl.when(s >= 2)
    def _():
        drain(slot)

    _rows_to_tiles(_pack_bf16_pairs(x_ref[...]), xt_scr, slot * td, td, XROW_TILE)

    def issue(i, carry):
        for k in range(TOP_K):
            row_copy(slot, i, dest_ref[0, 0, k * td + i]).start(priority=k % 2)
        return carry

    lax.fori_loop(0, td, issue, 0, unroll=4)

    @pl.when(s == pl.num_programs(0) - 1)
    def _():
        drain(slot)

        @pl.when(s >= 1)
        def _():
            drain(1 - slot)


def _dispatch(tail, dest, x1, n_rows, td):
    n = x1.shape[0]
    return pl.pallas_call(
        functools.partial(_dispatch_kernel, td=td),
        grid_spec=pltpu.PrefetchScalarGridSpec(
            num_scalar_prefetch=1,
            grid=(n // td,),
            in_specs=[pl.BlockSpec((1, 1, TOP_K * td), lambda i, tl: (i, 0, 0), memory_space=pltpu.SMEM),
                      pl.BlockSpec((td, D_MODEL), lambda i, tl: (i, 0))],
            out_specs=pl.BlockSpec(memory_space=pl.ANY),
            scratch_shapes=[pltpu.VMEM((MOE_BM * XROW_TILE, LANES), jnp.uint32),
                            pltpu.VMEM((2 * td * XROW_TILE, LANES), jnp.uint32),
                            pltpu.SemaphoreType.DMA((2,)), pltpu.SemaphoreType.DMA(())]),
        out_shape=jax.ShapeDtypeStruct((n_rows * XROW_TILE, LANES), jnp.uint32),
        compiler_params=_params("arbitrary"),
        name="dispatch",
    )(tail, dest, x1)


def _expert_kernel(first_ref, nblk_ref, nact_ref, xs_hbm, wg_ref, wu_ref, wd_ref, y_hbm,
                   xbuf, ybuf, wgb, wub, wdb, xsem, ysem):
    e = pl.program_id(0)
    nact = nact_ref[0]
    ring = EXPERT_RING
    ahead = ring - 2
    xr = MOE_BM * XROW_TILE
    yr = MOE_BM * XROW_TILE

    def x_copy(g):
        slot = g & (ring - 1)
        return pltpu.make_async_copy(xs_hbm.at[pl.ds(pl.multiple_of(g * xr, xr), xr)],
                                     xbuf.at[pl.ds(pl.multiple_of(slot * xr, xr), xr)], xsem.at[slot])

    def y_copy(g):
        slot = g & (ring - 1)
        return pltpu.make_async_copy(ybuf.at[pl.ds(pl.multiple_of(slot * yr, yr), yr)],
                                     y_hbm.at[pl.ds(pl.multiple_of(g * yr, yr), yr)], ysem.at[slot])

    @pl.when(e == 0)
    def _():
        for g0 in range(ahead):
            @pl.when(g0 < nact)
            def _():
                x_copy(g0).start()

    @pl.when(nblk_ref[e] > 0)
    def _():
        wgb[...] = wg_ref[0].astype(BF16)
        wub[...] = wu_ref[0].astype(BF16)
        wdb[...] = wd_ref[0].astype(BF16)

    def blocks(g0, nb):
        gs = [g0 + j for j in range(nb)]
        for g in gs:
            x_copy(g).wait()

            @pl.when(g + ahead < nact)
            def _():
                x_copy(g + ahead).start()

            @pl.when(g >= ring)
            def _():
                y_copy(g - ring).wait()

        xb = jnp.concatenate(
            [_unpack_bf16_pairs(_tiles_to_rows(xbuf, (g & (ring - 1)) * MOE_BM, MOE_BM, XROW_TILE)) for g in gs],
            axis=0).astype(BF16)
        hg = jnp.dot(xb, wgb[...], preferred_element_type=F32)
        hu = jnp.dot(xb, wub[...], preferred_element_type=F32)
        y = jnp.dot((_silu(hg) * hu).astype(BF16), wdb[...], preferred_element_type=F32)
        yp = _pack_bf16_pairs(y)
        for j, g in enumerate(gs):
            _rows_to_tiles(yp[j * MOE_BM:(j + 1) * MOE_BM], ybuf, (g & (ring - 1)) * MOE_BM, MOE_BM, XROW_TILE)
            y_copy(g).start()

        g_last = gs[-1]

        @pl.when(g_last == nact - 1)
        def _():
            for back in range(ring):
                @pl.when(g_last >= back)
                def _():
                    y_copy(g_last - back).wait()

    nblk = nblk_ref[e]

    def pair(b, carry):
        blocks(first_ref[e] + 2 * b, 2)
        return carry

    lax.fori_loop(0, nblk // 2, pair, 0)

    @pl.when(nblk % 2 == 1)
    def _():
        blocks(first_ref[e] + nblk - 1, 1)


def _experts(first, nblk, nact, xs, wg, wu, wd):
    n_rows = xs.shape[0] // XROW_TILE
    wspec = lambda s: pl.BlockSpec((1,) + s, lambda e, fr, nb, na: (e, 0, 0))
    return pl.pallas_call(
        _expert_kernel,
        grid_spec=pltpu.PrefetchScalarGridSpec(
            num_scalar_prefetch=3,
            grid=(N_EXPERTS,),
            in_specs=[pl.BlockSpec(memory_space=pl.ANY),
                      wspec((D_MODEL, EXP_HIDDEN)), wspec((D_MODEL, EXP_HIDDEN)), wspec((EXP_HIDDEN, D_MODEL))],
            out_specs=pl.BlockSpec(memory_space=pl.ANY),
            scratch_shapes=[pltpu.VMEM((EXPERT_RING * MOE_BM * XROW_TILE, LANES), jnp.uint32),
                            pltpu.VMEM((EXPERT_RING * MOE_BM * XROW_TILE, LANES), jnp.uint32),
                            pltpu.VMEM((D_MODEL, EXP_HIDDEN), BF16), pltpu.VMEM((D_MODEL, EXP_HIDDEN), BF16),
                            pltpu.VMEM((EXP_HIDDEN, D_MODEL), BF16),
                            pltpu.SemaphoreType.DMA((EXPERT_RING,)), pltpu.SemaphoreType.DMA((EXPERT_RING,))]),
        out_shape=jax.ShapeDtypeStruct((n_rows * XROW_TILE, LANES), jnp.uint32),
        compiler_params=_params("arbitrary"),
        name="experts",
    )(first, nblk, nact, xs, wg, wu, wd)


def _combine_kernel(dcur_ref, dnxt_ref, y_hbm, gate_ref, x_ref, wsg_ref, wsu_ref, wsd_ref, g_ref, b_ref,
                    outp_ref, outs_ref, buf, routed_scr, sem, *, tc, np_tiles):
    s = pl.program_id(0)
    ns = pl.num_programs(0)
    slot = s % 2

    def row_copy(d, slot_, k, i):
        dst = pl.multiple_of(((slot_ * TOP_K + k) * tc + i) * XROW_TILE, XROW_TILE)
        return pltpu.make_async_copy(y_hbm.at[pl.ds(pl.multiple_of(d * XROW_TILE, XROW_TILE), XROW_TILE)],
                                     buf.at[pl.ds(dst, XROW_TILE)], sem.at[slot_])

    def issue(dref, slot_):
        def body(i, carry):
            for k in range(TOP_K):
                row_copy(dref[0, 0, k * tc + i], slot_, k, i).start(priority=k % 2)
            return carry
        lax.fori_loop(0, tc, body, 0, unroll=4)

    @pl.when(s == 0)
    def _():
        issue(dcur_ref, 0)

    slot_rows = TOP_K * tc * XROW_TILE
    pltpu.make_async_copy(y_hbm.at[pl.ds(0, slot_rows)],
                          buf.at[pl.ds(pl.multiple_of(slot * slot_rows, slot_rows), slot_rows)],
                          sem.at[slot]).wait()

    grp = 16

    def gated_sum(gi):
        base = pl.multiple_of(gi * grp, grp)
        gate = gate_ref[pl.ds(base, grp), :]
        acc = None
        for k in range(TOP_K):
            rows = _unpack_bf16_pairs(_tiles_to_rows(buf, (slot * TOP_K + k) * tc + base, grp, XROW_TILE))
            term = rows * gate[:, k:k + 1]
            acc = term if acc is None else acc + term
        routed_scr[pl.ds(base, grp), :] = acc

    def sum_and_issue(gi, carry):
        for i in range(grp):
            for k in range(TOP_K):
                row_copy(dnxt_ref[0, 0, k * tc + gi * grp + i], 1 - slot, k, gi * grp + i).start(priority=k % 2)
        gated_sum(gi)
        return carry

    def sum_only(gi, carry):
        gated_sum(gi)
        return carry

    @pl.when(s + 1 < ns)
    def _():
        lax.fori_loop(0, tc // grp, sum_and_issue, 0)

    @pl.when(s + 1 == ns)
    def _():
        lax.fori_loop(0, tc // grp, sum_only, 0)

    x = x_ref[...]
    routed = routed_scr[...]
    xb = x.astype(BF16)
    shared = _mm(_silu(_mm(xb, wsg_ref[...])) * _mm(xb, wsu_ref[...]), wsd_ref[...])
    out = _layernorm(DEEPNORM_ALPHA * x + (routed + shared), g_ref[...], b_ref[...])

    @pl.when(s < np_tiles)
    def _():
        outp_ref[...] = out

    @pl.when(s >= np_tiles)
    def _():
        outs_ref[...] = out


def _combine(dest, y_sorted, gate, x1, wsg, wsu, wsd, g, b, n_prompt, tc):
    n = x1.shape[0]
    ns = n // tc
    np_tiles = n_prompt // tc
    const = lambda s: pl.BlockSpec(s, lambda i: (0, 0))
    dspec = lambda f: pl.BlockSpec((1, 1, TOP_K * tc), f, memory_space=pltpu.SMEM)
    return pl.pallas_call(
        functools.partial(_combine_kernel, tc=tc, np_tiles=np_tiles),
        grid=(ns,),
        in_specs=[dspec(lambda i: (i, 0, 0)), dspec(lambda i: (jnp.minimum(i + 1, ns - 1), 0, 0)),
                  pl.BlockSpec(memory_space=pl.ANY),
                  pl.BlockSpec((tc, TOP_K), lambda i: (i, 0)),
                  pl.BlockSpec((tc, D_MODEL), lambda i: (i, 0)),
                  const((D_MODEL, EXP_HIDDEN)), const((D_MODEL, EXP_HIDDEN)), const((EXP_HIDDEN, D_MODEL)),
                  const((1, D_MODEL)), const((1, D_MODEL))],
        out_specs=(pl.BlockSpec((tc, D_MODEL), lambda i: (jnp.minimum(i, np_tiles - 1), 0)),
                   pl.BlockSpec((tc, D_MODEL), lambda i: (jnp.maximum(i - np_tiles, 0), 0))),
        out_shape=(jax.ShapeDtypeStruct((n_prompt, D_MODEL), F32),
                   jax.ShapeDtypeStruct((n - n_prompt, D_MODEL), F32)),
        scratch_shapes=[pltpu.VMEM((2 * TOP_K * tc * XROW_TILE, LANES), jnp.uint32),
                        pltpu.VMEM((tc, D_MODEL), F32),
                        pltpu.SemaphoreType.DMA((2,))],
        compiler_params=_params("arbitrary"),
        name="combine",
    )(dest, dest, y_sorted, gate, x1, wsg, wsu, wsd, g, b)


def _pack_w_in(w_in):
    d = w_in.shape[0]
    o_z = GDN_CONV_CH
    o_a = o_z + GDN_V_W
    o_b = o_a + GDN_HEADS
    o_q = o_b + GDN_HEADS
    o_c = o_q + MLA_HEADS * (MLA_D_NOPE + MLA_D_ROPE)
    o_kr = o_c + MLA_KV_RANK
    zeros = lambda w: jnp.zeros((d, w), w_in.dtype)
    wq = w_in[:, o_q:o_c].reshape(d, MLA_HEADS, MLA_D_NOPE + MLA_D_ROPE)
    q_nope = wq[:, :, :MLA_D_NOPE].reshape(d, MLA_HEADS * MLA_D_NOPE)
    q_rope = jnp.pad(wq[:, :, MLA_D_NOPE:], ((0, 0), (0, 0), (0, LANES - MLA_D_ROPE))).reshape(d, MLA_HEADS * LANES)
    cols = [w_in[:, :o_a], q_nope, q_rope, w_in[:, o_c:o_kr], w_in[:, o_kr:], zeros(LANES - MLA_D_ROPE),
            w_in[:, o_a:o_q], zeros(LANES - 2 * GDN_HEADS)]
    return jnp.concatenate(cols, axis=1).astype(BF16)


def _rope_tables(pos):
    inv_freq = ROPE_THETA ** (-jnp.arange(0, MLA_D_ROPE, 2, dtype=F32) / MLA_D_ROPE)
    ang = pos.astype(F32)[:, None] * inv_freq[None, :]
    cos, sin = jnp.cos(ang), jnp.sin(ang)
    pad = jnp.zeros((pos.shape[0], LANES - MLA_D_ROPE), F32)
    return jnp.concatenate([cos, cos, pad], axis=1), jnp.concatenate([-sin, sin, pad], axis=1)


def _pick(t, pref):
    return pref if t % pref == 0 else t


def _token_mixers(x, pos, conv_hist, s0, past, wts, shared, cnt0, n_all, row0):
    b, t, _ = x.shape
    cs_tab, sn_tab = _rope_tables(pos)
    tt = _pick(t, 512)
    c = min(CHUNK, t)
    qkv, z, gb, q, k, v, latent, k_rope, conv_new = _front(
        x, wts["w_pack"], wts["conv_w"], conv_hist, wts["gpar"], wts["kvnw"], wts["wukv"], cs_tab, sn_tab, tt, c)
    og, s_new = _gdn(qkv, z, gb, s0, wts["gdn_nw"], _pick(t, 4 * CHUNK), c)
    if past is None:
        tb = _pick(t, 2048)
        om = _attn_prompt(q, k, v, tb, _pick(tb, 512))
    else:
        om = _attn_sample(q, k, v, past[0], past[1], wts["wukv"])
    n = b * t
    *shared, cnt = _mixln(og.reshape(n, -1), om.reshape(n, -1), x.reshape(n, D_MODEL), wts["w_out"],
                          wts["ln1_g"], wts["ln1_b"], wts["router_w"], wts["router_b"], cnt0,
                          shared, n_all, row0, _pick(n, 256))
    return shared, cnt, latent, k_rope, s_new, conv_new


def _moe(x1_all, idx, gate, rank, cnt, n_prompt, wts):
    n = x1_all.shape[0]
    counts = cnt[:, 0].astype(I32)
    padded = (counts + MOE_BM - 1) // MOE_BM * MOE_BM
    pend = jnp.cumsum(padded)
    pstart = pend - padded
    td = _pick(math.gcd(n_prompt, n - n_prompt), 256)
    dest = _dest(idx, rank, pstart.astype(F32).reshape(-1, 1), td)
    dest = dest.reshape(n // td, 1, TOP_K * td)
    n_blocks = n * TOP_K // MOE_BM + N_EXPERTS
    nact = (pend[-1:] // MOE_BM).astype(I32)
    tail = jnp.maximum(pend - MOE_BM, 0).astype(I32)

    xs = _dispatch(tail, dest, x1_all, n_blocks * MOE_BM, td)
    y_sorted = _experts((pstart // MOE_BM).astype(I32), (padded // MOE_BM).astype(I32), nact, xs,
                        wts["exp_wg"], wts["exp_wu"], wts["exp_wd"])
    return _combine(dest, y_sorted, gate.T, x1_all,
                    wts["sh_wg"], wts["sh_wu"], wts["sh_wd"], wts["ln2_g"], wts["ln2_b"], n_prompt, td)


def kernel(x_prompt, x_sample, cache_kv_latent, cache_k_rope, state_gdn, state_conv, w_in, gdn_conv_w, gdn_a_log, gdn_dt_bias, gdn_norm_w, mla_kv_norm_w, mla_w_uk, mla_w_uv, w_out, ln1_g, ln1_b, router_w, router_bias, exp_w_gate, exp_w_up, exp_w_down, shared_w_gate, shared_w_up, shared_w_down, ln2_g, ln2_b):
    assert w_in.shape[0] == 1, "single-layer stack"
    b_p, t_p, _ = x_prompt.shape
    b_s, t_s, _ = x_sample.shape
    past = cache_kv_latent.shape[2]
    l = 0
    pad4 = lambda a: jnp.pad(a.astype(F32), (0, LANES - GDN_HEADS))
    wts = {
        "w_pack": _pack_w_in(w_in[l]),
        "conv_w": gdn_conv_w[l],
        "gpar": jnp.stack([pad4(gdn_a_log[l]), pad4(gdn_dt_bias[l])]),
        "kvnw": mla_kv_norm_w[l].reshape(1, -1),
        "wukv": jnp.concatenate([mla_w_uk[l].reshape(MLA_KV_RANK, -1), mla_w_uv[l].reshape(MLA_KV_RANK, -1)],
                                axis=1).astype(BF16),
        "gdn_nw": gdn_norm_w[l].reshape(1, -1),
        "w_out": w_out[l].astype(BF16),
        "ln1_g": ln1_g[l].reshape(1, -1), "ln1_b": ln1_b[l].reshape(1, -1),
        "router_w": router_w[l].astype(BF16), "router_b": router_bias[l].reshape(-1, 1),
        "exp_wg": exp_w_gate[l], "exp_wu": exp_w_up[l], "exp_wd": exp_w_down[l],
        "sh_wg": shared_w_gate[l].astype(BF16), "sh_wu": shared_w_up[l].astype(BF16),
        "sh_wd": shared_w_down[l].astype(BF16),
        "ln2_g": ln2_g[l].reshape(1, -1), "ln2_b": ln2_b[l].reshape(1, -1),
    }
    n_p, n_s = b_p * t_p, b_s * t_s
    conv0 = jnp.zeros((b_p, GDN_CONV - 1, GDN_CONV_CH), F32)
    s0 = jnp.zeros((b_p, GDN_HEADS, GDN_DK, GDN_DV), F32)
    cnt0 = jnp.zeros((N_EXPERTS, LANES), F32)
    shared, cnt, lat_p, kr_p, sg_p, cv_p = _token_mixers(
        x_prompt, jnp.arange(t_p), conv0, s0, None, wts, None, cnt0, n_p + n_s, 0)
    shared, cnt, lat_s, kr_s, sg_s, cv_s = _token_mixers(
        x_sample, past + jnp.arange(t_s), state_conv[l], state_gdn[l],
        (cache_kv_latent[l], cache_k_rope[l]), wts, shared, cnt, n_p + n_s, n_p)
    y_p, y_s = _moe(*shared, cnt, n_p, wts)
    return (y_p.reshape(b_p, t_p, D_MODEL), y_s.reshape(b_s, t_s, D_MODEL),
            lat_p[None], kr_p[None], sg_p[None], cv_p[None],
            lat_s[None], kr_s[None], sg_s[None], cv_s[None])
```

```python
import functools
import math

import jax
import jax.numpy as jnp
from jax import lax
from jax.experimental import pallas as pl
from jax.experimental.pallas import tpu as pltpu

F32 = jnp.float32
BF16 = jnp.bfloat16
I32 = jnp.int32

D_MODEL = 1024
CHUNK = 64
GDN_HEADS = 4
GDN_DK = 128
GDN_DV = 128
GDN_CONV = 4
GDN_QK_W = GDN_HEADS * GDN_DK
GDN_V_W = GDN_HEADS * GDN_DV
GDN_CONV_CH = 2 * GDN_QK_W + GDN_V_W
MLA_HEADS = 4
MLA_D_NOPE = 128
MLA_D_ROPE = 64
MLA_D_V = 128
MLA_KV_RANK = 256
MLA_SCALE = (MLA_D_NOPE + MLA_D_ROPE) ** -0.5
QK_PRESCALE = MLA_SCALE * math.log2(math.e)
ROPE_THETA = 10000.0
N_EXPERTS = 256
N_GROUPS = 8
GROUP_SIZE = N_EXPERTS // N_GROUPS
TOPK_GROUPS = 4
TOP_K = 8
EXP_HIDDEN = 256
ROUTED_SCALE = 2.5
DEPTH = 1
DEEPNORM_ALPHA = (2.0 * DEPTH) ** 0.25
LN_EPS = 1e-5
RMS_EPS = 1e-6
L2_EPS = 1e-6

LANES = 128
PK_QKV = 0
PK_Z = PK_QKV + GDN_CONV_CH
PK_QNOPE = PK_Z + GDN_V_W
PK_QROPE = PK_QNOPE + MLA_HEADS * MLA_D_NOPE
PK_CKV = PK_QROPE + MLA_HEADS * LANES
PK_KROPE = PK_CKV + MLA_KV_RANK
PK_AB = PK_KROPE + LANES
PK_W = PK_AB + LANES
MLA_QK_W = 2 * LANES

MOE_BM = 256
EXPERT_RING = 8
VMEM_LIMIT = 56 * 1024 * 1024


def _mm(a, b):
    return jnp.dot(a.astype(BF16), b.astype(BF16), preferred_element_type=F32)


def _mm_nt(a, b):
    return lax.dot_general(a.astype(BF16), b.astype(BF16), (((1,), (1,)), ((), ())),
                           preferred_element_type=F32)


def _mm_tn(a, b):
    return lax.dot_general(a.astype(BF16), b.astype(BF16), (((0,), (0,)), ((), ())),
                           preferred_element_type=F32)


def _split3(x):
    hi = x.astype(BF16)
    r = x - hi.astype(F32)
    mid = r.astype(BF16)
    lo = (r - mid.astype(F32)).astype(BF16)
    return hi, mid, lo


def _sigmoid(x):
    return 1.0 / (1.0 + jnp.exp(-x))


def _silu(x):
    return x * _sigmoid(x)


def _softplus(x):
    return jnp.maximum(x, 0.0) + jnp.log1p(jnp.exp(-jnp.abs(x)))


def _rope(x, cs, sn):
    w = x.shape[-1]
    n = w // LANES
    if n > 1:
        cs = jnp.concatenate([cs] * n, axis=1)
        sn = jnp.concatenate([sn] * n, axis=1)
    lane = lax.broadcasted_iota(I32, x.shape, 1) & (LANES - 1)
    half = MLA_D_ROPE // 2
    swapped = jnp.where(lane < half, pltpu.roll(x, w - half, 1), pltpu.roll(x, half, 1))
    return x * cs + swapped * sn


def _params(*sem):
    return pltpu.CompilerParams(dimension_semantics=sem, vmem_limit_bytes=VMEM_LIMIT)


def _front_kernel(x_ref, w_ref, convw_ref, hist_ref, gpar_ref, kvnw_ref, wukv_ref, cs_ref, sn_ref,
                  qkv_ref, z_ref, gb_ref, q_ref, k_ref, v_ref, lat_ref, kr_ref, convnew_ref,
                  xp_scr, *, tt, c):
    t = pl.program_id(1)
    hrow = 8 - (GDN_CONV - 1)

    @pl.when(t == 0)
    def _():
        xp_scr[hrow:8, :] = hist_ref[0]

    proj = _mm(x_ref[0], w_ref[...])

    raw = proj[:, PK_QKV:PK_Z]
    xp_scr[8:8 + tt, :] = raw
    cw = convw_ref[...]
    y = raw * cw[GDN_CONV - 1:GDN_CONV]
    for i in range(GDN_CONV - 1):
        y = y + xp_scr[hrow + i:hrow + i + tt, :] * cw[i:i + 1]
    tail = xp_scr[tt + hrow:tt + 8, :]
    convnew_ref[0] = tail
    xp_scr[hrow:8, :] = tail
    qkv = _silu(y)
    for h in range(2 * GDN_HEADS):
        xh = qkv[:, h * GDN_DK:(h + 1) * GDN_DK]
        xh = xh * lax.rsqrt(jnp.sum(xh * xh, axis=-1, keepdims=True) + L2_EPS)
        if h < GDN_HEADS:
            xh = xh * GDN_DK ** -0.5
        qkv_ref[0, :, h * GDN_DK:(h + 1) * GDN_DK] = xh
    qkv_ref[0, :, 2 * GDN_QK_W:] = qkv[:, 2 * GDN_QK_W:]
    z_ref[0] = proj[:, PK_Z:PK_QNOPE]

    ab = proj[:, PK_AB:PK_W]
    gpar = gpar_ref[...]
    gc = -jnp.exp(gpar[0:1]) * _softplus(ab + gpar[1:2])
    pos = lax.broadcasted_iota(I32, ab.shape, 0) & (c - 1)
    step = 1
    while step < c:
        gc = gc + jnp.where(pos >= step, pltpu.roll(gc, step, 0), 0.0)
        step *= 2
    beta = _sigmoid(ab)
    lane = lax.broadcasted_iota(I32, ab.shape, 1)
    gb_ref[0] = jnp.where(lane < GDN_HEADS, gc, jnp.where(lane < 2 * GDN_HEADS, beta, 0.0))

    cs = cs_ref[...]
    sn = sn_ref[...]
    q_nope = proj[:, PK_QNOPE:PK_QROPE]
    q_rope = _rope(proj[:, PK_QROPE:PK_CKV], cs, sn)
    c_raw = proj[:, PK_CKV:PK_KROPE]
    latent = c_raw * lax.rsqrt(jnp.mean(c_raw * c_raw, axis=-1, keepdims=True) + RMS_EPS) * kvnw_ref[...]
    lat_ref[0] = latent
    k_rope = _rope(proj[:, PK_KROPE:PK_AB], cs, sn)
    kr_ref[0] = k_rope[:, :MLA_D_ROPE]
    kv = _mm(latent, wukv_ref[...])
    k_rope_b = k_rope.astype(BF16)
    for h in range(MLA_HEADS):
        q_ref[0, h, :, :LANES] = (q_nope[:, h * LANES:(h + 1) * LANES] * QK_PRESCALE).astype(BF16)
        q_ref[0, h, :, LANES:] = (q_rope[:, h * LANES:(h + 1) * LANES] * QK_PRESCALE).astype(BF16)
        k_ref[0, h, :, :LANES] = kv[:, h * LANES:(h + 1) * LANES].astype(BF16)
        k_ref[0, h, :, LANES:] = k_rope_b
        v_ref[0, h] = kv[:, (MLA_HEADS + h) * LANES:(MLA_HEADS + h + 1) * LANES].astype(BF16)


def _front(x, w_pack, conv_w, hist, gpar, kvnw, wukv, cs_tab, sn_tab, tt, c):
    b, t, _ = x.shape
    nt = t // tt
    const2 = lambda bi, ti: (0, 0)
    out_shape = (
        jax.ShapeDtypeStruct((b, t, GDN_CONV_CH), F32),
        jax.ShapeDtypeStruct((b, t, GDN_V_W), F32),
        jax.ShapeDtypeStruct((b, t, LANES), F32),
        jax.ShapeDtypeStruct((b, MLA_HEADS, t, MLA_QK_W), BF16),
        jax.ShapeDtypeStruct((b, MLA_HEADS, t, MLA_QK_W), BF16),
        jax.ShapeDtypeStruct((b, MLA_HEADS, t, MLA_D_V), BF16),
        jax.ShapeDtypeStruct((b, t, MLA_KV_RANK), F32),
        jax.ShapeDtypeStruct((b, t, MLA_D_ROPE), F32),
        jax.ShapeDtypeStruct((b, GDN_CONV - 1, GDN_CONV_CH), F32),
    )
    row3 = lambda w: pl.BlockSpec((1, tt, w), lambda bi, ti: (bi, ti, 0))
    head4 = lambda w: pl.BlockSpec((1, MLA_HEADS, tt, w), lambda bi, ti: (bi, 0, ti, 0))
    return pl.pallas_call(
        functools.partial(_front_kernel, tt=tt, c=c),
        grid=(b, nt),
        in_specs=[
            row3(D_MODEL),
            pl.BlockSpec((D_MODEL, PK_W), const2),
            pl.BlockSpec((GDN_CONV, GDN_CONV_CH), const2),
            pl.BlockSpec((1, GDN_CONV - 1, GDN_CONV_CH), lambda bi, ti: (bi, 0, 0)),
            pl.BlockSpec((2, LANES), const2),
            pl.BlockSpec((1, MLA_KV_RANK), const2),
            pl.BlockSpec((MLA_KV_RANK, 2 * MLA_HEADS * LANES), const2),
            pl.BlockSpec((tt, LANES), lambda bi, ti: (ti, 0)),
            pl.BlockSpec((tt, LANES), lambda bi, ti: (ti, 0)),
        ],
        out_specs=(
            row3(GDN_CONV_CH), row3(GDN_V_W), row3(LANES),
            head4(MLA_QK_W), head4(MLA_QK_W), head4(MLA_D_V),
            row3(MLA_KV_RANK), row3(MLA_D_ROPE),
            pl.BlockSpec((1, GDN_CONV - 1, GDN_CONV_CH), lambda bi, ti: (bi, 0, 0)),
        ),
        out_shape=out_shape,
        scratch_shapes=[pltpu.VMEM((tt + 8, GDN_CONV_CH), F32)],
        compiler_params=_params("arbitrary", "arbitrary"),
        name="front",
    )(x, w_pack, conv_w, hist, gpar, kvnw, wukv, cs_tab, sn_tab)


def _gdn_kernel(qkv_ref, z_ref, gb_ref, s0_ref, nw_ref, og_ref, sout_ref, s_scr, *, tg, c, bb):
    t = pl.program_id(1)
    nh = GDN_HEADS
    r = nh * c
    sh = c.bit_length() - 1

    @pl.when(t == 0)
    def _():
        s_scr[...] = s0_ref[...]

    row = lax.broadcasted_iota(I32, (r, r), 0)
    col = lax.broadcasted_iota(I32, (r, r), 1)
    same = (row >> sh) == (col >> sh)
    incl = same & (row >= col)
    strict = same & (row > col)
    eye = jnp.where(row == col, 1.0, 0.0)
    lane0 = jnp.where(lax.broadcasted_iota(I32, (r, LANES), 1) == 0, 1.0, 0.0).astype(BF16)
    nw = nw_ref[...]
    items = [(bi, ci) for ci in range(tg // c) for bi in range(bb)]
    chunks = range(len(items))

    def stacked(ref, it, base):
        bi, ci = items[it]
        return jnp.concatenate(
            [ref[bi, ci * c:(ci + 1) * c, base + h * LANES:base + (h + 1) * LANES] for h in range(nh)], axis=0)

    def col_bcast(it, lane):
        bi, ci = items[it]
        gbc = gb_ref[bi, ci * c:(ci + 1) * c, :]
        return jnp.concatenate(
            [jnp.broadcast_to(gbc[:, lane + h:lane + h + 1], (c, LANES)) for h in range(nh)], axis=0)

    def as_col(gc_b):
        return _lane_tile(gc_b, r) if r % LANES == 0 else gc_b[:, :r]

    def as_row(gc_b):
        if r % LANES == 0:
            return as_col(gc_b).T
        return sum(lax.dot_general(lane0, p, (((1,), (1,)), ((), ())), preferred_element_type=F32)
                   for p in _split3(gc_b))

    ks = [stacked(qkv_ref, ci, GDN_QK_W) for ci in chunks]
    gc = [col_bcast(ci, 0) for ci in chunks]
    beta = [col_bcast(ci, nh) for ci in chunks]
    decay, qk_kk = [], []
    for ci in chunks:
        decay.append(jnp.exp(jnp.where(incl, as_col(gc[ci]) - as_row(gc[ci]), -jnp.inf)))
        qk_kk.append(_mm_nt(jnp.concatenate([stacked(qkv_ref, ci, 0), ks[ci]], axis=0), ks[ci]))
    intra = [qk_kk[ci][:r] * decay[ci] for ci in chunks]
    n_pow = [jnp.where(strict, -as_col(beta[ci]) * qk_kk[ci][r:] * decay[ci], 0.0)
             for ci in chunks]
    t_inv = [eye + n_pow[ci] for ci in chunks]
    for _ in range(sh - 1):
        n_pow = [_mm(n_pow[ci], n_pow[ci]) for ci in chunks]
        t_inv = [t_inv[ci] + _mm(t_inv[ci], n_pow[ci]) for ci in chunks]
    egc = [jnp.exp(gc[ci]) for ci in chunks]
    uw = [_mm(t_inv[ci], jnp.concatenate([stacked(qkv_ref, ci, 2 * GDN_QK_W) * beta[ci],
                                          ks[ci] * beta[ci] * egc[ci]], axis=1)) for ci in chunks]

    for ci in chunks:
        bi, cpos = items[ci]
        u = uw[ci][:, :GDN_DV]
        w = uw[ci][:, GDN_DV:]
        qd = stacked(qkv_ref, ci, 0) * egc[ci]
        vn, qs_s = [], []
        for h in range(nh):
            hs = slice(h * c, (h + 1) * c)
            s_h = s_scr[bi, h]
            wq = _mm(jnp.concatenate([w[hs], qd[hs]], axis=0), s_h)
            vn_h = u[hs] - wq[:c]
            g_last = gc[ci][h * c + c - 1:h * c + c, :]
            kd = ks[ci][hs] * jnp.exp(g_last - gc[ci][hs])
            s_scr[bi, h] = s_h * jnp.exp(g_last) + _mm_tn(kd, vn_h)
            vn.append(vn_h)
            qs_s.append(wq[c:])
        o = jnp.concatenate(qs_s, axis=0) + _mm(intra[ci], jnp.concatenate(vn, axis=0))

        o = o * lax.rsqrt(jnp.mean(o * o, axis=-1, keepdims=True) + RMS_EPS) * nw
        o = o * _silu(stacked(z_ref, ci, 0))
        for h in range(nh):
            og_ref[bi, cpos * c:(cpos + 1) * c, h * LANES:(h + 1) * LANES] = o[h * c:(h + 1) * c].astype(BF16)

    @pl.when(t == pl.num_programs(1) - 1)
    def _():
        sout_ref[...] = s_scr[...]


def _gdn(qkv, z, gb, s0, nw, tg, c):
    b, t, _ = qkv.shape
    bb = 2 if b % 2 == 0 else 1
    row3 = lambda w: pl.BlockSpec((bb, tg, w), lambda bi, ti: (bi, ti, 0))
    st = pl.BlockSpec((bb, GDN_HEADS, GDN_DK, GDN_DV), lambda bi, ti: (bi, 0, 0, 0))
    return pl.pallas_call(
        functools.partial(_gdn_kernel, tg=tg, c=c, bb=bb),
        grid=(b // bb, t // tg),
        in_specs=[row3(GDN_CONV_CH), row3(GDN_V_W), row3(LANES), st,
                  pl.BlockSpec((1, GDN_DV), lambda bi, ti: (0, 0))],
        out_specs=(row3(GDN_V_W), st),
        out_shape=(jax.ShapeDtypeStruct((b, t, GDN_V_W), BF16),
                   jax.ShapeDtypeStruct((b, GDN_HEADS, GDN_DK, GDN_DV), F32)),
        scratch_shapes=[pltpu.VMEM((bb, GDN_HEADS, GDN_DK, GDN_DV), F32)],
        compiler_params=_params("arbitrary", "arbitrary"),
        name="gdn",
    )(qkv, z, gb, s0, nw)


ATTN_ROW_BLOCK = 32


def _lane_tile(x, width):
    return x if width == LANES else jnp.concatenate([x] * (width // LANES), axis=1)


def _attn_kernel(qi_ref, ki_ref, q_ref, k_ref, v_ref, o_ref, m_scr, l_scr, acc_scr, a_scr, s_scr, p_scr,
                 *, tb, sub):
    step = pl.program_id(2)
    qi = qi_ref[step]
    ki = ki_ref[step]
    csh = CHUNK.bit_length() - 1

    @pl.when(ki == 0)
    def _():
        m_scr[...] = jnp.full(m_scr.shape, -jnp.inf, F32)
        l_scr[...] = jnp.zeros(l_scr.shape, F32)
        acc_scr[...] = jnp.zeros(acc_scr.shape, F32)

    half = tb // 2
    rb = min(ATTN_ROW_BLOCK, half)

    def scores(h, j):
        rows = slice(h * half, (h + 1) * half)
        s_scr[rows, :] = lax.dot_general(q_ref[0, 0, rows, :], k_ref[0, 0, j * sub:(j + 1) * sub, :],
                                         (((1,), (1,)), ((), ())), preferred_element_type=F32)

    def softmax(h, j, masked):
        r0 = h * half

        def block(i):
            rr = pl.ds(pl.multiple_of(r0 + i * rb, rb), rb)
            s = s_scr[rr, :]
            if masked:
                qc = (r0 + i * rb + lax.broadcasted_iota(I32, s.shape, 0)) >> csh
                kc = (j * sub + lax.broadcasted_iota(I32, s.shape, 1)) >> csh
                s = jnp.where(kc <= qc, s, -jnp.inf)
            return rr, s

        def row_max(i, carry):
            rr, s = block(i)
            m_prev = m_scr[rr, :]
            m_new = jnp.maximum(m_prev, jnp.max(s, axis=-1, keepdims=True))
            a_scr[rr, :] = jnp.exp2(m_prev - m_new)
            m_scr[rr, :] = m_new
            return carry

        def row_exp(i, carry):
            rr, s = block(i)
            p = jnp.exp2(s - _lane_tile(m_scr[rr, :], sub))
            l_scr[rr, :] = a_scr[rr, :] * l_scr[rr, :] + jnp.sum(p, axis=-1, keepdims=True)
            p_scr[rr, :] = p.astype(BF16)
            return carry

        for i in range(half // rb):
            row_max(i, 0)
        for i in range(half // rb):
            row_exp(i, 0)

    def weighted_values(h, j):
        rows = slice(h * half, (h + 1) * half)
        acc_scr[rows, :] = a_scr[rows, :] * acc_scr[rows, :] + jnp.dot(
            p_scr[rows, :], v_ref[0, 0, j * sub:(j + 1) * sub, :], preferred_element_type=F32)

    def run(tasks):
        for h in range(2):
            if tasks[h]:
                scores(h, tasks[h][0][0])
        for n in range(max(len(t) for t in tasks)):
            for h in range(2):
                if n < len(tasks[h]):
                    j, masked = tasks[h][n]
                    softmax(h, j, masked)
                    weighted_values(h, j)
                    if n + 1 < len(tasks[h]):
                        scores(h, tasks[h][n + 1][0])

    def diagonal_tasks(h):
        r0 = h * half
        out = []
        for j in range(tb // sub):
            k0, k1 = j * sub, (j + 1) * sub
            if k0 >= r0 + half:
                continue
            out.append((j, k1 > r0 + CHUNK))
        return out

    @pl.when(ki < qi)
    def _():
        run([[(j, False) for j in range(tb // sub)]] * 2)

    @pl.when(ki == qi)
    def _():
        run([diagonal_tasks(0), diagonal_tasks(1)])
        o_ref[0] = (acc_scr[...] / l_scr[...]).astype(BF16)


def _attn_prompt(q, k, v, tb, sub):
    b, nh, t, _ = q.shape
    nt = t // tb
    pairs = [(qi, ki) for qi in range(nt) for ki in range(qi + 1)]
    qi_of = jnp.asarray([p[0] for p in pairs], I32)
    ki_of = jnp.asarray([p[1] for p in pairs], I32)
    return pl.pallas_call(
        functools.partial(_attn_kernel, tb=tb, sub=sub),
        grid_spec=pltpu.PrefetchScalarGridSpec(
            num_scalar_prefetch=2,
            grid=(b, nh, len(pairs)),
            in_specs=[pl.BlockSpec((1, 1, tb, MLA_QK_W), lambda bi, hi, s, qo, ko: (bi, hi, qo[s], 0)),
                      pl.BlockSpec((1, 1, tb, MLA_QK_W), lambda bi, hi, s, qo, ko: (bi, hi, ko[s], 0)),
                      pl.BlockSpec((1, 1, tb, MLA_D_V), lambda bi, hi, s, qo, ko: (bi, hi, ko[s], 0))],
            out_specs=pl.BlockSpec((1, tb, MLA_D_V), lambda bi, hi, s, qo, ko: (bi, qo[s], hi)),
            scratch_shapes=[pltpu.VMEM((tb, LANES), F32), pltpu.VMEM((tb, LANES), F32),
                            pltpu.VMEM((tb, MLA_D_V), F32), pltpu.VMEM((tb, LANES), F32),
                            pltpu.VMEM((tb, sub), F32), pltpu.VMEM((tb, sub), BF16)]),
        out_shape=jax.ShapeDtypeStruct((b, t, MLA_HEADS * MLA_D_V), BF16),
        compiler_params=_params("arbitrary", "arbitrary", "arbitrary"),
        name="attn_prompt",
    )(qi_of, ki_of, q, k, v)


def _attn_sample_kernel(q_ref, kn_ref, vn_ref, plat_ref, pkr_ref, wukv_ref, o_ref):
    plat = plat_ref[0].astype(BF16)
    pkr = pkr_ref[0].astype(BF16)
    for h in range(MLA_HEADS):
        q = q_ref[0, h]
        w_uk = wukv_ref[:, h * LANES:(h + 1) * LANES]
        w_uv = wukv_ref[:, (MLA_HEADS + h) * LANES:(MLA_HEADS + h + 1) * LANES]
        q_lat = _mm_nt(q[:, :MLA_D_NOPE], w_uk)
        s_past = _mm_nt(q_lat, plat) + _mm_nt(q[:, MLA_D_NOPE:MLA_D_NOPE + MLA_D_ROPE], pkr)
        s_new = _mm_nt(q, kn_ref[0, h])
        m = jnp.maximum(jnp.max(s_past, axis=-1, keepdims=True), jnp.max(s_new, axis=-1, keepdims=True))
        p_past = jnp.exp2(s_past - m)
        p_new = jnp.exp2(s_new - m)
        l = jnp.sum(p_past, axis=-1, keepdims=True) + jnp.sum(p_new, axis=-1, keepdims=True)
        o = _mm(_mm(p_past, plat), w_uv) + _mm(p_new, vn_ref[0, h])
        o_ref[0, :, h * MLA_D_V:(h + 1) * MLA_D_V] = (o / l).astype(BF16)


def _attn_sample(q, k_new, v_new, past_lat, past_kr, wukv):
    b, nh, ts, _ = q.shape
    past = past_lat.shape[1]
    b4 = lambda w: pl.BlockSpec((1, nh, ts, w), lambda bi: (bi, 0, 0, 0))
    return pl.pallas_call(
        _attn_sample_kernel,
        grid=(b,),
        in_specs=[b4(MLA_QK_W), b4(MLA_QK_W), b4(MLA_D_V),
                  pl.BlockSpec((1, past, MLA_KV_RANK), lambda bi: (bi, 0, 0)),
                  pl.BlockSpec((1, past, MLA_D_ROPE), lambda bi: (bi, 0, 0)),
                  pl.BlockSpec((MLA_KV_RANK, 2 * MLA_HEADS * LANES), lambda bi: (0, 0))],
        out_specs=pl.BlockSpec((1, ts, MLA_HEADS * MLA_D_V), lambda bi: (bi, 0, 0)),
        out_shape=jax.ShapeDtypeStruct((b, ts, MLA_HEADS * MLA_D_V), BF16),
        compiler_params=_params("arbitrary"),
        name="attn_sample",
    )(q, k_new, v_new, past_lat, past_kr, wukv)


def _layernorm(y, g, b):
    mu = jnp.mean(y, axis=-1, keepdims=True)
    d = y - mu
    var = jnp.mean(d * d, axis=-1, keepdims=True)
    return d * lax.rsqrt(var + LN_EPS) * g + b


N_SHARED = 4


def _mixln_kernel(og_ref, om_ref, x_ref, w_ref, g_ref, b_ref, rw_ref, rb_ref, cnt0_ref, *rest, tm):
    o_ref, idx_ref, gate_ref, rank_ref, cnt_ref, carry_scr = rest[-6:]

    @pl.when(pl.program_id(0) == 0)
    def _():
        carry_scr[...] = cnt0_ref[...]

    mix = (jnp.dot(og_ref[...], w_ref[:GDN_V_W, :], preferred_element_type=F32)
           + jnp.dot(om_ref[...], w_ref[GDN_V_W:, :], preferred_element_type=F32))
    x1 = _layernorm(DEEPNORM_ALPHA * x_ref[...] + mix, g_ref[...], b_ref[...])
    o_ref[...] = x1
    _route_tile(x1, rw_ref, rb_ref, carry_scr, idx_ref, gate_ref, rank_ref, tm)
    cnt_ref[...] = carry_scr[...]


def _mixln(og, om, x, w_out, g, b, rw, rb, cnt0, shared, n_all, row0, tm):
    n = x.shape[0]
    blk0 = row0 // tm
    rows = lambda w: pl.BlockSpec((tm, w), lambda i: (i, 0))
    const = lambda s: pl.BlockSpec(s, lambda i: (0, 0))
    kspec = pl.BlockSpec((TOP_K, tm), lambda i: (0, blk0 + i))
    first = shared is None
    n_in = 9
    kt = lambda dt: jax.ShapeDtypeStruct((TOP_K, n_all), dt)
    return pl.pallas_call(
        functools.partial(_mixln_kernel, tm=tm),
        grid=(n // tm,),
        in_specs=[rows(GDN_V_W), rows(MLA_HEADS * MLA_D_V), rows(D_MODEL),
                  const((D_MODEL, D_MODEL)), const((1, D_MODEL)), const((1, D_MODEL)),
                  const((D_MODEL, N_EXPERTS)), const((N_EXPERTS, 1)), const((N_EXPERTS, LANES))]
                 + ([] if first else [pl.BlockSpec(memory_space=pl.ANY)] * N_SHARED),
        out_specs=(pl.BlockSpec((tm, D_MODEL), lambda i: (blk0 + i, 0)), kspec, kspec, kspec,
                   const((N_EXPERTS, LANES))),
        out_shape=(jax.ShapeDtypeStruct((n_all, D_MODEL), F32), kt(I32), kt(F32), kt(I32),
                   jax.ShapeDtypeStruct((N_EXPERTS, LANES), F32)),
        scratch_shapes=[pltpu.VMEM((N_EXPERTS, LANES), F32)],
        input_output_aliases={} if first else {n_in + j: j for j in range(N_SHARED)},
        compiler_params=_params("arbitrary"),
        name="mixln",
    )(*((og, om, x, w_out, g, b, rw, rb, cnt0) + (() if first else tuple(shared))))


def _route_tile(x, rw_ref, rb_ref, carry_scr, idx_ref, gate_ref, rank_ref, tt):
    ninf = -jnp.inf
    big = float(2 * N_EXPERTS)
    scores = _sigmoid(_mm(x, rw_ref[...]).T)
    biased = scores + rb_ref[...]
    eio = lax.broadcasted_iota(I32, (N_EXPERTS, tt), 0).astype(F32)

    def first_argmax(vals, io):
        m = jnp.max(vals, axis=0, keepdims=True)
        i = jnp.min(jnp.where(vals == m, io, big), axis=0, keepdims=True)
        return m, i

    gs = []
    for g in range(N_GROUPS):
        blk = biased[g * GROUP_SIZE:(g + 1) * GROUP_SIZE]
        io = (lax.broadcasted_iota(I32, (GROUP_SIZE, tt), 0) + g * GROUP_SIZE).astype(F32)
        m1, i1 = first_argmax(blk, io)
        m2 = jnp.max(jnp.where(io == i1, ninf, blk), axis=0, keepdims=True)
        gs.append(m1 + m2)
    gio = lax.broadcasted_iota(I32, (N_GROUPS, tt), 0).astype(F32)
    gsc = jnp.zeros((N_GROUPS, tt), F32)
    for g in range(N_GROUPS):
        gsc = jnp.where(gio == float(g), gs[g], gsc)
    gsel = jnp.zeros((N_GROUPS, tt), F32)
    for _ in range(TOPK_GROUPS):
        _, gi = first_argmax(gsc, gio)
        hit = gio == gi
        gsel = jnp.where(hit, 1.0, gsel)
        gsc = jnp.where(hit, ninf, gsc)
    masked = jnp.concatenate(
        [jnp.where(jnp.max(jnp.where(gio == float(g), gsel, 0.0), axis=0, keepdims=True) > 0.0,
                   biased[g * GROUP_SIZE:(g + 1) * GROUP_SIZE], ninf) for g in range(N_GROUPS)], axis=0)

    idx, wts = [], []
    sel = jnp.zeros((N_EXPERTS, tt), F32)
    for _ in range(TOP_K):
        _, ei = first_argmax(masked, eio)
        hit = eio == ei
        wts.append(jnp.sum(jnp.where(hit, scores, 0.0), axis=0, keepdims=True))
        masked = jnp.where(hit, ninf, masked)
        sel = jnp.where(hit, 1.0, sel)
        idx.append(ei)
    wsum = wts[0]
    for w in wts[1:]:
        wsum = wsum + w

    t0 = lax.broadcasted_iota(I32, (tt, tt), 0)
    t1 = lax.broadcasted_iota(I32, (tt, tt), 1)
    before = jnp.where(t0 < t1, 1.0, 0.0).astype(BF16)
    sel_b = sel.astype(BF16)
    base = carry_scr[:, :1] + jnp.dot(sel_b, before, preferred_element_type=F32)
    ranks = [jnp.sum(jnp.where(eio == ei, base, 0.0), axis=0, keepdims=True) for ei in idx]
    carry_scr[...] = carry_scr[...] + jnp.dot(sel_b, jnp.ones((tt, LANES), BF16), preferred_element_type=F32)

    for k in range(TOP_K):
        idx_ref[k:k + 1, :] = idx[k].astype(I32)
        gate_ref[k:k + 1, :] = wts[k] / wsum * ROUTED_SCALE
        rank_ref[k:k + 1, :] = ranks[k].astype(I32)


def _dest_kernel(idx_ref, rank_ref, pstart_ref, dest_ref, *, tt, per_step):
    eio = lax.broadcasted_iota(I32, (N_EXPERTS, tt), 0)
    pstart = pstart_ref[...]
    for j in range(per_step):
        cols = slice(j * tt, (j + 1) * tt)
        for k in range(TOP_K):
            start = jnp.sum(jnp.where(eio == idx_ref[k:k + 1, cols], pstart, 0.0), axis=0, keepdims=True)
            dest_ref[j, k:k + 1, :] = start.astype(I32) + rank_ref[k:k + 1, cols]


def _dest(idx, rank, pstart, tt):
    n = idx.shape[1]
    per_step = next(s for s in (4, 3, 2, 1) if (n // tt) % s == 0)
    kspec = pl.BlockSpec((TOP_K, per_step * tt), lambda i: (0, i))
    return pl.pallas_call(
        functools.partial(_dest_kernel, tt=tt, per_step=per_step),
        grid=(n // (tt * per_step),),
        in_specs=[kspec, kspec, pl.BlockSpec((N_EXPERTS, 1), lambda i: (0, 0))],
        out_specs=pl.BlockSpec((per_step, TOP_K, tt), lambda i: (i, 0, 0)),
        out_shape=jax.ShapeDtypeStruct((n // tt, TOP_K, tt), I32),
        compiler_params=_params("arbitrary"),
        name="dest",
    )(idx, rank, pstart)


ROW_TILE = D_MODEL // LANES
XROW_TILE = ROW_TILE // 2


def _rows_to_tiles(x, tiles_ref, base, n, rt=ROW_TILE):
    for j in range(rt):
        tiles_ref[pl.ds(base * rt + j, n, stride=rt), :] = x[:, j * LANES:(j + 1) * LANES]


def _tiles_to_rows(tiles_ref, base, n, rt=ROW_TILE):
    return jnp.concatenate(
        [tiles_ref[pl.ds(base * rt + j, n, stride=rt), :] for j in range(rt)], axis=1)


def _pack_bf16_pairs(x):
    half = x.shape[1] // 2
    lo = pltpu.bitcast(x[:, :half].astype(BF16).astype(F32), jnp.uint32) >> 16
    hi = pltpu.bitcast(x[:, half:].astype(BF16).astype(F32), jnp.uint32) & jnp.uint32(0xFFFF0000)
    return lo | hi


def _unpack_bf16_pairs(w):
    lo = pltpu.bitcast(w << 16, F32)
    hi = pltpu.bitcast(w & jnp.uint32(0xFFFF0000), F32)
    return jnp.concatenate([lo, hi], axis=1)


def _dispatch_kernel(tail_ref, dest_ref, x_ref, xs_out, zbuf, xt_scr, sem, zsem, *, td):
    s = pl.program_id(0)

    @pl.when(s == 0)
    def _():
        zbuf[...] = jnp.zeros(zbuf.shape, zbuf.dtype)

        def zero_copy(e):
            first = pl.multiple_of(tail_ref[e] * XROW_TILE, MOE_BM * XROW_TILE)
            return pltpu.make_async_copy(zbuf, xs_out.at[pl.ds(first, MOE_BM * XROW_TILE)], zsem)

        def zstart(e, carry):
            zero_copy(e).start()
            return carry

        def zwait(e, carry):
            zero_copy(e).wait()
            return carry

        lax.fori_loop(0, N_EXPERTS, zstart, 0)
        lax.fori_loop(0, N_EXPERTS, zwait, 0)

    slot = s & 1

    def row_copy(slot_, i, d):
        src = pl.multiple_of((slot_ * td + i) * XROW_TILE, XROW_TILE)
        return pltpu.make_async_copy(xt_scr.at[pl.ds(src, XROW_TILE)],
                                     xs_out.at[pl.ds(pl.multiple_of(d * XROW_TILE, XROW_TILE), XROW_TILE)],
                                     sem.at[slot_])

    def drain(slot_):
        step_rows = TOP_K * td * XROW_TILE
        pltpu.make_async_copy(xs_out.at[pl.ds(0, step_rows)], xs_out.at[pl.ds(0, step_rows)],
                              sem.at[slot_]).wait()

    @pl.when(s >= 2)
    def _():
        drain(slot)

    _rows_to_tiles(_pack_bf16_pairs(x_ref[...]), xt_scr, slot * td, td, XROW_TILE)

    def issue(i, carry):
        for k in range(TOP_K):
            row_copy(slot, i, dest_ref[0, 0, k * td + i]).start(priority=k % 2)
        return carry

    lax.fori_loop(0, td, issue, 0, unroll=4)

    @pl.when(s == pl.num_programs(0) - 1)
    def _():
        drain(slot)

        @pl.when(s >= 1)
        def _():
            drain(1 - slot)


def _dispatch(tail, dest, x1, n_rows, td):
    n = x1.shape[0]
    return pl.pallas_call(
        functools.partial(_dispatch_kernel, td=td),
        grid_spec=pltpu.PrefetchScalarGridSpec(
            num_scalar_prefetch=1,
            grid=(n // td,),
            in_specs=[pl.BlockSpec((1, 1, TOP_K * td), lambda i, tl: (i, 0, 0), memory_space=pltpu.SMEM),
                      pl.BlockSpec((td, D_MODEL), lambda i, tl: (i, 0))],
            out_specs=pl.BlockSpec(memory_space=pl.ANY),
            scratch_shapes=[pltpu.VMEM((MOE_BM * XROW_TILE, LANES), jnp.uint32),
                            pltpu.VMEM((2 * td * XROW_TILE, LANES), jnp.uint32),
                            pltpu.SemaphoreType.DMA((2,)), pltpu.SemaphoreType.DMA(())]),
        out_shape=jax.ShapeDtypeStruct((n_rows * XROW_TILE, LANES), jnp.uint32),
        compiler_params=_params("arbitrary"),
        name="dispatch",
    )(tail, dest, x1)


def _expert_kernel(first_ref, nblk_ref, nact_ref, xs_hbm, wg_ref, wu_ref, wd_ref, y_hbm,
                   xbuf, ybuf, wgb, wub, wdb, xsem, ysem):
    e = pl.program_id(0)
    nact = nact_ref[0]
    ring = EXPERT_RING
    ahead = ring - 2
    xr = MOE_BM * XROW_TILE
    yr = MOE_BM * XROW_TILE

    def x_copy(g):
        slot = g & (ring - 1)
        return pltpu.make_async_copy(xs_hbm.at[pl.ds(pl.multiple_of(g * xr, xr), xr)],
                                     xbuf.at[pl.ds(pl.multiple_of(slot * xr, xr), xr)], xsem.at[slot])

    def y_copy(g):
        slot = g & (ring - 1)
        return pltpu.make_async_copy(ybuf.at[pl.ds(pl.multiple_of(slot * yr, yr), yr)],
                                     y_hbm.at[pl.ds(pl.multiple_of(g * yr, yr), yr)], ysem.at[slot])

    @pl.when(e == 0)
    def _():
        for g0 in range(ahead):
            @pl.when(g0 < nact)
            def _():
                x_copy(g0).start()

    @pl.when(nblk_ref[e] > 0)
    def _():
        wgb[...] = wg_ref[0].astype(BF16)
        wub[...] = wu_ref[0].astype(BF16)
        wdb[...] = wd_ref[0].astype(BF16)

    def blocks(g0, nb):
        gs = [g0 + j for j in range(nb)]
        for g in gs:
            x_copy(g).wait()

            @pl.when(g + ahead < nact)
            def _():
                x_copy(g + ahead).start()

            @pl.when(g >= ring)
            def _():
                y_copy(g - ring).wait()

        xb = jnp.concatenate(
            [_unpack_bf16_pairs(_tiles_to_rows(xbuf, (g & (ring - 1)) * MOE_BM, MOE_BM, XROW_TILE)) for g in gs],
            axis=0).astype(BF16)
        hg = jnp.dot(xb, wgb[...], preferred_element_type=F32)
        hu = jnp.dot(xb, wub[...], preferred_element_type=F32)
        y = jnp.dot((_silu(hg) * hu).astype(BF16), wdb[...], preferred_element_type=F32)
        yp = _pack_bf16_pairs(y)
        for j, g in enumerate(gs):
            _rows_to_tiles(yp[j * MOE_BM:(j + 1) * MOE_BM], ybuf, (g & (ring - 1)) * MOE_BM, MOE_BM, XROW_TILE)
            y_copy(g).start()

        g_last = gs[-1]

        @pl.when(g_last == nact - 1)
        def _():
            for back in range(ring):
                @pl.when(g_last >= back)
                def _():
                    y_copy(g_last - back).wait()

    nblk = nblk_ref[e]

    def pair(b, carry):
        blocks(first_ref[e] + 2 * b, 2)
        return carry

    lax.fori_loop(0, nblk // 2, pair, 0)

    @pl.when(nblk % 2 == 1)
    def _():
        blocks(first_ref[e] + nblk - 1, 1)


def _experts(first, nblk, nact, xs, wg, wu, wd):
    n_rows = xs.shape[0] // XROW_TILE
    wspec = lambda s: pl.BlockSpec((1,) + s, lambda e, fr, nb, na: (e, 0, 0))
    return pl.pallas_call(
        _expert_kernel,
        grid_spec=pltpu.PrefetchScalarGridSpec(
            num_scalar_prefetch=3,
            grid=(N_EXPERTS,),
            in_specs=[pl.BlockSpec(memory_space=pl.ANY),
                      wspec((D_MODEL, EXP_HIDDEN)), wspec((D_MODEL, EXP_HIDDEN)), wspec((EXP_HIDDEN, D_MODEL))],
            out_specs=pl.BlockSpec(memory_space=pl.ANY),
            scratch_shapes=[pltpu.VMEM((EXPERT_RING * MOE_BM * XROW_TILE, LANES), jnp.uint32),
                            pltpu.VMEM((EXPERT_RING * MOE_BM * XROW_TILE, LANES), jnp.uint32),
                            pltpu.VMEM((D_MODEL, EXP_HIDDEN), BF16), pltpu.VMEM((D_MODEL, EXP_HIDDEN), BF16),
                            pltpu.VMEM((EXP_HIDDEN, D_MODEL), BF16),
                            pltpu.SemaphoreType.DMA((EXPERT_RING,)), pltpu.SemaphoreType.DMA((EXPERT_RING,))]),
        out_shape=jax.ShapeDtypeStruct((n_rows * XROW_TILE, LANES), jnp.uint32),
        compiler_params=_params("arbitrary"),
        name="experts",
    )(first, nblk, nact, xs, wg, wu, wd)


def _combine_kernel(dcur_ref, dnxt_ref, y_hbm, gate_ref, x_ref, wsg_ref, wsu_ref, wsd_ref, g_ref, b_ref,
                    outp_ref, outs_ref, buf, routed_scr, sem, *, tc, np_tiles):
    s = pl.program_id(0)
    ns = pl.num_programs(0)
    slot = s % 2

    def row_copy(d, slot_, k, i):
        dst = pl.multiple_of(((slot_ * TOP_K + k) * tc + i) * XROW_TILE, XROW_TILE)
        return pltpu.make_async_copy(y_hbm.at[pl.ds(pl.multiple_of(d * XROW_TILE, XROW_TILE), XROW_TILE)],
                                     buf.at[pl.ds(dst, XROW_TILE)], sem.at[slot_])

    def issue(dref, slot_):
        def body(i, carry):
            for k in range(TOP_K):
                row_copy(dref[0, 0, k * tc + i], slot_, k, i).start(priority=k % 2)
            return carry
        lax.fori_loop(0, tc, body, 0, unroll=4)

    @pl.when(s == 0)
    def _():
        issue(dcur_ref, 0)

    slot_rows = TOP_K * tc * XROW_TILE
    pltpu.make_async_copy(y_hbm.at[pl.ds(0, slot_rows)],
                          buf.at[pl.ds(pl.multiple_of(slot * slot_rows, slot_rows), slot_rows)],
                          sem.at[slot]).wait()

    grp = 16

    def gated_sum(gi):
        base = pl.multiple_of(gi * grp, grp)
        gate = gate_ref[pl.ds(base, grp), :]
        acc = None
        for k in range(TOP_K):
            rows = _unpack_bf16_pairs(_tiles_to_rows(buf, (slot * TOP_K + k) * tc + base, grp, XROW_TILE))
            term = rows * gate[:, k:k + 1]
            acc = term if acc is None else acc + term
        routed_scr[pl.ds(base, grp), :] = acc

    def sum_and_issue(gi, carry):
        for i in range(grp):
            for k in range(TOP_K):
                row_copy(dnxt_ref[0, 0, k * tc + gi * grp + i], 1 - slot, k, gi * grp + i).start(priority=k % 2)
        gated_sum(gi)
        return carry

    def sum_only(gi, carry):
        gated_sum(gi)
        return carry

    @pl.when(s + 1 < ns)
    def _():
        lax.fori_loop(0, tc // grp, sum_and_issue, 0)

    @pl.when(s + 1 == ns)
    def _():
        lax.fori_loop(0, tc // grp, sum_only, 0)

    x = x_ref[...]
    routed = routed_scr[...]
    xb = x.astype(BF16)
    shared = _mm(_silu(_mm(xb, wsg_ref[...])) * _mm(xb, wsu_ref[...]), wsd_ref[...])
    out = _layernorm(DEEPNORM_ALPHA * x + (routed + shared), g_ref[...], b_ref[...])

    @pl.when(s < np_tiles)
    def _():
        outp_ref[...] = out

    @pl.when(s >= np_tiles)
    def _():
        outs_ref[...] = out


def _combine(dest, y_sorted, gate, x1, wsg, wsu, wsd, g, b, n_prompt, tc):
    n = x1.shape[0]
    ns = n // tc
    np_tiles = n_prompt // tc
    const = lambda s: pl.BlockSpec(s, lambda i: (0, 0))
    dspec = lambda f: pl.BlockSpec((1, 1, TOP_K * tc), f, memory_space=pltpu.SMEM)
    return pl.pallas_call(
        functools.partial(_combine_kernel, tc=tc, np_tiles=np_tiles),
        grid=(ns,),
        in_specs=[dspec(lambda i: (i, 0, 0)), dspec(lambda i: (jnp.minimum(i + 1, ns - 1), 0, 0)),
                  pl.BlockSpec(memory_space=pl.ANY),
                  pl.BlockSpec((tc, TOP_K), lambda i: (i, 0)),
                  pl.BlockSpec((tc, D_MODEL), lambda i: (i, 0)),
                  const((D_MODEL, EXP_HIDDEN)), const((D_MODEL, EXP_HIDDEN)), const((EXP_HIDDEN, D_MODEL)),
                  const((1, D_MODEL)), const((1, D_MODEL))],
        out_specs=(pl.BlockSpec((tc, D_MODEL), lambda i: (jnp.minimum(i, np_tiles - 1), 0)),
                   pl.BlockSpec((tc, D_MODEL), lambda i: (jnp.maximum(i - np_tiles, 0), 0))),
        out_shape=(jax.ShapeDtypeStruct((n_prompt, D_MODEL), F32),
                   jax.ShapeDtypeStruct((n - n_prompt, D_MODEL), F32)),
        scratch_shapes=[pltpu.VMEM((2 * TOP_K * tc * XROW_TILE, LANES), jnp.uint32),
                        pltpu.VMEM((tc, D_MODEL), F32),
                        pltpu.SemaphoreType.DMA((2,))],
        compiler_params=_params("arbitrary"),
        name="combine",
    )(dest, dest, y_sorted, gate, x1, wsg, wsu, wsd, g, b)


def _pack_w_in(w_in):
    d = w_in.shape[0]
    o_z = GDN_CONV_CH
    o_a = o_z + GDN_V_W
    o_b = o_a + GDN_HEADS
    o_q = o_b + GDN_HEADS
    o_c = o_q + MLA_HEADS * (MLA_D_NOPE + MLA_D_ROPE)
    o_kr = o_c + MLA_KV_RANK
    zeros = lambda w: jnp.zeros((d, w), w_in.dtype)
    wq = w_in[:, o_q:o_c].reshape(d, MLA_HEADS, MLA_D_NOPE + MLA_D_ROPE)
    q_nope = wq[:, :, :MLA_D_NOPE].reshape(d, MLA_HEADS * MLA_D_NOPE)
    q_rope = jnp.pad(wq[:, :, MLA_D_NOPE:], ((0, 0), (0, 0), (0, LANES - MLA_D_ROPE))).reshape(d, MLA_HEADS * LANES)
    cols = [w_in[:, :o_a], q_nope, q_rope, w_in[:, o_c:o_kr], w_in[:, o_kr:], zeros(LANES - MLA_D_ROPE),
            w_in[:, o_a:o_q], zeros(LANES - 2 * GDN_HEADS)]
    return jnp.concatenate(cols, axis=1).astype(BF16)


def _rope_tables(pos):
    inv_freq = ROPE_THETA ** (-jnp.arange(0, MLA_D_ROPE, 2, dtype=F32) / MLA_D_ROPE)
    ang = pos.astype(F32)[:, None] * inv_freq[None, :]
    cos, sin = jnp.cos(ang), jnp.sin(ang)
    pad = jnp.zeros((pos.shape[0], LANES - MLA_D_ROPE), F32)
    return jnp.concatenate([cos, cos, pad], axis=1), jnp.concatenate([-sin, sin, pad], axis=1)


FRONT_TILE = 512
GDN_TILE = 4 * CHUNK
ATTN_TILE = 2048
ATTN_SLAB = 512
TOKEN_TILE = 256


def _pick(t, pref):
    return pref if t % pref == 0 else t


def _token_mixers(x, pos, conv_hist, s0, past, wts, shared, cnt0, n_all, row0):
    b, t, _ = x.shape
    cs_tab, sn_tab = _rope_tables(pos)
    tt = _pick(t, FRONT_TILE)
    c = min(CHUNK, t)
    qkv, z, gb, q, k, v, latent, k_rope, conv_new = _front(
        x, wts["w_pack"], wts["conv_w"], conv_hist, wts["gpar"], wts["kvnw"], wts["wukv"], cs_tab, sn_tab, tt, c)
    og, s_new = _gdn(qkv, z, gb, s0, wts["gdn_nw"], _pick(t, GDN_TILE), c)
    if past is None:
        tb = _pick(t, ATTN_TILE)
        om = _attn_prompt(q, k, v, tb, _pick(tb, ATTN_SLAB))
    else:
        om = _attn_sample(q, k, v, past[0], past[1], wts["wukv"])
    n = b * t
    *shared, cnt = _mixln(og.reshape(n, -1), om.reshape(n, -1), x.reshape(n, D_MODEL), wts["w_out"],
                          wts["ln1_g"], wts["ln1_b"], wts["router_w"], wts["router_b"], cnt0,
                          shared, n_all, row0, _pick(n, TOKEN_TILE))
    return shared, cnt, latent, k_rope, s_new, conv_new


def _moe(x1_all, idx, gate, rank, cnt, n_prompt, wts):
    n = x1_all.shape[0]
    counts = cnt[:, 0].astype(I32)
    padded = (counts + MOE_BM - 1) // MOE_BM * MOE_BM
    pend = jnp.cumsum(padded)
    pstart = pend - padded
    td = _pick(math.gcd(n_prompt, n - n_prompt), TOKEN_TILE)
    dest = _dest(idx, rank, pstart.astype(F32).reshape(-1, 1), td)
    dest = dest.reshape(n // td, 1, TOP_K * td)
    n_blocks = n * TOP_K // MOE_BM + N_EXPERTS
    nact = (pend[-1:] // MOE_BM).astype(I32)
    tail = jnp.maximum(pend - MOE_BM, 0).astype(I32)

    xs = _dispatch(tail, dest, x1_all, n_blocks * MOE_BM, td)
    y_sorted = _experts((pstart // MOE_BM).astype(I32), (padded // MOE_BM).astype(I32), nact, xs,
                        wts["exp_wg"], wts["exp_wu"], wts["exp_wd"])
    return _combine(dest, y_sorted, gate.T, x1_all,
                    wts["sh_wg"], wts["sh_wu"], wts["sh_wd"], wts["ln2_g"], wts["ln2_b"], n_prompt, td)


def kernel(x_prompt, x_sample, cache_kv_latent, cache_k_rope, state_gdn, state_conv, w_in, gdn_conv_w, gdn_a_log, gdn_dt_bias, gdn_norm_w, mla_kv_norm_w, mla_w_uk, mla_w_uv, w_out, ln1_g, ln1_b, router_w, router_bias, exp_w_gate, exp_w_up, exp_w_down, shared_w_gate, shared_w_up, shared_w_down, ln2_g, ln2_b):
    assert w_in.shape[0] == 1, "single-layer stack"
    b_p, t_p, _ = x_prompt.shape
    b_s, t_s, _ = x_sample.shape
    past = cache_kv_latent.shape[2]
    l = 0
    pad4 = lambda a: jnp.pad(a.astype(F32), (0, LANES - GDN_HEADS))
    wts = {
        "w_pack": _pack_w_in(w_in[l]),
        "conv_w": gdn_conv_w[l],
        "gpar": jnp.stack([pad4(gdn_a_log[l]), pad4(gdn_dt_bias[l])]),
        "kvnw": mla_kv_norm_w[l].reshape(1, -1),
        "wukv": jnp.concatenate([mla_w_uk[l].reshape(MLA_KV_RANK, -1), mla_w_uv[l].reshape(MLA_KV_RANK, -1)],
                                axis=1).astype(BF16),
        "gdn_nw": gdn_norm_w[l].reshape(1, -1),
        "w_out": w_out[l].astype(BF16),
        "ln1_g": ln1_g[l].reshape(1, -1), "ln1_b": ln1_b[l].reshape(1, -1),
        "router_w": router_w[l].astype(BF16), "router_b": router_bias[l].reshape(-1, 1),
        "exp_wg": exp_w_gate[l], "exp_wu": exp_w_up[l], "exp_wd": exp_w_down[l],
        "sh_wg": shared_w_gate[l].astype(BF16), "sh_wu": shared_w_up[l].astype(BF16),
        "sh_wd": shared_w_down[l].astype(BF16),
        "ln2_g": ln2_g[l].reshape(1, -1), "ln2_b": ln2_b[l].reshape(1, -1),
    }
    n_p, n_s = b_p * t_p, b_s * t_s
    conv0 = jnp.zeros((b_p, GDN_CONV - 1, GDN_CONV_CH), F32)
    s0 = jnp.zeros((b_p, GDN_HEADS, GDN_DK, GDN_DV), F32)
    cnt0 = jnp.zeros((N_EXPERTS, LANES), F32)
    shared, cnt, lat_p, kr_p, sg_p, cv_p = _token_mixers(
        x_prompt, jnp.arange(t_p), conv0, s0, None, wts, None, cnt0, n_p + n_s, 0)
    shared, cnt, lat_s, kr_s, sg_s, cv_s = _token_mixers(
        x_sample, past + jnp.arange(t_s), state_conv[l], state_gdn[l],
        (cache_kv_latent[l], cache_k_rope[l]), wts, shared, cnt, n_p + n_s, n_p)
    y_p, y_s = _moe(*shared, cnt, n_p, wts)
    return (y_p.reshape(b_p, t_p, D_MODEL), y_s.reshape(b_s, t_s, D_MODEL),
            lat_p[None], kr_p[None], sg_p[None], cv_p[None],
            lat_s[None], kr_s[None], sg_s[None], cv_s[None])
```

```python
import functools
import math

import jax
import jax.numpy as jnp
from jax import lax
from jax.experimental import pallas as pl
from jax.experimental.pallas import tpu as pltpu

F32 = jnp.float32
BF16 = jnp.bfloat16
I32 = jnp.int32

D_MODEL = 1024
CHUNK = 64
GDN_HEADS = 4
GDN_DK = 128
GDN_DV = 128
GDN_CONV = 4
GDN_QK_W = GDN_HEADS * GDN_DK
GDN_V_W = GDN_HEADS * GDN_DV
GDN_CONV_CH = 2 * GDN_QK_W + GDN_V_W
MLA_HEADS = 4
MLA_D_NOPE = 128
MLA_D_ROPE = 64
MLA_D_V = 128
MLA_KV_RANK = 256
MLA_SCALE = (MLA_D_NOPE + MLA_D_ROPE) ** -0.5
QK_PRESCALE = MLA_SCALE * math.log2(math.e)
ROPE_THETA = 10000.0
N_EXPERTS = 256
N_GROUPS = 8
GROUP_SIZE = N_EXPERTS // N_GROUPS
TOPK_GROUPS = 4
TOP_K = 8
EXP_HIDDEN = 256
ROUTED_SCALE = 2.5
DEPTH = 1
DEEPNORM_ALPHA = (2.0 * DEPTH) ** 0.25
LN_EPS = 1e-5
RMS_EPS = 1e-6
L2_EPS = 1e-6

LANES = 128
PK_QKV = 0
PK_Z = PK_QKV + GDN_CONV_CH
PK_QNOPE = PK_Z + GDN_V_W
PK_QROPE = PK_QNOPE + MLA_HEADS * MLA_D_NOPE
PK_CKV = PK_QROPE + MLA_HEADS * LANES
PK_KROPE = PK_CKV + MLA_KV_RANK
PK_AB = PK_KROPE + LANES
PK_W = PK_AB + LANES
MLA_QK_W = 2 * LANES

MOE_BM = 256
EXPERT_RING = 8
VMEM_LIMIT = 56 * 1024 * 1024


def _mm(a, b):
    return jnp.dot(a.astype(BF16), b.astype(BF16), preferred_element_type=F32)


def _mm_nt(a, b):
    return lax.dot_general(a.astype(BF16), b.astype(BF16), (((1,), (1,)), ((), ())),
                           preferred_element_type=F32)


def _mm_tn(a, b):
    return lax.dot_general(a.astype(BF16), b.astype(BF16), (((0,), (0,)), ((), ())),
                           preferred_element_type=F32)


def _split3(x):
    hi = x.astype(BF16)
    r = x - hi.astype(F32)
    mid = r.astype(BF16)
    lo = (r - mid.astype(F32)).astype(BF16)
    return hi, mid, lo


def _sigmoid(x):
    return 1.0 / (1.0 + jnp.exp(-x))


def _silu(x):
    return x * _sigmoid(x)


def _softplus(x):
    return jnp.maximum(x, 0.0) + jnp.log1p(jnp.exp(-jnp.abs(x)))


def _rope(x, cs, sn):
    w = x.shape[-1]
    n = w // LANES
    if n > 1:
        cs = jnp.concatenate([cs] * n, axis=1)
        sn = jnp.concatenate([sn] * n, axis=1)
    lane = lax.broadcasted_iota(I32, x.shape, 1) & (LANES - 1)
    half = MLA_D_ROPE // 2
    swapped = jnp.where(lane < half, pltpu.roll(x, w - half, 1), pltpu.roll(x, half, 1))
    return x * cs + swapped * sn


def _params(*sem):
    return pltpu.CompilerParams(dimension_semantics=sem, vmem_limit_bytes=VMEM_LIMIT)


def _front_kernel(x_ref, w_ref, convw_ref, hist_ref, gpar_ref, kvnw_ref, wukv_ref, cs_ref, sn_ref,
                  qkv_ref, z_ref, gb_ref, q_ref, k_ref, v_ref, lat_ref, kr_ref, convnew_ref,
                  xp_scr, *, tt, c):
    t = pl.program_id(1)
    hrow = 8 - (GDN_CONV - 1)

    @pl.when(t == 0)
    def _():
        xp_scr[hrow:8, :] = hist_ref[0]

    proj = _mm(x_ref[0], w_ref[...])

    raw = proj[:, PK_QKV:PK_Z]
    xp_scr[8:8 + tt, :] = raw
    cw = convw_ref[...]
    y = raw * cw[GDN_CONV - 1:GDN_CONV]
    for i in range(GDN_CONV - 1):
        y = y + xp_scr[hrow + i:hrow + i + tt, :] * cw[i:i + 1]
    tail = xp_scr[tt + hrow:tt + 8, :]
    convnew_ref[0] = tail
    xp_scr[hrow:8, :] = tail
    qkv = _silu(y)
    for h in range(2 * GDN_HEADS):
        xh = qkv[:, h * GDN_DK:(h + 1) * GDN_DK]
        xh = xh * lax.rsqrt(jnp.sum(xh * xh, axis=-1, keepdims=True) + L2_EPS)
        if h < GDN_HEADS:
            xh = xh * GDN_DK ** -0.5
        qkv_ref[0, :, h * GDN_DK:(h + 1) * GDN_DK] = xh
    qkv_ref[0, :, 2 * GDN_QK_W:] = qkv[:, 2 * GDN_QK_W:]
    z_ref[0] = proj[:, PK_Z:PK_QNOPE]

    ab = proj[:, PK_AB:PK_W]
    gpar = gpar_ref[...]
    gc = -jnp.exp(gpar[0:1]) * _softplus(ab + gpar[1:2])
    pos = lax.broadcasted_iota(I32, ab.shape, 0) & (c - 1)
    step = 1
    while step < c:
        gc = gc + jnp.where(pos >= step, pltpu.roll(gc, step, 0), 0.0)
        step *= 2
    beta = _sigmoid(ab)
    lane = lax.broadcasted_iota(I32, ab.shape, 1)
    gb_ref[0] = jnp.where(lane < GDN_HEADS, gc, jnp.where(lane < 2 * GDN_HEADS, beta, 0.0))

    cs = cs_ref[...]
    sn = sn_ref[...]
    q_nope = proj[:, PK_QNOPE:PK_QROPE]
    q_rope = _rope(proj[:, PK_QROPE:PK_CKV], cs, sn)
    c_raw = proj[:, PK_CKV:PK_KROPE]
    latent = c_raw * lax.rsqrt(jnp.mean(c_raw * c_raw, axis=-1, keepdims=True) + RMS_EPS) * kvnw_ref[...]
    lat_ref[0] = latent
    k_rope = _rope(proj[:, PK_KROPE:PK_AB], cs, sn)
    kr_ref[0] = k_rope[:, :MLA_D_ROPE]
    kv = _mm(latent, wukv_ref[...])
    k_rope_b = k_rope.astype(BF16)
    for h in range(MLA_HEADS):
        q_ref[0, h, :, :LANES] = (q_nope[:, h * LANES:(h + 1) * LANES] * QK_PRESCALE).astype(BF16)
        q_ref[0, h, :, LANES:] = (q_rope[:, h * LANES:(h + 1) * LANES] * QK_PRESCALE).astype(BF16)
        k_ref[0, h, :, :LANES] = kv[:, h * LANES:(h + 1) * LANES].astype(BF16)
        k_ref[0, h, :, LANES:] = k_rope_b
        v_ref[0, h] = kv[:, (MLA_HEADS + h) * LANES:(MLA_HEADS + h + 1) * LANES].astype(BF16)


def _front(x, w_pack, conv_w, hist, gpar, kvnw, wukv, cs_tab, sn_tab, tt, c):
    b, t, _ = x.shape
    nt = t // tt
    const2 = lambda bi, ti: (0, 0)
    out_shape = (
        jax.ShapeDtypeStruct((b, t, GDN_CONV_CH), F32),
        jax.ShapeDtypeStruct((b, t, GDN_V_W), F32),
        jax.ShapeDtypeStruct((b, t, LANES), F32),
        jax.ShapeDtypeStruct((b, MLA_HEADS, t, MLA_QK_W), BF16),
        jax.ShapeDtypeStruct((b, MLA_HEADS, t, MLA_QK_W), BF16),
        jax.ShapeDtypeStruct((b, MLA_HEADS, t, MLA_D_V), BF16),
        jax.ShapeDtypeStruct((b, t, MLA_KV_RANK), F32),
        jax.ShapeDtypeStruct((b, t, MLA_D_ROPE), F32),
        jax.ShapeDtypeStruct((b, GDN_CONV - 1, GDN_CONV_CH), F32),
    )
    row3 = lambda w: pl.BlockSpec((1, tt, w), lambda bi, ti: (bi, ti, 0))
    head4 = lambda w: pl.BlockSpec((1, MLA_HEADS, tt, w), lambda bi, ti: (bi, 0, ti, 0))
    return pl.pallas_call(
        functools.partial(_front_kernel, tt=tt, c=c),
        grid=(b, nt),
        in_specs=[
            row3(D_MODEL),
            pl.BlockSpec((D_MODEL, PK_W), const2),
            pl.BlockSpec((GDN_CONV, GDN_CONV_CH), const2),
            pl.BlockSpec((1, GDN_CONV - 1, GDN_CONV_CH), lambda bi, ti: (bi, 0, 0)),
            pl.BlockSpec((2, LANES), const2),
            pl.BlockSpec((1, MLA_KV_RANK), const2),
            pl.BlockSpec((MLA_KV_RANK, 2 * MLA_HEADS * LANES), const2),
            pl.BlockSpec((tt, LANES), lambda bi, ti: (ti, 0)),
            pl.BlockSpec((tt, LANES), lambda bi, ti: (ti, 0)),
        ],
        out_specs=(
            row3(GDN_CONV_CH), row3(GDN_V_W), row3(LANES),
            head4(MLA_QK_W), head4(MLA_QK_W), head4(MLA_D_V),
            row3(MLA_KV_RANK), row3(MLA_D_ROPE),
            pl.BlockSpec((1, GDN_CONV - 1, GDN_CONV_CH), lambda bi, ti: (bi, 0, 0)),
        ),
        out_shape=out_shape,
        scratch_shapes=[pltpu.VMEM((tt + 8, GDN_CONV_CH), F32)],
        compiler_params=_params("arbitrary", "arbitrary"),
        name="front",
    )(x, w_pack, conv_w, hist, gpar, kvnw, wukv, cs_tab, sn_tab)


def _gdn_kernel(qkv_ref, z_ref, gb_ref, s0_ref, nw_ref, og_ref, sout_ref, s_scr, *, tg, c, bb):
    t = pl.program_id(1)
    nh = GDN_HEADS
    r = nh * c
    sh = c.bit_length() - 1

    @pl.when(t == 0)
    def _():
        s_scr[...] = s0_ref[...]

    row = lax.broadcasted_iota(I32, (r, r), 0)
    col = lax.broadcasted_iota(I32, (r, r), 1)
    same = (row >> sh) == (col >> sh)
    incl = same & (row >= col)
    strict = same & (row > col)
    eye = jnp.where(row == col, 1.0, 0.0)
    lane0 = jnp.where(lax.broadcasted_iota(I32, (r, LANES), 1) == 0, 1.0, 0.0).astype(BF16)
    nw = nw_ref[...]
    items = [(bi, ci) for ci in range(tg // c) for bi in range(bb)]
    chunks = range(len(items))

    def stacked(ref, it, base):
        bi, ci = items[it]
        return jnp.concatenate(
            [ref[bi, ci * c:(ci + 1) * c, base + h * LANES:base + (h + 1) * LANES] for h in range(nh)], axis=0)

    def col_bcast(it, lane):
        bi, ci = items[it]
        gbc = gb_ref[bi, ci * c:(ci + 1) * c, :]
        return jnp.concatenate(
            [jnp.broadcast_to(gbc[:, lane + h:lane + h + 1], (c, LANES)) for h in range(nh)], axis=0)

    def as_col(gc_b):
        return _lane_tile(gc_b, r) if r % LANES == 0 else gc_b[:, :r]

    def as_row(gc_b):
        if r % LANES == 0:
            return as_col(gc_b).T
        return sum(lax.dot_general(lane0, p, (((1,), (1,)), ((), ())), preferred_element_type=F32)
                   for p in _split3(gc_b))

    ks = [stacked(qkv_ref, ci, GDN_QK_W) for ci in chunks]
    gc = [col_bcast(ci, 0) for ci in chunks]
    beta = [col_bcast(ci, nh) for ci in chunks]
    decay, qk_kk = [], []
    for ci in chunks:
        decay.append(jnp.exp(jnp.where(incl, as_col(gc[ci]) - as_row(gc[ci]), -jnp.inf)))
        qk_kk.append(_mm_nt(jnp.concatenate([stacked(qkv_ref, ci, 0), ks[ci]], axis=0), ks[ci]))
    intra = [qk_kk[ci][:r] * decay[ci] for ci in chunks]
    n_pow = [jnp.where(strict, -as_col(beta[ci]) * qk_kk[ci][r:] * decay[ci], 0.0)
             for ci in chunks]
    t_inv = [eye + n_pow[ci] for ci in chunks]
    for _ in range(sh - 1):
        n_pow = [_mm(n_pow[ci], n_pow[ci]) for ci in chunks]
        t_inv = [t_inv[ci] + _mm(t_inv[ci], n_pow[ci]) for ci in chunks]
    egc = [jnp.exp(gc[ci]) for ci in chunks]
    uw = [_mm(t_inv[ci], jnp.concatenate([stacked(qkv_ref, ci, 2 * GDN_QK_W) * beta[ci],
                                          ks[ci] * beta[ci] * egc[ci]], axis=1)) for ci in chunks]

    for ci in chunks:
        bi, cpos = items[ci]
        u = uw[ci][:, :GDN_DV]
        w = uw[ci][:, GDN_DV:]
        qd = stacked(qkv_ref, ci, 0) * egc[ci]
        vn, qs_s = [], []
        for h in range(nh):
            hs = slice(h * c, (h + 1) * c)
            s_h = s_scr[bi, h]
            wq = _mm(jnp.concatenate([w[hs], qd[hs]], axis=0), s_h)
            vn_h = u[hs] - wq[:c]
            g_last = gc[ci][h * c + c - 1:h * c + c, :]
            kd = ks[ci][hs] * jnp.exp(g_last - gc[ci][hs])
            s_scr[bi, h] = s_h * jnp.exp(g_last) + _mm_tn(kd, vn_h)
            vn.append(vn_h)
            qs_s.append(wq[c:])
        o = jnp.concatenate(qs_s, axis=0) + _mm(intra[ci], jnp.concatenate(vn, axis=0))

        o = o * lax.rsqrt(jnp.mean(o * o, axis=-1, keepdims=True) + RMS_EPS) * nw
        o = o * _silu(stacked(z_ref, ci, 0))
        for h in range(nh):
            og_ref[bi, cpos * c:(cpos + 1) * c, h * LANES:(h + 1) * LANES] = o[h * c:(h + 1) * c].astype(BF16)

    @pl.when(t == pl.num_programs(1) - 1)
    def _():
        sout_ref[...] = s_scr[...]


def _gdn(qkv, z, gb, s0, nw, tg, c):
    b, t, _ = qkv.shape
    bb = 2 if b % 2 == 0 else 1
    row3 = lambda w: pl.BlockSpec((bb, tg, w), lambda bi, ti: (bi, ti, 0))
    st = pl.BlockSpec((bb, GDN_HEADS, GDN_DK, GDN_DV), lambda bi, ti: (bi, 0, 0, 0))
    return pl.pallas_call(
        functools.partial(_gdn_kernel, tg=tg, c=c, bb=bb),
        grid=(b // bb, t // tg),
        in_specs=[row3(GDN_CONV_CH), row3(GDN_V_W), row3(LANES), st,
                  pl.BlockSpec((1, GDN_DV), lambda bi, ti: (0, 0))],
        out_specs=(row3(GDN_V_W), st),
        out_shape=(jax.ShapeDtypeStruct((b, t, GDN_V_W), BF16),
                   jax.ShapeDtypeStruct((b, GDN_HEADS, GDN_DK, GDN_DV), F32)),
        scratch_shapes=[pltpu.VMEM((bb, GDN_HEADS, GDN_DK, GDN_DV), F32)],
        compiler_params=_params("arbitrary", "arbitrary"),
        name="gdn",
    )(qkv, z, gb, s0, nw)


ATTN_ROW_BLOCK = 32


def _lane_tile(x, width):
    return x if width == LANES else jnp.concatenate([x] * (width // LANES), axis=1)


def _attn_kernel(qi_ref, ki_ref, q_ref, k_ref, v_ref, o_ref, m_scr, l_scr, acc_scr, a_scr, s_scr, p_scr,
                 *, tb, sub):
    step = pl.program_id(2)
    qi = qi_ref[step]
    ki = ki_ref[step]
    csh = CHUNK.bit_length() - 1

    @pl.when(ki == 0)
    def _():
        m_scr[...] = jnp.full(m_scr.shape, -jnp.inf, F32)
        l_scr[...] = jnp.zeros(l_scr.shape, F32)
        acc_scr[...] = jnp.zeros(acc_scr.shape, F32)

    half = tb // 2
    rb = min(ATTN_ROW_BLOCK, half)

    def scores(h, j):
        rows = slice(h * half, (h + 1) * half)
        s_scr[rows, :] = lax.dot_general(q_ref[0, 0, rows, :], k_ref[0, 0, j * sub:(j + 1) * sub, :],
                                         (((1,), (1,)), ((), ())), preferred_element_type=F32)

    def softmax(h, j, masked):
        r0 = h * half

        def block(i):
            rr = pl.ds(pl.multiple_of(r0 + i * rb, rb), rb)
            s = s_scr[rr, :]
            if masked:
                qc = (r0 + i * rb + lax.broadcasted_iota(I32, s.shape, 0)) >> csh
                kc = (j * sub + lax.broadcasted_iota(I32, s.shape, 1)) >> csh
                s = jnp.where(kc <= qc, s, -jnp.inf)
            return rr, s

        def row_max(i, carry):
            rr, s = block(i)
            m_prev = m_scr[rr, :]
            m_new = jnp.maximum(m_prev, jnp.max(s, axis=-1, keepdims=True))
            a_scr[rr, :] = jnp.exp2(m_prev - m_new)
            m_scr[rr, :] = m_new
            return carry

        def row_exp(i, carry):
            rr, s = block(i)
            p = jnp.exp2(s - _lane_tile(m_scr[rr, :], sub))
            l_scr[rr, :] = a_scr[rr, :] * l_scr[rr, :] + jnp.sum(p, axis=-1, keepdims=True)
            p_scr[rr, :] = p.astype(BF16)
            return carry

        for i in range(half // rb):
            row_max(i, 0)
        for i in range(half // rb):
            row_exp(i, 0)

    def weighted_values(h, j):
        rows = slice(h * half, (h + 1) * half)
        acc_scr[rows, :] = a_scr[rows, :] * acc_scr[rows, :] + jnp.dot(
            p_scr[rows, :], v_ref[0, 0, j * sub:(j + 1) * sub, :], preferred_element_type=F32)

    def run(tasks):
        for h in range(2):
            if tasks[h]:
                scores(h, tasks[h][0][0])
        for n in range(max(len(t) for t in tasks)):
            for h in range(2):
                if n < len(tasks[h]):
                    j, masked = tasks[h][n]
                    softmax(h, j, masked)
                    weighted_values(h, j)
                    if n + 1 < len(tasks[h]):
                        scores(h, tasks[h][n + 1][0])

    def diagonal_tasks(h):
        r0 = h * half
        out = []
        for j in range(tb // sub):
            k0, k1 = j * sub, (j + 1) * sub
            if k0 >= r0 + half:
                continue
            out.append((j, k1 > r0 + CHUNK))
        return out

    @pl.when(ki < qi)
    def _():
        run([[(j, False) for j in range(tb // sub)]] * 2)

    @pl.when(ki == qi)
    def _():
        run([diagonal_tasks(0), diagonal_tasks(1)])
        o_ref[0] = (acc_scr[...] / l_scr[...]).astype(BF16)


def _attn_prompt(q, k, v, tb, sub):
    b, nh, t, _ = q.shape
    nt = t // tb
    pairs = [(qi, ki) for qi in range(nt) for ki in range(qi + 1)]
    qi_of = jnp.asarray([p[0] for p in pairs], I32)
    ki_of = jnp.asarray([p[1] for p in pairs], I32)
    return pl.pallas_call(
        functools.partial(_attn_kernel, tb=tb, sub=sub),
        grid_spec=pltpu.PrefetchScalarGridSpec(
            num_scalar_prefetch=2,
            grid=(b, nh, len(pairs)),
            in_specs=[pl.BlockSpec((1, 1, tb, MLA_QK_W), lambda bi, hi, s, qo, ko: (bi, hi, qo[s], 0)),
                      pl.BlockSpec((1, 1, tb, MLA_QK_W), lambda bi, hi, s, qo, ko: (bi, hi, ko[s], 0)),
                      pl.BlockSpec((1, 1, tb, MLA_D_V), lambda bi, hi, s, qo, ko: (bi, hi, ko[s], 0))],
            out_specs=pl.BlockSpec((1, tb, MLA_D_V), lambda bi, hi, s, qo, ko: (bi, qo[s], hi)),
            scratch_shapes=[pltpu.VMEM((tb, LANES), F32), pltpu.VMEM((tb, LANES), F32),
                            pltpu.VMEM((tb, MLA_D_V), F32), pltpu.VMEM((tb, LANES), F32),
                            pltpu.VMEM((tb, sub), F32), pltpu.VMEM((tb, sub), BF16)]),
        out_shape=jax.ShapeDtypeStruct((b, t, MLA_HEADS * MLA_D_V), BF16),
        compiler_params=_params("arbitrary", "arbitrary", "arbitrary"),
        name="attn_prompt",
    )(qi_of, ki_of, q, k, v)


def _attn_sample_kernel(q_ref, kn_ref, vn_ref, plat_ref, pkr_ref, wukv_ref, o_ref):
    plat = plat_ref[0].astype(BF16)
    pkr = pkr_ref[0].astype(BF16)
    ts = q_ref.shape[2]
    heads = range(MLA_HEADS)
    w_uk = lambda h: wukv_ref[:, h * LANES:(h + 1) * LANES]
    w_uv = lambda h: wukv_ref[:, (MLA_HEADS + h) * LANES:(MLA_HEADS + h + 1) * LANES]
    q_lat = jnp.concatenate([_mm_nt(q_ref[0, h][:, :MLA_D_NOPE], w_uk(h)) for h in heads], axis=0)
    q_rope = jnp.concatenate([q_ref[0, h][:, MLA_D_NOPE:MLA_D_NOPE + MLA_D_ROPE] for h in heads], axis=0)
    s_past = _mm_nt(q_lat, plat) + _mm_nt(q_rope, pkr)
    s_new = jnp.concatenate([_mm_nt(q_ref[0, h], kn_ref[0, h]) for h in heads], axis=0)
    m = jnp.maximum(jnp.max(s_past, axis=-1, keepdims=True), jnp.max(s_new, axis=-1, keepdims=True))
    p_past = jnp.exp2(s_past - m)
    p_new = jnp.exp2(s_new - m)
    l = jnp.sum(p_past, axis=-1, keepdims=True) + jnp.sum(p_new, axis=-1, keepdims=True)
    ctx = _mm(p_past, plat)
    for h in heads:
        rows = slice(h * ts, (h + 1) * ts)
        o = _mm(ctx[rows], w_uv(h)) + _mm(p_new[rows], vn_ref[0, h])
        o_ref[0, :, h * MLA_D_V:(h + 1) * MLA_D_V] = (o / l[rows]).astype(BF16)


def _attn_sample(q, k_new, v_new, past_lat, past_kr, wukv):
    b, nh, ts, _ = q.shape
    past = past_lat.shape[1]
    b4 = lambda w: pl.BlockSpec((1, nh, ts, w), lambda bi: (bi, 0, 0, 0))
    return pl.pallas_call(
        _attn_sample_kernel,
        grid=(b,),
        in_specs=[b4(MLA_QK_W), b4(MLA_QK_W), b4(MLA_D_V),
                  pl.BlockSpec((1, past, MLA_KV_RANK), lambda bi: (bi, 0, 0)),
                  pl.BlockSpec((1, past, MLA_D_ROPE), lambda bi: (bi, 0, 0)),
                  pl.BlockSpec((MLA_KV_RANK, 2 * MLA_HEADS * LANES), lambda bi: (0, 0))],
        out_specs=pl.BlockSpec((1, ts, MLA_HEADS * MLA_D_V), lambda bi: (bi, 0, 0)),
        out_shape=jax.ShapeDtypeStruct((b, ts, MLA_HEADS * MLA_D_V), BF16),
        compiler_params=_params("arbitrary"),
        name="attn_sample",
    )(q, k_new, v_new, past_lat, past_kr, wukv)


def _layernorm(y, g, b):
    mu = jnp.mean(y, axis=-1, keepdims=True)
    d = y - mu
    var = jnp.mean(d * d, axis=-1, keepdims=True)
    return d * lax.rsqrt(var + LN_EPS) * g + b


N_SHARED = 4


def _mixln_kernel(og_ref, om_ref, x_ref, w_ref, g_ref, b_ref, rw_ref, rb_ref, cnt0_ref, *rest, tm):
    o_ref, idx_ref, gate_ref, rank_ref, cnt_ref, carry_scr = rest[-6:]

    @pl.when(pl.program_id(0) == 0)
    def _():
        carry_scr[...] = cnt0_ref[...]

    mix = (jnp.dot(og_ref[...], w_ref[:GDN_V_W, :], preferred_element_type=F32)
           + jnp.dot(om_ref[...], w_ref[GDN_V_W:, :], preferred_element_type=F32))
    x1 = _layernorm(DEEPNORM_ALPHA * x_ref[...] + mix, g_ref[...], b_ref[...])
    o_ref[...] = x1
    _route_tile(x1, rw_ref, rb_ref, carry_scr, idx_ref, gate_ref, rank_ref, tm)
    cnt_ref[...] = carry_scr[...]


def _mixln(og, om, x, w_out, g, b, rw, rb, cnt0, shared, n_all, row0, tm):
    n = x.shape[0]
    blk0 = row0 // tm
    rows = lambda w: pl.BlockSpec((tm, w), lambda i: (i, 0))
    const = lambda s: pl.BlockSpec(s, lambda i: (0, 0))
    kspec = pl.BlockSpec((TOP_K, tm), lambda i: (0, blk0 + i))
    first = shared is None
    n_in = 9
    kt = lambda dt: jax.ShapeDtypeStruct((TOP_K, n_all), dt)
    return pl.pallas_call(
        functools.partial(_mixln_kernel, tm=tm),
        grid=(n // tm,),
        in_specs=[rows(GDN_V_W), rows(MLA_HEADS * MLA_D_V), rows(D_MODEL),
                  const((D_MODEL, D_MODEL)), const((1, D_MODEL)), const((1, D_MODEL)),
                  const((D_MODEL, N_EXPERTS)), const((N_EXPERTS, 1)), const((N_EXPERTS, LANES))]
                 + ([] if first else [pl.BlockSpec(memory_space=pl.ANY)] * N_SHARED),
        out_specs=(pl.BlockSpec((tm, D_MODEL), lambda i: (blk0 + i, 0)), kspec, kspec, kspec,
                   const((N_EXPERTS, LANES))),
        out_shape=(jax.ShapeDtypeStruct((n_all, D_MODEL), F32), kt(I32), kt(F32), kt(I32),
                   jax.ShapeDtypeStruct((N_EXPERTS, LANES), F32)),
        scratch_shapes=[pltpu.VMEM((N_EXPERTS, LANES), F32)],
        input_output_aliases={} if first else {n_in + j: j for j in range(N_SHARED)},
        compiler_params=_params("arbitrary"),
        name="mixln",
    )(*((og, om, x, w_out, g, b, rw, rb, cnt0) + (() if first else tuple(shared))))


def _route_tile(x, rw_ref, rb_ref, carry_scr, idx_ref, gate_ref, rank_ref, tt):
    ninf = -jnp.inf
    big = float(2 * N_EXPERTS)
    scores = _sigmoid(_mm(x, rw_ref[...]).T)
    biased = scores + rb_ref[...]
    eio = lax.broadcasted_iota(I32, (N_EXPERTS, tt), 0).astype(F32)

    def first_argmax(vals, io):
        m = jnp.max(vals, axis=0, keepdims=True)
        i = jnp.min(jnp.where(vals == m, io, big), axis=0, keepdims=True)
        return m, i

    gs = []
    for g in range(N_GROUPS):
        blk = biased[g * GROUP_SIZE:(g + 1) * GROUP_SIZE]
        io = (lax.broadcasted_iota(I32, (GROUP_SIZE, tt), 0) + g * GROUP_SIZE).astype(F32)
        m1, i1 = first_argmax(blk, io)
        m2 = jnp.max(jnp.where(io == i1, ninf, blk), axis=0, keepdims=True)
        gs.append(m1 + m2)
    gio = lax.broadcasted_iota(I32, (N_GROUPS, tt), 0).astype(F32)
    gsc = jnp.zeros((N_GROUPS, tt), F32)
    for g in range(N_GROUPS):
        gsc = jnp.where(gio == float(g), gs[g], gsc)
    gsel = jnp.zeros((N_GROUPS, tt), F32)
    for _ in range(TOPK_GROUPS):
        _, gi = first_argmax(gsc, gio)
        hit = gio == gi
        gsel = jnp.where(hit, 1.0, gsel)
        gsc = jnp.where(hit, ninf, gsc)
    masked = jnp.concatenate(
        [jnp.where(jnp.max(jnp.where(gio == float(g), gsel, 0.0), axis=0, keepdims=True) > 0.0,
                   biased[g * GROUP_SIZE:(g + 1) * GROUP_SIZE], ninf) for g in range(N_GROUPS)], axis=0)

    idx, wts = [], []
    sel = jnp.zeros((N_EXPERTS, tt), F32)
    for _ in range(TOP_K):
        _, ei = first_argmax(masked, eio)
        hit = eio == ei
        wts.append(jnp.sum(jnp.where(hit, scores, 0.0), axis=0, keepdims=True))
        masked = jnp.where(hit, ninf, masked)
        sel = jnp.where(hit, 1.0, sel)
        idx.append(ei)
    wsum = wts[0]
    for w in wts[1:]:
        wsum = wsum + w

    t0 = lax.broadcasted_iota(I32, (tt, tt), 0)
    t1 = lax.broadcasted_iota(I32, (tt, tt), 1)
    before = jnp.where(t0 < t1, 1.0, 0.0).astype(BF16)
    sel_b = sel.astype(BF16)
    base = carry_scr[:, :1] + jnp.dot(sel_b, before, preferred_element_type=F32)
    ranks = [jnp.sum(jnp.where(eio == ei, base, 0.0), axis=0, keepdims=True) for ei in idx]
    carry_scr[...] = carry_scr[...] + jnp.dot(sel_b, jnp.ones((tt, LANES), BF16), preferred_element_type=F32)

    for k in range(TOP_K):
        idx_ref[k:k + 1, :] = idx[k].astype(I32)
        gate_ref[k:k + 1, :] = wts[k] / wsum * ROUTED_SCALE
        rank_ref[k:k + 1, :] = ranks[k].astype(I32)


def _dest_kernel(idx_ref, rank_ref, pstart_ref, dest_ref, *, tt, per_step):
    eio = lax.broadcasted_iota(I32, (N_EXPERTS, tt), 0)
    pstart = pstart_ref[...]
    for j in range(per_step):
        cols = slice(j * tt, (j + 1) * tt)
        for k in range(TOP_K):
            start = jnp.sum(jnp.where(eio == idx_ref[k:k + 1, cols], pstart, 0.0), axis=0, keepdims=True)
            dest_ref[j, k:k + 1, :] = start.astype(I32) + rank_ref[k:k + 1, cols]


def _dest(idx, rank, pstart, tt):
    n = idx.shape[1]
    per_step = next(s for s in (4, 3, 2, 1) if (n // tt) % s == 0)
    kspec = pl.BlockSpec((TOP_K, per_step * tt), lambda i: (0, i))
    return pl.pallas_call(
        functools.partial(_dest_kernel, tt=tt, per_step=per_step),
        grid=(n // (tt * per_step),),
        in_specs=[kspec, kspec, pl.BlockSpec((N_EXPERTS, 1), lambda i: (0, 0))],
        out_specs=pl.BlockSpec((per_step, TOP_K, tt), lambda i: (i, 0, 0)),
        out_shape=jax.ShapeDtypeStruct((n // tt, TOP_K, tt), I32),
        compiler_params=_params("arbitrary"),
        name="dest",
    )(idx, rank, pstart)


ROW_TILE = D_MODEL // LANES
XROW_TILE = ROW_TILE // 2


def _rows_to_tiles(x, tiles_ref, base, n, rt=ROW_TILE):
    for j in range(rt):
        tiles_ref[pl.ds(base * rt + j, n, stride=rt), :] = x[:, j * LANES:(j + 1) * LANES]


def _tiles_to_rows(tiles_ref, base, n, rt=ROW_TILE):
    return jnp.concatenate(
        [tiles_ref[pl.ds(base * rt + j, n, stride=rt), :] for j in range(rt)], axis=1)


def _pack_bf16_pairs(x):
    half = x.shape[1] // 2
    lo = pltpu.bitcast(x[:, :half].astype(BF16).astype(F32), jnp.uint32) >> 16
    hi = pltpu.bitcast(x[:, half:].astype(BF16).astype(F32), jnp.uint32) & jnp.uint32(0xFFFF0000)
    return lo | hi


def _unpack_bf16_pairs(w):
    lo = pltpu.bitcast(w << 16, F32)
    hi = pltpu.bitcast(w & jnp.uint32(0xFFFF0000), F32)
    return jnp.concatenate([lo, hi], axis=1)


def _dispatch_kernel(tail_ref, dest_ref, x_ref, xs_out, zbuf, xt_scr, sem, zsem, *, td):
    s = pl.program_id(0)

    @pl.when(s == 0)
    def _():
        zbuf[...] = jnp.zeros(zbuf.shape, zbuf.dtype)

        def zero_copy(e):
            first = pl.multiple_of(tail_ref[e] * XROW_TILE, MOE_BM * XROW_TILE)
            return pltpu.make_async_copy(zbuf, xs_out.at[pl.ds(first, MOE_BM * XROW_TILE)], zsem)

        def zstart(e, carry):
            zero_copy(e).start()
            return carry

        def zwait(e, carry):
            zero_copy(e).wait()
            return carry

        lax.fori_loop(0, N_EXPERTS, zstart, 0)
        lax.fori_loop(0, N_EXPERTS, zwait, 0)

    slot = s & 1

    def row_copy(slot_, i, d):
        src = pl.multiple_of((slot_ * td + i) * XROW_TILE, XROW_TILE)
        return pltpu.make_async_copy(xt_scr.at[pl.ds(src, XROW_TILE)],
                                     xs_out.at[pl.ds(pl.multiple_of(d * XROW_TILE, XROW_TILE), XROW_TILE)],
                                     sem.at[slot_])

    def drain(slot_):
        step_rows = TOP_K * td * XROW_TILE
        pltpu.make_async_copy(xs_out.at[pl.ds(0, step_rows)], xs_out.at[pl.ds(0, step_rows)],
                              sem.at[slot_]).wait()

    @pl.when(s >= 2)
    def _():
        drain(slot)

    _rows_to_tiles(_pack_bf16_pairs(x_ref[...]), xt_scr, slot * td, td, XROW_TILE)

    def issue(i, carry):
        for k in range(TOP_K):
            row_copy(slot, i, dest_ref[0, 0, k * td + i]).start(priority=k % 2)
        return carry

    lax.fori_loop(0, td, issue, 0, unroll=4)

    @pl.when(s == pl.num_programs(0) - 1)
    def _():
        drain(slot)

        @pl.when(s >= 1)
        def _():
            drain(1 - slot)


def _dispatch(tail, dest, x1, n_rows, td):
    n = x1.shape[0]
    return pl.pallas_call(
        functools.partial(_dispatch_kernel, td=td),
        grid_spec=pltpu.PrefetchScalarGridSpec(
            num_scalar_prefetch=1,
            grid=(n // td,),
            in_specs=[pl.BlockSpec((1, 1, TOP_K * td), lambda i, tl: (i, 0, 0), memory_space=pltpu.SMEM),
                      pl.BlockSpec((td, D_MODEL), lambda i, tl: (i, 0))],
            out_specs=pl.BlockSpec(memory_space=pl.ANY),
            scratch_shapes=[pltpu.VMEM((MOE_BM * XROW_TILE, LANES), jnp.uint32),
                            pltpu.VMEM((2 * td * XROW_TILE, LANES), jnp.uint32),
                            pltpu.SemaphoreType.DMA((2,)), pltpu.SemaphoreType.DMA(())]),
        out_shape=jax.ShapeDtypeStruct((n_rows * XROW_TILE, LANES), jnp.uint32),
        compiler_params=_params("arbitrary"),
        name="dispatch",
    )(tail, dest, x1)


def _expert_kernel(first_ref, nblk_ref, nact_ref, xs_hbm, wg_ref, wu_ref, wd_ref, y_hbm,
                   xbuf, ybuf, wgb, wub, wdb, xsem, ysem):
    e = pl.program_id(0)
    nact = nact_ref[0]
    ring = EXPERT_RING
    ahead = ring - 2
    xr = MOE_BM * XROW_TILE
    yr = MOE_BM * XROW_TILE

    def x_copy(g):
        slot = g & (ring - 1)
        return pltpu.make_async_copy(xs_hbm.at[pl.ds(pl.multiple_of(g * xr, xr), xr)],
                                     xbuf.at[pl.ds(pl.multiple_of(slot * xr, xr), xr)], xsem.at[slot])

    def y_copy(g):
        slot = g & (ring - 1)
        return pltpu.make_async_copy(ybuf.at[pl.ds(pl.multiple_of(slot * yr, yr), yr)],
                                     y_hbm.at[pl.ds(pl.multiple_of(g * yr, yr), yr)], ysem.at[slot])

    @pl.when(e == 0)
    def _():
        for g0 in range(ahead):
            @pl.when(g0 < nact)
            def _():
                x_copy(g0).start()

    @pl.when(nblk_ref[e] > 0)
    def _():
        wgb[...] = wg_ref[0].astype(BF16)
        wub[...] = wu_ref[0].astype(BF16)
        wdb[...] = wd_ref[0].astype(BF16)

    def blocks(g0, nb):
        gs = [g0 + j for j in range(nb)]
        for g in gs:
            x_copy(g).wait()

            @pl.when(g + ahead < nact)
            def _():
                x_copy(g + ahead).start()

            @pl.when(g >= ring)
            def _():
                y_copy(g - ring).wait()

        xb = jnp.concatenate(
            [_unpack_bf16_pairs(_tiles_to_rows(xbuf, (g & (ring - 1)) * MOE_BM, MOE_BM, XROW_TILE)) for g in gs],
            axis=0).astype(BF16)
        hg = jnp.dot(xb, wgb[...], preferred_element_type=F32)
        hu = jnp.dot(xb, wub[...], preferred_element_type=F32)
        y = jnp.dot((_silu(hg) * hu).astype(BF16), wdb[...], preferred_element_type=F32)
        yp = _pack_bf16_pairs(y)
        for j, g in enumerate(gs):
            _rows_to_tiles(yp[j * MOE_BM:(j + 1) * MOE_BM], ybuf, (g & (ring - 1)) * MOE_BM, MOE_BM, XROW_TILE)
            y_copy(g).start()

        g_last = gs[-1]

        @pl.when(g_last == nact - 1)
        def _():
            for back in range(ring):
                @pl.when(g_last >= back)
                def _():
                    y_copy(g_last - back).wait()

    nblk = nblk_ref[e]

    def pair(b, carry):
        blocks(first_ref[e] + 2 * b, 2)
        return carry

    lax.fori_loop(0, nblk // 2, pair, 0)

    @pl.when(nblk % 2 == 1)
    def _():
        blocks(first_ref[e] + nblk - 1, 1)


def _experts(first, nblk, nact, xs, wg, wu, wd):
    n_rows = xs.shape[0] // XROW_TILE
    wspec = lambda s: pl.BlockSpec((1,) + s, lambda e, fr, nb, na: (e, 0, 0))
    return pl.pallas_call(
        _expert_kernel,
        grid_spec=pltpu.PrefetchScalarGridSpec(
            num_scalar_prefetch=3,
            grid=(N_EXPERTS,),
            in_specs=[pl.BlockSpec(memory_space=pl.ANY),
                      wspec((D_MODEL, EXP_HIDDEN)), wspec((D_MODEL, EXP_HIDDEN)), wspec((EXP_HIDDEN, D_MODEL))],
            out_specs=pl.BlockSpec(memory_space=pl.ANY),
            scratch_shapes=[pltpu.VMEM((EXPERT_RING * MOE_BM * XROW_TILE, LANES), jnp.uint32),
                            pltpu.VMEM((EXPERT_RING * MOE_BM * XROW_TILE, LANES), jnp.uint32),
                            pltpu.VMEM((D_MODEL, EXP_HIDDEN), BF16), pltpu.VMEM((D_MODEL, EXP_HIDDEN), BF16),
                            pltpu.VMEM((EXP_HIDDEN, D_MODEL), BF16),
                            pltpu.SemaphoreType.DMA((EXPERT_RING,)), pltpu.SemaphoreType.DMA((EXPERT_RING,))]),
        out_shape=jax.ShapeDtypeStruct((n_rows * XROW_TILE, LANES), jnp.uint32),
        compiler_params=_params("arbitrary"),
        name="experts",
    )(first, nblk, nact, xs, wg, wu, wd)


def _combine_kernel(dcur_ref, dnxt_ref, y_hbm, gate_ref, x_ref, wsg_ref, wsu_ref, wsd_ref, g_ref, b_ref,
                    outp_ref, outs_ref, buf, routed_scr, sem, *, tc, np_tiles):
    s = pl.program_id(0)
    ns = pl.num_programs(0)
    slot = s % 2

    def row_copy(d, slot_, k, i):
        dst = pl.multiple_of(((slot_ * TOP_K + k) * tc + i) * XROW_TILE, XROW_TILE)
        return pltpu.make_async_copy(y_hbm.at[pl.ds(pl.multiple_of(d * XROW_TILE, XROW_TILE), XROW_TILE)],
                                     buf.at[pl.ds(dst, XROW_TILE)], sem.at[slot_])

    def issue(dref, slot_):
        def body(i, carry):
            for k in range(TOP_K):
                row_copy(dref[0, 0, k * tc + i], slot_, k, i).start(priority=k % 2)
            return carry
        lax.fori_loop(0, tc, body, 0, unroll=4)

    @pl.when(s == 0)
    def _():
        issue(dcur_ref, 0)

    slot_rows = TOP_K * tc * XROW_TILE
    pltpu.make_async_copy(y_hbm.at[pl.ds(0, slot_rows)],
                          buf.at[pl.ds(pl.multiple_of(slot * slot_rows, slot_rows), slot_rows)],
                          sem.at[slot]).wait()

    grp = 16

    def gated_sum(gi):
        base = pl.multiple_of(gi * grp, grp)
        gate = gate_ref[pl.ds(base, grp), :]
        acc = None
        for k in range(TOP_K):
            rows = _unpack_bf16_pairs(_tiles_to_rows(buf, (slot * TOP_K + k) * tc + base, grp, XROW_TILE))
            term = rows * gate[:, k:k + 1]
            acc = term if acc is None else acc + term
        routed_scr[pl.ds(base, grp), :] = acc

    def sum_and_issue(gi, carry):
        for i in range(grp):
            for k in range(TOP_K):
                row_copy(dnxt_ref[0, 0, k * tc + gi * grp + i], 1 - slot, k, gi * grp + i).start(priority=k % 2)
        gated_sum(gi)
        return carry

    def sum_only(gi, carry):
        gated_sum(gi)
        return carry

    @pl.when(s + 1 < ns)
    def _():
        lax.fori_loop(0, tc // grp, sum_and_issue, 0)

    @pl.when(s + 1 == ns)
    def _():
        lax.fori_loop(0, tc // grp, sum_only, 0)

    x = x_ref[...]
    routed = routed_scr[...]
    xb = x.astype(BF16)
    shared = _mm(_silu(_mm(xb, wsg_ref[...])) * _mm(xb, wsu_ref[...]), wsd_ref[...])
    out = _layernorm(DEEPNORM_ALPHA * x + (routed + shared), g_ref[...], b_ref[...])

    @pl.when(s < np_tiles)
    def _():
        outp_ref[...] = out

    @pl.when(s >= np_tiles)
    def _():
        outs_ref[...] = out


def _combine(dest, y_sorted, gate, x1, wsg, wsu, wsd, g, b, n_prompt, tc):
    n = x1.shape[0]
    ns = n // tc
    np_tiles = n_prompt // tc
    const = lambda s: pl.BlockSpec(s, lambda i: (0, 0))
    dspec = lambda f: pl.BlockSpec((1, 1, TOP_K * tc), f, memory_space=pltpu.SMEM)
    return pl.pallas_call(
        functools.partial(_combine_kernel, tc=tc, np_tiles=np_tiles),
        grid=(ns,),
        in_specs=[dspec(lambda i: (i, 0, 0)), dspec(lambda i: (jnp.minimum(i + 1, ns - 1), 0, 0)),
                  pl.BlockSpec(memory_space=pl.ANY),
                  pl.BlockSpec((tc, TOP_K), lambda i: (i, 0)),
                  pl.BlockSpec((tc, D_MODEL), lambda i: (i, 0)),
                  const((D_MODEL, EXP_HIDDEN)), const((D_MODEL, EXP_HIDDEN)), const((EXP_HIDDEN, D_MODEL)),
                  const((1, D_MODEL)), const((1, D_MODEL))],
        out_specs=(pl.BlockSpec((tc, D_MODEL), lambda i: (jnp.minimum(i, np_tiles - 1), 0)),
                   pl.BlockSpec((tc, D_MODEL), lambda i: (jnp.maximum(i - np_tiles, 0), 0))),
        out_shape=(jax.ShapeDtypeStruct((n_prompt, D_MODEL), F32),
                   jax.ShapeDtypeStruct((n - n_prompt, D_MODEL), F32)),
        scratch_shapes=[pltpu.VMEM((2 * TOP_K * tc * XROW_TILE, LANES), jnp.uint32),
                        pltpu.VMEM((tc, D_MODEL), F32),
                        pltpu.SemaphoreType.DMA((2,))],
        compiler_params=_params("arbitrary"),
        name="combine",
    )(dest, dest, y_sorted, gate, x1, wsg, wsu, wsd, g, b)


def _pack_w_in(w_in):
    d = w_in.shape[0]
    o_z = GDN_CONV_CH
    o_a = o_z + GDN_V_W
    o_b = o_a + GDN_HEADS
    o_q = o_b + GDN_HEADS
    o_c = o_q + MLA_HEADS * (MLA_D_NOPE + MLA_D_ROPE)
    o_kr = o_c + MLA_KV_RANK
    zeros = lambda w: jnp.zeros((d, w), w_in.dtype)
    wq = w_in[:, o_q:o_c].reshape(d, MLA_HEADS, MLA_D_NOPE + MLA_D_ROPE)
    q_nope = wq[:, :, :MLA_D_NOPE].reshape(d, MLA_HEADS * MLA_D_NOPE)
    q_rope = jnp.pad(wq[:, :, MLA_D_NOPE:], ((0, 0), (0, 0), (0, LANES - MLA_D_ROPE))).reshape(d, MLA_HEADS * LANES)
    cols = [w_in[:, :o_a], q_nope, q_rope, w_in[:, o_c:o_kr], w_in[:, o_kr:], zeros(LANES - MLA_D_ROPE),
            w_in[:, o_a:o_q], zeros(LANES - 2 * GDN_HEADS)]
    return jnp.concatenate(cols, axis=1).astype(BF16)


def _rope_tables(pos):
    inv_freq = ROPE_THETA ** (-jnp.arange(0, MLA_D_ROPE, 2, dtype=F32) / MLA_D_ROPE)
    ang = pos.astype(F32)[:, None] * inv_freq[None, :]
    cos, sin = jnp.cos(ang), jnp.sin(ang)
    pad = jnp.zeros((pos.shape[0], LANES - MLA_D_ROPE), F32)
    return jnp.concatenate([cos, cos, pad], axis=1), jnp.concatenate([-sin, sin, pad], axis=1)


FRONT_TILE = 512
GDN_TILE = 4 * CHUNK
ATTN_TILE = 2048
ATTN_SLAB = 512
TOKEN_TILE = 256


def _pick(t, pref):
    return pref if t % pref == 0 else t


def _token_mixers(x, pos, conv_hist, s0, past, wts, shared, cnt0, n_all, row0):
    b, t, _ = x.shape
    cs_tab, sn_tab = _rope_tables(pos)
    tt = _pick(t, FRONT_TILE)
    c = min(CHUNK, t)
    qkv, z, gb, q, k, v, latent, k_rope, conv_new = _front(
        x, wts["w_pack"], wts["conv_w"], conv_hist, wts["gpar"], wts["kvnw"], wts["wukv"], cs_tab, sn_tab, tt, c)
    og, s_new = _gdn(qkv, z, gb, s0, wts["gdn_nw"], _pick(t, GDN_TILE), c)
    if past is None:
        tb = _pick(t, ATTN_TILE)
        om = _attn_prompt(q, k, v, tb, _pick(tb, ATTN_SLAB))
    else:
        om = _attn_sample(q, k, v, past[0], past[1], wts["wukv"])
    n = b * t
    *shared, cnt = _mixln(og.reshape(n, -1), om.reshape(n, -1), x.reshape(n, D_MODEL), wts["w_out"],
                          wts["ln1_g"], wts["ln1_b"], wts["router_w"], wts["router_b"], cnt0,
                          shared, n_all, row0, _pick(n, TOKEN_TILE))
    return shared, cnt, latent, k_rope, s_new, conv_new


def _moe(x1_all, idx, gate, rank, cnt, n_prompt, wts):
    n = x1_all.shape[0]
    counts = cnt[:, 0].astype(I32)
    padded = (counts + MOE_BM - 1) // MOE_BM * MOE_BM
    pend = jnp.cumsum(padded)
    pstart = pend - padded
    td = _pick(math.gcd(n_prompt, n - n_prompt), TOKEN_TILE)
    dest = _dest(idx, rank, pstart.astype(F32).reshape(-1, 1), td)
    dest = dest.reshape(n // td, 1, TOP_K * td)
    n_blocks = n * TOP_K // MOE_BM + N_EXPERTS
    nact = (pend[-1:] // MOE_BM).astype(I32)
    tail = jnp.maximum(pend - MOE_BM, 0).astype(I32)

    xs = _dispatch(tail, dest, x1_all, n_blocks * MOE_BM, td)
    y_sorted = _experts((pstart // MOE_BM).astype(I32), (padded // MOE_BM).astype(I32), nact, xs,
                        wts["exp_wg"], wts["exp_wu"], wts["exp_wd"])
    return _combine(dest, y_sorted, gate.T, x1_all,
                    wts["sh_wg"], wts["sh_wu"], wts["sh_wd"], wts["ln2_g"], wts["ln2_b"], n_prompt, td)


def kernel(x_prompt, x_sample, cache_kv_latent, cache_k_rope, state_gdn, state_conv, w_in, gdn_conv_w, gdn_a_log, gdn_dt_bias, gdn_norm_w, mla_kv_norm_w, mla_w_uk, mla_w_uv, w_out, ln1_g, ln1_b, router_w, router_bias, exp_w_gate, exp_w_up, exp_w_down, shared_w_gate, shared_w_up, shared_w_down, ln2_g, ln2_b):
    assert w_in.shape[0] == 1, "single-layer stack"
    b_p, t_p, _ = x_prompt.shape
    b_s, t_s, _ = x_sample.shape
    past = cache_kv_latent.shape[2]
    l = 0
    pad4 = lambda a: jnp.pad(a.astype(F32), (0, LANES - GDN_HEADS))
    wts = {
        "w_pack": _pack_w_in(w_in[l]),
        "conv_w": gdn_conv_w[l],
        "gpar": jnp.stack([pad4(gdn_a_log[l]), pad4(gdn_dt_bias[l])]),
        "kvnw": mla_kv_norm_w[l].reshape(1, -1),
        "wukv": jnp.concatenate([mla_w_uk[l].reshape(MLA_KV_RANK, -1), mla_w_uv[l].reshape(MLA_KV_RANK, -1)],
                                axis=1).astype(BF16),
        "gdn_nw": gdn_norm_w[l].reshape(1, -1),
        "w_out": w_out[l].astype(BF16),
        "ln1_g": ln1_g[l].reshape(1, -1), "ln1_b": ln1_b[l].reshape(1, -1),
        "router_w": router_w[l].astype(BF16), "router_b": router_bias[l].reshape(-1, 1),
        "exp_wg": exp_w_gate[l], "exp_wu": exp_w_up[l], "exp_wd": exp_w_down[l],
        "sh_wg": shared_w_gate[l].astype(BF16), "sh_wu": shared_w_up[l].astype(BF16),
        "sh_wd": shared_w_down[l].astype(BF16),
        "ln2_g": ln2_g[l].reshape(1, -1), "ln2_b": ln2_b[l].reshape(1, -1),
    }
    n_p, n_s = b_p * t_p, b_s * t_s
    conv0 = jnp.zeros((b_p, GDN_CONV - 1, GDN_CONV_CH), F32)
    s0 = jnp.zeros((b_p, GDN_HEADS, GDN_DK, GDN_DV), F32)
    cnt0 = jnp.zeros((N_EXPERTS, LANES), F32)
    shared, cnt, lat_p, kr_p, sg_p, cv_p = _token_mixers(
        x_prompt, jnp.arange(t_p), conv0, s0, None, wts, None, cnt0, n_p + n_s, 0)
    shared, cnt, lat_s, kr_s, sg_s, cv_s = _token_mixers(
        x_sample, past + jnp.arange(t_s), state_conv[l], state_gdn[l],
        (cache_kv_latent[l], cache_k_rope[l]), wts, shared, cnt, n_p + n_s, n_p)
    y_p, y_s = _moe(*shared, cnt, n_p, wts)
    return (y_p.reshape(b_p, t_p, D_MODEL), y_s.reshape(b_s, t_s, D_MODEL),
            lat_p[None], kr_p[None], sg_p[None], cv_p[None],
            lat_s[None], kr_s[None], sg_s[None], cv_s[None])
```

```python
import functools
import math

import jax
import jax.numpy as jnp
from jax import lax
from jax.experimental import pallas as pl
from jax.experimental.pallas import tpu as pltpu

F32 = jnp.float32
BF16 = jnp.bfloat16
I32 = jnp.int32

D_MODEL = 1024
CHUNK = 64
GDN_HEADS = 4
GDN_DK = 128
GDN_DV = 128
GDN_CONV = 4
GDN_QK_W = GDN_HEADS * GDN_DK
GDN_V_W = GDN_HEADS * GDN_DV
GDN_CONV_CH = 2 * GDN_QK_W + GDN_V_W
MLA_HEADS = 4
MLA_D_NOPE = 128
MLA_D_ROPE = 64
MLA_D_V = 128
MLA_KV_RANK = 256
MLA_SCALE = (MLA_D_NOPE + MLA_D_ROPE) ** -0.5
QK_PRESCALE = MLA_SCALE * math.log2(math.e)
ROPE_THETA = 10000.0
N_EXPERTS = 256
N_GROUPS = 8
GROUP_SIZE = N_EXPERTS // N_GROUPS
TOPK_GROUPS = 4
TOP_K = 8
EXP_HIDDEN = 256
ROUTED_SCALE = 2.5
DEPTH = 1
DEEPNORM_ALPHA = (2.0 * DEPTH) ** 0.25
LN_EPS = 1e-5
RMS_EPS = 1e-6
L2_EPS = 1e-6

LANES = 128
PK_QKV = 0
PK_Z = PK_QKV + GDN_CONV_CH
PK_QNOPE = PK_Z + GDN_V_W
PK_QROPE = PK_QNOPE + MLA_HEADS * MLA_D_NOPE
PK_CKV = PK_QROPE + MLA_HEADS * LANES
PK_KROPE = PK_CKV + MLA_KV_RANK
PK_AB = PK_KROPE + LANES
PK_W = PK_AB + LANES
MLA_QK_W = 2 * LANES

MOE_BM = 256
EXPERT_RING = 8
VMEM_LIMIT = 56 * 1024 * 1024


def _mm(a, b):
    return jnp.dot(a.astype(BF16), b.astype(BF16), preferred_element_type=F32)


def _mm_nt(a, b):
    return lax.dot_general(a.astype(BF16), b.astype(BF16), (((1,), (1,)), ((), ())),
                           preferred_element_type=F32)


def _mm_tn(a, b):
    return lax.dot_general(a.astype(BF16), b.astype(BF16), (((0,), (0,)), ((), ())),
                           preferred_element_type=F32)


def _split3(x):
    hi = x.astype(BF16)
    r = x - hi.astype(F32)
    mid = r.astype(BF16)
    lo = (r - mid.astype(F32)).astype(BF16)
    return hi, mid, lo


def _sigmoid(x):
    return 1.0 / (1.0 + jnp.exp(-x))


def _silu(x):
    return x * _sigmoid(x)


def _softplus(x):
    return jnp.maximum(x, 0.0) + jnp.log1p(jnp.exp(-jnp.abs(x)))


def _rope(x, cs, sn):
    w = x.shape[-1]
    n = w // LANES
    if n > 1:
        cs = jnp.concatenate([cs] * n, axis=1)
        sn = jnp.concatenate([sn] * n, axis=1)
    lane = lax.broadcasted_iota(I32, x.shape, 1) & (LANES - 1)
    half = MLA_D_ROPE // 2
    swapped = jnp.where(lane < half, pltpu.roll(x, w - half, 1), pltpu.roll(x, half, 1))
    return x * cs + swapped * sn


def _params(*sem):
    return pltpu.CompilerParams(dimension_semantics=sem, vmem_limit_bytes=VMEM_LIMIT)


def _front_kernel(x_ref, w_ref, convw_ref, hist_ref, gpar_ref, kvnw_ref, wukv_ref, cs_ref, sn_ref,
                  qkv_ref, z_ref, gb_ref, q_ref, k_ref, v_ref, lat_ref, kr_ref, convnew_ref,
                  xp_scr, *, tt, c):
    t = pl.program_id(1)
    hrow = 8 - (GDN_CONV - 1)

    @pl.when(t == 0)
    def _():
        xp_scr[hrow:8, :] = hist_ref[0]

    proj = _mm(x_ref[0], w_ref[...])

    raw = proj[:, PK_QKV:PK_Z]
    xp_scr[8:8 + tt, :] = raw
    cw = convw_ref[...]
    y = raw * cw[GDN_CONV - 1:GDN_CONV]
    for i in range(GDN_CONV - 1):
        y = y + xp_scr[hrow + i:hrow + i + tt, :] * cw[i:i + 1]
    tail = xp_scr[tt + hrow:tt + 8, :]
    convnew_ref[0] = tail
    xp_scr[hrow:8, :] = tail
    qkv = _silu(y)
    for h in range(2 * GDN_HEADS):
        xh = qkv[:, h * GDN_DK:(h + 1) * GDN_DK]
        xh = xh * lax.rsqrt(jnp.sum(xh * xh, axis=-1, keepdims=True) + L2_EPS)
        if h < GDN_HEADS:
            xh = xh * GDN_DK ** -0.5
        qkv_ref[0, :, h * GDN_DK:(h + 1) * GDN_DK] = xh
    qkv_ref[0, :, 2 * GDN_QK_W:] = qkv[:, 2 * GDN_QK_W:]
    z_ref[0] = proj[:, PK_Z:PK_QNOPE]

    ab = proj[:, PK_AB:PK_W]
    gpar = gpar_ref[...]
    gc = -jnp.exp(gpar[0:1]) * _softplus(ab + gpar[1:2])
    pos = lax.broadcasted_iota(I32, ab.shape, 0) & (c - 1)
    step = 1
    while step < c:
        gc = gc + jnp.where(pos >= step, pltpu.roll(gc, step, 0), 0.0)
        step *= 2
    beta = _sigmoid(ab)
    lane = lax.broadcasted_iota(I32, ab.shape, 1)
    gb_ref[0] = jnp.where(lane < GDN_HEADS, gc, jnp.where(lane < 2 * GDN_HEADS, beta, 0.0))

    cs = cs_ref[...]
    sn = sn_ref[...]
    q_nope = proj[:, PK_QNOPE:PK_QROPE]
    q_rope = _rope(proj[:, PK_QROPE:PK_CKV], cs, sn)
    c_raw = proj[:, PK_CKV:PK_KROPE]
    latent = c_raw * lax.rsqrt(jnp.mean(c_raw * c_raw, axis=-1, keepdims=True) + RMS_EPS) * kvnw_ref[...]
    lat_ref[0] = latent
    k_rope = _rope(proj[:, PK_KROPE:PK_AB], cs, sn)
    kr_ref[0] = k_rope[:, :MLA_D_ROPE]
    kv = _mm(latent, wukv_ref[...])
    k_rope_b = k_rope.astype(BF16)
    for h in range(MLA_HEADS):
        q_ref[0, h, :, :LANES] = (q_nope[:, h * LANES:(h + 1) * LANES] * QK_PRESCALE).astype(BF16)
        q_ref[0, h, :, LANES:] = (q_rope[:, h * LANES:(h + 1) * LANES] * QK_PRESCALE).astype(BF16)
        k_ref[0, h, :, :LANES] = kv[:, h * LANES:(h + 1) * LANES].astype(BF16)
        k_ref[0, h, :, LANES:] = k_rope_b
        v_ref[0, h] = kv[:, (MLA_HEADS + h) * LANES:(MLA_HEADS + h + 1) * LANES].astype(BF16)


def _front(x, w_pack, conv_w, hist, gpar, kvnw, wukv, cs_tab, sn_tab, tt, c):
    b, t, _ = x.shape
    nt = t // tt
    const2 = lambda bi, ti: (0, 0)
    out_shape = (
        jax.ShapeDtypeStruct((b, t, GDN_CONV_CH), F32),
        jax.ShapeDtypeStruct((b, t, GDN_V_W), F32),
        jax.ShapeDtypeStruct((b, t, LANES), F32),
        jax.ShapeDtypeStruct((b, MLA_HEADS, t, MLA_QK_W), BF16),
        jax.ShapeDtypeStruct((b, MLA_HEADS, t, MLA_QK_W), BF16),
        jax.ShapeDtypeStruct((b, MLA_HEADS, t, MLA_D_V), BF16),
        jax.ShapeDtypeStruct((b, t, MLA_KV_RANK), F32),
        jax.ShapeDtypeStruct((b, t, MLA_D_ROPE), F32),
        jax.ShapeDtypeStruct((b, GDN_CONV - 1, GDN_CONV_CH), F32),
    )
    row3 = lambda w: pl.BlockSpec((1, tt, w), lambda bi, ti: (bi, ti, 0))
    head4 = lambda w: pl.BlockSpec((1, MLA_HEADS, tt, w), lambda bi, ti: (bi, 0, ti, 0))
    return pl.pallas_call(
        functools.partial(_front_kernel, tt=tt, c=c),
        grid=(b, nt),
        in_specs=[
            row3(D_MODEL),
            pl.BlockSpec((D_MODEL, PK_W), const2),
            pl.BlockSpec((GDN_CONV, GDN_CONV_CH), const2),
            pl.BlockSpec((1, GDN_CONV - 1, GDN_CONV_CH), lambda bi, ti: (bi, 0, 0)),
            pl.BlockSpec((2, LANES), const2),
            pl.BlockSpec((1, MLA_KV_RANK), const2),
            pl.BlockSpec((MLA_KV_RANK, 2 * MLA_HEADS * LANES), const2),
            pl.BlockSpec((tt, LANES), lambda bi, ti: (ti, 0)),
            pl.BlockSpec((tt, LANES), lambda bi, ti: (ti, 0)),
        ],
        out_specs=(
            row3(GDN_CONV_CH), row3(GDN_V_W), row3(LANES),
            head4(MLA_QK_W), head4(MLA_QK_W), head4(MLA_D_V),
            row3(MLA_KV_RANK), row3(MLA_D_ROPE),
            pl.BlockSpec((1, GDN_CONV - 1, GDN_CONV_CH), lambda bi, ti: (bi, 0, 0)),
        ),
        out_shape=out_shape,
        scratch_shapes=[pltpu.VMEM((tt + 8, GDN_CONV_CH), F32)],
        compiler_params=_params("arbitrary", "arbitrary"),
        name="front",
    )(x, w_pack, conv_w, hist, gpar, kvnw, wukv, cs_tab, sn_tab)


def _gdn_kernel(qkv_ref, z_ref, gb_ref, s0_ref, nw_ref, og_ref, sout_ref, s_scr, *, tg, c, bb):
    t = pl.program_id(1)
    nh = GDN_HEADS
    r = nh * c
    sh = c.bit_length() - 1

    @pl.when(t == 0)
    def _():
        s_scr[...] = s0_ref[...]

    row = lax.broadcasted_iota(I32, (r, r), 0)
    col = lax.broadcasted_iota(I32, (r, r), 1)
    same = (row >> sh) == (col >> sh)
    incl = same & (row >= col)
    strict = same & (row > col)
    eye = jnp.where(row == col, 1.0, 0.0)
    lane0 = jnp.where(lax.broadcasted_iota(I32, (r, LANES), 1) == 0, 1.0, 0.0).astype(BF16)
    nw = nw_ref[...]
    items = [(bi, ci) for ci in range(tg // c) for bi in range(bb)]
    chunks = range(len(items))

    def stacked(ref, it, base):
        bi, ci = items[it]
        return jnp.concatenate(
            [ref[bi, ci * c:(ci + 1) * c, base + h * LANES:base + (h + 1) * LANES] for h in range(nh)], axis=0)

    def col_bcast(it, lane):
        bi, ci = items[it]
        gbc = gb_ref[bi, ci * c:(ci + 1) * c, :]
        return jnp.concatenate(
            [jnp.broadcast_to(gbc[:, lane + h:lane + h + 1], (c, LANES)) for h in range(nh)], axis=0)

    def as_col(gc_b):
        return _lane_tile(gc_b, r) if r % LANES == 0 else gc_b[:, :r]

    def as_row(gc_b):
        if r % LANES == 0:
            return as_col(gc_b).T
        return sum(lax.dot_general(lane0, p, (((1,), (1,)), ((), ())), preferred_element_type=F32)
                   for p in _split3(gc_b))

    ks = [stacked(qkv_ref, ci, GDN_QK_W) for ci in chunks]
    gc = [col_bcast(ci, 0) for ci in chunks]
    beta = [col_bcast(ci, nh) for ci in chunks]
    decay, qk_kk = [], []
    for ci in chunks:
        decay.append(jnp.exp(jnp.where(incl, as_col(gc[ci]) - as_row(gc[ci]), -jnp.inf)))
        qk_kk.append(_mm_nt(jnp.concatenate([stacked(qkv_ref, ci, 0), ks[ci]], axis=0), ks[ci]))
    intra = [qk_kk[ci][:r] * decay[ci] for ci in chunks]
    n_pow = [jnp.where(strict, -as_col(beta[ci]) * qk_kk[ci][r:] * decay[ci], 0.0)
             for ci in chunks]
    t_inv = [eye + n_pow[ci] for ci in chunks]
    for _ in range(sh - 1):
        n_pow = [_mm(n_pow[ci], n_pow[ci]) for ci in chunks]
        t_inv = [t_inv[ci] + _mm(t_inv[ci], n_pow[ci]) for ci in chunks]
    egc = [jnp.exp(gc[ci]) for ci in chunks]
    uw = [_mm(t_inv[ci], jnp.concatenate([stacked(qkv_ref, ci, 2 * GDN_QK_W) * beta[ci],
                                          ks[ci] * beta[ci] * egc[ci]], axis=1)) for ci in chunks]

    for ci in chunks:
        bi, cpos = items[ci]
        u = uw[ci][:, :GDN_DV]
        w = uw[ci][:, GDN_DV:]
        qd = stacked(qkv_ref, ci, 0) * egc[ci]
        vn, qs_s = [], []
        for h in range(nh):
            hs = slice(h * c, (h + 1) * c)
            s_h = s_scr[bi, h]
            wq = _mm(jnp.concatenate([w[hs], qd[hs]], axis=0), s_h)
            vn_h = u[hs] - wq[:c]
            g_last = gc[ci][h * c + c - 1:h * c + c, :]
            kd = ks[ci][hs] * jnp.exp(g_last - gc[ci][hs])
            s_scr[bi, h] = s_h * jnp.exp(g_last) + _mm_tn(kd, vn_h)
            vn.append(vn_h)
            qs_s.append(wq[c:])
        o = jnp.concatenate(qs_s, axis=0) + _mm(intra[ci], jnp.concatenate(vn, axis=0))

        o = o * lax.rsqrt(jnp.mean(o * o, axis=-1, keepdims=True) + RMS_EPS) * nw
        o = o * _silu(stacked(z_ref, ci, 0))
        for h in range(nh):
            og_ref[bi, cpos * c:(cpos + 1) * c, h * LANES:(h + 1) * LANES] = o[h * c:(h + 1) * c].astype(BF16)

    @pl.when(t == pl.num_programs(1) - 1)
    def _():
        sout_ref[...] = s_scr[...]


def _gdn(qkv, z, gb, s0, nw, tg, c):
    b, t, _ = qkv.shape
    bb = 2 if b % 2 == 0 else 1
    row3 = lambda w: pl.BlockSpec((bb, tg, w), lambda bi, ti: (bi, ti, 0))
    st = pl.BlockSpec((bb, GDN_HEADS, GDN_DK, GDN_DV), lambda bi, ti: (bi, 0, 0, 0))
    return pl.pallas_call(
        functools.partial(_gdn_kernel, tg=tg, c=c, bb=bb),
        grid=(b // bb, t // tg),
        in_specs=[row3(GDN_CONV_CH), row3(GDN_V_W), row3(LANES), st,
                  pl.BlockSpec((1, GDN_DV), lambda bi, ti: (0, 0))],
        out_specs=(row3(GDN_V_W), st),
        out_shape=(jax.ShapeDtypeStruct((b, t, GDN_V_W), BF16),
                   jax.ShapeDtypeStruct((b, GDN_HEADS, GDN_DK, GDN_DV), F32)),
        scratch_shapes=[pltpu.VMEM((bb, GDN_HEADS, GDN_DK, GDN_DV), F32)],
        compiler_params=_params("arbitrary", "arbitrary"),
        name="gdn",
    )(qkv, z, gb, s0, nw)


ATTN_ROW_BLOCK = 32


def _lane_tile(x, width):
    return x if width == LANES else jnp.concatenate([x] * (width // LANES), axis=1)


def _attn_kernel(qi_ref, ki_ref, q_ref, k_ref, v_ref, o_ref, m_scr, l_scr, acc_scr, a_scr, s_scr, p_scr,
                 *, tb, sub):
    step = pl.program_id(2)
    qi = qi_ref[step]
    ki = ki_ref[step]
    csh = CHUNK.bit_length() - 1

    @pl.when(ki == 0)
    def _():
        m_scr[...] = jnp.full(m_scr.shape, -jnp.inf, F32)
        l_scr[...] = jnp.zeros(l_scr.shape, F32)
        acc_scr[...] = jnp.zeros(acc_scr.shape, F32)

    half = tb // 2
    rb = min(ATTN_ROW_BLOCK, half)

    def scores(h, j):
        rows = slice(h * half, (h + 1) * half)
        s_scr[rows, :] = lax.dot_general(q_ref[0, 0, rows, :], k_ref[0, 0, j * sub:(j + 1) * sub, :],
                                         (((1,), (1,)), ((), ())), preferred_element_type=F32)

    def softmax(h, j, masked):
        r0 = h * half

        def block(i):
            rr = pl.ds(pl.multiple_of(r0 + i * rb, rb), rb)
            s = s_scr[rr, :]
            if masked:
                qc = (r0 + i * rb + lax.broadcasted_iota(I32, s.shape, 0)) >> csh
                kc = (j * sub + lax.broadcasted_iota(I32, s.shape, 1)) >> csh
                s = jnp.where(kc <= qc, s, -jnp.inf)
            return rr, s

        def row_max(i, carry):
            rr, s = block(i)
            m_prev = m_scr[rr, :]
            m_new = jnp.maximum(m_prev, jnp.max(s, axis=-1, keepdims=True))
            a_scr[rr, :] = jnp.exp2(m_prev - m_new)
            m_scr[rr, :] = m_new
            return carry

        def row_exp(i, carry):
            rr, s = block(i)
            p = jnp.exp2(s - _lane_tile(m_scr[rr, :], sub))
            l_scr[rr, :] = a_scr[rr, :] * l_scr[rr, :] + jnp.sum(p, axis=-1, keepdims=True)
            p_scr[rr, :] = p.astype(BF16)
            return carry

        for i in range(half // rb):
            row_max(i, 0)
        for i in range(half // rb):
            row_exp(i, 0)

    def weighted_values(h, j):
        rows = slice(h * half, (h + 1) * half)
        acc_scr[rows, :] = a_scr[rows, :] * acc_scr[rows, :] + jnp.dot(
            p_scr[rows, :], v_ref[0, 0, j * sub:(j + 1) * sub, :], preferred_element_type=F32)

    def run(tasks):
        for h in range(2):
            if tasks[h]:
                scores(h, tasks[h][0][0])
        for n in range(max(len(t) for t in tasks)):
            for h in range(2):
                if n < len(tasks[h]):
                    j, masked = tasks[h][n]
                    softmax(h, j, masked)
                    weighted_values(h, j)
                    if n + 1 < len(tasks[h]):
                        scores(h, tasks[h][n + 1][0])

    def diagonal_tasks(h):
        r0 = h * half
        out = []
        for j in range(tb // sub):
            k0, k1 = j * sub, (j + 1) * sub
            if k0 >= r0 + half:
                continue
            out.append((j, k1 > r0 + CHUNK))
        return out

    @pl.when(ki < qi)
    def _():
        run([[(j, False) for j in range(tb // sub)]] * 2)

    @pl.when(ki == qi)
    def _():
        run([diagonal_tasks(0), diagonal_tasks(1)])
        o_ref[0] = (acc_scr[...] / l_scr[...]).astype(BF16)


def _attn_prompt(q, k, v, tb, sub):
    b, nh, t, _ = q.shape
    nt = t // tb
    pairs = [(qi, ki) for qi in range(nt) for ki in range(qi + 1)]
    qi_of = jnp.asarray([p[0] for p in pairs], I32)
    ki_of = jnp.asarray([p[1] for p in pairs], I32)
    return pl.pallas_call(
        functools.partial(_attn_kernel, tb=tb, sub=sub),
        grid_spec=pltpu.PrefetchScalarGridSpec(
            num_scalar_prefetch=2,
            grid=(b, nh, len(pairs)),
            in_specs=[pl.BlockSpec((1, 1, tb, MLA_QK_W), lambda bi, hi, s, qo, ko: (bi, hi, qo[s], 0)),
                      pl.BlockSpec((1, 1, tb, MLA_QK_W), lambda bi, hi, s, qo, ko: (bi, hi, ko[s], 0)),
                      pl.BlockSpec((1, 1, tb, MLA_D_V), lambda bi, hi, s, qo, ko: (bi, hi, ko[s], 0))],
            out_specs=pl.BlockSpec((1, tb, MLA_D_V), lambda bi, hi, s, qo, ko: (bi, qo[s], hi)),
            scratch_shapes=[pltpu.VMEM((tb, LANES), F32), pltpu.VMEM((tb, LANES), F32),
                            pltpu.VMEM((tb, MLA_D_V), F32), pltpu.VMEM((tb, LANES), F32),
                            pltpu.VMEM((tb, sub), F32), pltpu.VMEM((tb, sub), BF16)]),
        out_shape=jax.ShapeDtypeStruct((b, t, MLA_HEADS * MLA_D_V), BF16),
        compiler_params=_params("arbitrary", "arbitrary", "arbitrary"),
        name="attn_prompt",
    )(qi_of, ki_of, q, k, v)


def _attn_sample_kernel(q_ref, kn_ref, vn_ref, plat_ref, pkr_ref, wukv_ref, o_ref):
    plat = plat_ref[0].astype(BF16)
    pkr = pkr_ref[0].astype(BF16)
    ts = q_ref.shape[2]
    heads = range(MLA_HEADS)
    w_uk = lambda h: wukv_ref[:, h * LANES:(h + 1) * LANES]
    w_uv = lambda h: wukv_ref[:, (MLA_HEADS + h) * LANES:(MLA_HEADS + h + 1) * LANES]
    q_lat = jnp.concatenate([_mm_nt(q_ref[0, h][:, :MLA_D_NOPE], w_uk(h)) for h in heads], axis=0)
    q_rope = jnp.concatenate([q_ref[0, h][:, MLA_D_NOPE:MLA_D_NOPE + MLA_D_ROPE] for h in heads], axis=0)
    s_past = _mm_nt(q_lat, plat) + _mm_nt(q_rope, pkr)
    s_new = jnp.concatenate([_mm_nt(q_ref[0, h], kn_ref[0, h]) for h in heads], axis=0)
    m = jnp.maximum(jnp.max(s_past, axis=-1, keepdims=True), jnp.max(s_new, axis=-1, keepdims=True))
    p_past = jnp.exp2(s_past - m)
    p_new = jnp.exp2(s_new - m)
    l = jnp.sum(p_past, axis=-1, keepdims=True) + jnp.sum(p_new, axis=-1, keepdims=True)
    ctx = _mm(p_past, plat)
    for h in heads:
        rows = slice(h * ts, (h + 1) * ts)
        o = _mm(ctx[rows], w_uv(h)) + _mm(p_new[rows], vn_ref[0, h])
        o_ref[0, :, h * MLA_D_V:(h + 1) * MLA_D_V] = (o / l[rows]).astype(BF16)


def _attn_sample(q, k_new, v_new, past_lat, past_kr, wukv):
    b, nh, ts, _ = q.shape
    past = past_lat.shape[1]
    b4 = lambda w: pl.BlockSpec((1, nh, ts, w), lambda bi: (bi, 0, 0, 0))
    return pl.pallas_call(
        _attn_sample_kernel,
        grid=(b,),
        in_specs=[b4(MLA_QK_W), b4(MLA_QK_W), b4(MLA_D_V),
                  pl.BlockSpec((1, past, MLA_KV_RANK), lambda bi: (bi, 0, 0)),
                  pl.BlockSpec((1, past, MLA_D_ROPE), lambda bi: (bi, 0, 0)),
                  pl.BlockSpec((MLA_KV_RANK, 2 * MLA_HEADS * LANES), lambda bi: (0, 0))],
        out_specs=pl.BlockSpec((1, ts, MLA_HEADS * MLA_D_V), lambda bi: (bi, 0, 0)),
        out_shape=jax.ShapeDtypeStruct((b, ts, MLA_HEADS * MLA_D_V), BF16),
        compiler_params=_params("arbitrary"),
        name="attn_sample",
    )(q, k_new, v_new, past_lat, past_kr, wukv)


def _layernorm(y, g, b):
    mu = jnp.mean(y, axis=-1, keepdims=True)
    d = y - mu
    var = jnp.mean(d * d, axis=-1, keepdims=True)
    return d * lax.rsqrt(var + LN_EPS) * g + b


N_SHARED = 4


def _mixln_kernel(og_ref, om_ref, x_ref, w_ref, g_ref, b_ref, rw_ref, rb_ref, cnt0_ref, *rest, tm):
    o_ref, idx_ref, gate_ref, rank_ref, cnt_ref, carry_scr = rest[-6:]

    @pl.when(pl.program_id(0) == 0)
    def _():
        carry_scr[...] = cnt0_ref[...]

    mix = (jnp.dot(og_ref[...], w_ref[:GDN_V_W, :], preferred_element_type=F32)
           + jnp.dot(om_ref[...], w_ref[GDN_V_W:, :], preferred_element_type=F32))
    x1 = _layernorm(DEEPNORM_ALPHA * x_ref[...] + mix, g_ref[...], b_ref[...])
    o_ref[...] = x1
    _route_tile(x1, rw_ref, rb_ref, carry_scr, idx_ref, gate_ref, rank_ref, tm)
    cnt_ref[...] = carry_scr[...]


def _mixln(og, om, x, w_out, g, b, rw, rb, cnt0, shared, n_all, row0, tm):
    n = x.shape[0]
    blk0 = row0 // tm
    rows = lambda w: pl.BlockSpec((tm, w), lambda i: (i, 0))
    const = lambda s: pl.BlockSpec(s, lambda i: (0, 0))
    kspec = pl.BlockSpec((TOP_K, tm), lambda i: (0, blk0 + i))
    first = shared is None
    n_in = 9
    kt = lambda dt: jax.ShapeDtypeStruct((TOP_K, n_all), dt)
    return pl.pallas_call(
        functools.partial(_mixln_kernel, tm=tm),
        grid=(n // tm,),
        in_specs=[rows(GDN_V_W), rows(MLA_HEADS * MLA_D_V), rows(D_MODEL),
                  const((D_MODEL, D_MODEL)), const((1, D_MODEL)), const((1, D_MODEL)),
                  const((D_MODEL, N_EXPERTS)), const((N_EXPERTS, 1)), const((N_EXPERTS, LANES))]
                 + ([] if first else [pl.BlockSpec(memory_space=pl.ANY)] * N_SHARED),
        out_specs=(pl.BlockSpec((tm, D_MODEL), lambda i: (blk0 + i, 0)), kspec, kspec, kspec,
                   const((N_EXPERTS, LANES))),
        out_shape=(jax.ShapeDtypeStruct((n_all, D_MODEL), F32), kt(I32), kt(F32), kt(I32),
                   jax.ShapeDtypeStruct((N_EXPERTS, LANES), F32)),
        scratch_shapes=[pltpu.VMEM((N_EXPERTS, LANES), F32)],
        input_output_aliases={} if first else {n_in + j: j for j in range(N_SHARED)},
        compiler_params=_params("arbitrary"),
        name="mixln",
    )(*((og, om, x, w_out, g, b, rw, rb, cnt0) + (() if first else tuple(shared))))


def _route_tile(x, rw_ref, rb_ref, carry_scr, idx_ref, gate_ref, rank_ref, tt):
    ninf = -jnp.inf
    big = float(2 * N_EXPERTS)
    scores = _sigmoid(_mm(x, rw_ref[...]).T)
    biased = scores + rb_ref[...]
    eio = lax.broadcasted_iota(I32, (N_EXPERTS, tt), 0).astype(F32)

    def first_argmax(vals, io):
        m = jnp.max(vals, axis=0, keepdims=True)
        i = jnp.min(jnp.where(vals == m, io, big), axis=0, keepdims=True)
        return m, i

    gs = []
    for g in range(N_GROUPS):
        blk = biased[g * GROUP_SIZE:(g + 1) * GROUP_SIZE]
        io = (lax.broadcasted_iota(I32, (GROUP_SIZE, tt), 0) + g * GROUP_SIZE).astype(F32)
        m1, i1 = first_argmax(blk, io)
        m2 = jnp.max(jnp.where(io == i1, ninf, blk), axis=0, keepdims=True)
        gs.append(m1 + m2)
    gio = lax.broadcasted_iota(I32, (N_GROUPS, tt), 0).astype(F32)
    gsc = jnp.zeros((N_GROUPS, tt), F32)
    for g in range(N_GROUPS):
        gsc = jnp.where(gio == float(g), gs[g], gsc)
    gsel = jnp.zeros((N_GROUPS, tt), F32)
    for _ in range(TOPK_GROUPS):
        _, gi = first_argmax(gsc, gio)
        hit = gio == gi
        gsel = jnp.where(hit, 1.0, gsel)
        gsc = jnp.where(hit, ninf, gsc)
    masked = jnp.concatenate(
        [jnp.where(jnp.max(jnp.where(gio == float(g), gsel, 0.0), axis=0, keepdims=True) > 0.0,
                   biased[g * GROUP_SIZE:(g + 1) * GROUP_SIZE], ninf) for g in range(N_GROUPS)], axis=0)

    idx, wts = [], []
    sel = jnp.zeros((N_EXPERTS, tt), F32)
    for _ in range(TOP_K):
        _, ei = first_argmax(masked, eio)
        hit = eio == ei
        wts.append(jnp.sum(jnp.where(hit, scores, 0.0), axis=0, keepdims=True))
        masked = jnp.where(hit, ninf, masked)
        sel = jnp.where(hit, 1.0, sel)
        idx.append(ei)
    wsum = wts[0]
    for w in wts[1:]:
        wsum = wsum + w

    t0 = lax.broadcasted_iota(I32, (tt, tt), 0)
    t1 = lax.broadcasted_iota(I32, (tt, tt), 1)
    before = jnp.where(t0 < t1, 1.0, 0.0).astype(BF16)
    sel_b = sel.astype(BF16)
    base = carry_scr[:, :1] + jnp.dot(sel_b, before, preferred_element_type=F32)
    ranks = [jnp.sum(jnp.where(eio == ei, base, 0.0), axis=0, keepdims=True) for ei in idx]
    carry_scr[...] = carry_scr[...] + jnp.dot(sel_b, jnp.ones((tt, LANES), BF16), preferred_element_type=F32)

    for k in range(TOP_K):
        idx_ref[k:k + 1, :] = idx[k].astype(I32)
        gate_ref[k:k + 1, :] = wts[k] / wsum * ROUTED_SCALE
        rank_ref[k:k + 1, :] = ranks[k].astype(I32)


def _dest_kernel(idx_ref, rank_ref, pstart_ref, dest_ref, *, tt, per_step):
    eio = lax.broadcasted_iota(I32, (N_EXPERTS, tt), 0)
    pstart = pstart_ref[...]
    for j in range(per_step):
        cols = slice(j * tt, (j + 1) * tt)
        for k in range(TOP_K):
            start = jnp.sum(jnp.where(eio == idx_ref[k:k + 1, cols], pstart, 0.0), axis=0, keepdims=True)
            dest_ref[j, k:k + 1, :] = start.astype(I32) + rank_ref[k:k + 1, cols]


def _dest(idx, rank, pstart, tt):
    n = idx.shape[1]
    per_step = next(s for s in (4, 3, 2, 1) if (n // tt) % s == 0)
    kspec = pl.BlockSpec((TOP_K, per_step * tt), lambda i: (0, i))
    return pl.pallas_call(
        functools.partial(_dest_kernel, tt=tt, per_step=per_step),
        grid=(n // (tt * per_step),),
        in_specs=[kspec, kspec, pl.BlockSpec((N_EXPERTS, 1), lambda i: (0, 0))],
        out_specs=pl.BlockSpec((per_step, TOP_K, tt), lambda i: (i, 0, 0)),
        out_shape=jax.ShapeDtypeStruct((n // tt, TOP_K, tt), I32),
        compiler_params=_params("arbitrary"),
        name="dest",
    )(idx, rank, pstart)


ROW_TILE = D_MODEL // LANES
XROW_TILE = ROW_TILE // 2


def _rows_to_tiles(x, tiles_ref, base, n, rt=ROW_TILE):
    for j in range(rt):
        tiles_ref[pl.ds(base * rt + j, n, stride=rt), :] = x[:, j * LANES:(j + 1) * LANES]


def _tiles_to_rows(tiles_ref, base, n, rt=ROW_TILE):
    return jnp.concatenate(
        [tiles_ref[pl.ds(base * rt + j, n, stride=rt), :] for j in range(rt)], axis=1)


def _pack_bf16_pairs(x):
    half = x.shape[1] // 2
    lo = pltpu.bitcast(x[:, :half].astype(BF16).astype(F32), jnp.uint32) >> 16
    hi = pltpu.bitcast(x[:, half:].astype(BF16).astype(F32), jnp.uint32) & jnp.uint32(0xFFFF0000)
    return lo | hi


def _unpack_bf16_pairs(w):
    lo = pltpu.bitcast(w << 16, F32)
    hi = pltpu.bitcast(w & jnp.uint32(0xFFFF0000), F32)
    return jnp.concatenate([lo, hi], axis=1)


def _dispatch_kernel(tail_ref, dest_ref, x_ref, xs_out, zbuf, xt_scr, sem, zsem, *, td):
    s = pl.program_id(0)

    @pl.when(s == 0)
    def _():
        zbuf[...] = jnp.zeros(zbuf.shape, zbuf.dtype)

        def zero_copy(e):
            first = pl.multiple_of(tail_ref[e] * XROW_TILE, MOE_BM * XROW_TILE)
            return pltpu.make_async_copy(zbuf, xs_out.at[pl.ds(first, MOE_BM * XROW_TILE)], zsem)

        def zstart(e, carry):
            zero_copy(e).start()
            return carry

        def zwait(e, carry):
            zero_copy(e).wait()
            return carry

        lax.fori_loop(0, N_EXPERTS, zstart, 0)
        lax.fori_loop(0, N_EXPERTS, zwait, 0)

    slot = s & 1

    def row_copy(slot_, i, d):
        src = pl.multiple_of((slot_ * td + i) * XROW_TILE, XROW_TILE)
        return pltpu.make_async_copy(xt_scr.at[pl.ds(src, XROW_TILE)],
                                     xs_out.at[pl.ds(pl.multiple_of(d * XROW_TILE, XROW_TILE), XROW_TILE)],
                                     sem.at[slot_])

    def drain(slot_):
        step_rows = TOP_K * td * XROW_TILE
        pltpu.make_async_copy(xs_out.at[pl.ds(0, step_rows)], xs_out.at[pl.ds(0, step_rows)],
                              sem.at[slot_]).wait()

    @pl.when(s >= 2)
    def _():
        drain(slot)

    _rows_to_tiles(_pack_bf16_pairs(x_ref[...]), xt_scr, slot * td, td, XROW_TILE)

    def issue(i, carry):
        for k in range(TOP_K):
            row_copy(slot, i, dest_ref[0, 0, k * td + i]).start(priority=k % 2)
        return carry

    lax.fori_loop(0, td, issue, 0, unroll=4)

    @pl.when(s == pl.num_programs(0) - 1)
    def _():
        drain(slot)

        @pl.when(s >= 1)
        def _():
            drain(1 - slot)


def _dispatch(tail, dest, x1, n_rows, td):
    n = x1.shape[0]
    return pl.pallas_call(
        functools.partial(_dispatch_kernel, td=td),
        grid_spec=pltpu.PrefetchScalarGridSpec(
            num_scalar_prefetch=1,
            grid=(n // td,),
            in_specs=[pl.BlockSpec((1, 1, TOP_K * td), lambda i, tl: (i, 0, 0), memory_space=pltpu.SMEM),
                      pl.BlockSpec((td, D_MODEL), lambda i, tl: (i, 0))],
            out_specs=pl.BlockSpec(memory_space=pl.ANY),
            scratch_shapes=[pltpu.VMEM((MOE_BM * XROW_TILE, LANES), jnp.uint32),
                            pltpu.VMEM((2 * td * XROW_TILE, LANES), jnp.uint32),
                            pltpu.SemaphoreType.DMA((2,)), pltpu.SemaphoreType.DMA(())]),
        out_shape=jax.ShapeDtypeStruct((n_rows * XROW_TILE, LANES), jnp.uint32),
        compiler_params=_params("arbitrary"),
        name="dispatch",
    )(tail, dest, x1)


def _expert_kernel(first_ref, nblk_ref, nact_ref, xs_hbm, wg_ref, wu_ref, wd_ref, y_hbm,
                   xbuf, ybuf, wgb, wub, wdb, xsem, ysem):
    e = pl.program_id(0)
    nact = nact_ref[0]
    ring = EXPERT_RING
    ahead = ring - 2
    xr = MOE_BM * XROW_TILE
    yr = MOE_BM * XROW_TILE

    def x_copy(g):
        slot = g & (ring - 1)
        return pltpu.make_async_copy(xs_hbm.at[pl.ds(pl.multiple_of(g * xr, xr), xr)],
                                     xbuf.at[pl.ds(pl.multiple_of(slot * xr, xr), xr)], xsem.at[slot])

    def y_copy(g):
        slot = g & (ring - 1)
        return pltpu.make_async_copy(ybuf.at[pl.ds(pl.multiple_of(slot * yr, yr), yr)],
                                     y_hbm.at[pl.ds(pl.multiple_of(g * yr, yr), yr)], ysem.at[slot])

    @pl.when(e == 0)
    def _():
        for g0 in range(ahead):
            @pl.when(g0 < nact)
            def _():
                x_copy(g0).start()

    @pl.when(nblk_ref[e] > 0)
    def _():
        wgb[...] = wg_ref[0].astype(BF16)
        wub[...] = wu_ref[0].astype(BF16)
        wdb[...] = wd_ref[0].astype(BF16)

    def blocks(g0, nb):
        gs = [g0 + j for j in range(nb)]
        for g in gs:
            x_copy(g).wait()

            @pl.when(g + ahead < nact)
            def _():
                x_copy(g + ahead).start()

            @pl.when(g >= ring)
            def _():
                y_copy(g - ring).wait()

        xb = jnp.concatenate(
            [_unpack_bf16_pairs(_tiles_to_rows(xbuf, (g & (ring - 1)) * MOE_BM, MOE_BM, XROW_TILE)) for g in gs],
            axis=0).astype(BF16)
        hg = jnp.dot(xb, wgb[...], preferred_element_type=F32)
        hu = jnp.dot(xb, wub[...], preferred_element_type=F32)
        y = jnp.dot((_silu(hg) * hu).astype(BF16), wdb[...], preferred_element_type=F32)
        yp = _pack_bf16_pairs(y)
        for j, g in enumerate(gs):
            _rows_to_tiles(yp[j * MOE_BM:(j + 1) * MOE_BM], ybuf, (g & (ring - 1)) * MOE_BM, MOE_BM, XROW_TILE)
            y_copy(g).start()

        g_last = gs[-1]

        @pl.when(g_last == nact - 1)
        def _():
            for back in range(ring):
                @pl.when(g_last >= back)
                def _():
                    y_copy(g_last - back).wait()

    nblk = nblk_ref[e]

    def pair(b, carry):
        blocks(first_ref[e] + 2 * b, 2)
        return carry

    lax.fori_loop(0, nblk // 2, pair, 0)

    @pl.when(nblk % 2 == 1)
    def _():
        blocks(first_ref[e] + nblk - 1, 1)


def _experts(first, nblk, nact, xs, wg, wu, wd):
    n_rows = xs.shape[0] // XROW_TILE
    wspec = lambda s: pl.BlockSpec((1,) + s, lambda e, fr, nb, na: (e, 0, 0))
    return pl.pallas_call(
        _expert_kernel,
        grid_spec=pltpu.PrefetchScalarGridSpec(
            num_scalar_prefetch=3,
            grid=(N_EXPERTS,),
            in_specs=[pl.BlockSpec(memory_space=pl.ANY),
                      wspec((D_MODEL, EXP_HIDDEN)), wspec((D_MODEL, EXP_HIDDEN)), wspec((EXP_HIDDEN, D_MODEL))],
            out_specs=pl.BlockSpec(memory_space=pl.ANY),
            scratch_shapes=[pltpu.VMEM((EXPERT_RING * MOE_BM * XROW_TILE, LANES), jnp.uint32),
                            pltpu.VMEM((EXPERT_RING * MOE_BM * XROW_TILE, LANES), jnp.uint32),
                            pltpu.VMEM((D_MODEL, EXP_HIDDEN), BF16), pltpu.VMEM((D_MODEL, EXP_HIDDEN), BF16),
                            pltpu.VMEM((EXP_HIDDEN, D_MODEL), BF16),
                            pltpu.SemaphoreType.DMA((EXPERT_RING,)), pltpu.SemaphoreType.DMA((EXPERT_RING,))]),
        out_shape=jax.ShapeDtypeStruct((n_rows * XROW_TILE, LANES), jnp.uint32),
        compiler_params=_params("arbitrary"),
        name="experts",
    )(first, nblk, nact, xs, wg, wu, wd)


def _combine_kernel(dcur_ref, dnxt_ref, y_hbm, gate_ref, x_ref, wsg_ref, wsu_ref, wsd_ref, g_ref, b_ref,
                    outp_ref, outs_ref, buf, routed_scr, sem, *, tc, np_tiles):
    s = pl.program_id(0)
    ns = pl.num_programs(0)
    slot = s % 2

    def row_copy(d, slot_, k, i):
        dst = pl.multiple_of(((slot_ * TOP_K + k) * tc + i) * XROW_TILE, XROW_TILE)
        return pltpu.make_async_copy(y_hbm.at[pl.ds(pl.multiple_of(d * XROW_TILE, XROW_TILE), XROW_TILE)],
                                     buf.at[pl.ds(dst, XROW_TILE)], sem.at[slot_])

    def issue(dref, slot_):
        def body(i, carry):
            for k in range(TOP_K):
                row_copy(dref[0, 0, k * tc + i], slot_, k, i).start(priority=k % 2)
            return carry
        lax.fori_loop(0, tc, body, 0, unroll=4)

    @pl.when(s == 0)
    def _():
        issue(dcur_ref, 0)

    slot_rows = TOP_K * tc * XROW_TILE
    pltpu.make_async_copy(y_hbm.at[pl.ds(0, slot_rows)],
                          buf.at[pl.ds(pl.multiple_of(slot * slot_rows, slot_rows), slot_rows)],
                          sem.at[slot]).wait()

    grp = 16

    def gated_sum(gi):
        base = pl.multiple_of(gi * grp, grp)
        gate = gate_ref[pl.ds(base, grp), :]
        acc = None
        for k in range(TOP_K):
            rows = _unpack_bf16_pairs(_tiles_to_rows(buf, (slot * TOP_K + k) * tc + base, grp, XROW_TILE))
            term = rows * gate[:, k:k + 1]
            acc = term if acc is None else acc + term
        routed_scr[pl.ds(base, grp), :] = acc

    def sum_and_issue(gi, carry):
        for i in range(grp):
            for k in range(TOP_K):
                row_copy(dnxt_ref[0, 0, k * tc + gi * grp + i], 1 - slot, k, gi * grp + i).start(priority=k % 2)
        gated_sum(gi)
        return carry

    def sum_only(gi, carry):
        gated_sum(gi)
        return carry

    @pl.when(s + 1 < ns)
    def _():
        lax.fori_loop(0, tc // grp, sum_and_issue, 0)

    @pl.when(s + 1 == ns)
    def _():
        lax.fori_loop(0, tc // grp, sum_only, 0)

    x = x_ref[...]
    routed = routed_scr[...]
    xb = x.astype(BF16)
    shared = _mm(_silu(_mm(xb, wsg_ref[...])) * _mm(xb, wsu_ref[...]), wsd_ref[...])
    out = _layernorm(DEEPNORM_ALPHA * x + (routed + shared), g_ref[...], b_ref[...])

    @pl.when(s < np_tiles)
    def _():
        outp_ref[...] = out

    @pl.when(s >= np_tiles)
    def _():
        outs_ref[...] = out


def _combine(dest, y_sorted, gate, x1, wsg, wsu, wsd, g, b, n_prompt, tc):
    n = x1.shape[0]
    ns = n // tc
    np_tiles = n_prompt // tc
    const = lambda s: pl.BlockSpec(s, lambda i: (0, 0))
    dspec = lambda f: pl.BlockSpec((1, 1, TOP_K * tc), f, memory_space=pltpu.SMEM)
    return pl.pallas_call(
        functools.partial(_combine_kernel, tc=tc, np_tiles=np_tiles),
        grid=(ns,),
        in_specs=[dspec(lambda i: (i, 0, 0)), dspec(lambda i: (jnp.minimum(i + 1, ns - 1), 0, 0)),
                  pl.BlockSpec(memory_space=pl.ANY),
                  pl.BlockSpec((tc, TOP_K), lambda i: (i, 0)),
                  pl.BlockSpec((tc, D_MODEL), lambda i: (i, 0)),
                  const((D_MODEL, EXP_HIDDEN)), const((D_MODEL, EXP_HIDDEN)), const((EXP_HIDDEN, D_MODEL)),
                  const((1, D_MODEL)), const((1, D_MODEL))],
        out_specs=(pl.BlockSpec((tc, D_MODEL), lambda i: (jnp.minimum(i, np_tiles - 1), 0)),
                   pl.BlockSpec((tc, D_MODEL), lambda i: (jnp.maximum(i - np_tiles, 0), 0))),
        out_shape=(jax.ShapeDtypeStruct((n_prompt, D_MODEL), F32),
                   jax.ShapeDtypeStruct((n - n_prompt, D_MODEL), F32)),
        scratch_shapes=[pltpu.VMEM((2 * TOP_K * tc * XROW_TILE, LANES), jnp.uint32),
                        pltpu.VMEM((tc, D_MODEL), F32),
                        pltpu.SemaphoreType.DMA((2,))],
        compiler_params=_params("arbitrary"),
        name="combine",
    )(dest, dest, y_sorted, gate, x1, wsg, wsu, wsd, g, b)


def _pack_w_in(w_in):
    d = w_in.shape[0]
    o_z = GDN_CONV_CH
    o_a = o_z + GDN_V_W
    o_b = o_a + GDN_HEADS
    o_q = o_b + GDN_HEADS
    o_c = o_q + MLA_HEADS * (MLA_D_NOPE + MLA_D_ROPE)
    o_kr = o_c + MLA_KV_RANK
    zeros = lambda w: jnp.zeros((d, w), w_in.dtype)
    wq = w_in[:, o_q:o_c].reshape(d, MLA_HEADS, MLA_D_NOPE + MLA_D_ROPE)
    q_nope = wq[:, :, :MLA_D_NOPE].reshape(d, MLA_HEADS * MLA_D_NOPE)
    q_rope = jnp.pad(wq[:, :, MLA_D_NOPE:], ((0, 0), (0, 0), (0, LANES - MLA_D_ROPE))).reshape(d, MLA_HEADS * LANES)
    cols = [w_in[:, :o_a], q_nope, q_rope, w_in[:, o_c:o_kr], w_in[:, o_kr:], zeros(LANES - MLA_D_ROPE),
            w_in[:, o_a:o_q], zeros(LANES - 2 * GDN_HEADS)]
    return jnp.concatenate(cols, axis=1).astype(BF16)


def _rope_tables(pos):
    inv_freq = ROPE_THETA ** (-jnp.arange(0, MLA_D_ROPE, 2, dtype=F32) / MLA_D_ROPE)
    ang = pos.astype(F32)[:, None] * inv_freq[None, :]
    cos, sin = jnp.cos(ang), jnp.sin(ang)
    pad = jnp.zeros((pos.shape[0], LANES - MLA_D_ROPE), F32)
    return jnp.concatenate([cos, cos, pad], axis=1), jnp.concatenate([-sin, sin, pad], axis=1)


FRONT_TILE = 512
GDN_TILE = 4 * CHUNK
ATTN_TILE = 2048
ATTN_SLAB = 1024
TOKEN_TILE = 256


def _pick(t, pref):
    return pref if t % pref == 0 else t


def _token_mixers(x, pos, conv_hist, s0, past, wts, shared, cnt0, n_all, row0):
    b, t, _ = x.shape
    cs_tab, sn_tab = _rope_tables(pos)
    tt = _pick(t, FRONT_TILE)
    c = min(CHUNK, t)
    qkv, z, gb, q, k, v, latent, k_rope, conv_new = _front(
        x, wts["w_pack"], wts["conv_w"], conv_hist, wts["gpar"], wts["kvnw"], wts["wukv"], cs_tab, sn_tab, tt, c)
    og, s_new = _gdn(qkv, z, gb, s0, wts["gdn_nw"], _pick(t, GDN_TILE), c)
    if past is None:
        tb = _pick(t, ATTN_TILE)
        om = _attn_prompt(q, k, v, tb, _pick(tb, ATTN_SLAB))
    else:
        om = _attn_sample(q, k, v, past[0], past[1], wts["wukv"])
    n = b * t
    *shared, cnt = _mixln(og.reshape(n, -1), om.reshape(n, -1), x.reshape(n, D_MODEL), wts["w_out"],
                          wts["ln1_g"], wts["ln1_b"], wts["router_w"], wts["router_b"], cnt0,
                          shared, n_all, row0, _pick(n, TOKEN_TILE))
    return shared, cnt, latent, k_rope, s_new, conv_new


def _moe(x1_all, idx, gate, rank, cnt, n_prompt, wts):
    n = x1_all.shape[0]
    counts = cnt[:, 0].astype(I32)
    padded = (counts + MOE_BM - 1) // MOE_BM * MOE_BM
    pend = jnp.cumsum(padded)
    pstart = pend - padded
    td = _pick(math.gcd(n_prompt, n - n_prompt), TOKEN_TILE)
    dest = _dest(idx, rank, pstart.astype(F32).reshape(-1, 1), td)
    dest = dest.reshape(n // td, 1, TOP_K * td)
    n_blocks = n * TOP_K // MOE_BM + N_EXPERTS
    nact = (pend[-1:] // MOE_BM).astype(I32)
    tail = jnp.maximum(pend - MOE_BM, 0).astype(I32)

    xs = _dispatch(tail, dest, x1_all, n_blocks * MOE_BM, td)
    y_sorted = _experts((pstart // MOE_BM).astype(I32), (padded // MOE_BM).astype(I32), nact, xs,
                        wts["exp_wg"], wts["exp_wu"], wts["exp_wd"])
    return _combine(dest, y_sorted, gate.T, x1_all,
                    wts["sh_wg"], wts["sh_wu"], wts["sh_wd"], wts["ln2_g"], wts["ln2_b"], n_prompt, td)


def kernel(x_prompt, x_sample, cache_kv_latent, cache_k_rope, state_gdn, state_conv, w_in, gdn_conv_w, gdn_a_log, gdn_dt_bias, gdn_norm_w, mla_kv_norm_w, mla_w_uk, mla_w_uv, w_out, ln1_g, ln1_b, router_w, router_bias, exp_w_gate, exp_w_up, exp_w_down, shared_w_gate, shared_w_up, shared_w_down, ln2_g, ln2_b):
    assert w_in.shape[0] == 1, "single-layer stack"
    b_p, t_p, _ = x_prompt.shape
    b_s, t_s, _ = x_sample.shape
    past = cache_kv_latent.shape[2]
    l = 0
    pad4 = lambda a: jnp.pad(a.astype(F32), (0, LANES - GDN_HEADS))
    wts = {
        "w_pack": _pack_w_in(w_in[l]),
        "conv_w": gdn_conv_w[l],
        "gpar": jnp.stack([pad4(gdn_a_log[l]), pad4(gdn_dt_bias[l])]),
        "kvnw": mla_kv_norm_w[l].reshape(1, -1),
        "wukv": jnp.concatenate([mla_w_uk[l].reshape(MLA_KV_RANK, -1), mla_w_uv[l].reshape(MLA_KV_RANK, -1)],
                                axis=1).astype(BF16),
        "gdn_nw": gdn_norm_w[l].reshape(1, -1),
        "w_out": w_out[l].astype(BF16),
        "ln1_g": ln1_g[l].reshape(1, -1), "ln1_b": ln1_b[l].reshape(1, -1),
        "router_w": router_w[l].astype(BF16), "router_b": router_bias[l].reshape(-1, 1),
        "exp_wg": exp_w_gate[l], "exp_wu": exp_w_up[l], "exp_wd": exp_w_down[l],
        "sh_wg": shared_w_gate[l].astype(BF16), "sh_wu": shared_w_up[l].astype(BF16),
        "sh_wd": shared_w_down[l].astype(BF16),
        "ln2_g": ln2_g[l].reshape(1, -1), "ln2_b": ln2_b[l].reshape(1, -1),
    }
    n_p, n_s = b_p * t_p, b_s * t_s
    conv0 = jnp.zeros((b_p, GDN_CONV - 1, GDN_CONV_CH), F32)
    s0 = jnp.zeros((b_p, GDN_HEADS, GDN_DK, GDN_DV), F32)
    cnt0 = jnp.zeros((N_EXPERTS, LANES), F32)
    shared, cnt, lat_p, kr_p, sg_p, cv_p = _token_mixers(
        x_prompt, jnp.arange(t_p), conv0, s0, None, wts, None, cnt0, n_p + n_s, 0)
    shared, cnt, lat_s, kr_s, sg_s, cv_s = _token_mixers(
        x_sample, past + jnp.arange(t_s), state_conv[l], state_gdn[l],
        (cache_kv_latent[l], cache_k_rope[l]), wts, shared, cnt, n_p + n_s, n_p)
    y_p, y_s = _moe(*shared, cnt, n_p, wts)
    return (y_p.reshape(b_p, t_p, D_MODEL), y_s.reshape(b_s, t_s, D_MODEL),
            lat_p[None], kr_p[None], sg_p[None], cv_p[None],
            lat_s[None], kr_s[None], sg_s[None], cv_s[None])
```
